```python
import jax, jax.numpy as jnp
from jax import lax
import numpy as np

D_MODEL = 1024
BATCH = 4
SEQ = 8192
DEPTH = 2

N_MIXERS = 2
GRID_W = 64
N_FOURIER_GROUPS = 4
FOURIER_GROUP_DIM = D_MODEL // N_FOURIER_GROUPS
N_HEADS = 16
HEAD_DIM = D_MODEL // N_HEADS
WIN_ROWS = 8
WIN_COLS = 16
N_EXPERTS = 32
N_GROUPS = 4
EXPERTS_PER_GROUP = N_EXPERTS // N_GROUPS
TOP_K = 2
D_EXPERT = D_MODEL // 2
ROUTE_BLOCK = 256
DEEPNORM_ALPHA = (2 * DEPTH) ** 0.25
DEEPNORM_BETA = (8 * DEPTH) ** -0.25
LN_EPS = 1e-5
N_FOURIER_LAYERS = (DEPTH + N_MIXERS - 1) // N_MIXERS
N_NA_LAYERS = DEPTH // N_MIXERS

kernel_name = 'hybrid_fnet_natten_grouped_moe_deepnorm'


def layer_norm(x, g, b):
    xf = x.astype(jnp.float32)
    mu = xf.mean(-1, keepdims=True)
    var = jnp.square(xf - mu).mean(-1, keepdims=True)
    y = (xf - mu) * lax.rsqrt(var + LN_EPS) * g.astype(jnp.float32) + b.astype(jnp.float32)
    return y.astype(x.dtype)


def fourier_mixer(x, w_in, w_out):
    B, S, _ = x.shape
    u = (x @ w_in).reshape(B, S, N_FOURIER_GROUPS, FOURIER_GROUP_DIM)
    f = jnp.fft.fft2(u.astype(jnp.float32), axes=(1, 3), norm='ortho').real
    return f.reshape(B, S, D_MODEL).astype(x.dtype) @ w_out


def neighbourhood_attention_mixer(x, w_qkv, rpb, w_out):
    B, S, _ = x.shape
    rows = S // GRID_W
    kr = min(WIN_ROWS, rows)
    kc = min(WIN_COLS, GRID_W)
    qkv = (x @ w_qkv).reshape(B, S, 3, N_HEADS, HEAD_DIM)
    q = qkv[:, :, 0] * (HEAD_DIM ** -0.5)
    k = qkv[:, :, 1]
    v = qkv[:, :, 2]
    row_start = jnp.clip(jnp.arange(rows) - kr // 2, 0, rows - kr)
    cols = jnp.arange(GRID_W)
    col_start = jnp.clip(cols - kc // 2, 0, GRID_W - kc)
    col_mask = (cols[None, :] >= col_start[:, None]) & (cols[None, :] < col_start[:, None] + kc)
    mask = jnp.broadcast_to(col_mask[:, None, :], (GRID_W, kr, GRID_W)).reshape(GRID_W, kr * GRID_W)
    col_idx = jnp.clip(cols[None, :] - cols[:, None] + WIN_COLS - 1, 0, 2 * WIN_COLS - 2)

    def one_row(r):
        rs = row_start[r]
        q_r = lax.dynamic_slice_in_dim(q, r * GRID_W, GRID_W, axis=1)
        k_r = lax.dynamic_slice_in_dim(k, rs * GRID_W, kr * GRID_W, axis=1)
        v_r = lax.dynamic_slice_in_dim(v, rs * GRID_W, kr * GRID_W, axis=1)
        s = jnp.einsum('bqhd,bkhd->bhqk', q_r, k_r).astype(jnp.float32)
        row_idx = rs + jnp.arange(kr) - r + WIN_ROWS - 1
        bias = rpb[:, row_idx][:, :, col_idx]
        bias = bias.transpose(0, 2, 1, 3).reshape(N_HEADS, GRID_W, kr * GRID_W)
        s = jnp.where(mask, s + bias.astype(jnp.float32), -jnp.inf)
        p = jax.nn.softmax(s, axis=-1).astype(v.dtype)
        return jnp.einsum('bhqk,bkhd->bqhd', p, v_r)

    o = lax.map(one_row, jnp.arange(rows))
    o = o.transpose(1, 0, 2, 3, 4).reshape(B, S, D_MODEL)
    return o @ w_out


def grouped_moe(h, w_router, b_router, w_gate, w_up, w_down):
    B, S, D = h.shape
    T = B * S
    xt = h.reshape(T, D)
    scores = jax.nn.sigmoid((xt @ w_router).astype(jnp.float32))
    sel = (scores + b_router.astype(jnp.float32)).reshape(T, N_GROUPS, EXPERTS_PER_GROUP)
    group_score = lax.top_k(sel, TOP_K)[0].sum(-1)
    group = jnp.argmax(group_score, axis=-1)
    in_group = jnp.take_along_axis(sel, group[:, None, None], axis=1)[:, 0]
    _, local = lax.top_k(in_group, TOP_K)
    expert_idx = group[:, None] * EXPERTS_PER_GROUP + local
    gate = jnp.take_along_axis(scores, expert_idx, axis=1)
    gate = gate / gate.sum(-1, keepdims=True)
    A = T * TOP_K
    flat_e = expert_idx.reshape(A)
    order = jnp.argsort(flat_e, stable=True)
    sorted_e = flat_e[order]
    sorted_tok = order // TOP_K
    sorted_gate = gate.reshape(A)[order]
    counts = jnp.bincount(flat_e, length=N_EXPERTS)
    padded = (counts + ROUTE_BLOCK - 1) // ROUTE_BLOCK * ROUTE_BLOCK
    pad_end = jnp.cumsum(padded)
    pad_start = pad_end - padded
    seg_start = jnp.cumsum(counts) - counts
    dest = pad_start[sorted_e] + jnp.arange(A) - seg_start[sorted_e]
    n_blocks = -(-A // ROUTE_BLOCK) + N_EXPERTS
    P = n_blocks * ROUTE_BLOCK
    slot_tok = jnp.full((P,), T, jnp.int32).at[dest].set(sorted_tok.astype(jnp.int32))
    x_pad = jnp.concatenate([xt, jnp.zeros((1, D), xt.dtype)], axis=0)
    xs = x_pad[slot_tok].reshape(n_blocks, ROUTE_BLOCK, D)
    block_expert = jnp.minimum(
        jnp.searchsorted(pad_end, jnp.arange(n_blocks) * ROUTE_BLOCK, side='right'), N_EXPERTS - 1)

    def expert_block(args):
        xb, e = args
        hb = jax.nn.silu(xb @ w_gate[e]) * (xb @ w_up[e])
        return hb @ w_down[e]

    ys = lax.map(expert_block, (xs, block_expert)).reshape(P, D)
    contrib = ys[dest] * sorted_gate[:, None].astype(ys.dtype)
    out = jax.ops.segment_sum(contrib, sorted_tok, num_segments=T)
    return out.reshape(B, S, D)


def setup_inputs(seed: int = 0) -> dict:
    key = jax.random.key(seed)
    ks = jax.random.split(key, 14)
    f32 = jnp.float32

    def nrm(k, shape, scale):
        return jax.random.normal(k, shape, f32) * scale

    d_inv = D_MODEL ** -0.5
    return {
        'x': nrm(ks[0], (BATCH, SEQ, D_MODEL), 1.0),
        'fourier_w_in': nrm(ks[1], (N_FOURIER_LAYERS, D_MODEL, D_MODEL), d_inv),
        'fourier_w_out': nrm(ks[2], (N_FOURIER_LAYERS, D_MODEL, D_MODEL), d_inv * DEEPNORM_BETA),
        'na_w_qkv': nrm(ks[3], (N_NA_LAYERS, D_MODEL, 3 * D_MODEL), d_inv),
        'na_rpb': nrm(ks[4], (N_NA_LAYERS, N_HEADS, 2 * WIN_ROWS - 1, 2 * WIN_COLS - 1), 0.02),
        'na_w_out': nrm(ks[5], (N_NA_LAYERS, D_MODEL, D_MODEL), d_inv * DEEPNORM_BETA),
        'router_w': nrm(ks[6], (D_MODEL, N_EXPERTS), d_inv),
        'router_b': nrm(ks[7], (N_EXPERTS,), 0.01),
        'expert_w_gate': nrm(ks[8], (DEPTH, N_EXPERTS, D_MODEL, D_EXPERT), d_inv),
        'expert_w_up': nrm(ks[9], (DEPTH, N_EXPERTS, D_MODEL, D_EXPERT), d_inv),
        'expert_w_down': nrm(ks[10], (DEPTH, N_EXPERTS, D_EXPERT, D_MODEL), D_EXPERT ** -0.5 * DEEPNORM_BETA),
        'ln_g': 1.0 + nrm(ks[11], (DEPTH, 2, D_MODEL), 0.02),
        'ln_b': nrm(ks[12], (DEPTH, 2, D_MODEL), 0.02),
    }


def reference(x, fourier_w_in, fourier_w_out, na_w_qkv, na_rpb, na_w_out, router_w, router_b,
              expert_w_gate, expert_w_up, expert_w_down, ln_g, ln_b):
    for i in range(DEPTH):
        j = i // N_MIXERS
        if i % N_MIXERS == 0:
            y = fourier_mixer(x, fourier_w_in[j], fourier_w_out[j])
        else:
            y = neighbourhood_attention_mixer(x, na_w_qkv[j], na_rpb[j], na_w_out[j])
        x = layer_norm(DEEPNORM_ALPHA * x + y, ln_g[i, 0], ln_b[i, 0])
        y = grouped_moe(x, router_w, router_b, expert_w_gate[i], expert_w_up[i], expert_w_down[i])
        x = layer_norm(DEEPNORM_ALPHA * x + y, ln_g[i, 1], ln_b[i, 1])
    return x
```

```python
import functools

import jax
import jax.numpy as jnp
import numpy as np
from jax import lax
from jax.experimental import pallas as pl
from jax.experimental.pallas import tpu as pltpu

F32 = jnp.float32
BF16 = jnp.bfloat16
I32 = jnp.int32

D_MODEL = 1024
GRID_W = 64
N_FOURIER_GROUPS = 4
FOURIER_GROUP_DIM = D_MODEL // N_FOURIER_GROUPS
N_HEADS = 16
HEAD_DIM = D_MODEL // N_HEADS
WIN_ROWS = 8
WIN_COLS = 16
N_EXPERTS = 32
N_GROUPS = 4
EXPERTS_PER_GROUP = N_EXPERTS // N_GROUPS
D_EXPERT = D_MODEL // 2
DEPTH = 2
DEEPNORM_ALPHA = (2 * DEPTH) ** 0.25
LN_EPS = 1e-5

V7X_VMEM_BYTES = 64 * 1024 * 1024
LANES = 128

FFT_N1 = 64
FFT_N2 = 128
FFT_CHUNK = 128

MM_BM = 1024
LN_BM = 512
ROUTER_BT = 1024
EXPERT_BM = 256
COMBINE_BT = 256
NA_RQ = 4
NA_KROWS = NA_RQ + WIN_ROWS
NA_HEADS_PER_STEP = 8
NEG_BIAS = -1e30


def _vmem_limit(nbytes):
    return int(min(max(nbytes, 32 * 1024 * 1024), V7X_VMEM_BYTES - 8 * 1024 * 1024))


def _mm_kernel(a_ref, b_ref, o_ref, *, precision):
    if precision is None:
        a = a_ref[...].astype(BF16)
        b = b_ref[...].astype(BF16)
        acc = jnp.dot(a, b, preferred_element_type=F32)
    else:
        acc = jnp.dot(a_ref[...], b_ref[...], preferred_element_type=F32, precision=precision)
    o_ref[...] = acc.astype(o_ref.dtype)


def _matmul(a, b, out_dtype, bm, bn, precision=None):
    m, k = a.shape
    _, n = b.shape
    est = 2 * (bm * k * a.dtype.itemsize + k * bn * b.dtype.itemsize + bm * bn * 4) + 3 * bm * bn * 4
    return pl.pallas_call(
        functools.partial(_mm_kernel, precision=precision),
        grid=(m // bm, n // bn),
        in_specs=[pl.BlockSpec((bm, k), lambda i, j: (i, 0)),
                  pl.BlockSpec((k, bn), lambda i, j: (0, j))],
        out_specs=pl.BlockSpec((bm, bn), lambda i, j: (i, j)),
        out_shape=jax.ShapeDtypeStruct((m, n), out_dtype),
        compiler_params=pltpu.CompilerParams(
            dimension_semantics=("parallel", "parallel"), vmem_limit_bytes=_vmem_limit(est)),
        name="matmul",
    )(a, b)


def _fourier_channel_tables():
    n = FOURIER_GROUP_DIM
    c = np.arange(n)
    ang = 2.0 * np.pi * ((c[:, None] * c[None, :]) % n) / n
    cos, sin = np.cos(ang) / np.sqrt(n), np.sin(ang) / np.sqrt(n)
    half = FFT_CHUNK
    tabs = [np.concatenate([cos[:, h * half:(h + 1) * half], sin[:, h * half:(h + 1) * half]], axis=1)
            for h in range(n // half)]
    return np.stack(tabs).astype(np.float32)


def _fourier_seq_tables(seq):
    n1, n2 = FFT_N1, FFT_N2
    assert n1 * n2 == seq
    k2 = np.arange(n2)
    s2 = np.arange(n2)
    m1 = np.empty((n1, 2 * n2, 2 * n2), np.float32)
    for s1 in range(n1):
        ang = 2.0 * np.pi * ((k2[:, None] * (s1 + n1 * s2[None, :])) % seq) / seq
        mr, mi = np.cos(ang) / np.sqrt(n2), np.sin(ang) / np.sqrt(n2)
        m1[s1] = np.block([[mr, -mi], [mi, mr]])
    k1 = np.arange(n1)
    ang = 2.0 * np.pi * ((k1[:, None] * k1[None, :]) % n1) / n1
    w2 = np.concatenate([np.cos(ang), -np.sin(ang)], axis=1) / np.sqrt(n1)
    return m1, w2.astype(np.float32)


def _fft_kernel(a_ref, b_ref, m1_ref, w2_ref, o_ref, zs_ref):
    n1, n2, c = FFT_N1, FFT_N2, FFT_CHUNK

    def stage1(s1, carry):
        rows = pl.ds(s1, n2, stride=n1)
        x = jnp.concatenate([a_ref[0, rows, :], b_ref[0, rows, :]], axis=0).astype(BF16)
        z = jnp.dot(m1_ref[s1], x, preferred_element_type=F32)
        zs_ref[pl.ds(pl.multiple_of(s1 * 2 * n2, 2 * n2), 2 * n2), :] = z
        return carry

    lax.fori_loop(0, n1, stage1, 0)

    def stage2(kk, carry):
        k2 = 2 * kk
        parts = []
        for d in range(2):
            zr = zs_ref[pl.ds(k2 + d, n1, stride=2 * n2), :]
            zi = zs_ref[pl.ds(n2 + k2 + d, n1, stride=2 * n2), :]
            parts.append(jnp.concatenate([zr, zi], axis=0))
        z = jnp.concatenate(parts, axis=1).astype(BF16)
        y = jnp.dot(w2_ref[...], z, preferred_element_type=F32)
        o_ref[0, pl.ds(k2, n1, stride=n2), :] = y[:, :c]
        o_ref[0, pl.ds(k2 + 1, n1, stride=n2), :] = y[:, c:]
        return carry

    lax.fori_loop(0, n2 // 2, stage2, 0)


def _seq_fft(ab, m1, w2):
    b, s, two_d = ab.shape
    d = two_d // 2
    c = FFT_CHUNK
    est = 2 * (s * 2 * c * 4 + m1.size * 2 + s * c * 4) + FFT_N1 * 2 * FFT_N2 * c * 4 + (4 << 20)
    return pl.pallas_call(
        _fft_kernel,
        grid=(b, d // c),
        in_specs=[pl.BlockSpec((1, s, c), lambda i, j: (i, 0, 2 * j)),
                  pl.BlockSpec((1, s, c), lambda i, j: (i, 0, 2 * j + 1)),
                  pl.BlockSpec(m1.shape, lambda i, j: (0, 0, 0)),
                  pl.BlockSpec(w2.shape, lambda i, j: (0, 0))],
        out_specs=pl.BlockSpec((1, s, c), lambda i, j: (i, 0, j)),
        out_shape=jax.ShapeDtypeStruct((b, s, d), F32),
        scratch_shapes=[pltpu.VMEM((FFT_N1 * 2 * FFT_N2, c), F32)],
        compiler_params=pltpu.CompilerParams(
            dimension_semantics=("parallel", "parallel"), vmem_limit_bytes=_vmem_limit(est)),
        name="seq_fft",
    )(ab, ab, m1, w2)


def _fourier_mixer_pre_out(x, w_in):
    b, s, d = x.shape
    g, gd, c = N_FOURIER_GROUPS, FOURIER_GROUP_DIM, FFT_CHUNK
    halves = gd // c
    cs = jnp.asarray(_fourier_channel_tables())
    w_ab = pl.pallas_call(
        functools.partial(_mm_kernel, precision=lax.Precision.HIGHEST),
        grid=(g, halves),
        in_specs=[pl.BlockSpec((d, gd), lambda i, h: (0, i)),
                  pl.BlockSpec((None, gd, 2 * c), lambda i, h: (h, 0, 0))],
        out_specs=pl.BlockSpec((d, 2 * c), lambda i, h: (0, i * halves + h)),
        out_shape=jax.ShapeDtypeStruct((d, 2 * d), BF16),
        name="fourier_weight_fold",
    )(w_in, cs)
    ab = _matmul(x.reshape(b * s, d), w_ab, F32, MM_BM, 1024).reshape(b, s, 2 * d)
    m1, w2 = _fourier_seq_tables(s)
    return _seq_fft(ab, jnp.asarray(m1, dtype=BF16), jnp.asarray(w2, dtype=BF16))


def _layer_norm_rows(z, g, b):
    mu = jnp.mean(z, axis=-1, keepdims=True)
    zc = z - mu
    var = jnp.mean(zc * zc, axis=-1, keepdims=True)
    return zc * lax.rsqrt(var + LN_EPS) * g + b


def _mm_ln_kernel(a_ref, w_ref, x_ref, g_ref, b_ref, o_ref):
    y = jnp.dot(a_ref[...].astype(BF16), w_ref[...], preferred_element_type=F32)
    z = DEEPNORM_ALPHA * x_ref[...] + y
    o_ref[...] = _layer_norm_rows(z, g_ref[...], b_ref[...])


def _proj_residual_ln(a, w_bf16, x, ln_g, ln_b):
    t, k = a.shape
    d = w_bf16.shape[1]
    bm = LN_BM
    est = 2 * (bm * k * a.dtype.itemsize + k * d * 2 + 2 * bm * d * 4) + 4 * bm * d * 4
    return pl.pallas_call(
        _mm_ln_kernel,
        grid=(t // bm,),
        in_specs=[pl.BlockSpec((bm, k), lambda i: (i, 0)),
                  pl.BlockSpec((k, d), lambda i: (0, 0)),
                  pl.BlockSpec((bm, d), lambda i: (i, 0)),
                  pl.BlockSpec((1, d), lambda i: (0, 0)),
                  pl.BlockSpec((1, d), lambda i: (0, 0))],
        out_specs=pl.BlockSpec((bm, d), lambda i: (i, 0)),
        out_shape=jax.ShapeDtypeStruct((t, d), F32),
        compiler_params=pltpu.CompilerParams(
            dimension_semantics=("parallel",), vmem_limit_bytes=_vmem_limit(est)),
        name="proj_residual_ln",
    )(a, w_bf16, x, ln_g.reshape(1, d), ln_b.reshape(1, d))


def _top2_rows(v, iota):
    n_rows = v.shape[0]
    m1 = jnp.max(v, axis=0, keepdims=True)
    i1 = jnp.min(jnp.where(v == m1, iota, n_rows), axis=0, keepdims=True)
    v2 = jnp.where(iota == i1, -jnp.inf, v)
    m2 = jnp.max(v2, axis=0, keepdims=True)
    i2 = jnp.min(jnp.where(v2 == m2, iota, n_rows), axis=0, keepdims=True)
    return m1, i1, m2, i2


def _router_kernel(x_ref, wt_ref, b_ref, eidx_ref, gate_ref):
    x = x_ref[...]
    w = wt_ref[...]
    xh = x.astype(BF16)
    xl = (x - xh.astype(F32)).astype(BF16)
    wh = w.astype(BF16)
    wl = (w - wh.astype(F32)).astype(BF16)
    nt = (((1,), (1,)), ((), ()))
    logits = (lax.dot_general(wh, xh, nt, preferred_element_type=F32)
              + lax.dot_general(wh, xl, nt, preferred_element_type=F32)
              + lax.dot_general(wl, xh, nt, preferred_element_type=F32))
    scores = 1.0 / (1.0 + jnp.exp(-logits))
    sel = scores + b_ref[...]
    epg = EXPERTS_PER_GROUP
    bt = x.shape[0]
    iota = lax.broadcasted_iota(I32, (epg, bt), 0)

    best = None
    for g in range(N_GROUPS):
        m1, _, m2, _ = _top2_rows(sel[g * epg:(g + 1) * epg], iota)
        gs = m1 + m2
        if best is None:
            best, gidx = gs, jnp.zeros((1, bt), I32)
        else:
            better = gs > best
            gidx = jnp.where(better, g, gidx)
            best = jnp.where(better, gs, best)

    sel_in = sel[0:epg]
    sc_in = scores[0:epg]
    for g in range(1, N_GROUPS):
        pick = gidx == g
        sel_in = jnp.where(pick, sel[g * epg:(g + 1) * epg], sel_in)
        sc_in = jnp.where(pick, scores[g * epg:(g + 1) * epg], sc_in)
    _, i1, _, i2 = _top2_rows(sel_in, iota)
    g1 = jnp.sum(jnp.where(iota == i1, sc_in, 0.0), axis=0, keepdims=True)
    g2 = jnp.sum(jnp.where(iota == i2, sc_in, 0.0), axis=0, keepdims=True)
    denom = g1 + g2
    eidx_ref[0:1, :] = gidx * epg + i1
    eidx_ref[1:2, :] = gidx * epg + i2
    gate_ref[0:1, :] = g1 / denom
    gate_ref[1:2, :] = g2 / denom


def _router(x, w_router_t, b_router):
    t, d = x.shape
    e = w_router_t.shape[0]
    bt = ROUTER_BT
    return pl.pallas_call(
        _router_kernel,
        grid=(t // bt,),
        in_specs=[pl.BlockSpec((bt, d), lambda i: (i, 0)),
                  pl.BlockSpec((e, d), lambda i: (0, 0)),
                  pl.BlockSpec((e, 1), lambda i: (0, 0))],
        out_specs=[pl.BlockSpec((2, bt), lambda i: (0, i)),
                   pl.BlockSpec((2, bt), lambda i: (0, i))],
        out_shape=[jax.ShapeDtypeStruct((2, t), I32), jax.ShapeDtypeStruct((2, t), F32)],
        compiler_params=pltpu.CompilerParams(dimension_semantics=("parallel",)),
        name="router",
    )(x, w_router_t, b_router.reshape(e, 1))


def _route_slots(eidx, bm):
    t = eidx.shape[1]
    a = 2 * t
    flat_e = eidx.T.reshape(a)
    onehot = (flat_e[:, None] == jnp.arange(N_EXPERTS, dtype=I32)[None, :]).astype(I32)
    csum = jnp.cumsum(onehot, axis=0)
    rank = jnp.sum(csum * onehot, axis=1) - 1
    counts = csum[-1]
    padded = (counts + bm - 1) // bm * bm
    pad_end = jnp.cumsum(padded)
    pad_start = pad_end - padded
    dest = (jnp.sum(onehot * pad_start[None, :], axis=1) + rank).astype(I32)
    p = a + N_EXPERTS * bm
    slot_tok = jnp.zeros((p,), I32).at[dest].set(jnp.arange(a, dtype=I32) // 2, unique_indices=True)
    nb = p // bm
    block_expert = jnp.minimum(
        jnp.searchsorted(pad_end, jnp.arange(nb, dtype=I32) * bm, side="right"), N_EXPERTS - 1).astype(I32)
    n_active = (pad_end[-1] // bm).astype(I32).reshape(1)
    return dest.reshape(t, 2), slot_tok, block_expert, n_active


def _expert_kernel(be_ref, tok_ref, nact_ref, x_hbm, wg_ref, wu_ref, wd_ref, o_ref,
                   xbuf, wg_bf, wu_bf, wd_bf, sem):
    bm = EXPERT_BM
    i = pl.program_id(0)
    nact = nact_ref[0]
    slot = i % 2

    def issue(blk, dst_slot):
        base = blk * bm

        def body(r, carry):
            tok = tok_ref[base + r]
            pltpu.make_async_copy(x_hbm.at[pl.ds(tok, 1), :], xbuf.at[dst_slot, pl.ds(r, 1), :],
                                  sem.at[dst_slot]).start()
            return carry

        lax.fori_loop(0, bm, body, 0, unroll=8)

    @pl.when(jnp.logical_and(i == 0, nact > 0))
    def _():
        issue(0, 0)

    @pl.when(i + 1 < nact)
    def _():
        issue(i + 1, 1 - slot)

    changed = jnp.logical_or(i == 0, be_ref[i] != be_ref[jnp.maximum(i - 1, 0)])

    @pl.when(jnp.logical_and(changed, i < nact))
    def _():
        wg_bf[...] = wg_ref[0].astype(BF16)
        wu_bf[...] = wu_ref[0].astype(BF16)
        wd_bf[...] = wd_ref[0].astype(BF16)

    @pl.when(i < nact)
    def _():
        pltpu.make_async_copy(x_hbm.at[pl.ds(0, bm), :], xbuf.at[slot], sem.at[slot]).wait()
        x = xbuf[slot].astype(BF16)
        g = jnp.dot(x, wg_bf[...], preferred_element_type=F32)
        u = jnp.dot(x, wu_bf[...], preferred_element_type=F32)
        h = (g / (1.0 + jnp.exp(-g))) * u
        o_ref[...] = jnp.dot(h.astype(BF16), wd_bf[...], preferred_element_type=F32)

    @pl.when(i >= nact)
    def _():
        o_ref[...] = jnp.zeros_like(o_ref)


def _experts(x, slot_tok, block_expert, n_active, w_gate, w_up, w_down):
    t, d = x.shape
    e, _, f = w_gate.shape
    bm = EXPERT_BM
    p = slot_tok.shape[0]
    nb = p // bm
    est = 2 * 3 * d * f * 4 + 3 * d * f * 2 + 2 * bm * d * 4 + 2 * bm * d * 4 + 6 * bm * d * 4
    grid_spec = pltpu.PrefetchScalarGridSpec(
        num_scalar_prefetch=3,
        grid=(nb,),
        in_specs=[pl.BlockSpec(memory_space=pl.ANY),
                  pl.BlockSpec((1, d, f), lambda i, be, tok, na: (be[i], 0, 0)),
                  pl.BlockSpec((1, d, f), lambda i, be, tok, na: (be[i], 0, 0)),
                  pl.BlockSpec((1, f, d), lambda i, be, tok, na: (be[i], 0, 0))],
        out_specs=pl.BlockSpec((bm, d), lambda i, be, tok, na: (i, 0)),
        scratch_shapes=[pltpu.VMEM((2, bm, d), F32),
                        pltpu.VMEM((d, f), BF16), pltpu.VMEM((d, f), BF16), pltpu.VMEM((f, d), BF16),
                        pltpu.SemaphoreType.DMA((2,))],
    )
    return pl.pallas_call(
        _expert_kernel,
        grid_spec=grid_spec,
        out_shape=jax.ShapeDtypeStruct((p, d), F32),
        compiler_params=pltpu.CompilerParams(
            dimension_semantics=("arbitrary",), vmem_limit_bytes=_vmem_limit(est)),
        name="experts",
    )(block_expert, slot_tok, n_active, x, w_gate, w_up, w_down)


def _combine_kernel(dest_ref, x_ref, gate_ref, g_ref, b_ref, ys_hbm, o_ref, ybuf, sem):
    bt = COMBINE_BT
    i = pl.program_id(0)
    n = pl.num_programs(0)
    slot = i % 2

    def issue(blk, dst_slot):
        base = blk * 2 * bt

        def body(r, carry):
            for k in range(2):
                d = dest_ref[base + 2 * r + k]
                pltpu.make_async_copy(ys_hbm.at[pl.ds(d, 1), :], ybuf.at[dst_slot, pl.ds(k * bt + r, 1), :],
                                      sem.at[dst_slot]).start()
            return carry

        lax.fori_loop(0, bt, body, 0, unroll=4)

    @pl.when(i == 0)
    def _():
        issue(0, 0)

    @pl.when(i + 1 < n)
    def _():
        issue(i + 1, 1 - slot)

    pltpu.make_async_copy(ys_hbm.at[pl.ds(0, 2 * bt), :], ybuf.at[slot], sem.at[slot]).wait()
    gate = gate_ref[...]
    y = ybuf[slot, 0:bt, :] * gate[:, 0:1] + ybuf[slot, bt:2 * bt, :] * gate[:, 1:2]
    z = DEEPNORM_ALPHA * x_ref[...] + y
    o_ref[...] = _layer_norm_rows(z, g_ref[...], b_ref[...])


def _combine_ln(x, ys, dest, gate_t, ln_g, ln_b):
    t, d = x.shape
    bt = COMBINE_BT
    grid_spec = pltpu.PrefetchScalarGridSpec(
        num_scalar_prefetch=1,
        grid=(t // bt,),
        in_specs=[pl.BlockSpec((bt, d), lambda i, dst: (i, 0)),
                  pl.BlockSpec((bt, 2), lambda i, dst: (i, 0)),
                  pl.BlockSpec((1, d), lambda i, dst: (0, 0)),
                  pl.BlockSpec((1, d), lambda i, dst: (0, 0)),
                  pl.BlockSpec(memory_space=pl.ANY)],
        out_specs=pl.BlockSpec((bt, d), lambda i, dst: (i, 0)),
        scratch_shapes=[pltpu.VMEM((2, 2 * bt, d), F32), pltpu.SemaphoreType.DMA((2,))],
    )
    return pl.pallas_call(
        _combine_kernel,
        grid_spec=grid_spec,
        out_shape=jax.ShapeDtypeStruct((t, d), F32),
        compiler_params=pltpu.CompilerParams(dimension_semantics=("arbitrary",)),
        name="combine_ln",
    )(dest.reshape(2 * t), x, gate_t, ln_g.reshape(1, d), ln_b.reshape(1, d), ys)


def _moe_residual_ln(x, w_router_t, b_router, w_gate, w_up, w_down, ln_g, ln_b):
    eidx, gate = _router(x, w_router_t, b_router)
    dest, slot_tok, block_expert, n_active = _route_slots(eidx, EXPERT_BM)
    ys = _experts(x, slot_tok, block_expert, n_active, w_gate, w_up, w_down)
    return _combine_ln(x, ys, dest, gate.T, ln_g, ln_b)


def _na_bias_tables(rpb, rows):
    w = GRID_W
    kc = min(WIN_COLS, w)
    kr = min(WIN_ROWS, rows)
    cols = np.arange(w)
    col_start = np.clip(cols - kc // 2, 0, w - kc)
    col_mask = (cols[None, :] >= col_start[:, None]) & (cols[None, :] < col_start[:, None] + kc)
    col_idx = np.clip(cols[None, :] - cols[:, None] + WIN_COLS - 1, 0, 2 * WIN_COLS - 2)
    tb = rpb.astype(F32)[:, :, col_idx]
    tb = jnp.where(col_mask[None, None], tb, NEG_BIAS)
    n_ri = 2 * WIN_ROWS - 1
    tb = jnp.concatenate([tb, jnp.full((N_HEADS, 1, w, w), NEG_BIAS, F32)], axis=1)
    n_blocks = rows // NA_RQ
    ri = np.full((3, NA_RQ, NA_KROWS), n_ri, np.int32)
    for ty, blk in enumerate((0, 1, n_blocks - 1)):
        ks = int(np.clip(blk - 1, 0, n_blocks - 3)) * NA_RQ
        for rq in range(NA_RQ):
            r = blk * NA_RQ + rq
            rs = int(np.clip(r - kr // 2, 0, rows - kr))
            for j in range(NA_KROWS):
                key_row = ks + j
                if rs <= key_row < rs + kr:
                    ri[ty, rq, j] = key_row - r + WIN_ROWS - 1
    bt = tb[:, ri]
    return bt.transpose(1, 0, 2, 4, 3, 5).reshape(3, N_HEADS, NA_RQ * w, NA_KROWS * w)


def _na_kernel(q_ref, k0_ref, k1_ref, k2_ref, v0_ref, v1_ref, v2_ref, bias_ref, o_ref):
    dh = HEAD_DIM
    nt = (((1,), (1,)), ((), ()))
    scale = HEAD_DIM ** -0.5
    outs = []
    for h in range(NA_HEADS_PER_STEP):
        cols = slice(h * dh, (h + 1) * dh)
        q = q_ref[0, :, cols] * scale
        k = jnp.concatenate([k0_ref[0, :, cols], k1_ref[0, :, cols], k2_ref[0, :, cols]], axis=0)
        v = jnp.concatenate([v0_ref[0, :, cols], v1_ref[0, :, cols], v2_ref[0, :, cols]], axis=0)
        s = lax.dot_general(q, k, nt, preferred_element_type=F32) + bias_ref[0, h]
        m = jnp.max(s, axis=-1, keepdims=True)
        p = jnp.exp(s - m)
        l = jnp.sum(p, axis=-1, keepdims=True)
        o = jnp.dot(p.astype(BF16), v, preferred_element_type=F32)
        outs.append(o / l)
    o_ref[0] = jnp.concatenate(outs, axis=-1).astype(o_ref.dtype)


def _neighbourhood_attention(qkv, bias, rows):
    b, s, _ = qkv.shape
    d = D_MODEL
    w = GRID_W
    tq = NA_RQ * w
    n_blocks = rows // NA_RQ
    hw = NA_HEADS_PER_STEP * HEAD_DIM
    n_hh = d // hw
    assert NA_KROWS * w == 3 * tq and n_blocks >= 3

    def kv_map(part, j):
        def index_map(hh, bi, i):
            return (bi, jnp.clip(i - 1, 0, n_blocks - 3) + j, part * n_hh + hh)
        return index_map

    def bias_map(hh, bi, i):
        ty = jnp.where(i == 0, 0, jnp.where(i == n_blocks - 1, 2, 1))
        return (ty, hh, 0, 0)

    blk = (1, tq, hw)
    est = 2 * (7 * tq * hw * 2 + NA_HEADS_PER_STEP * tq * 3 * tq * 4 + tq * hw * 2) + 8 * tq * 3 * tq * 4
    return pl.pallas_call(
        _na_kernel,
        grid=(n_hh, b, n_blocks),
        in_specs=[pl.BlockSpec(blk, lambda hh, bi, i: (bi, i, hh))]
                 + [pl.BlockSpec(blk, kv_map(1, j)) for j in range(3)]
                 + [pl.BlockSpec(blk, kv_map(2, j)) for j in range(3)]
                 + [pl.BlockSpec((1, NA_HEADS_PER_STEP, tq, 3 * tq), bias_map)],
        out_specs=pl.BlockSpec(blk, lambda hh, bi, i: (bi, i, hh)),
        out_shape=jax.ShapeDtypeStruct((b, s, d), BF16),
        compiler_params=pltpu.CompilerParams(
            dimension_semantics=("parallel", "parallel", "parallel"), vmem_limit_bytes=_vmem_limit(est)),
        name="neighbourhood_attention",
    )(qkv, qkv, qkv, qkv, qkv, qkv, qkv, bias)


def kernel(x, fourier_w_in, fourier_w_out, na_w_qkv, na_rpb, na_w_out, router_w, router_b,
           expert_w_gate, expert_w_up, expert_w_down, ln_g, ln_b):
    b, s, d = x.shape
    t = b * s
    rows = s // GRID_W
    w_router_t = router_w.T
    xt = x.reshape(t, d)

    f = _fourier_mixer_pre_out(x, fourier_w_in[0])
    xt = _proj_residual_ln(f.reshape(t, d), fourier_w_out[0].astype(BF16), xt, ln_g[0, 0], ln_b[0, 0])
    xt = _moe_residual_ln(xt, w_router_t, router_b, expert_w_gate[0], expert_w_up[0], expert_w_down[0],
                          ln_g[0, 1], ln_b[0, 1])

    qkv = _matmul(xt, na_w_qkv[0].astype(BF16), BF16, MM_BM, 1024).reshape(b, s, 3 * d)
    bias = _na_bias_tables(na_rpb[0], rows)
    o = _neighbourhood_attention(qkv, bias, rows)
    xt = _proj_residual_ln(o.reshape(t, d), na_w_out[0].astype(BF16), xt, ln_g[1, 0], ln_b[1, 0])
    xt = _moe_residual_ln(xt, w_router_t, router_b, expert_w_gate[1], expert_w_up[1], expert_w_down[1],
                          ln_g[1, 1], ln_b[1, 1])
    return xt.reshape(b, s, d)
```

```python
import functools

import jax
import jax.numpy as jnp
import numpy as np
from jax import lax
from jax.experimental import pallas as pl
from jax.experimental.pallas import tpu as pltpu

F32 = jnp.float32
BF16 = jnp.bfloat16
I32 = jnp.int32

D_MODEL = 1024
GRID_W = 64
N_FOURIER_GROUPS = 4
FOURIER_GROUP_DIM = D_MODEL // N_FOURIER_GROUPS
N_HEADS = 16
HEAD_DIM = D_MODEL // N_HEADS
WIN_ROWS = 8
WIN_COLS = 16
N_EXPERTS = 32
N_GROUPS = 4
EXPERTS_PER_GROUP = N_EXPERTS // N_GROUPS
D_EXPERT = D_MODEL // 2
DEPTH = 2
DEEPNORM_ALPHA = (2 * DEPTH) ** 0.25
LN_EPS = 1e-5

V7X_VMEM_BYTES = 64 * 1024 * 1024
LANES = 128

FFT_N1 = 64
FFT_N2 = 128
FFT_CHUNK = 128

MM_BM = 1024
LN_BM = 512
ROUTER_BT = 1024
EXPERT_BM = 256
COMBINE_BT = 256
NA_RQ = 4
NA_KROWS = NA_RQ + WIN_ROWS
NA_HEADS_PER_STEP = 8
NEG_BIAS = -1e30


def _vmem_limit(nbytes):
    return int(min(max(nbytes, 32 * 1024 * 1024), V7X_VMEM_BYTES - 8 * 1024 * 1024))


def _mm_kernel(a_ref, b_ref, o_ref, *, precision):
    if precision is None:
        a = a_ref[...].astype(BF16)
        b = b_ref[...].astype(BF16)
        acc = jnp.dot(a, b, preferred_element_type=F32)
    else:
        acc = jnp.dot(a_ref[...], b_ref[...], preferred_element_type=F32, precision=precision)
    o_ref[...] = acc.astype(o_ref.dtype)


def _matmul(a, b, out_dtype, bm, bn, precision=None):
    m, k = a.shape
    _, n = b.shape
    est = 2 * (bm * k * a.dtype.itemsize + k * bn * b.dtype.itemsize + bm * bn * 4) + 3 * bm * bn * 4
    return pl.pallas_call(
        functools.partial(_mm_kernel, precision=precision),
        grid=(m // bm, n // bn),
        in_specs=[pl.BlockSpec((bm, k), lambda i, j: (i, 0)),
                  pl.BlockSpec((k, bn), lambda i, j: (0, j))],
        out_specs=pl.BlockSpec((bm, bn), lambda i, j: (i, j)),
        out_shape=jax.ShapeDtypeStruct((m, n), out_dtype),
        compiler_params=pltpu.CompilerParams(
            dimension_semantics=("parallel", "parallel"), vmem_limit_bytes=_vmem_limit(est)),
        name="matmul",
    )(a, b)


def _fourier_channel_tables():
    n = FOURIER_GROUP_DIM
    c = np.arange(n)
    ang = 2.0 * np.pi * ((c[:, None] * c[None, :]) % n) / n
    cos, sin = np.cos(ang) / np.sqrt(n), np.sin(ang) / np.sqrt(n)
    half = FFT_CHUNK
    tabs = [np.concatenate([cos[:, h * half:(h + 1) * half], sin[:, h * half:(h + 1) * half]], axis=1)
            for h in range(n // half)]
    return np.stack(tabs).astype(np.float32)


def _fourier_seq_tables(seq):
    n1, n2 = FFT_N1, FFT_N2
    assert n1 * n2 == seq
    k2 = np.arange(n2)
    s2 = np.arange(n2)
    m1 = np.empty((n1, 2 * n2, 2 * n2), np.float32)
    for s1 in range(n1):
        ang = 2.0 * np.pi * ((k2[:, None] * (s1 + n1 * s2[None, :])) % seq) / seq
        mr, mi = np.cos(ang) / np.sqrt(n2), np.sin(ang) / np.sqrt(n2)
        m1[s1] = np.block([[mr, -mi], [mi, mr]])
    k1 = np.arange(n1)
    ang = 2.0 * np.pi * ((k1[:, None] * k1[None, :]) % n1) / n1
    w2 = np.concatenate([np.cos(ang), -np.sin(ang)], axis=1) / np.sqrt(n1)
    return m1, w2.astype(np.float32)


def _fft_kernel(a_ref, b_ref, m1_ref, w2_ref, o_ref, zs_ref):
    n1, n2, c = FFT_N1, FFT_N2, FFT_CHUNK

    def stage1(s1, carry):
        rows = pl.ds(s1, n2, stride=n1)
        x = jnp.concatenate([a_ref[0, rows, :], b_ref[0, rows, :]], axis=0).astype(BF16)
        z = jnp.dot(m1_ref[s1], x, preferred_element_type=F32)
        zs_ref[pl.ds(pl.multiple_of(s1 * 2 * n2, 2 * n2), 2 * n2), :] = z
        return carry

    lax.fori_loop(0, n1, stage1, 0, unroll=4)

    def stage2(kk, carry):
        k2 = 2 * kk
        parts = []
        for d in range(2):
            zr = zs_ref[pl.ds(k2 + d, n1, stride=2 * n2), :]
            zi = zs_ref[pl.ds(n2 + k2 + d, n1, stride=2 * n2), :]
            parts.append(jnp.concatenate([zr, zi], axis=0))
        z = jnp.concatenate(parts, axis=1).astype(BF16)
        y = jnp.dot(w2_ref[...], z, preferred_element_type=F32)
        o_ref[0, pl.ds(k2, n1, stride=n2), :] = y[:, :c]
        o_ref[0, pl.ds(k2 + 1, n1, stride=n2), :] = y[:, c:]
        return carry

    lax.fori_loop(0, n2 // 2, stage2, 0, unroll=4)


def _seq_fft(ab, m1, w2):
    b, s, two_d = ab.shape
    d = two_d // 2
    c = FFT_CHUNK
    est = 2 * (s * 2 * c * 4 + m1.size * 2 + s * c * 4) + FFT_N1 * 2 * FFT_N2 * c * 4 + (4 << 20)
    return pl.pallas_call(
        _fft_kernel,
        grid=(b, d // c),
        in_specs=[pl.BlockSpec((1, s, c), lambda i, j: (i, 0, 2 * j)),
                  pl.BlockSpec((1, s, c), lambda i, j: (i, 0, 2 * j + 1)),
                  pl.BlockSpec(m1.shape, lambda i, j: (0, 0, 0)),
                  pl.BlockSpec(w2.shape, lambda i, j: (0, 0))],
        out_specs=pl.BlockSpec((1, s, c), lambda i, j: (i, 0, j)),
        out_shape=jax.ShapeDtypeStruct((b, s, d), F32),
        scratch_shapes=[pltpu.VMEM((FFT_N1 * 2 * FFT_N2, c), F32)],
        compiler_params=pltpu.CompilerParams(
            dimension_semantics=("parallel", "parallel"), vmem_limit_bytes=_vmem_limit(est)),
        name="seq_fft",
    )(ab, ab, m1, w2)


def _fourier_mixer_pre_out(x, w_in):
    b, s, d = x.shape
    g, gd, c = N_FOURIER_GROUPS, FOURIER_GROUP_DIM, FFT_CHUNK
    halves = gd // c
    cs = jnp.asarray(_fourier_channel_tables())
    w_ab = pl.pallas_call(
        functools.partial(_mm_kernel, precision=lax.Precision.HIGHEST),
        grid=(g, halves),
        in_specs=[pl.BlockSpec((d, gd), lambda i, h: (0, i)),
                  pl.BlockSpec((None, gd, 2 * c), lambda i, h: (h, 0, 0))],
        out_specs=pl.BlockSpec((d, 2 * c), lambda i, h: (0, i * halves + h)),
        out_shape=jax.ShapeDtypeStruct((d, 2 * d), BF16),
        name="fourier_weight_fold",
    )(w_in, cs)
    ab = _matmul(x.reshape(b * s, d), w_ab, F32, MM_BM, 1024).reshape(b, s, 2 * d)
    m1, w2 = _fourier_seq_tables(s)
    return _seq_fft(ab, jnp.asarray(m1, dtype=BF16), jnp.asarray(w2, dtype=BF16))


def _layer_norm_rows(z, g, b):
    mu = jnp.mean(z, axis=-1, keepdims=True)
    zc = z - mu
    var = jnp.mean(zc * zc, axis=-1, keepdims=True)
    return zc * lax.rsqrt(var + LN_EPS) * g + b


def _mm_ln_kernel(a_ref, w_ref, x_ref, g_ref, b_ref, o_ref):
    y = jnp.dot(a_ref[...].astype(BF16), w_ref[...], preferred_element_type=F32)
    z = DEEPNORM_ALPHA * x_ref[...] + y
    o_ref[...] = _layer_norm_rows(z, g_ref[...], b_ref[...])


def _proj_residual_ln(a, w_bf16, x, ln_g, ln_b):
    t, k = a.shape
    d = w_bf16.shape[1]
    bm = LN_BM
    est = 2 * (bm * k * a.dtype.itemsize + k * d * 2 + 2 * bm * d * 4) + 4 * bm * d * 4
    return pl.pallas_call(
        _mm_ln_kernel,
        grid=(t // bm,),
        in_specs=[pl.BlockSpec((bm, k), lambda i: (i, 0)),
                  pl.BlockSpec((k, d), lambda i: (0, 0)),
                  pl.BlockSpec((bm, d), lambda i: (i, 0)),
                  pl.BlockSpec((1, d), lambda i: (0, 0)),
                  pl.BlockSpec((1, d), lambda i: (0, 0))],
        out_specs=pl.BlockSpec((bm, d), lambda i: (i, 0)),
        out_shape=jax.ShapeDtypeStruct((t, d), F32),
        compiler_params=pltpu.CompilerParams(
            dimension_semantics=("parallel",), vmem_limit_bytes=_vmem_limit(est)),
        name="proj_residual_ln",
    )(a, w_bf16, x, ln_g.reshape(1, d), ln_b.reshape(1, d))


def _top2_rows(v, iota):
    n_rows = v.shape[0]
    m1 = jnp.max(v, axis=0, keepdims=True)
    i1 = jnp.min(jnp.where(v == m1, iota, n_rows), axis=0, keepdims=True)
    v2 = jnp.where(iota == i1, -jnp.inf, v)
    m2 = jnp.max(v2, axis=0, keepdims=True)
    i2 = jnp.min(jnp.where(v2 == m2, iota, n_rows), axis=0, keepdims=True)
    return m1, i1, m2, i2


def _router_kernel(x_ref, wt_ref, b_ref, eidx_ref, gate_ref):
    x = x_ref[...]
    w = wt_ref[...]
    xh = x.astype(BF16)
    xl = (x - xh.astype(F32)).astype(BF16)
    wh = w.astype(BF16)
    wl = (w - wh.astype(F32)).astype(BF16)
    nt = (((1,), (1,)), ((), ()))
    logits = (lax.dot_general(wh, xh, nt, preferred_element_type=F32)
              + lax.dot_general(wh, xl, nt, preferred_element_type=F32)
              + lax.dot_general(wl, xh, nt, preferred_element_type=F32))
    scores = 1.0 / (1.0 + jnp.exp(-logits))
    sel = scores + b_ref[...]
    epg = EXPERTS_PER_GROUP
    bt = x.shape[0]
    iota = lax.broadcasted_iota(I32, (epg, bt), 0)

    best = None
    for g in range(N_GROUPS):
        m1, _, m2, _ = _top2_rows(sel[g * epg:(g + 1) * epg], iota)
        gs = m1 + m2
        if best is None:
            best, gidx = gs, jnp.zeros((1, bt), I32)
        else:
            better = gs > best
            gidx = jnp.where(better, g, gidx)
            best = jnp.where(better, gs, best)

    sel_in = sel[0:epg]
    sc_in = scores[0:epg]
    for g in range(1, N_GROUPS):
        pick = gidx == g
        sel_in = jnp.where(pick, sel[g * epg:(g + 1) * epg], sel_in)
        sc_in = jnp.where(pick, scores[g * epg:(g + 1) * epg], sc_in)
    _, i1, _, i2 = _top2_rows(sel_in, iota)
    g1 = jnp.sum(jnp.where(iota == i1, sc_in, 0.0), axis=0, keepdims=True)
    g2 = jnp.sum(jnp.where(iota == i2, sc_in, 0.0), axis=0, keepdims=True)
    denom = g1 + g2
    eidx_ref[0:1, :] = gidx * epg + i1
    eidx_ref[1:2, :] = gidx * epg + i2
    gate_ref[0:1, :] = g1 / denom
    gate_ref[1:2, :] = g2 / denom


def _router(x, w_router_t, b_router):
    t, d = x.shape
    e = w_router_t.shape[0]
    bt = ROUTER_BT
    return pl.pallas_call(
        _router_kernel,
        grid=(t // bt,),
        in_specs=[pl.BlockSpec((bt, d), lambda i: (i, 0)),
                  pl.BlockSpec((e, d), lambda i: (0, 0)),
                  pl.BlockSpec((e, 1), lambda i: (0, 0))],
        out_specs=[pl.BlockSpec((2, bt), lambda i: (0, i)),
                   pl.BlockSpec((2, bt), lambda i: (0, i))],
        out_shape=[jax.ShapeDtypeStruct((2, t), I32), jax.ShapeDtypeStruct((2, t), F32)],
        compiler_params=pltpu.CompilerParams(dimension_semantics=("parallel",)),
        name="router",
    )(x, w_router_t, b_router.reshape(e, 1))


RANK_SUB = 256
RANK_NSUB = 8


def _rank_kernel(keys_ref, tri_ref, rank_ref, counts_ref, carry_ref, *, n_classes):
    nsub, sub = RANK_NSUB, RANK_SUB

    @pl.when(pl.program_id(0) == 0)
    def _():
        carry_ref[...] = jnp.zeros_like(carry_ref)

    cls = lax.broadcasted_iota(I32, (nsub, n_classes, sub), 1)
    onehot = cls == keys_ref[...]
    oh = jnp.where(onehot, 1.0, 0.0).reshape(nsub * n_classes, sub).astype(BF16)
    pref = jnp.dot(oh, tri_ref[...], preferred_element_type=F32).reshape(nsub, n_classes, sub)
    carry = carry_ref[...]
    for j in range(nsub):
        before = pref[j] + (carry - 1.0)
        rank_ref[j] = jnp.sum(jnp.where(onehot[j], before, 0.0), axis=0, keepdims=True).astype(I32)
        carry = carry + pref[j][:, sub - 1:sub]
    carry_ref[...] = carry
    counts_ref[...] = carry.astype(I32)


def _rank_within_class(keys, n_classes):
    n = keys.shape[0]
    nsub, sub = RANK_NSUB, RANK_SUB
    tri = jnp.asarray(np.triu(np.ones((sub, sub), np.float32)), dtype=BF16)
    rank, counts = pl.pallas_call(
        functools.partial(_rank_kernel, n_classes=n_classes),
        grid=(n // (nsub * sub),),
        in_specs=[pl.BlockSpec((nsub, 1, sub), lambda i: (i, 0, 0)),
                  pl.BlockSpec((sub, sub), lambda i: (0, 0))],
        out_specs=[pl.BlockSpec((nsub, 1, sub), lambda i: (i, 0, 0)),
                   pl.BlockSpec((n_classes, 1), lambda i: (0, 0))],
        out_shape=[jax.ShapeDtypeStruct((n // sub, 1, sub), I32),
                   jax.ShapeDtypeStruct((n_classes, 1), I32)],
        scratch_shapes=[pltpu.VMEM((n_classes, 1), F32)],
        compiler_params=pltpu.CompilerParams(dimension_semantics=("arbitrary",)),
        name="rank_within_class",
    )(keys.reshape(n // sub, 1, sub), tri)
    return rank.reshape(n), counts.reshape(n_classes)


def _fill_slots_kernel(dest_ref, cnt_ref, pstart_ref, pend_ref, slot_ref, *, n_tok):
    n_items = dest_ref.shape[0]
    n_classes = cnt_ref.shape[0]
    n_slots = slot_ref.shape[0]

    def zero(p, carry):
        slot_ref[p] = 0
        return carry

    def pad_class(k, carry):
        lax.fori_loop(pstart_ref[k] + cnt_ref[k], pend_ref[k], zero, 0)
        return carry

    lax.fori_loop(0, n_classes, pad_class, 0)
    lax.fori_loop(pend_ref[n_classes - 1], n_slots, zero, 0)

    def place(i, carry):
        slot_ref[dest_ref[i]] = lax.rem(i, n_tok)
        return carry

    lax.fori_loop(0, n_items, place, 0, unroll=8)


def _fill_slots(dest, counts, pad_start, pad_end, n_slots, n_tok):
    smem = pl.BlockSpec(memory_space=pltpu.SMEM)
    return pl.pallas_call(
        functools.partial(_fill_slots_kernel, n_tok=n_tok),
        in_specs=[smem, smem, smem, smem],
        out_specs=smem,
        out_shape=jax.ShapeDtypeStruct((n_slots,), I32),
        name="fill_slots",
    )(dest, counts, pad_start.astype(I32), pad_end.astype(I32))


def _route_slots(eidx, bm):
    t = eidx.shape[1]
    a = 2 * t
    keys = eidx.reshape(a)
    rank, counts = _rank_within_class(keys, N_EXPERTS)
    padded = (counts + bm - 1) // bm * bm
    pad_end = jnp.cumsum(padded)
    pad_start = pad_end - padded
    dest = jnp.take(pad_start, keys) + rank
    p = a + N_EXPERTS * bm
    slot_tok = _fill_slots(dest, counts, pad_start, pad_end, p, t)
    nb = p // bm
    block_expert = jnp.minimum(
        jnp.searchsorted(pad_end, jnp.arange(nb, dtype=I32) * bm, side="right"), N_EXPERTS - 1).astype(I32)
    n_active = (pad_end[-1] // bm).astype(I32).reshape(1)
    return dest, slot_tok, block_expert, n_active


def _expert_kernel(be_ref, tok_ref, nact_ref, x_hbm, wg_ref, wu_ref, wd_ref, o_ref,
                   xbuf, wg_bf, wu_bf, wd_bf, sem):
    bm = EXPERT_BM
    i = pl.program_id(0)
    nact = nact_ref[0]
    slot = i % 2

    def issue(blk, dst_slot):
        base = blk * bm

        def body(r, carry):
            tok = tok_ref[base + r]
            pltpu.make_async_copy(x_hbm.at[pl.ds(tok, 1), :], xbuf.at[dst_slot, pl.ds(r, 1), :],
                                  sem.at[dst_slot]).start()
            return carry

        lax.fori_loop(0, bm, body, 0, unroll=8)

    @pl.when(jnp.logical_and(i == 0, nact > 0))
    def _():
        issue(0, 0)

    @pl.when(i + 1 < nact)
    def _():
        issue(i + 1, 1 - slot)

    changed = jnp.logical_or(i == 0, be_ref[i] != be_ref[jnp.maximum(i - 1, 0)])

    @pl.when(jnp.logical_and(changed, i < nact))
    def _():
        wg_bf[...] = wg_ref[0].astype(BF16)
        wu_bf[...] = wu_ref[0].astype(BF16)
        wd_bf[...] = wd_ref[0].astype(BF16)

    @pl.when(i < nact)
    def _():
        pltpu.make_async_copy(x_hbm.at[pl.ds(0, bm), :], xbuf.at[slot], sem.at[slot]).wait()
        x = xbuf[slot].astype(BF16)
        g = jnp.dot(x, wg_bf[...], preferred_element_type=F32)
        u = jnp.dot(x, wu_bf[...], preferred_element_type=F32)
        h = (g / (1.0 + jnp.exp(-g))) * u
        o_ref[...] = jnp.dot(h.astype(BF16), wd_bf[...], preferred_element_type=F32)

    @pl.when(i >= nact)
    def _():
        o_ref[...] = jnp.zeros_like(o_ref)


def _experts(x, slot_tok, block_expert, n_active, w_gate, w_up, w_down, layer):
    t, d = x.shape
    _, e, _, f = w_gate.shape
    bm = EXPERT_BM
    p = slot_tok.shape[0]
    nb = p // bm
    est = 2 * 3 * d * f * 4 + 3 * d * f * 2 + 2 * bm * d * 4 + 2 * bm * d * 4 + 6 * bm * d * 4
    grid_spec = pltpu.PrefetchScalarGridSpec(
        num_scalar_prefetch=3,
        grid=(nb,),
        in_specs=[pl.BlockSpec(memory_space=pl.ANY),
                  pl.BlockSpec((None, 1, d, f), lambda i, be, tok, na: (layer, be[i], 0, 0)),
                  pl.BlockSpec((None, 1, d, f), lambda i, be, tok, na: (layer, be[i], 0, 0)),
                  pl.BlockSpec((None, 1, f, d), lambda i, be, tok, na: (layer, be[i], 0, 0))],
        out_specs=pl.BlockSpec((bm, d), lambda i, be, tok, na: (i, 0)),
        scratch_shapes=[pltpu.VMEM((2, bm, d), F32),
                        pltpu.VMEM((d, f), BF16), pltpu.VMEM((d, f), BF16), pltpu.VMEM((f, d), BF16),
                        pltpu.SemaphoreType.DMA((2,))],
    )
    return pl.pallas_call(
        _expert_kernel,
        grid_spec=grid_spec,
        out_shape=jax.ShapeDtypeStruct((p, d), F32),
        compiler_params=pltpu.CompilerParams(
            dimension_semantics=("arbitrary",), vmem_limit_bytes=_vmem_limit(est)),
        name="experts",
    )(block_expert, slot_tok, n_active, x, w_gate, w_up, w_down)


def _combine_kernel(dest_ref, x_ref, gate_ref, g_ref, b_ref, ys_hbm, o_ref, ybuf, sem):
    bt = COMBINE_BT
    i = pl.program_id(0)
    n = pl.num_programs(0)
    slot = i % 2

    def issue(blk, dst_slot):
        base = blk * bt

        def body(r, carry):
            for k in range(2):
                d = dest_ref[k * (n * bt) + base + r]
                pltpu.make_async_copy(ys_hbm.at[pl.ds(d, 1), :], ybuf.at[dst_slot, pl.ds(k * bt + r, 1), :],
                                      sem.at[dst_slot]).start()
            return carry

        lax.fori_loop(0, bt, body, 0, unroll=4)

    @pl.when(i == 0)
    def _():
        issue(0, 0)

    @pl.when(i + 1 < n)
    def _():
        issue(i + 1, 1 - slot)

    pltpu.make_async_copy(ys_hbm.at[pl.ds(0, 2 * bt), :], ybuf.at[slot], sem.at[slot]).wait()
    gate = gate_ref[...]
    y = ybuf[slot, 0:bt, :] * gate[:, 0:1] + ybuf[slot, bt:2 * bt, :] * gate[:, 1:2]
    z = DEEPNORM_ALPHA * x_ref[...] + y
    o_ref[...] = _layer_norm_rows(z, g_ref[...], b_ref[...])


def _combine_ln(x, ys, dest, gate_t, ln_g, ln_b):
    t, d = x.shape
    bt = COMBINE_BT
    grid_spec = pltpu.PrefetchScalarGridSpec(
        num_scalar_prefetch=1,
        grid=(t // bt,),
        in_specs=[pl.BlockSpec((bt, d), lambda i, dst: (i, 0)),
                  pl.BlockSpec((bt, 2), lambda i, dst: (i, 0)),
                  pl.BlockSpec((1, d), lambda i, dst: (0, 0)),
                  pl.BlockSpec((1, d), lambda i, dst: (0, 0)),
                  pl.BlockSpec(memory_space=pl.ANY)],
        out_specs=pl.BlockSpec((bt, d), lambda i, dst: (i, 0)),
        scratch_shapes=[pltpu.VMEM((2, 2 * bt, d), F32), pltpu.SemaphoreType.DMA((2,))],
    )
    return pl.pallas_call(
        _combine_kernel,
        grid_spec=grid_spec,
        out_shape=jax.ShapeDtypeStruct((t, d), F32),
        compiler_params=pltpu.CompilerParams(dimension_semantics=("arbitrary",)),
        name="combine_ln",
    )(dest, x, gate_t, ln_g.reshape(1, d), ln_b.reshape(1, d), ys)


def _moe_residual_ln(x, w_router_t, b_router, w_gate, w_up, w_down, layer, ln_g, ln_b):
    eidx, gate = _router(x, w_router_t, b_router)
    dest, slot_tok, block_expert, n_active = _route_slots(eidx, EXPERT_BM)
    ys = _experts(x, slot_tok, block_expert, n_active, w_gate, w_up, w_down, layer)
    return _combine_ln(x, ys, dest, gate.T, ln_g, ln_b)


def _na_bias_tables(rpb, rows):
    w = GRID_W
    kc = min(WIN_COLS, w)
    kr = min(WIN_ROWS, rows)
    cols = np.arange(w)
    col_start = np.clip(cols - kc // 2, 0, w - kc)
    col_mask = (cols[None, :] >= col_start[:, None]) & (cols[None, :] < col_start[:, None] + kc)
    col_idx = np.clip(cols[None, :] - cols[:, None] + WIN_COLS - 1, 0, 2 * WIN_COLS - 2)
    tb = rpb.astype(F32)[:, :, col_idx]
    tb = jnp.where(col_mask[None, None], tb, NEG_BIAS)
    n_ri = 2 * WIN_ROWS - 1
    tb = jnp.concatenate([tb, jnp.full((N_HEADS, 1, w, w), NEG_BIAS, F32)], axis=1)
    n_blocks = rows // NA_RQ
    ri = np.full((3, NA_RQ, NA_KROWS), n_ri, np.int32)
    for ty, blk in enumerate((0, 1, n_blocks - 1)):
        ks = int(np.clip(blk - 1, 0, n_blocks - 3)) * NA_RQ
        for rq in range(NA_RQ):
            r = blk * NA_RQ + rq
            rs = int(np.clip(r - kr // 2, 0, rows - kr))
            for j in range(NA_KROWS):
                key_row = ks + j
                if rs <= key_row < rs + kr:
                    ri[ty, rq, j] = key_row - r + WIN_ROWS - 1
    bt = tb[:, ri]
    return bt.transpose(1, 0, 2, 4, 3, 5).reshape(3, N_HEADS, NA_RQ * w, NA_KROWS * w)


def _na_kernel(q_ref, k0_ref, k1_ref, k2_ref, v0_ref, v1_ref, v2_ref, bias_ref, o_ref):
    dh = HEAD_DIM
    nt = (((1,), (1,)), ((), ()))
    scale = HEAD_DIM ** -0.5
    outs = []
    for h in range(NA_HEADS_PER_STEP):
        cols = slice(h * dh, (h + 1) * dh)
        q = q_ref[0, :, cols] * scale
        k = jnp.concatenate([k0_ref[0, :, cols], k1_ref[0, :, cols], k2_ref[0, :, cols]], axis=0)
        v = jnp.concatenate([v0_ref[0, :, cols], v1_ref[0, :, cols], v2_ref[0, :, cols]], axis=0)
        s = lax.dot_general(q, k, nt, preferred_element_type=F32) + bias_ref[0, h]
        m = jnp.max(s, axis=-1, keepdims=True)
        p = jnp.exp(s - m)
        l = jnp.sum(p, axis=-1, keepdims=True)
        o = jnp.dot(p.astype(BF16), v, preferred_element_type=F32)
        outs.append(o / l)
    o_ref[0] = jnp.concatenate(outs, axis=-1).astype(o_ref.dtype)


def _neighbourhood_attention(qkv, bias, rows):
    b, s, _ = qkv.shape
    d = D_MODEL
    w = GRID_W
    tq = NA_RQ * w
    n_blocks = rows // NA_RQ
    hw = NA_HEADS_PER_STEP * HEAD_DIM
    n_hh = d // hw
    assert NA_KROWS * w == 3 * tq and n_blocks >= 3

    def kv_map(part, j):
        def index_map(hh, bi, i):
            return (bi, jnp.clip(i - 1, 0, n_blocks - 3) + j, part * n_hh + hh)
        return index_map

    def bias_map(hh, bi, i):
        ty = jnp.where(i == 0, 0, jnp.where(i == n_blocks - 1, 2, 1))
        return (ty, hh, 0, 0)

    blk = (1, tq, hw)
    est = 2 * (7 * tq * hw * 2 + NA_HEADS_PER_STEP * tq * 3 * tq * 4 + tq * hw * 2) + 8 * tq * 3 * tq * 4
    return pl.pallas_call(
        _na_kernel,
        grid=(n_hh, b, n_blocks),
        in_specs=[pl.BlockSpec(blk, lambda hh, bi, i: (bi, i, hh))]
                 + [pl.BlockSpec(blk, kv_map(1, j)) for j in range(3)]
                 + [pl.BlockSpec(blk, kv_map(2, j)) for j in range(3)]
                 + [pl.BlockSpec((1, NA_HEADS_PER_STEP, tq, 3 * tq), bias_map)],
        out_specs=pl.BlockSpec(blk, lambda hh, bi, i: (bi, i, hh)),
        out_shape=jax.ShapeDtypeStruct((b, s, d), BF16),
        compiler_params=pltpu.CompilerParams(
            dimension_semantics=("parallel", "parallel", "parallel"), vmem_limit_bytes=_vmem_limit(est)),
        name="neighbourhood_attention",
    )(qkv, qkv, qkv, qkv, qkv, qkv, qkv, bias)


def kernel(x, fourier_w_in, fourier_w_out, na_w_qkv, na_rpb, na_w_out, router_w, router_b,
           expert_w_gate, expert_w_up, expert_w_down, ln_g, ln_b):
    b, s, d = x.shape
    t = b * s
    rows = s // GRID_W
    w_router_t = router_w.T
    xt = x.reshape(t, d)

    f = _fourier_mixer_pre_out(x, fourier_w_in[0])
    xt = _proj_residual_ln(f.reshape(t, d), fourier_w_out[0].astype(BF16), xt, ln_g[0, 0], ln_b[0, 0])
    xt = _moe_residual_ln(xt, w_router_t, router_b, expert_w_gate, expert_w_up, expert_w_down, 0,
                          ln_g[0, 1], ln_b[0, 1])

    qkv = _matmul(xt, na_w_qkv[0].astype(BF16), BF16, MM_BM, 1024).reshape(b, s, 3 * d)
    bias = _na_bias_tables(na_rpb[0], rows)
    o = _neighbourhood_attention(qkv, bias, rows)
    xt = _proj_residual_ln(o.reshape(t, d), na_w_out[0].astype(BF16), xt, ln_g[1, 0], ln_b[1, 0])
    xt = _moe_residual_ln(xt, w_router_t, router_b, expert_w_gate, expert_w_up, expert_w_down, 1,
                          ln_g[1, 1], ln_b[1, 1])
    return xt.reshape(b, s, d)
```

```python
import functools

import jax
import jax.numpy as jnp
import numpy as np
from jax import lax
from jax.experimental import pallas as pl
from jax.experimental.pallas import tpu as pltpu

F32 = jnp.float32
BF16 = jnp.bfloat16
I32 = jnp.int32

D_MODEL = 1024
GRID_W = 64
N_FOURIER_GROUPS = 4
FOURIER_GROUP_DIM = D_MODEL // N_FOURIER_GROUPS
N_HEADS = 16
HEAD_DIM = D_MODEL // N_HEADS
WIN_ROWS = 8
WIN_COLS = 16
N_EXPERTS = 32
N_GROUPS = 4
EXPERTS_PER_GROUP = N_EXPERTS // N_GROUPS
D_EXPERT = D_MODEL // 2
DEPTH = 2
DEEPNORM_ALPHA = (2 * DEPTH) ** 0.25
LN_EPS = 1e-5

V7X_VMEM_BYTES = 64 * 1024 * 1024
LANES = 128

FFT_N1 = 64
FFT_N2 = 128
FFT_CHUNK = 128

MM_BM = 1024
LN_BM = 512
ROUTER_BT = 1024
EXPERT_BM = 256
COMBINE_BT = 256
NA_RQ = 4
NA_KROWS = NA_RQ + WIN_ROWS
NA_HEADS_PER_STEP = 8
NEG_BIAS = -1e30


def _vmem_limit(nbytes):
    return int(min(max(nbytes, 32 * 1024 * 1024), V7X_VMEM_BYTES - 8 * 1024 * 1024))


def _mm_kernel(a_ref, b_ref, o_ref, *, precision):
    if precision is None:
        a = a_ref[...].astype(BF16)
        b = b_ref[...].astype(BF16)
        acc = jnp.dot(a, b, preferred_element_type=F32)
    else:
        acc = jnp.dot(a_ref[...], b_ref[...], preferred_element_type=F32, precision=precision)
    o_ref[...] = acc.astype(o_ref.dtype)


def _matmul(a, b, out_dtype, bm, bn, precision=None):
    m, k = a.shape
    _, n = b.shape
    est = 2 * (bm * k * a.dtype.itemsize + k * bn * b.dtype.itemsize + bm * bn * 4) + 3 * bm * bn * 4
    return pl.pallas_call(
        functools.partial(_mm_kernel, precision=precision),
        grid=(m // bm, n // bn),
        in_specs=[pl.BlockSpec((bm, k), lambda i, j: (i, 0)),
                  pl.BlockSpec((k, bn), lambda i, j: (0, j))],
        out_specs=pl.BlockSpec((bm, bn), lambda i, j: (i, j)),
        out_shape=jax.ShapeDtypeStruct((m, n), out_dtype),
        compiler_params=pltpu.CompilerParams(
            dimension_semantics=("parallel", "parallel"), vmem_limit_bytes=_vmem_limit(est)),
        name="matmul",
    )(a, b)


def _fourier_channel_tables():
    n = FOURIER_GROUP_DIM
    c = np.arange(n)
    ang = 2.0 * np.pi * ((c[:, None] * c[None, :]) % n) / n
    cos, sin = np.cos(ang) / np.sqrt(n), np.sin(ang) / np.sqrt(n)
    half = FFT_CHUNK
    tabs = [np.concatenate([cos[:, h * half:(h + 1) * half], sin[:, h * half:(h + 1) * half]], axis=1)
            for h in range(n // half)]
    return np.stack(tabs).astype(np.float32)


def _fourier_seq_tables(seq):
    n1, n2 = FFT_N1, FFT_N2
    assert n1 * n2 == seq
    k2 = np.arange(n2)
    s2 = np.arange(n2)
    m1 = np.empty((n1, 2 * n2, 2 * n2), np.float32)
    for s1 in range(n1):
        ang = 2.0 * np.pi * ((k2[:, None] * (s1 + n1 * s2[None, :])) % seq) / seq
        mr, mi = np.cos(ang) / np.sqrt(n2), np.sin(ang) / np.sqrt(n2)
        m1[s1] = np.block([[mr, -mi], [mi, mr]])
    k1 = np.arange(n1)
    ang = 2.0 * np.pi * ((k1[:, None] * k1[None, :]) % n1) / n1
    w2 = np.concatenate([np.cos(ang), -np.sin(ang)], axis=1) / np.sqrt(n1)
    return m1, w2.astype(np.float32)


def _fft_kernel(a_ref, b_ref, m1_ref, w2_ref, o_ref, zs_ref):
    n1, n2, c = FFT_N1, FFT_N2, FFT_CHUNK

    def stage1(s1, carry):
        rows = pl.ds(s1, n2, stride=n1)
        x = jnp.concatenate([a_ref[0, rows, :], b_ref[0, rows, :]], axis=0).astype(BF16)
        z = jnp.dot(m1_ref[s1], x, preferred_element_type=F32)
        zs_ref[pl.ds(pl.multiple_of(s1 * 2 * n2, 2 * n2), 2 * n2), :] = z
        return carry

    lax.fori_loop(0, n1, stage1, 0, unroll=4)

    def stage2(kk, carry):
        k2 = 2 * kk
        parts = []
        for d in range(2):
            zr = zs_ref[pl.ds(k2 + d, n1, stride=2 * n2), :]
            zi = zs_ref[pl.ds(n2 + k2 + d, n1, stride=2 * n2), :]
            parts.append(jnp.concatenate([zr, zi], axis=0))
        z = jnp.concatenate(parts, axis=1).astype(BF16)
        y = jnp.dot(w2_ref[...], z, preferred_element_type=F32)
        o_ref[0, pl.ds(k2, n1, stride=n2), :] = y[:, :c]
        o_ref[0, pl.ds(k2 + 1, n1, stride=n2), :] = y[:, c:]
        return carry

    lax.fori_loop(0, n2 // 2, stage2, 0, unroll=4)


def _seq_fft(ab, m1, w2):
    b, s, two_d = ab.shape
    d = two_d // 2
    c = FFT_CHUNK
    est = 2 * (s * 2 * c * 4 + m1.size * 2 + s * c * 4) + FFT_N1 * 2 * FFT_N2 * c * 4 + (4 << 20)
    return pl.pallas_call(
        _fft_kernel,
        grid=(b, d // c),
        in_specs=[pl.BlockSpec((1, s, c), lambda i, j: (i, 0, 2 * j)),
                  pl.BlockSpec((1, s, c), lambda i, j: (i, 0, 2 * j + 1)),
                  pl.BlockSpec(m1.shape, lambda i, j: (0, 0, 0)),
                  pl.BlockSpec(w2.shape, lambda i, j: (0, 0))],
        out_specs=pl.BlockSpec((1, s, c), lambda i, j: (i, 0, j)),
        out_shape=jax.ShapeDtypeStruct((b, s, d), F32),
        scratch_shapes=[pltpu.VMEM((FFT_N1 * 2 * FFT_N2, c), F32)],
        compiler_params=pltpu.CompilerParams(
            dimension_semantics=("parallel", "parallel"), vmem_limit_bytes=_vmem_limit(est)),
        name="seq_fft",
    )(ab, ab, m1, w2)


def _fourier_mixer_pre_out(x, w_in):
    b, s, d = x.shape
    g, gd, c = N_FOURIER_GROUPS, FOURIER_GROUP_DIM, FFT_CHUNK
    halves = gd // c
    cs = jnp.asarray(_fourier_channel_tables())
    w_ab = pl.pallas_call(
        functools.partial(_mm_kernel, precision=lax.Precision.HIGHEST),
        grid=(g, halves),
        in_specs=[pl.BlockSpec((d, gd), lambda i, h: (0, i)),
                  pl.BlockSpec((None, gd, 2 * c), lambda i, h: (h, 0, 0))],
        out_specs=pl.BlockSpec((d, 2 * c), lambda i, h: (0, i * halves + h)),
        out_shape=jax.ShapeDtypeStruct((d, 2 * d), BF16),
        name="fourier_weight_fold",
    )(w_in, cs)
    ab = _matmul(x.reshape(b * s, d), w_ab, F32, MM_BM, 1024).reshape(b, s, 2 * d)
    m1, w2 = _fourier_seq_tables(s)
    return _seq_fft(ab, jnp.asarray(m1, dtype=BF16), jnp.asarray(w2, dtype=BF16))


def _layer_norm_rows(z, g, b):
    mu = jnp.mean(z, axis=-1, keepdims=True)
    zc = z - mu
    var = jnp.mean(zc * zc, axis=-1, keepdims=True)
    return zc * lax.rsqrt(var + LN_EPS) * g + b


def _mm_ln_kernel(a_ref, w_ref, x_ref, g_ref, b_ref, o_ref):
    y = jnp.dot(a_ref[...].astype(BF16), w_ref[...], preferred_element_type=F32)
    z = DEEPNORM_ALPHA * x_ref[...] + y
    o_ref[...] = _layer_norm_rows(z, g_ref[...], b_ref[...])


def _proj_residual_ln(a, w_bf16, x, ln_g, ln_b):
    t, k = a.shape
    d = w_bf16.shape[1]
    bm = LN_BM
    est = 2 * (bm * k * a.dtype.itemsize + k * d * 2 + 2 * bm * d * 4) + 4 * bm * d * 4
    return pl.pallas_call(
        _mm_ln_kernel,
        grid=(t // bm,),
        in_specs=[pl.BlockSpec((bm, k), lambda i: (i, 0)),
                  pl.BlockSpec((k, d), lambda i: (0, 0)),
                  pl.BlockSpec((bm, d), lambda i: (i, 0)),
                  pl.BlockSpec((1, d), lambda i: (0, 0)),
                  pl.BlockSpec((1, d), lambda i: (0, 0))],
        out_specs=pl.BlockSpec((bm, d), lambda i: (i, 0)),
        out_shape=jax.ShapeDtypeStruct((t, d), F32),
        compiler_params=pltpu.CompilerParams(
            dimension_semantics=("parallel",), vmem_limit_bytes=_vmem_limit(est)),
        name="proj_residual_ln",
    )(a, w_bf16, x, ln_g.reshape(1, d), ln_b.reshape(1, d))


def _top2_rows(v, iota):
    n_rows = v.shape[0]
    m1 = jnp.max(v, axis=0, keepdims=True)
    i1 = jnp.min(jnp.where(v == m1, iota, n_rows), axis=0, keepdims=True)
    v2 = jnp.where(iota == i1, -jnp.inf, v)
    m2 = jnp.max(v2, axis=0, keepdims=True)
    i2 = jnp.min(jnp.where(v2 == m2, iota, n_rows), axis=0, keepdims=True)
    return m1, i1, m2, i2


def _router_kernel(x_ref, wt_ref, b_ref, eidx_ref, gate_ref):
    x = x_ref[...]
    w = wt_ref[...]
    xh = x.astype(BF16)
    xl = (x - xh.astype(F32)).astype(BF16)
    wh = w.astype(BF16)
    wl = (w - wh.astype(F32)).astype(BF16)
    nt = (((1,), (1,)), ((), ()))
    logits = (lax.dot_general(wh, xh, nt, preferred_element_type=F32)
              + lax.dot_general(wh, xl, nt, preferred_element_type=F32)
              + lax.dot_general(wl, xh, nt, preferred_element_type=F32))
    scores = 1.0 / (1.0 + jnp.exp(-logits))
    sel = scores + b_ref[...]
    epg = EXPERTS_PER_GROUP
    bt = x.shape[0]
    iota = lax.broadcasted_iota(I32, (epg, bt), 0)

    best = None
    for g in range(N_GROUPS):
        m1, _, m2, _ = _top2_rows(sel[g * epg:(g + 1) * epg], iota)
        gs = m1 + m2
        if best is None:
            best, gidx = gs, jnp.zeros((1, bt), I32)
        else:
            better = gs > best
            gidx = jnp.where(better, g, gidx)
            best = jnp.where(better, gs, best)

    sel_in = sel[0:epg]
    sc_in = scores[0:epg]
    for g in range(1, N_GROUPS):
        pick = gidx == g
        sel_in = jnp.where(pick, sel[g * epg:(g + 1) * epg], sel_in)
        sc_in = jnp.where(pick, scores[g * epg:(g + 1) * epg], sc_in)
    _, i1, _, i2 = _top2_rows(sel_in, iota)
    g1 = jnp.sum(jnp.where(iota == i1, sc_in, 0.0), axis=0, keepdims=True)
    g2 = jnp.sum(jnp.where(iota == i2, sc_in, 0.0), axis=0, keepdims=True)
    denom = g1 + g2
    eidx_ref[0:1, :] = gidx * epg + i1
    eidx_ref[1:2, :] = gidx * epg + i2
    gate_ref[0:1, :] = g1 / denom
    gate_ref[1:2, :] = g2 / denom


def _router(x, w_router_t, b_router):
    t, d = x.shape
    e = w_router_t.shape[0]
    bt = ROUTER_BT
    return pl.pallas_call(
        _router_kernel,
        grid=(t // bt,),
        in_specs=[pl.BlockSpec((bt, d), lambda i: (i, 0)),
                  pl.BlockSpec((e, d), lambda i: (0, 0)),
                  pl.BlockSpec((e, 1), lambda i: (0, 0))],
        out_specs=[pl.BlockSpec((2, bt), lambda i: (0, i)),
                   pl.BlockSpec((2, bt), lambda i: (0, i))],
        out_shape=[jax.ShapeDtypeStruct((2, t), I32), jax.ShapeDtypeStruct((2, t), F32)],
        compiler_params=pltpu.CompilerParams(dimension_semantics=("parallel",)),
        name="router",
    )(x, w_router_t, b_router.reshape(e, 1))


RANK_SUB = 256
RANK_NSUB = 8


def _rank_kernel(keys_ref, tri_ref, rank_ref, counts_ref, carry_ref, *, n_classes):
    nsub, sub = RANK_NSUB, RANK_SUB

    @pl.when(pl.program_id(0) == 0)
    def _():
        carry_ref[...] = jnp.zeros_like(carry_ref)

    cls = lax.broadcasted_iota(I32, (nsub, n_classes, sub), 1)
    onehot = cls == keys_ref[...]
    oh = jnp.where(onehot, 1.0, 0.0).reshape(nsub * n_classes, sub).astype(BF16)
    pref = jnp.dot(oh, tri_ref[...], preferred_element_type=F32).reshape(nsub, n_classes, sub)
    carry = carry_ref[...]
    for j in range(nsub):
        before = pref[j] + (carry - 1.0)
        rank_ref[j] = jnp.sum(jnp.where(onehot[j], before, 0.0), axis=0, keepdims=True).astype(I32)
        carry = carry + pref[j][:, sub - 1:sub]
    carry_ref[...] = carry
    counts_ref[...] = carry.astype(I32)


def _rank_within_class(keys, n_classes):
    n = keys.shape[0]
    nsub, sub = RANK_NSUB, RANK_SUB
    tri = jnp.asarray(np.triu(np.ones((sub, sub), np.float32)), dtype=BF16)
    rank, counts = pl.pallas_call(
        functools.partial(_rank_kernel, n_classes=n_classes),
        grid=(n // (nsub * sub),),
        in_specs=[pl.BlockSpec((nsub, 1, sub), lambda i: (i, 0, 0)),
                  pl.BlockSpec((sub, sub), lambda i: (0, 0))],
        out_specs=[pl.BlockSpec((nsub, 1, sub), lambda i: (i, 0, 0)),
                   pl.BlockSpec((n_classes, 1), lambda i: (0, 0))],
        out_shape=[jax.ShapeDtypeStruct((n // sub, 1, sub), I32),
                   jax.ShapeDtypeStruct((n_classes, 1), I32)],
        scratch_shapes=[pltpu.VMEM((n_classes, 1), F32)],
        compiler_params=pltpu.CompilerParams(dimension_semantics=("arbitrary",)),
        name="rank_within_class",
    )(keys.reshape(n // sub, 1, sub), tri)
    return rank.reshape(n), counts.reshape(n_classes)


def _fill_slots_kernel(dest_ref, cnt_ref, pstart_ref, pend_ref, slot_ref, *, n_tok):
    n_items = dest_ref.shape[0]
    n_classes = cnt_ref.shape[0]
    n_slots = slot_ref.shape[0]

    def zero(p, carry):
        slot_ref[p] = 0
        return carry

    def pad_class(k, carry):
        lax.fori_loop(pstart_ref[k] + cnt_ref[k], pend_ref[k], zero, 0)
        return carry

    lax.fori_loop(0, n_classes, pad_class, 0)
    lax.fori_loop(pend_ref[n_classes - 1], n_slots, zero, 0)

    def place(i, carry):
        tok = jnp.bitwise_and(i, n_tok - 1) if n_tok & (n_tok - 1) == 0 else lax.rem(i, n_tok)
        slot_ref[dest_ref[i]] = tok
        return carry

    lax.fori_loop(0, n_items, place, 0, unroll=16)


def _fill_slots(dest, counts, pad_start, pad_end, n_slots, n_tok):
    smem = pl.BlockSpec(memory_space=pltpu.SMEM)
    return pl.pallas_call(
        functools.partial(_fill_slots_kernel, n_tok=n_tok),
        in_specs=[smem, smem, smem, smem],
        out_specs=smem,
        out_shape=jax.ShapeDtypeStruct((n_slots,), I32),
        name="fill_slots",
    )(dest, counts, pad_start.astype(I32), pad_end.astype(I32))


def _route_slots(eidx, bm):
    t = eidx.shape[1]
    a = 2 * t
    keys = eidx.reshape(a)
    rank, counts = _rank_within_class(keys, N_EXPERTS)
    padded = (counts + bm - 1) // bm * bm
    pad_end = jnp.cumsum(padded)
    pad_start = pad_end - padded
    dest = jnp.take(pad_start, keys) + rank
    p = a + N_EXPERTS * bm
    slot_tok = _fill_slots(dest, counts, pad_start, pad_end, p, t)
    nb = p // bm
    block_start = jnp.arange(nb, dtype=I32) * bm
    block_expert = jnp.minimum(
        jnp.sum((pad_end[None, :] <= block_start[:, None]).astype(I32), axis=1), N_EXPERTS - 1)
    n_active = (pad_end[-1] // bm).astype(I32).reshape(1)
    return dest, slot_tok, block_expert, n_active


def _expert_kernel(be_ref, tok_ref, nact_ref, x_hbm, wg_ref, wu_ref, wd_ref, o_ref,
                   xbuf, wg_bf, wu_bf, wd_bf, sem):
    bm = EXPERT_BM
    i = pl.program_id(0)
    nact = nact_ref[0]
    slot = i % 2

    def issue(blk, dst_slot):
        base = blk * bm

        def body(r, carry):
            tok = tok_ref[base + r]
            pltpu.make_async_copy(x_hbm.at[pl.ds(tok, 1), :], xbuf.at[dst_slot, pl.ds(r, 1), :],
                                  sem.at[dst_slot]).start()
            return carry

        lax.fori_loop(0, bm, body, 0, unroll=8)

    @pl.when(jnp.logical_and(i == 0, nact > 0))
    def _():
        issue(0, 0)

    @pl.when(i + 1 < nact)
    def _():
        issue(i + 1, 1 - slot)

    changed = jnp.logical_or(i == 0, be_ref[i] != be_ref[jnp.maximum(i - 1, 0)])

    @pl.when(jnp.logical_and(changed, i < nact))
    def _():
        wg_bf[...] = wg_ref[0].astype(BF16)
        wu_bf[...] = wu_ref[0].astype(BF16)
        wd_bf[...] = wd_ref[0].astype(BF16)

    @pl.when(i < nact)
    def _():
        pltpu.make_async_copy(x_hbm.at[pl.ds(0, bm), :], xbuf.at[slot], sem.at[slot]).wait()
        x = xbuf[slot].astype(BF16)
        g = jnp.dot(x, wg_bf[...], preferred_element_type=F32)
        u = jnp.dot(x, wu_bf[...], preferred_element_type=F32)
        h = (g / (1.0 + jnp.exp(-g))) * u
        o_ref[...] = jnp.dot(h.astype(BF16), wd_bf[...], preferred_element_type=F32)

    @pl.when(i >= nact)
    def _():
        o_ref[...] = jnp.zeros_like(o_ref)


def _experts(x, slot_tok, block_expert, n_active, w_gate, w_up, w_down, layer):
    t, d = x.shape
    _, e, _, f = w_gate.shape
    bm = EXPERT_BM
    p = slot_tok.shape[0]
    nb = p // bm
    est = 2 * 3 * d * f * 4 + 3 * d * f * 2 + 2 * bm * d * 4 + 2 * bm * d * 4 + 6 * bm * d * 4
    grid_spec = pltpu.PrefetchScalarGridSpec(
        num_scalar_prefetch=3,
        grid=(nb,),
        in_specs=[pl.BlockSpec(memory_space=pl.ANY),
                  pl.BlockSpec((None, 1, d, f), lambda i, be, tok, na: (layer, be[i], 0, 0)),
                  pl.BlockSpec((None, 1, d, f), lambda i, be, tok, na: (layer, be[i], 0, 0)),
                  pl.BlockSpec((None, 1, f, d), lambda i, be, tok, na: (layer, be[i], 0, 0))],
        out_specs=pl.BlockSpec((bm, d), lambda i, be, tok, na: (i, 0)),
        scratch_shapes=[pltpu.VMEM((2, bm, d), F32),
                        pltpu.VMEM((d, f), BF16), pltpu.VMEM((d, f), BF16), pltpu.VMEM((f, d), BF16),
                        pltpu.SemaphoreType.DMA((2,))],
    )
    return pl.pallas_call(
        _expert_kernel,
        grid_spec=grid_spec,
        out_shape=jax.ShapeDtypeStruct((p, d), F32),
        compiler_params=pltpu.CompilerParams(
            dimension_semantics=("arbitrary",), vmem_limit_bytes=_vmem_limit(est)),
        name="experts",
    )(block_expert, slot_tok, n_active, x, w_gate, w_up, w_down)


def _combine_kernel(dest_ref, x_ref, gate_ref, g_ref, b_ref, ys_hbm, o_ref, ybuf, sem):
    bt = COMBINE_BT
    i = pl.program_id(0)
    n = pl.num_programs(0)
    slot = i % 2

    def issue(blk, dst_slot):
        base = blk * bt

        def body(r, carry):
            for k in range(2):
                d = dest_ref[k * (n * bt) + base + r]
                pltpu.make_async_copy(ys_hbm.at[pl.ds(d, 1), :], ybuf.at[dst_slot, pl.ds(k * bt + r, 1), :],
                                      sem.at[dst_slot]).start()
            return carry

        lax.fori_loop(0, bt, body, 0, unroll=4)

    @pl.when(i == 0)
    def _():
        issue(0, 0)

    @pl.when(i + 1 < n)
    def _():
        issue(i + 1, 1 - slot)

    pltpu.make_async_copy(ys_hbm.at[pl.ds(0, 2 * bt), :], ybuf.at[slot], sem.at[slot]).wait()
    gate = gate_ref[...]
    y = ybuf[slot, 0:bt, :] * gate[:, 0:1] + ybuf[slot, bt:2 * bt, :] * gate[:, 1:2]
    z = DEEPNORM_ALPHA * x_ref[...] + y
    o_ref[...] = _layer_norm_rows(z, g_ref[...], b_ref[...])


def _combine_ln(x, ys, dest, gate_t, ln_g, ln_b):
    t, d = x.shape
    bt = COMBINE_BT
    grid_spec = pltpu.PrefetchScalarGridSpec(
        num_scalar_prefetch=1,
        grid=(t // bt,),
        in_specs=[pl.BlockSpec((bt, d), lambda i, dst: (i, 0)),
                  pl.BlockSpec((bt, 2), lambda i, dst: (i, 0)),
                  pl.BlockSpec((1, d), lambda i, dst: (0, 0)),
                  pl.BlockSpec((1, d), lambda i, dst: (0, 0)),
                  pl.BlockSpec(memory_space=pl.ANY)],
        out_specs=pl.BlockSpec((bt, d), lambda i, dst: (i, 0)),
        scratch_shapes=[pltpu.VMEM((2, 2 * bt, d), F32), pltpu.SemaphoreType.DMA((2,))],
    )
    return pl.pallas_call(
        _combine_kernel,
        grid_spec=grid_spec,
        out_shape=jax.ShapeDtypeStruct((t, d), F32),
        compiler_params=pltpu.CompilerParams(dimension_semantics=("arbitrary",)),
        name="combine_ln",
    )(dest, x, gate_t, ln_g.reshape(1, d), ln_b.reshape(1, d), ys)


def _moe_residual_ln(x, w_router_t, b_router, w_gate, w_up, w_down, layer, ln_g, ln_b):
    eidx, gate = _router(x, w_router_t, b_router)
    dest, slot_tok, block_expert, n_active = _route_slots(eidx, EXPERT_BM)
    ys = _experts(x, slot_tok, block_expert, n_active, w_gate, w_up, w_down, layer)
    return _combine_ln(x, ys, dest, gate.T, ln_g, ln_b)


def _na_bias_tables(rpb, rows):
    w = GRID_W
    kc = min(WIN_COLS, w)
    kr = min(WIN_ROWS, rows)
    cols = np.arange(w)
    col_start = np.clip(cols - kc // 2, 0, w - kc)
    col_mask = (cols[None, :] >= col_start[:, None]) & (cols[None, :] < col_start[:, None] + kc)
    col_idx = np.clip(cols[None, :] - cols[:, None] + WIN_COLS - 1, 0, 2 * WIN_COLS - 2)
    tb = rpb.astype(F32)[:, :, col_idx]
    tb = jnp.where(col_mask[None, None], tb, NEG_BIAS)
    n_ri = 2 * WIN_ROWS - 1
    tb = jnp.concatenate([tb, jnp.full((N_HEADS, 1, w, w), NEG_BIAS, F32)], axis=1)
    n_blocks = rows // NA_RQ
    ri = np.full((3, NA_RQ, NA_KROWS), n_ri, np.int32)
    for ty, blk in enumerate((0, 1, n_blocks - 1)):
        ks = int(np.clip(blk - 1, 0, n_blocks - 3)) * NA_RQ
        for rq in range(NA_RQ):
            r = blk * NA_RQ + rq
            rs = int(np.clip(r - kr // 2, 0, rows - kr))
            for j in range(NA_KROWS):
                key_row = ks + j
                if rs <= key_row < rs + kr:
                    ri[ty, rq, j] = key_row - r + WIN_ROWS - 1
    bt = tb[:, ri]
    return bt.transpose(1, 0, 2, 4, 3, 5).reshape(3, N_HEADS, NA_RQ * w, NA_KROWS * w)


def _na_kernel(q_ref, k0_ref, k1_ref, k2_ref, v0_ref, v1_ref, v2_ref, bias_ref, o_ref):
    dh = HEAD_DIM
    nt = (((1,), (1,)), ((), ()))
    scale = HEAD_DIM ** -0.5
    outs = []
    for h in range(NA_HEADS_PER_STEP):
        cols = slice(h * dh, (h + 1) * dh)
        q = q_ref[0, :, cols] * scale
        k = jnp.concatenate([k0_ref[0, :, cols], k1_ref[0, :, cols], k2_ref[0, :, cols]], axis=0)
        v = jnp.concatenate([v0_ref[0, :, cols], v1_ref[0, :, cols], v2_ref[0, :, cols]], axis=0)
        s = lax.dot_general(q, k, nt, preferred_element_type=F32) + bias_ref[0, h]
        m = jnp.max(s, axis=-1, keepdims=True)
        p = jnp.exp(s - m)
        l = jnp.sum(p, axis=-1, keepdims=True)
        o = jnp.dot(p.astype(BF16), v, preferred_element_type=F32)
        outs.append(o / l)
    o_ref[0] = jnp.concatenate(outs, axis=-1).astype(o_ref.dtype)


def _neighbourhood_attention(qkv, bias, rows):
    b, s, _ = qkv.shape
    d = D_MODEL
    w = GRID_W
    tq = NA_RQ * w
    n_blocks = rows // NA_RQ
    hw = NA_HEADS_PER_STEP * HEAD_DIM
    n_hh = d // hw
    assert NA_KROWS * w == 3 * tq and n_blocks >= 3

    def kv_map(part, j):
        def index_map(hh, bi, i):
            return (bi, jnp.clip(i - 1, 0, n_blocks - 3) + j, part * n_hh + hh)
        return index_map

    def bias_map(hh, bi, i):
        ty = jnp.where(i == 0, 0, jnp.where(i == n_blocks - 1, 2, 1))
        return (ty, hh, 0, 0)

    blk = (1, tq, hw)
    est = 2 * (7 * tq * hw * 2 + NA_HEADS_PER_STEP * tq * 3 * tq * 4 + tq * hw * 2) + 8 * tq * 3 * tq * 4
    return pl.pallas_call(
        _na_kernel,
        grid=(n_hh, b, n_blocks),
        in_specs=[pl.BlockSpec(blk, lambda hh, bi, i: (bi, i, hh))]
                 + [pl.BlockSpec(blk, kv_map(1, j)) for j in range(3)]
                 + [pl.BlockSpec(blk, kv_map(2, j)) for j in range(3)]
                 + [pl.BlockSpec((1, NA_HEADS_PER_STEP, tq, 3 * tq), bias_map)],
        out_specs=pl.BlockSpec(blk, lambda hh, bi, i: (bi, i, hh)),
        out_shape=jax.ShapeDtypeStruct((b, s, d), BF16),
        compiler_params=pltpu.CompilerParams(
            dimension_semantics=("parallel", "parallel", "parallel"), vmem_limit_bytes=_vmem_limit(est)),
        name="neighbourhood_attention",
    )(qkv, qkv, qkv, qkv, qkv, qkv, qkv, bias)


def kernel(x, fourier_w_in, fourier_w_out, na_w_qkv, na_rpb, na_w_out, router_w, router_b,
           expert_w_gate, expert_w_up, expert_w_down, ln_g, ln_b):
    b, s, d = x.shape
    t = b * s
    rows = s // GRID_W
    w_router_t = router_w.T
    xt = x.reshape(t, d)

    f = _fourier_mixer_pre_out(x, fourier_w_in[0])
    xt = _proj_residual_ln(f.reshape(t, d), fourier_w_out[0].astype(BF16), xt, ln_g[0, 0], ln_b[0, 0])
    xt = _moe_residual_ln(xt, w_router_t, router_b, expert_w_gate, expert_w_up, expert_w_down, 0,
                          ln_g[0, 1], ln_b[0, 1])

    qkv = _matmul(xt, na_w_qkv[0].astype(BF16), BF16, MM_BM, 1024).reshape(b, s, 3 * d)
    bias = _na_bias_tables(na_rpb[0], rows)
    o = _neighbourhood_attention(qkv, bias, rows)
    xt = _proj_residual_ln(o.reshape(t, d), na_w_out[0].astype(BF16), xt, ln_g[1, 0], ln_b[1, 0])
    xt = _moe_residual_ln(xt, w_router_t, router_b, expert_w_gate, expert_w_up, expert_w_down, 1,
                          ln_g[1, 1], ln_b[1, 1])
    return xt.reshape(b, s, d)
```

```python
import functools

import jax
import jax.numpy as jnp
import numpy as np
from jax import lax
from jax.experimental import pallas as pl
from jax.experimental.pallas import tpu as pltpu

F32 = jnp.float32
BF16 = jnp.bfloat16
I32 = jnp.int32

D_MODEL = 1024
GRID_W = 64
N_FOURIER_GROUPS = 4
FOURIER_GROUP_DIM = D_MODEL // N_FOURIER_GROUPS
N_HEADS = 16
HEAD_DIM = D_MODEL // N_HEADS
WIN_ROWS = 8
WIN_COLS = 16
N_EXPERTS = 32
N_GROUPS = 4
EXPERTS_PER_GROUP = N_EXPERTS // N_GROUPS
D_EXPERT = D_MODEL // 2
DEPTH = 2
DEEPNORM_ALPHA = (2 * DEPTH) ** 0.25
LN_EPS = 1e-5

V7X_VMEM_BYTES = 64 * 1024 * 1024
LANES = 128

FFT_N1 = 64
FFT_N2 = 128
FFT_CHUNK = 128

MM_BM = 1024
LN_BM = 512
ROUTER_BT = 1024
EXPERT_BM = 256
COMBINE_BT = 512
NA_RQ = 4
NA_KROWS = NA_RQ + WIN_ROWS
NA_HEADS_PER_STEP = 8
NEG_BIAS = -1e30


def _vmem_limit(nbytes):
    return int(min(max(nbytes, 32 * 1024 * 1024), V7X_VMEM_BYTES - 8 * 1024 * 1024))


def _mm_kernel(a_ref, b_ref, o_ref, *, precision):
    if precision is None:
        a = a_ref[...].astype(BF16)
        b = b_ref[...].astype(BF16)
        acc = jnp.dot(a, b, preferred_element_type=F32)
    else:
        acc = jnp.dot(a_ref[...], b_ref[...], preferred_element_type=F32, precision=precision)
    o_ref[...] = acc.astype(o_ref.dtype)


def _matmul(a, b, out_dtype, bm, bn, precision=None):
    m, k = a.shape
    _, n = b.shape
    est = 2 * (bm * k * a.dtype.itemsize + k * bn * b.dtype.itemsize + bm * bn * 4) + 3 * bm * bn * 4
    return pl.pallas_call(
        functools.partial(_mm_kernel, precision=precision),
        grid=(m // bm, n // bn),
        in_specs=[pl.BlockSpec((bm, k), lambda i, j: (i, 0)),
                  pl.BlockSpec((k, bn), lambda i, j: (0, j))],
        out_specs=pl.BlockSpec((bm, bn), lambda i, j: (i, j)),
        out_shape=jax.ShapeDtypeStruct((m, n), out_dtype),
        compiler_params=pltpu.CompilerParams(
            dimension_semantics=("parallel", "parallel"), vmem_limit_bytes=_vmem_limit(est)),
        name="matmul",
    )(a, b)


def _fourier_channel_tables():
    n = FOURIER_GROUP_DIM
    c = np.arange(n)
    ang = 2.0 * np.pi * ((c[:, None] * c[None, :]) % n) / n
    cos, sin = np.cos(ang) / np.sqrt(n), np.sin(ang) / np.sqrt(n)
    half = FFT_CHUNK
    tabs = [np.concatenate([cos[:, h * half:(h + 1) * half], sin[:, h * half:(h + 1) * half]], axis=1)
            for h in range(n // half)]
    return np.stack(tabs).astype(np.float32)


def _fourier_seq_tables(seq):
    n1, n2 = FFT_N1, FFT_N2
    assert n1 * n2 == seq
    k2 = np.arange(n2)
    s2 = np.arange(n2)
    m1 = np.empty((n1, 2 * n2, 2 * n2), np.float32)
    for s1 in range(n1):
        ang = 2.0 * np.pi * ((k2[:, None] * (s1 + n1 * s2[None, :])) % seq) / seq
        mr, mi = np.cos(ang) / np.sqrt(n2), np.sin(ang) / np.sqrt(n2)
        m1[s1] = np.block([[mr, -mi], [mi, mr]])
    k1 = np.arange(n1)
    ang = 2.0 * np.pi * ((k1[:, None] * k1[None, :]) % n1) / n1
    w2 = np.concatenate([np.cos(ang), -np.sin(ang)], axis=1) / np.sqrt(n1)
    return m1, w2.astype(np.float32)


def _fft_kernel(a_ref, b_ref, m1_ref, w2_ref, o_ref, zs_ref):
    n1, n2, c = FFT_N1, FFT_N2, FFT_CHUNK

    def stage1(s1, carry):
        rows = pl.ds(s1, n2, stride=n1)
        x = jnp.concatenate([a_ref[0, rows, :], b_ref[0, rows, :]], axis=0).astype(BF16)
        z = jnp.dot(m1_ref[s1], x, preferred_element_type=F32)
        zs_ref[pl.ds(pl.multiple_of(s1 * 2 * n2, 2 * n2), 2 * n2), :] = z
        return carry

    lax.fori_loop(0, n1, stage1, 0, unroll=4)

    def stage2(kk, carry):
        k2 = 2 * kk
        parts = []
        for d in range(2):
            zr = zs_ref[pl.ds(k2 + d, n1, stride=2 * n2), :]
            zi = zs_ref[pl.ds(n2 + k2 + d, n1, stride=2 * n2), :]
            parts.append(jnp.concatenate([zr, zi], axis=0))
        z = jnp.concatenate(parts, axis=1).astype(BF16)
        y = jnp.dot(w2_ref[...], z, preferred_element_type=F32)
        o_ref[0, pl.ds(k2, n1, stride=n2), :] = y[:, :c]
        o_ref[0, pl.ds(k2 + 1, n1, stride=n2), :] = y[:, c:]
        return carry

    lax.fori_loop(0, n2 // 2, stage2, 0, unroll=4)


def _seq_fft(ab, m1, w2):
    b, s, two_d = ab.shape
    d = two_d // 2
    c = FFT_CHUNK
    est = 2 * (s * 2 * c * 4 + m1.size * 2 + s * c * 4) + FFT_N1 * 2 * FFT_N2 * c * 4 + (4 << 20)
    return pl.pallas_call(
        _fft_kernel,
        grid=(b, d // c),
        in_specs=[pl.BlockSpec((1, s, c), lambda i, j: (i, 0, 2 * j)),
                  pl.BlockSpec((1, s, c), lambda i, j: (i, 0, 2 * j + 1)),
                  pl.BlockSpec(m1.shape, lambda i, j: (0, 0, 0)),
                  pl.BlockSpec(w2.shape, lambda i, j: (0, 0))],
        out_specs=pl.BlockSpec((1, s, c), lambda i, j: (i, 0, j)),
        out_shape=jax.ShapeDtypeStruct((b, s, d), F32),
        scratch_shapes=[pltpu.VMEM((FFT_N1 * 2 * FFT_N2, c), F32)],
        compiler_params=pltpu.CompilerParams(
            dimension_semantics=("parallel", "parallel"), vmem_limit_bytes=_vmem_limit(est)),
        name="seq_fft",
    )(ab, ab, m1, w2)


def _fourier_mixer_pre_out(x, w_in):
    b, s, d = x.shape
    g, gd, c = N_FOURIER_GROUPS, FOURIER_GROUP_DIM, FFT_CHUNK
    halves = gd // c
    cs = jnp.asarray(_fourier_channel_tables())
    w_ab = pl.pallas_call(
        functools.partial(_mm_kernel, precision=lax.Precision.HIGHEST),
        grid=(g, halves),
        in_specs=[pl.BlockSpec((d, gd), lambda i, h: (0, i)),
                  pl.BlockSpec((None, gd, 2 * c), lambda i, h: (h, 0, 0))],
        out_specs=pl.BlockSpec((d, 2 * c), lambda i, h: (0, i * halves + h)),
        out_shape=jax.ShapeDtypeStruct((d, 2 * d), BF16),
        name="fourier_weight_fold",
    )(w_in, cs)
    ab = _matmul(x.reshape(b * s, d), w_ab, F32, MM_BM, 1024).reshape(b, s, 2 * d)
    m1, w2 = _fourier_seq_tables(s)
    return _seq_fft(ab, jnp.asarray(m1, dtype=BF16), jnp.asarray(w2, dtype=BF16))


def _layer_norm_rows(z, g, b):
    mu = jnp.mean(z, axis=-1, keepdims=True)
    zc = z - mu
    var = jnp.mean(zc * zc, axis=-1, keepdims=True)
    return zc * lax.rsqrt(var + LN_EPS) * g + b


def _mm_ln_kernel(a_ref, w_ref, x_ref, g_ref, b_ref, o_ref):
    y = jnp.dot(a_ref[...].astype(BF16), w_ref[...], preferred_element_type=F32)
    z = DEEPNORM_ALPHA * x_ref[...] + y
    o_ref[...] = _layer_norm_rows(z, g_ref[...], b_ref[...])


def _proj_residual_ln(a, w_bf16, x, ln_g, ln_b):
    t, k = a.shape
    d = w_bf16.shape[1]
    bm = LN_BM
    est = 2 * (bm * k * a.dtype.itemsize + k * d * 2 + 2 * bm * d * 4) + 4 * bm * d * 4
    return pl.pallas_call(
        _mm_ln_kernel,
        grid=(t // bm,),
        in_specs=[pl.BlockSpec((bm, k), lambda i: (i, 0)),
                  pl.BlockSpec((k, d), lambda i: (0, 0)),
                  pl.BlockSpec((bm, d), lambda i: (i, 0)),
                  pl.BlockSpec((1, d), lambda i: (0, 0)),
                  pl.BlockSpec((1, d), lambda i: (0, 0))],
        out_specs=pl.BlockSpec((bm, d), lambda i: (i, 0)),
        out_shape=jax.ShapeDtypeStruct((t, d), F32),
        compiler_params=pltpu.CompilerParams(
            dimension_semantics=("parallel",), vmem_limit_bytes=_vmem_limit(est)),
        name="proj_residual_ln",
    )(a, w_bf16, x, ln_g.reshape(1, d), ln_b.reshape(1, d))


def _top2_rows(v, iota):
    n_rows = v.shape[0]
    m1 = jnp.max(v, axis=0, keepdims=True)
    i1 = jnp.min(jnp.where(v == m1, iota, n_rows), axis=0, keepdims=True)
    v2 = jnp.where(iota == i1, -jnp.inf, v)
    m2 = jnp.max(v2, axis=0, keepdims=True)
    i2 = jnp.min(jnp.where(v2 == m2, iota, n_rows), axis=0, keepdims=True)
    return m1, i1, m2, i2


def _router_kernel(x_ref, wt_ref, b_ref, eidx_ref, gate_ref):
    x = x_ref[...]
    w = wt_ref[...]
    xh = x.astype(BF16)
    xl = (x - xh.astype(F32)).astype(BF16)
    wh = w.astype(BF16)
    wl = (w - wh.astype(F32)).astype(BF16)
    nt = (((1,), (1,)), ((), ()))
    logits = (lax.dot_general(wh, xh, nt, preferred_element_type=F32)
              + lax.dot_general(wh, xl, nt, preferred_element_type=F32)
              + lax.dot_general(wl, xh, nt, preferred_element_type=F32))
    scores = 1.0 / (1.0 + jnp.exp(-logits))
    sel = scores + b_ref[...]
    epg = EXPERTS_PER_GROUP
    bt = x.shape[0]
    iota = lax.broadcasted_iota(I32, (epg, bt), 0)

    best = None
    for g in range(N_GROUPS):
        m1, _, m2, _ = _top2_rows(sel[g * epg:(g + 1) * epg], iota)
        gs = m1 + m2
        if best is None:
            best, gidx = gs, jnp.zeros((1, bt), I32)
        else:
            better = gs > best
            gidx = jnp.where(better, g, gidx)
            best = jnp.where(better, gs, best)

    sel_in = sel[0:epg]
    sc_in = scores[0:epg]
    for g in range(1, N_GROUPS):
        pick = gidx == g
        sel_in = jnp.where(pick, sel[g * epg:(g + 1) * epg], sel_in)
        sc_in = jnp.where(pick, scores[g * epg:(g + 1) * epg], sc_in)
    _, i1, _, i2 = _top2_rows(sel_in, iota)
    g1 = jnp.sum(jnp.where(iota == i1, sc_in, 0.0), axis=0, keepdims=True)
    g2 = jnp.sum(jnp.where(iota == i2, sc_in, 0.0), axis=0, keepdims=True)
    denom = g1 + g2
    eidx_ref[0:1, :] = gidx * epg + i1
    eidx_ref[1:2, :] = gidx * epg + i2
    gate_ref[0:1, :] = g1 / denom
    gate_ref[1:2, :] = g2 / denom


def _router(x, w_router_t, b_router):
    t, d = x.shape
    e = w_router_t.shape[0]
    bt = ROUTER_BT
    return pl.pallas_call(
        _router_kernel,
        grid=(t // bt,),
        in_specs=[pl.BlockSpec((bt, d), lambda i: (i, 0)),
                  pl.BlockSpec((e, d), lambda i: (0, 0)),
                  pl.BlockSpec((e, 1), lambda i: (0, 0))],
        out_specs=[pl.BlockSpec((2, bt), lambda i: (0, i)),
                   pl.BlockSpec((2, bt), lambda i: (0, i))],
        out_shape=[jax.ShapeDtypeStruct((2, t), I32), jax.ShapeDtypeStruct((2, t), F32)],
        compiler_params=pltpu.CompilerParams(dimension_semantics=("parallel",)),
        name="router",
    )(x, w_router_t, b_router.reshape(e, 1))


RANK_SUB = 256
RANK_NSUB = 8


def _rank_kernel(keys_ref, tri_ref, rank_ref, counts_ref, carry_ref, *, n_classes):
    nsub, sub = RANK_NSUB, RANK_SUB

    @pl.when(pl.program_id(0) == 0)
    def _():
        carry_ref[...] = jnp.zeros_like(carry_ref)

    cls = lax.broadcasted_iota(I32, (nsub, n_classes, sub), 1)
    onehot = cls == keys_ref[...]
    oh = jnp.where(onehot, 1.0, 0.0).reshape(nsub * n_classes, sub).astype(BF16)
    pref = jnp.dot(oh, tri_ref[...], preferred_element_type=F32).reshape(nsub, n_classes, sub)
    carry = carry_ref[...]
    for j in range(nsub):
        before = pref[j] + (carry - 1.0)
        rank_ref[j] = jnp.sum(jnp.where(onehot[j], before, 0.0), axis=0, keepdims=True).astype(I32)
        carry = carry + pref[j][:, sub - 1:sub]
    carry_ref[...] = carry
    counts_ref[...] = carry.astype(I32)


def _rank_within_class(keys, n_classes):
    n = keys.shape[0]
    nsub, sub = RANK_NSUB, RANK_SUB
    tri = jnp.asarray(np.triu(np.ones((sub, sub), np.float32)), dtype=BF16)
    rank, counts = pl.pallas_call(
        functools.partial(_rank_kernel, n_classes=n_classes),
        grid=(n // (nsub * sub),),
        in_specs=[pl.BlockSpec((nsub, 1, sub), lambda i: (i, 0, 0)),
                  pl.BlockSpec((sub, sub), lambda i: (0, 0))],
        out_specs=[pl.BlockSpec((nsub, 1, sub), lambda i: (i, 0, 0)),
                   pl.BlockSpec((n_classes, 1), lambda i: (0, 0))],
        out_shape=[jax.ShapeDtypeStruct((n // sub, 1, sub), I32),
                   jax.ShapeDtypeStruct((n_classes, 1), I32)],
        scratch_shapes=[pltpu.VMEM((n_classes, 1), F32)],
        compiler_params=pltpu.CompilerParams(dimension_semantics=("arbitrary",)),
        name="rank_within_class",
    )(keys.reshape(n // sub, 1, sub), tri)
    return rank.reshape(n), counts.reshape(n_classes)


FILL_CHUNK = 1024


def _fill_slots_kernel(dest_ref, cnt_ref, pstart_ref, pend_ref, slot_ref, *, n_items, bm):
    n_classes = cnt_ref.shape[0]
    n_slots = slot_ref.shape[0]
    step = pl.program_id(0)

    @pl.when(step == 0)
    def _():
        def pad_class(k, carry):
            def pad(p, c):
                slot_ref[p] = n_items + k * bm + jnp.bitwise_and(p, bm - 1)
                return c
            lax.fori_loop(pstart_ref[k] + cnt_ref[k], pend_ref[k], pad, 0)
            return carry

        lax.fori_loop(0, n_classes, pad_class, 0)

        def zero(p, carry):
            slot_ref[p] = 0
            return carry

        lax.fori_loop(pend_ref[n_classes - 1], n_slots, zero, 0)

    base = step * FILL_CHUNK

    def place(j, carry):
        slot_ref[dest_ref[j]] = base + j
        return carry

    lax.fori_loop(0, FILL_CHUNK, place, 0, unroll=16)


def _fill_slots(dest, counts, pad_start, pad_end, n_slots, bm):
    n_items = dest.shape[0]
    assert bm & (bm - 1) == 0 and n_items % FILL_CHUNK == 0
    smem = pl.BlockSpec(memory_space=pltpu.SMEM)
    return pl.pallas_call(
        functools.partial(_fill_slots_kernel, n_items=n_items, bm=bm),
        grid=(n_items // FILL_CHUNK,),
        in_specs=[pl.BlockSpec((FILL_CHUNK,), lambda i: (i,), memory_space=pltpu.SMEM), smem, smem, smem],
        out_specs=smem,
        out_shape=jax.ShapeDtypeStruct((n_slots,), I32),
        compiler_params=pltpu.CompilerParams(dimension_semantics=("arbitrary",)),
        name="fill_slots",
    )(dest, counts, pad_start.astype(I32), pad_end.astype(I32))


def _route_slots(eidx, bm):
    t = eidx.shape[1]
    a = 2 * t
    keys = eidx.reshape(a)
    rank, counts = _rank_within_class(keys, N_EXPERTS)
    padded = (counts + bm - 1) // bm * bm
    pad_end = jnp.cumsum(padded)
    pad_start = pad_end - padded
    dest = jnp.take(pad_start, keys) + rank
    p = a + N_EXPERTS * bm
    slot_item = _fill_slots(dest, counts, pad_start, pad_end, p, bm)
    nb = p // bm
    block_start = jnp.arange(nb, dtype=I32) * bm
    block_expert = jnp.minimum(
        jnp.sum((pad_end[None, :] <= block_start[:, None]).astype(I32), axis=1), N_EXPERTS - 1)
    n_active = (pad_end[-1] // bm).astype(I32).reshape(1)
    return slot_item, block_expert, n_active


SUBLANES = 8
SUBLANE_SHIFT = SUBLANES.bit_length() - 1


def _expert_kernel(be_ref, item_ref, nact_ref, x_hbm, wg_ref, wu_ref, wd_ref, y_hbm,
                   xbuf, ybuf, wg_bf, wu_bf, wd_bf, gsem, ssem, *, n_tok, n_classes):
    bm = EXPERT_BM
    tiles = bm // SUBLANES
    d = xbuf.shape[-1]
    i = pl.program_id(0)
    nact = nact_ref[0]
    slot = i % 2

    def row_copies(blk, buf_slot, gather):
        base = blk * bm

        def body(rt, carry):
            for u in range(SUBLANES):
                item = item_ref[base + rt * SUBLANES + u]
                if gather:
                    tok = jnp.bitwise_and(item, n_tok - 1)
                    pltpu.make_async_copy(
                        x_hbm.at[lax.shift_right_logical(tok, SUBLANE_SHIFT),
                                 pl.ds(jnp.bitwise_and(tok, SUBLANES - 1), 1), :],
                        xbuf.at[buf_slot, rt, pl.ds(u, 1), :], gsem.at[buf_slot]).start()
                else:
                    pltpu.make_async_copy(
                        ybuf.at[buf_slot, rt, pl.ds(u, 1), :],
                        y_hbm.at[lax.shift_right_logical(item, SUBLANE_SHIFT),
                                 pl.ds(jnp.bitwise_and(item, SUBLANES - 1), 1), :],
                        ssem.at[buf_slot]).start()
            return carry

        lax.fori_loop(0, tiles, body, 0)

    def wait_gather(buf_slot):
        pltpu.make_async_copy(x_hbm.at[pl.ds(0, tiles)], xbuf.at[buf_slot], gsem.at[buf_slot]).wait()

    def wait_scatter(buf_slot):
        pltpu.make_async_copy(ybuf.at[buf_slot], y_hbm.at[pl.ds(0, tiles)], ssem.at[buf_slot]).wait()

    @pl.when(i == 0)
    def _():
        row_copies(0, 0, True)
        ybuf[1] = jnp.zeros(ybuf.shape[1:], ybuf.dtype)
        spare_tile0 = 2 * n_tok // SUBLANES

        def spare_copy(k):
            return pltpu.make_async_copy(
                ybuf.at[1], y_hbm.at[pl.ds(pl.multiple_of(spare_tile0 + k * tiles, tiles), tiles)], ssem.at[1])

        @pl.loop(0, n_classes)
        def _(k):
            spare_copy(k).start()

        @pl.loop(0, n_classes)
        def _(k):
            spare_copy(k).wait()

    @pl.when(i + 1 < nact)
    def _():
        row_copies(i + 1, 1 - slot, True)

    changed = jnp.logical_or(i == 0, be_ref[i] != be_ref[jnp.maximum(i - 1, 0)])

    @pl.when(jnp.logical_and(changed, i < nact))
    def _():
        wg_bf[...] = wg_ref[0].astype(BF16)
        wu_bf[...] = wu_ref[0].astype(BF16)
        wd_bf[...] = wd_ref[0].astype(BF16)

    @pl.when(i < nact)
    def _():
        wait_gather(slot)

        @pl.when(i >= 2)
        def _():
            wait_scatter(slot)

        x = xbuf[slot].reshape(bm, d).astype(BF16)
        g = jnp.dot(x, wg_bf[...], preferred_element_type=F32)
        u = jnp.dot(x, wu_bf[...], preferred_element_type=F32)
        h = (g / (1.0 + jnp.exp(-g))) * u
        y = jnp.dot(h.astype(BF16), wd_bf[...], preferred_element_type=F32)
        ybuf[slot] = y.reshape(tiles, SUBLANES, d)
        row_copies(i, slot, False)

    @pl.when(i == nact - 1)
    def _():
        wait_scatter(slot)

        @pl.when(i >= 1)
        def _():
            wait_scatter(1 - slot)


def _experts(x, slot_item, block_expert, n_active, w_gate, w_up, w_down, layer):
    t, d = x.shape
    _, e, _, f = w_gate.shape
    bm = EXPERT_BM
    p = slot_item.shape[0]
    nb = p // bm
    n_rows = 2 * t + e * bm
    assert t & (t - 1) == 0 and t % SUBLANES == 0 and bm % SUBLANES == 0
    est = 2 * 3 * d * f * 4 + 3 * d * f * 2 + 4 * bm * d * 4 + 8 * bm * d * 4
    grid_spec = pltpu.PrefetchScalarGridSpec(
        num_scalar_prefetch=3,
        grid=(nb,),
        in_specs=[pl.BlockSpec(memory_space=pl.ANY),
                  pl.BlockSpec((None, 1, d, f), lambda i, be, it, na: (layer, be[i], 0, 0)),
                  pl.BlockSpec((None, 1, d, f), lambda i, be, it, na: (layer, be[i], 0, 0)),
                  pl.BlockSpec((None, 1, f, d), lambda i, be, it, na: (layer, be[i], 0, 0))],
        out_specs=pl.BlockSpec(memory_space=pl.ANY),
        scratch_shapes=[pltpu.VMEM((2, bm // SUBLANES, SUBLANES, d), F32),
                        pltpu.VMEM((2, bm // SUBLANES, SUBLANES, d), F32),
                        pltpu.VMEM((d, f), BF16), pltpu.VMEM((d, f), BF16), pltpu.VMEM((f, d), BF16),
                        pltpu.SemaphoreType.DMA((2,)), pltpu.SemaphoreType.DMA((2,))],
    )
    y = pl.pallas_call(
        functools.partial(_expert_kernel, n_tok=t, n_classes=e),
        grid_spec=grid_spec,
        out_shape=jax.ShapeDtypeStruct((n_rows // SUBLANES, SUBLANES, d), F32),
        compiler_params=pltpu.CompilerParams(
            dimension_semantics=("arbitrary",), vmem_limit_bytes=_vmem_limit(est)),
        name="experts",
    )(block_expert, slot_item, n_active, x.reshape(t // SUBLANES, SUBLANES, d), w_gate, w_up, w_down)
    return y.reshape(n_rows, d)


def _combine_kernel(x_ref, y0_ref, y1_ref, gate_ref, g_ref, b_ref, o_ref):
    gate = gate_ref[...]
    y = y0_ref[...] * gate[:, 0:1] + y1_ref[...] * gate[:, 1:2]
    z = DEEPNORM_ALPHA * x_ref[...] + y
    o_ref[...] = _layer_norm_rows(z, g_ref[...], b_ref[...])


def _combine_ln(x, y_items, gate_t, ln_g, ln_b):
    t, d = x.shape
    bt = COMBINE_BT
    nblk = t // bt
    return pl.pallas_call(
        _combine_kernel,
        grid=(nblk,),
        in_specs=[pl.BlockSpec((bt, d), lambda i: (i, 0)),
                  pl.BlockSpec((bt, d), lambda i: (i, 0)),
                  pl.BlockSpec((bt, d), lambda i: (nblk + i, 0)),
                  pl.BlockSpec((bt, 2), lambda i: (i, 0)),
                  pl.BlockSpec((1, d), lambda i: (0, 0)),
                  pl.BlockSpec((1, d), lambda i: (0, 0))],
        out_specs=pl.BlockSpec((bt, d), lambda i: (i, 0)),
        out_shape=jax.ShapeDtypeStruct((t, d), F32),
        compiler_params=pltpu.CompilerParams(
            dimension_semantics=("parallel",), vmem_limit_bytes=_vmem_limit(10 * bt * d * 4)),
        name="combine_ln",
    )(x, y_items, y_items, gate_t, ln_g.reshape(1, d), ln_b.reshape(1, d))


def _moe_residual_ln(x, w_router_t, b_router, w_gate, w_up, w_down, layer, ln_g, ln_b):
    eidx, gate = _router(x, w_router_t, b_router)
    slot_item, block_expert, n_active = _route_slots(eidx, EXPERT_BM)
    y_items = _experts(x, slot_item, block_expert, n_active, w_gate, w_up, w_down, layer)
    return _combine_ln(x, y_items, gate.T, ln_g, ln_b)


def _na_bias_tables(rpb, rows):
    w = GRID_W
    kc = min(WIN_COLS, w)
    kr = min(WIN_ROWS, rows)
    cols = np.arange(w)
    col_start = np.clip(cols - kc // 2, 0, w - kc)
    col_mask = (cols[None, :] >= col_start[:, None]) & (cols[None, :] < col_start[:, None] + kc)
    col_idx = np.clip(cols[None, :] - cols[:, None] + WIN_COLS - 1, 0, 2 * WIN_COLS - 2)
    tb = rpb.astype(F32)[:, :, col_idx]
    tb = jnp.where(col_mask[None, None], tb, NEG_BIAS)
    n_ri = 2 * WIN_ROWS - 1
    tb = jnp.concatenate([tb, jnp.full((N_HEADS, 1, w, w), NEG_BIAS, F32)], axis=1)
    n_blocks = rows // NA_RQ
    ri = np.full((3, NA_RQ, NA_KROWS), n_ri, np.int32)
    for ty, blk in enumerate((0, 1, n_blocks - 1)):
        ks = int(np.clip(blk - 1, 0, n_blocks - 3)) * NA_RQ
        for rq in range(NA_RQ):
            r = blk * NA_RQ + rq
            rs = int(np.clip(r - kr // 2, 0, rows - kr))
            for j in range(NA_KROWS):
                key_row = ks + j
                if rs <= key_row < rs + kr:
                    ri[ty, rq, j] = key_row - r + WIN_ROWS - 1
    bt = tb[:, ri]
    return bt.transpose(1, 0, 2, 4, 3, 5).reshape(3, N_HEADS, NA_RQ * w, NA_KROWS * w)


def _na_kernel(q_ref, k0_ref, k1_ref, k2_ref, v0_ref, v1_ref, v2_ref, bias_ref, o_ref):
    dh = HEAD_DIM
    nt = (((1,), (1,)), ((), ()))
    scale = HEAD_DIM ** -0.5
    outs = []
    for h in range(NA_HEADS_PER_STEP):
        cols = slice(h * dh, (h + 1) * dh)
        q = q_ref[0, :, cols] * scale
        k = jnp.concatenate([k0_ref[0, :, cols], k1_ref[0, :, cols], k2_ref[0, :, cols]], axis=0)
        v = jnp.concatenate([v0_ref[0, :, cols], v1_ref[0, :, cols], v2_ref[0, :, cols]], axis=0)
        s = lax.dot_general(q, k, nt, preferred_element_type=F32) + bias_ref[0, h]
        m = jnp.max(s, axis=-1, keepdims=True)
        p = jnp.exp(s - m)
        l = jnp.sum(p, axis=-1, keepdims=True)
        o = jnp.dot(p.astype(BF16), v, preferred_element_type=F32)
        outs.append(o / l)
    o_ref[0] = jnp.concatenate(outs, axis=-1).astype(o_ref.dtype)


def _neighbourhood_attention(qkv, bias, rows):
    b, s, _ = qkv.shape
    d = D_MODEL
    w = GRID_W
    tq = NA_RQ * w
    n_blocks = rows // NA_RQ
    hw = NA_HEADS_PER_STEP * HEAD_DIM
    n_hh = d // hw
    assert NA_KROWS * w == 3 * tq and n_blocks >= 3

    def kv_map(part, j):
        def index_map(hh, bi, i):
            return (bi, jnp.clip(i - 1, 0, n_blocks - 3) + j, part * n_hh + hh)
        return index_map

    def bias_map(hh, bi, i):
        ty = jnp.where(i == 0, 0, jnp.where(i == n_blocks - 1, 2, 1))
        return (ty, hh, 0, 0)

    blk = (1, tq, hw)
    est = 2 * (7 * tq * hw * 2 + NA_HEADS_PER_STEP * tq * 3 * tq * 4 + tq * hw * 2) + 8 * tq * 3 * tq * 4
    return pl.pallas_call(
        _na_kernel,
        grid=(n_hh, b, n_blocks),
        in_specs=[pl.BlockSpec(blk, lambda hh, bi, i: (bi, i, hh))]
                 + [pl.BlockSpec(blk, kv_map(1, j)) for j in range(3)]
                 + [pl.BlockSpec(blk, kv_map(2, j)) for j in range(3)]
                 + [pl.BlockSpec((1, NA_HEADS_PER_STEP, tq, 3 * tq), bias_map)],
        out_specs=pl.BlockSpec(blk, lambda hh, bi, i: (bi, i, hh)),
        out_shape=jax.ShapeDtypeStruct((b, s, d), BF16),
        compiler_params=pltpu.CompilerParams(
            dimension_semantics=("parallel", "parallel", "parallel"), vmem_limit_bytes=_vmem_limit(est)),
        name="neighbourhood_attention",
    )(qkv, qkv, qkv, qkv, qkv, qkv, qkv, bias)


def kernel(x, fourier_w_in, fourier_w_out, na_w_qkv, na_rpb, na_w_out, router_w, router_b,
           expert_w_gate, expert_w_up, expert_w_down, ln_g, ln_b):
    b, s, d = x.shape
    t = b * s
    rows = s // GRID_W
    w_router_t = router_w.T
    xt = x.reshape(t, d)

    f = _fourier_mixer_pre_out(x, fourier_w_in[0])
    xt = _proj_residual_ln(f.reshape(t, d), fourier_w_out[0].astype(BF16), xt, ln_g[0, 0], ln_b[0, 0])
    xt = _moe_residual_ln(xt, w_router_t, router_b, expert_w_gate, expert_w_up, expert_w_down, 0,
                          ln_g[0, 1], ln_b[0, 1])

    qkv = _matmul(xt, na_w_qkv[0].astype(BF16), BF16, MM_BM, 1024).reshape(b, s, 3 * d)
    bias = _na_bias_tables(na_rpb[0], rows)
    o = _neighbourhood_attention(qkv, bias, rows)
    xt = _proj_residual_ln(o.reshape(t, d), na_w_out[0].astype(BF16), xt, ln_g[1, 0], ln_b[1, 0])
    xt = _moe_residual_ln(xt, w_router_t, router_b, expert_w_gate, expert_w_up, expert_w_down, 1,
                          ln_g[1, 1], ln_b[1, 1])
    return xt.reshape(b, s, d)
```

```python
import functools

import jax
import jax.numpy as jnp
import numpy as np
from jax import lax
from jax.experimental import pallas as pl
from jax.experimental.pallas import tpu as pltpu

F32 = jnp.float32
BF16 = jnp.bfloat16
I32 = jnp.int32

D_MODEL = 1024
GRID_W = 64
N_FOURIER_GROUPS = 4
FOURIER_GROUP_DIM = D_MODEL // N_FOURIER_GROUPS
N_HEADS = 16
HEAD_DIM = D_MODEL // N_HEADS
WIN_ROWS = 8
WIN_COLS = 16
N_EXPERTS = 32
N_GROUPS = 4
EXPERTS_PER_GROUP = N_EXPERTS // N_GROUPS
D_EXPERT = D_MODEL // 2
DEPTH = 2
DEEPNORM_ALPHA = (2 * DEPTH) ** 0.25
LN_EPS = 1e-5

V7X_VMEM_BYTES = 64 * 1024 * 1024
LANES = 128

FFT_N1 = 64
FFT_N2 = 128
FFT_CHUNK = 128

MM_BM = 1024
LN_BM = 512
ROUTER_BT = 1024
EXPERT_BM = 256
COMBINE_BT = 512
NA_RQ = 4
NA_KROWS = NA_RQ + WIN_ROWS
NA_HEADS_PER_STEP = 8
NEG_BIAS = -1e30


def _vmem_limit(nbytes):
    return int(min(max(nbytes, 32 * 1024 * 1024), V7X_VMEM_BYTES - 8 * 1024 * 1024))


def _mm_kernel(a_ref, b_ref, o_ref, *, precision):
    if precision is None:
        a = a_ref[...].astype(BF16)
        b = b_ref[...].astype(BF16)
        acc = jnp.dot(a, b, preferred_element_type=F32)
    else:
        acc = jnp.dot(a_ref[...], b_ref[...], preferred_element_type=F32, precision=precision)
    o_ref[...] = acc.astype(o_ref.dtype)


def _matmul(a, b, out_dtype, bm, bn, precision=None):
    m, k = a.shape
    _, n = b.shape
    est = 2 * (bm * k * a.dtype.itemsize + k * bn * b.dtype.itemsize + bm * bn * 4) + 3 * bm * bn * 4
    return pl.pallas_call(
        functools.partial(_mm_kernel, precision=precision),
        grid=(m // bm, n // bn),
        in_specs=[pl.BlockSpec((bm, k), lambda i, j: (i, 0)),
                  pl.BlockSpec((k, bn), lambda i, j: (0, j))],
        out_specs=pl.BlockSpec((bm, bn), lambda i, j: (i, j)),
        out_shape=jax.ShapeDtypeStruct((m, n), out_dtype),
        compiler_params=pltpu.CompilerParams(
            dimension_semantics=("parallel", "parallel"), vmem_limit_bytes=_vmem_limit(est)),
        name="matmul",
    )(a, b)


def _fourier_channel_tables():
    n = FOURIER_GROUP_DIM
    c = np.arange(n)
    ang = 2.0 * np.pi * ((c[:, None] * c[None, :]) % n) / n
    cos, sin = np.cos(ang) / np.sqrt(n), np.sin(ang) / np.sqrt(n)
    half = FFT_CHUNK
    tabs = [np.concatenate([cos[:, h * half:(h + 1) * half], sin[:, h * half:(h + 1) * half]], axis=1)
            for h in range(n // half)]
    return np.stack(tabs).astype(np.float32)


def _fourier_seq_tables(seq):
    n1, n2 = FFT_N1, FFT_N2
    assert n1 * n2 == seq
    k2 = np.arange(n2)
    s2 = np.arange(n2)
    m1 = np.empty((n1, 2 * n2, 2 * n2), np.float32)
    for s1 in range(n1):
        ang = 2.0 * np.pi * ((k2[:, None] * (s1 + n1 * s2[None, :])) % seq) / seq
        mr, mi = np.cos(ang) / np.sqrt(n2), np.sin(ang) / np.sqrt(n2)
        m1[s1] = np.block([[mr, -mi], [mi, mr]])
    k1 = np.arange(n1)
    ang = 2.0 * np.pi * ((k1[:, None] * k1[None, :]) % n1) / n1
    w2 = np.concatenate([np.cos(ang), -np.sin(ang)], axis=1) / np.sqrt(n1)
    return m1, w2.astype(np.float32)


def _fft_kernel(a_ref, b_ref, m1_ref, w2_ref, o_ref, zs_ref):
    n1, n2, c = FFT_N1, FFT_N2, FFT_CHUNK

    def stage1(s1, carry):
        rows = pl.ds(s1, n2, stride=n1)
        x = jnp.concatenate([a_ref[0, rows, :], b_ref[0, rows, :]], axis=0).astype(BF16)
        z = jnp.dot(m1_ref[s1], x, preferred_element_type=F32)
        zs_ref[pl.ds(pl.multiple_of(s1 * 2 * n2, 2 * n2), 2 * n2), :] = z
        return carry

    lax.fori_loop(0, n1, stage1, 0, unroll=4)

    def stage2(kk, carry):
        k2 = 2 * kk
        parts = []
        for d in range(2):
            zr = zs_ref[pl.ds(k2 + d, n1, stride=2 * n2), :]
            zi = zs_ref[pl.ds(n2 + k2 + d, n1, stride=2 * n2), :]
            parts.append(jnp.concatenate([zr, zi], axis=0))
        z = jnp.concatenate(parts, axis=1).astype(BF16)
        y = jnp.dot(w2_ref[...], z, preferred_element_type=F32)
        o_ref[0, pl.ds(k2, n1, stride=n2), :] = y[:, :c]
        o_ref[0, pl.ds(k2 + 1, n1, stride=n2), :] = y[:, c:]
        return carry

    lax.fori_loop(0, n2 // 2, stage2, 0, unroll=4)


def _seq_fft(ab, m1, w2):
    b, s, two_d = ab.shape
    d = two_d // 2
    c = FFT_CHUNK
    est = 2 * (s * 2 * c * 4 + m1.size * 2 + s * c * 4) + FFT_N1 * 2 * FFT_N2 * c * 4 + (4 << 20)
    return pl.pallas_call(
        _fft_kernel,
        grid=(b, d // c),
        in_specs=[pl.BlockSpec((1, s, c), lambda i, j: (i, 0, 2 * j)),
                  pl.BlockSpec((1, s, c), lambda i, j: (i, 0, 2 * j + 1)),
                  pl.BlockSpec(m1.shape, lambda i, j: (0, 0, 0)),
                  pl.BlockSpec(w2.shape, lambda i, j: (0, 0))],
        out_specs=pl.BlockSpec((1, s, c), lambda i, j: (i, 0, j)),
        out_shape=jax.ShapeDtypeStruct((b, s, d), F32),
        scratch_shapes=[pltpu.VMEM((FFT_N1 * 2 * FFT_N2, c), F32)],
        compiler_params=pltpu.CompilerParams(
            dimension_semantics=("parallel", "parallel"), vmem_limit_bytes=_vmem_limit(est)),
        name="seq_fft",
    )(ab, ab, m1, w2)


def _fourier_mixer_pre_out(x, w_in):
    b, s, d = x.shape
    g, gd, c = N_FOURIER_GROUPS, FOURIER_GROUP_DIM, FFT_CHUNK
    halves = gd // c
    cs = jnp.asarray(_fourier_channel_tables())
    w_ab = pl.pallas_call(
        functools.partial(_mm_kernel, precision=lax.Precision.HIGHEST),
        grid=(g, halves),
        in_specs=[pl.BlockSpec((d, gd), lambda i, h: (0, i)),
                  pl.BlockSpec((None, gd, 2 * c), lambda i, h: (h, 0, 0))],
        out_specs=pl.BlockSpec((d, 2 * c), lambda i, h: (0, i * halves + h)),
        out_shape=jax.ShapeDtypeStruct((d, 2 * d), BF16),
        name="fourier_weight_fold",
    )(w_in, cs)
    ab = _matmul(x.reshape(b * s, d), w_ab, F32, MM_BM, 1024).reshape(b, s, 2 * d)
    m1, w2 = _fourier_seq_tables(s)
    return _seq_fft(ab, jnp.asarray(m1, dtype=BF16), jnp.asarray(w2, dtype=BF16))


def _layer_norm_rows(z, g, b):
    mu = jnp.mean(z, axis=-1, keepdims=True)
    zc = z - mu
    var = jnp.mean(zc * zc, axis=-1, keepdims=True)
    return zc * lax.rsqrt(var + LN_EPS) * g + b


def _mm_ln_kernel(a_ref, w_ref, x_ref, g_ref, b_ref, o_ref):
    y = jnp.dot(a_ref[...].astype(BF16), w_ref[...], preferred_element_type=F32)
    z = DEEPNORM_ALPHA * x_ref[...] + y
    o_ref[...] = _layer_norm_rows(z, g_ref[...], b_ref[...])


def _proj_residual_ln(a, w_bf16, x, ln_g, ln_b):
    t, k = a.shape
    d = w_bf16.shape[1]
    bm = LN_BM
    est = 2 * (bm * k * a.dtype.itemsize + k * d * 2 + 2 * bm * d * 4) + 4 * bm * d * 4
    return pl.pallas_call(
        _mm_ln_kernel,
        grid=(t // bm,),
        in_specs=[pl.BlockSpec((bm, k), lambda i: (i, 0)),
                  pl.BlockSpec((k, d), lambda i: (0, 0)),
                  pl.BlockSpec((bm, d), lambda i: (i, 0)),
                  pl.BlockSpec((1, d), lambda i: (0, 0)),
                  pl.BlockSpec((1, d), lambda i: (0, 0))],
        out_specs=pl.BlockSpec((bm, d), lambda i: (i, 0)),
        out_shape=jax.ShapeDtypeStruct((t, d), F32),
        compiler_params=pltpu.CompilerParams(
            dimension_semantics=("parallel",), vmem_limit_bytes=_vmem_limit(est)),
        name="proj_residual_ln",
    )(a, w_bf16, x, ln_g.reshape(1, d), ln_b.reshape(1, d))


def _top2_rows(v, iota):
    n_rows = v.shape[0]
    m1 = jnp.max(v, axis=0, keepdims=True)
    i1 = jnp.min(jnp.where(v == m1, iota, n_rows), axis=0, keepdims=True)
    v2 = jnp.where(iota == i1, -jnp.inf, v)
    m2 = jnp.max(v2, axis=0, keepdims=True)
    i2 = jnp.min(jnp.where(v2 == m2, iota, n_rows), axis=0, keepdims=True)
    return m1, i1, m2, i2


def _router_kernel(x_ref, wt_ref, b_ref, eidx_ref, gate_ref):
    x = x_ref[...]
    w = wt_ref[...]
    xh = x.astype(BF16)
    xl = (x - xh.astype(F32)).astype(BF16)
    wh = w.astype(BF16)
    wl = (w - wh.astype(F32)).astype(BF16)
    nt = (((1,), (1,)), ((), ()))
    logits = (lax.dot_general(wh, xh, nt, preferred_element_type=F32)
              + lax.dot_general(wh, xl, nt, preferred_element_type=F32)
              + lax.dot_general(wl, xh, nt, preferred_element_type=F32))
    scores = 1.0 / (1.0 + jnp.exp(-logits))
    sel = scores + b_ref[...]
    epg = EXPERTS_PER_GROUP
    bt = x.shape[0]
    iota = lax.broadcasted_iota(I32, (epg, bt), 0)

    best = None
    for g in range(N_GROUPS):
        m1, _, m2, _ = _top2_rows(sel[g * epg:(g + 1) * epg], iota)
        gs = m1 + m2
        if best is None:
            best, gidx = gs, jnp.zeros((1, bt), I32)
        else:
            better = gs > best
            gidx = jnp.where(better, g, gidx)
            best = jnp.where(better, gs, best)

    sel_in = sel[0:epg]
    sc_in = scores[0:epg]
    for g in range(1, N_GROUPS):
        pick = gidx == g
        sel_in = jnp.where(pick, sel[g * epg:(g + 1) * epg], sel_in)
        sc_in = jnp.where(pick, scores[g * epg:(g + 1) * epg], sc_in)
    _, i1, _, i2 = _top2_rows(sel_in, iota)
    g1 = jnp.sum(jnp.where(iota == i1, sc_in, 0.0), axis=0, keepdims=True)
    g2 = jnp.sum(jnp.where(iota == i2, sc_in, 0.0), axis=0, keepdims=True)
    denom = g1 + g2
    eidx_ref[0:1, :] = gidx * epg + i1
    eidx_ref[1:2, :] = gidx * epg + i2
    gate_ref[0:1, :] = g1 / denom
    gate_ref[1:2, :] = g2 / denom


def _router(x, w_router_t, b_router):
    t, d = x.shape
    e = w_router_t.shape[0]
    bt = ROUTER_BT
    return pl.pallas_call(
        _router_kernel,
        grid=(t // bt,),
        in_specs=[pl.BlockSpec((bt, d), lambda i: (i, 0)),
                  pl.BlockSpec((e, d), lambda i: (0, 0)),
                  pl.BlockSpec((e, 1), lambda i: (0, 0))],
        out_specs=[pl.BlockSpec((2, bt), lambda i: (0, i)),
                   pl.BlockSpec((2, bt), lambda i: (0, i))],
        out_shape=[jax.ShapeDtypeStruct((2, t), I32), jax.ShapeDtypeStruct((2, t), F32)],
        compiler_params=pltpu.CompilerParams(dimension_semantics=("parallel",)),
        name="router",
    )(x, w_router_t, b_router.reshape(e, 1))


RANK_SUB = 256
RANK_NSUB = 8


def _rank_kernel(keys_ref, tri_ref, rank_ref, counts_ref, carry_ref, *, n_classes):
    nsub, sub = RANK_NSUB, RANK_SUB

    @pl.when(pl.program_id(0) == 0)
    def _():
        carry_ref[...] = jnp.zeros_like(carry_ref)

    cls = lax.broadcasted_iota(I32, (nsub, n_classes, sub), 1)
    onehot = cls == keys_ref[...]
    oh = jnp.where(onehot, 1.0, 0.0).reshape(nsub * n_classes, sub).astype(BF16)
    pref = jnp.dot(oh, tri_ref[...], preferred_element_type=F32).reshape(nsub, n_classes, sub)
    carry = carry_ref[...]
    for j in range(nsub):
        before = pref[j] + (carry - 1.0)
        rank_ref[j] = jnp.sum(jnp.where(onehot[j], before, 0.0), axis=0, keepdims=True).astype(I32)
        carry = carry + pref[j][:, sub - 1:sub]
    carry_ref[...] = carry
    counts_ref[...] = carry.astype(I32)


def _rank_within_class(keys, n_classes):
    n = keys.shape[0]
    nsub, sub = RANK_NSUB, RANK_SUB
    tri = jnp.asarray(np.triu(np.ones((sub, sub), np.float32)), dtype=BF16)
    rank, counts = pl.pallas_call(
        functools.partial(_rank_kernel, n_classes=n_classes),
        grid=(n // (nsub * sub),),
        in_specs=[pl.BlockSpec((nsub, 1, sub), lambda i: (i, 0, 0)),
                  pl.BlockSpec((sub, sub), lambda i: (0, 0))],
        out_specs=[pl.BlockSpec((nsub, 1, sub), lambda i: (i, 0, 0)),
                   pl.BlockSpec((n_classes, 1), lambda i: (0, 0))],
        out_shape=[jax.ShapeDtypeStruct((n // sub, 1, sub), I32),
                   jax.ShapeDtypeStruct((n_classes, 1), I32)],
        scratch_shapes=[pltpu.VMEM((n_classes, 1), F32)],
        compiler_params=pltpu.CompilerParams(dimension_semantics=("arbitrary",)),
        name="rank_within_class",
    )(keys.reshape(n // sub, 1, sub), tri)
    return rank.reshape(n), counts.reshape(n_classes)


FILL_CHUNK = 1024


def _fill_slots_kernel(dest_ref, cnt_ref, pstart_ref, pend_ref, slot_ref, *, n_items, bm):
    n_classes = cnt_ref.shape[0]
    n_slots = slot_ref.shape[0]
    step = pl.program_id(0)

    @pl.when(step == 0)
    def _():
        def pad_class(k, carry):
            def pad(p, c):
                slot_ref[p] = n_items + k * bm + jnp.bitwise_and(p, bm - 1)
                return c
            lax.fori_loop(pstart_ref[k] + cnt_ref[k], pend_ref[k], pad, 0)
            return carry

        lax.fori_loop(0, n_classes, pad_class, 0)

        def zero(p, carry):
            slot_ref[p] = 0
            return carry

        lax.fori_loop(pend_ref[n_classes - 1], n_slots, zero, 0)

    base = step * FILL_CHUNK

    def place(j, carry):
        slot_ref[dest_ref[j]] = base + j
        return carry

    lax.fori_loop(0, FILL_CHUNK, place, 0, unroll=16)


def _fill_slots(dest, counts, pad_start, pad_end, n_slots, bm):
    n_items = dest.shape[0]
    assert bm & (bm - 1) == 0 and n_items % FILL_CHUNK == 0
    smem = pl.BlockSpec(memory_space=pltpu.SMEM)
    return pl.pallas_call(
        functools.partial(_fill_slots_kernel, n_items=n_items, bm=bm),
        grid=(n_items // FILL_CHUNK,),
        in_specs=[pl.BlockSpec((FILL_CHUNK,), lambda i: (i,), memory_space=pltpu.SMEM), smem, smem, smem],
        out_specs=smem,
        out_shape=jax.ShapeDtypeStruct((n_slots,), I32),
        compiler_params=pltpu.CompilerParams(dimension_semantics=("arbitrary",)),
        name="fill_slots",
    )(dest, counts, pad_start.astype(I32), pad_end.astype(I32))


def _route_slots(eidx, bm):
    t = eidx.shape[1]
    a = 2 * t
    keys = eidx.reshape(a)
    rank, counts = _rank_within_class(keys, N_EXPERTS)
    padded = (counts + bm - 1) // bm * bm
    pad_end = jnp.cumsum(padded)
    pad_start = pad_end - padded
    dest = jnp.take(pad_start, keys) + rank
    p = a + (N_EXPERTS + 2) * bm
    slot_item = _fill_slots(dest, counts, pad_start, pad_end, p, bm)
    nb = p // bm
    block_start = jnp.arange(nb, dtype=I32) * bm
    block_expert = jnp.minimum(
        jnp.sum((pad_end[None, :] <= block_start[:, None]).astype(I32), axis=1), N_EXPERTS - 1)
    n_active = (pad_end[-1] // bm).astype(I32).reshape(1)
    return slot_item, block_expert, n_active


SUBLANES = 8
SUBLANE_SHIFT = SUBLANES.bit_length() - 1


def _expert_kernel(be_ref, item_ref, nact_ref, x_hbm, wg0_ref, wu0_ref, wd0_ref, wg1_ref, wu1_ref, wd1_ref,
                   y_hbm, xa, xb, ya, yb, wg_bf, wu_bf, wd_bf, gsem, ssem, *, n_tok, n_classes):
    bm = EXPERT_BM
    tiles = bm // SUBLANES
    d = xa.shape[-1]
    g = pl.program_id(0)
    n_blocks = 2 * pl.num_programs(0)
    nact = nact_ref[0]
    xbufs, ybufs = (xa, xb), (ya, yb)
    wrefs = ((wg0_ref, wu0_ref, wd0_ref), (wg1_ref, wu1_ref, wd1_ref))

    def row_copies(blk, par, gather, unrolled):
        base = blk * bm

        def body(rt, carry):
            for u in range(SUBLANES):
                item = item_ref[base + rt * SUBLANES + u]
                if gather:
                    tok = jnp.bitwise_and(item, n_tok - 1)
                    pltpu.make_async_copy(
                        x_hbm.at[lax.shift_right_logical(tok, SUBLANE_SHIFT),
                                 pl.ds(jnp.bitwise_and(tok, SUBLANES - 1), 1), :],
                        xbufs[par].at[rt, pl.ds(u, 1), :], gsem.at[par]).start()
                else:
                    pltpu.make_async_copy(
                        ybufs[par].at[rt, pl.ds(u, 1), :],
                        y_hbm.at[lax.shift_right_logical(item, SUBLANE_SHIFT),
                                 pl.ds(jnp.bitwise_and(item, SUBLANES - 1), 1), :],
                        ssem.at[par]).start()
            return carry

        if unrolled:
            for rt in range(tiles):
                body(rt, 0)
        else:
            lax.fori_loop(0, tiles, body, 0)

    def wait_gather(par):
        pltpu.make_async_copy(x_hbm.at[pl.ds(0, tiles)], xbufs[par], gsem.at[par]).wait()

    def wait_scatter(par):
        pltpu.make_async_copy(ybufs[par], y_hbm.at[pl.ds(0, tiles)], ssem.at[par]).wait()

    def wait_scatter_before_reuse(par):
        if par == 1:
            wait_scatter(1)
        else:
            @pl.when(g >= 1)
            def _():
                wait_scatter(0)

    @pl.when(g == 0)
    def _():
        row_copies(0, 0, True, unrolled=False)
        yb[...] = jnp.zeros(yb.shape, yb.dtype)
        spare_tile0 = 2 * n_tok // SUBLANES

        def spare_copy(k):
            return pltpu.make_async_copy(
                yb, y_hbm.at[pl.ds(pl.multiple_of(spare_tile0 + k * tiles, tiles), tiles)], ssem.at[1])

        @pl.loop(0, n_classes)
        def _(k):
            spare_copy(k).start()

        @pl.loop(0, n_classes)
        def _(k):
            spare_copy(k).wait()

    for par in range(2):
        blk = 2 * g + par
        wg_ref, wu_ref, wd_ref = wrefs[par]
        changed = jnp.logical_or(blk == 0, be_ref[blk] != be_ref[jnp.maximum(blk - 1, 0)])

        @pl.when(jnp.logical_and(changed, blk < nact))
        def _():
            wg_bf[...] = wg_ref[0].astype(BF16)
            wu_bf[...] = wu_ref[0].astype(BF16)
            wd_bf[...] = wd_ref[0].astype(BF16)

        @pl.when(blk < nact)
        def _():
            wait_gather(par)
            wait_scatter_before_reuse(par)
            row_copies(jnp.minimum(blk + 1, n_blocks - 1), 1 - par, True, unrolled=True)
            row_copies(jnp.maximum(blk - 1, 0), 1 - par, False, unrolled=True)
            x = xbufs[par][...].reshape(bm, d).astype(BF16)
            gate = jnp.dot(x, wg_bf[...], preferred_element_type=F32)
            up = jnp.dot(x, wu_bf[...], preferred_element_type=F32)
            h = (gate / (1.0 + jnp.exp(-gate))) * up
            y = jnp.dot(h.astype(BF16), wd_bf[...], preferred_element_type=F32)
            ybufs[par][...] = y.reshape(tiles, SUBLANES, d)

        @pl.when(blk == nact)
        def _():
            wait_gather(par)
            wait_scatter_before_reuse(par)
            row_copies(blk - 1, 1 - par, False, unrolled=False)
            wait_scatter(1 - par)


def _experts(x, slot_item, block_expert, n_active, w_gate, w_up, w_down, layer):
    t, d = x.shape
    _, e, _, f = w_gate.shape
    bm = EXPERT_BM
    n_blocks = slot_item.shape[0] // bm
    n_rows = 2 * t + e * bm
    assert t & (t - 1) == 0 and t % SUBLANES == 0 and bm % SUBLANES == 0 and n_blocks % 2 == 0
    est = 2 * 6 * d * f * 4 + 3 * d * f * 2 + 4 * bm * d * 4 + 8 * bm * d * 4

    def w_spec(shape, par):
        return pl.BlockSpec((None, 1) + shape, lambda i, be, it, na: (layer, be[2 * i + par], 0, 0))

    buf = pltpu.VMEM((bm // SUBLANES, SUBLANES, d), F32)
    grid_spec = pltpu.PrefetchScalarGridSpec(
        num_scalar_prefetch=3,
        grid=(n_blocks // 2,),
        in_specs=[pl.BlockSpec(memory_space=pl.ANY),
                  w_spec((d, f), 0), w_spec((d, f), 0), w_spec((f, d), 0),
                  w_spec((d, f), 1), w_spec((d, f), 1), w_spec((f, d), 1)],
        out_specs=pl.BlockSpec(memory_space=pl.ANY),
        scratch_shapes=[buf, buf, buf, buf,
                        pltpu.VMEM((d, f), BF16), pltpu.VMEM((d, f), BF16), pltpu.VMEM((f, d), BF16),
                        pltpu.SemaphoreType.DMA((2,)), pltpu.SemaphoreType.DMA((2,))],
    )
    y = pl.pallas_call(
        functools.partial(_expert_kernel, n_tok=t, n_classes=e),
        grid_spec=grid_spec,
        out_shape=jax.ShapeDtypeStruct((n_rows // SUBLANES, SUBLANES, d), F32),
        compiler_params=pltpu.CompilerParams(
            dimension_semantics=("arbitrary",), vmem_limit_bytes=_vmem_limit(est)),
        name="experts",
    )(block_expert, slot_item, n_active, x.reshape(t // SUBLANES, SUBLANES, d),
      w_gate, w_up, w_down, w_gate, w_up, w_down)
    return y.reshape(n_rows, d)


def _combine_kernel(x_ref, y0_ref, y1_ref, gate_ref, g_ref, b_ref, o_ref):
    gate = gate_ref[...]
    y = y0_ref[...] * gate[:, 0:1] + y1_ref[...] * gate[:, 1:2]
    z = DEEPNORM_ALPHA * x_ref[...] + y
    o_ref[...] = _layer_norm_rows(z, g_ref[...], b_ref[...])


def _combine_ln(x, y_items, gate_t, ln_g, ln_b):
    t, d = x.shape
    bt = COMBINE_BT
    nblk = t // bt
    return pl.pallas_call(
        _combine_kernel,
        grid=(nblk,),
        in_specs=[pl.BlockSpec((bt, d), lambda i: (i, 0)),
                  pl.BlockSpec((bt, d), lambda i: (i, 0)),
                  pl.BlockSpec((bt, d), lambda i: (nblk + i, 0)),
                  pl.BlockSpec((bt, 2), lambda i: (i, 0)),
                  pl.BlockSpec((1, d), lambda i: (0, 0)),
                  pl.BlockSpec((1, d), lambda i: (0, 0))],
        out_specs=pl.BlockSpec((bt, d), lambda i: (i, 0)),
        out_shape=jax.ShapeDtypeStruct((t, d), F32),
        compiler_params=pltpu.CompilerParams(
            dimension_semantics=("parallel",), vmem_limit_bytes=_vmem_limit(10 * bt * d * 4)),
        name="combine_ln",
    )(x, y_items, y_items, gate_t, ln_g.reshape(1, d), ln_b.reshape(1, d))


def _moe_residual_ln(x, w_router_t, b_router, w_gate, w_up, w_down, layer, ln_g, ln_b):
    eidx, gate = _router(x, w_router_t, b_router)
    slot_item, block_expert, n_active = _route_slots(eidx, EXPERT_BM)
    y_items = _experts(x, slot_item, block_expert, n_active, w_gate, w_up, w_down, layer)
    return _combine_ln(x, y_items, gate.T, ln_g, ln_b)


def _na_bias_tables(rpb, rows):
    w = GRID_W
    kc = min(WIN_COLS, w)
    kr = min(WIN_ROWS, rows)
    cols = np.arange(w)
    col_start = np.clip(cols - kc // 2, 0, w - kc)
    col_mask = (cols[None, :] >= col_start[:, None]) & (cols[None, :] < col_start[:, None] + kc)
    col_idx = np.clip(cols[None, :] - cols[:, None] + WIN_COLS - 1, 0, 2 * WIN_COLS - 2)
    tb = rpb.astype(F32)[:, :, col_idx]
    tb = jnp.where(col_mask[None, None], tb, NEG_BIAS)
    n_ri = 2 * WIN_ROWS - 1
    tb = jnp.concatenate([tb, jnp.full((N_HEADS, 1, w, w), NEG_BIAS, F32)], axis=1)
    n_blocks = rows // NA_RQ
    ri = np.full((3, NA_RQ, NA_KROWS), n_ri, np.int32)
    for ty, blk in enumerate((0, 1, n_blocks - 1)):
        ks = int(np.clip(blk - 1, 0, n_blocks - 3)) * NA_RQ
        for rq in range(NA_RQ):
            r = blk * NA_RQ + rq
            rs = int(np.clip(r - kr // 2, 0, rows - kr))
            for j in range(NA_KROWS):
                key_row = ks + j
                if rs <= key_row < rs + kr:
                    ri[ty, rq, j] = key_row - r + WIN_ROWS - 1
    bt = tb[:, ri]
    return bt.transpose(1, 0, 2, 4, 3, 5).reshape(3, N_HEADS, NA_RQ * w, NA_KROWS * w)


def _na_kernel(q_ref, k0_ref, k1_ref, k2_ref, v0_ref, v1_ref, v2_ref, bias_ref, o_ref):
    dh = HEAD_DIM
    nt = (((1,), (1,)), ((), ()))
    scale = HEAD_DIM ** -0.5
    outs = []
    for h in range(NA_HEADS_PER_STEP):
        cols = slice(h * dh, (h + 1) * dh)
        q = q_ref[0, :, cols] * scale
        k = jnp.concatenate([k0_ref[0, :, cols], k1_ref[0, :, cols], k2_ref[0, :, cols]], axis=0)
        v = jnp.concatenate([v0_ref[0, :, cols], v1_ref[0, :, cols], v2_ref[0, :, cols]], axis=0)
        s = lax.dot_general(q, k, nt, preferred_element_type=F32) + bias_ref[0, h]
        m = jnp.max(s, axis=-1, keepdims=True)
        p = jnp.exp(s - m)
        l = jnp.sum(p, axis=-1, keepdims=True)
        o = jnp.dot(p.astype(BF16), v, preferred_element_type=F32)
        outs.append(o / l)
    o_ref[0] = jnp.concatenate(outs, axis=-1).astype(o_ref.dtype)


def _neighbourhood_attention(qkv, bias, rows):
    b, s, _ = qkv.shape
    d = D_MODEL
    w = GRID_W
    tq = NA_RQ * w
    n_blocks = rows // NA_RQ
    hw = NA_HEADS_PER_STEP * HEAD_DIM
    n_hh = d // hw
    assert NA_KROWS * w == 3 * tq and n_blocks >= 3

    def kv_map(part, j):
        def index_map(hh, bi, i):
            return (bi, jnp.clip(i - 1, 0, n_blocks - 3) + j, part * n_hh + hh)
        return index_map

    def bias_map(hh, bi, i):
        ty = jnp.where(i == 0, 0, jnp.where(i == n_blocks - 1, 2, 1))
        return (ty, hh, 0, 0)

    blk = (1, tq, hw)
    est = 2 * (7 * tq * hw * 2 + NA_HEADS_PER_STEP * tq * 3 * tq * 4 + tq * hw * 2) + 8 * tq * 3 * tq * 4
    return pl.pallas_call(
        _na_kernel,
        grid=(n_hh, b, n_blocks),
        in_specs=[pl.BlockSpec(blk, lambda hh, bi, i: (bi, i, hh))]
                 + [pl.BlockSpec(blk, kv_map(1, j)) for j in range(3)]
                 + [pl.BlockSpec(blk, kv_map(2, j)) for j in range(3)]
                 + [pl.BlockSpec((1, NA_HEADS_PER_STEP, tq, 3 * tq), bias_map)],
        out_specs=pl.BlockSpec(blk, lambda hh, bi, i: (bi, i, hh)),
        out_shape=jax.ShapeDtypeStruct((b, s, d), BF16),
        compiler_params=pltpu.CompilerParams(
            dimension_semantics=("parallel", "parallel", "parallel"), vmem_limit_bytes=_vmem_limit(est)),
        name="neighbourhood_attention",
    )(qkv, qkv, qkv, qkv, qkv, qkv, qkv, bias)


def kernel(x, fourier_w_in, fourier_w_out, na_w_qkv, na_rpb, na_w_out, router_w, router_b,
           expert_w_gate, expert_w_up, expert_w_down, ln_g, ln_b):
    b, s, d = x.shape
    t = b * s
    rows = s // GRID_W
    w_router_t = router_w.T
    xt = x.reshape(t, d)

    f = _fourier_mixer_pre_out(x, fourier_w_in[0])
    xt = _proj_residual_ln(f.reshape(t, d), fourier_w_out[0].astype(BF16), xt, ln_g[0, 0], ln_b[0, 0])
    xt = _moe_residual_ln(xt, w_router_t, router_b, expert_w_gate, expert_w_up, expert_w_down, 0,
                          ln_g[0, 1], ln_b[0, 1])

    qkv = _matmul(xt, na_w_qkv[0].astype(BF16), BF16, MM_BM, 1024).reshape(b, s, 3 * d)
    bias = _na_bias_tables(na_rpb[0], rows)
    o = _neighbourhood_attention(qkv, bias, rows)
    xt = _proj_residual_ln(o.reshape(t, d), na_w_out[0].astype(BF16), xt, ln_g[1, 0], ln_b[1, 0])
    xt = _moe_residual_ln(xt, w_router_t, router_b, expert_w_gate, expert_w_up, expert_w_down, 1,
                          ln_g[1, 1], ln_b[1, 1])
    return xt.reshape(b, s, d)
```

```python
import functools

import jax
import jax.numpy as jnp
import numpy as np
from jax import lax
from jax.experimental import pallas as pl
from jax.experimental.pallas import tpu as pltpu

F32 = jnp.float32
BF16 = jnp.bfloat16
I32 = jnp.int32

D_MODEL = 1024
GRID_W = 64
N_FOURIER_GROUPS = 4
FOURIER_GROUP_DIM = D_MODEL // N_FOURIER_GROUPS
N_HEADS = 16
HEAD_DIM = D_MODEL // N_HEADS
WIN_ROWS = 8
WIN_COLS = 16
N_EXPERTS = 32
N_GROUPS = 4
EXPERTS_PER_GROUP = N_EXPERTS // N_GROUPS
D_EXPERT = D_MODEL // 2
DEPTH = 2
DEEPNORM_ALPHA = (2 * DEPTH) ** 0.25
LN_EPS = 1e-5

V7X_VMEM_BYTES = 64 * 1024 * 1024
LANES = 128

FFT_N1 = 64
FFT_N2 = 128
FFT_CHUNK = 128

MM_BM = 1024
LN_BM = 512
ROUTER_BT = 1024
EXPERT_BM = 256
COMBINE_BT = 512
NA_RQ = 4
NA_KROWS = NA_RQ + WIN_ROWS
NA_HEADS_PER_STEP = 8
NEG_BIAS = -1e30


def _vmem_limit(nbytes):
    return int(min(max(nbytes, 32 * 1024 * 1024), V7X_VMEM_BYTES - 8 * 1024 * 1024))


def _mm_kernel(a_ref, b_ref, o_ref, *, precision):
    if precision is None:
        a = a_ref[...].astype(BF16)
        b = b_ref[...].astype(BF16)
        acc = jnp.dot(a, b, preferred_element_type=F32)
    else:
        acc = jnp.dot(a_ref[...], b_ref[...], preferred_element_type=F32, precision=precision)
    o_ref[...] = acc.astype(o_ref.dtype)


def _matmul(a, b, out_dtype, bm, bn, precision=None):
    m, k = a.shape
    _, n = b.shape
    est = 2 * (bm * k * a.dtype.itemsize + k * bn * b.dtype.itemsize + bm * bn * 4) + 3 * bm * bn * 4
    return pl.pallas_call(
        functools.partial(_mm_kernel, precision=precision),
        grid=(m // bm, n // bn),
        in_specs=[pl.BlockSpec((bm, k), lambda i, j: (i, 0)),
                  pl.BlockSpec((k, bn), lambda i, j: (0, j))],
        out_specs=pl.BlockSpec((bm, bn), lambda i, j: (i, j)),
        out_shape=jax.ShapeDtypeStruct((m, n), out_dtype),
        compiler_params=pltpu.CompilerParams(
            dimension_semantics=("parallel", "parallel"), vmem_limit_bytes=_vmem_limit(est)),
        name="matmul",
    )(a, b)


def _fourier_channel_tables():
    n = FOURIER_GROUP_DIM
    c = np.arange(n)
    ang = 2.0 * np.pi * ((c[:, None] * c[None, :]) % n) / n
    cos, sin = np.cos(ang) / np.sqrt(n), np.sin(ang) / np.sqrt(n)
    half = FFT_CHUNK
    tabs = [np.concatenate([cos[:, h * half:(h + 1) * half], sin[:, h * half:(h + 1) * half]], axis=1)
            for h in range(n // half)]
    return np.stack(tabs).astype(np.float32)


def _fourier_seq_tables(seq):
    n1, n2 = FFT_N1, FFT_N2
    assert n1 * n2 == seq
    k2 = np.arange(n2)
    s2 = np.arange(n2)
    m1 = np.empty((n1, 2 * n2, 2 * n2), np.float32)
    for s1 in range(n1):
        ang = 2.0 * np.pi * ((k2[:, None] * (s1 + n1 * s2[None, :])) % seq) / seq
        mr, mi = np.cos(ang) / np.sqrt(n2), np.sin(ang) / np.sqrt(n2)
        m1[s1] = np.block([[mr, -mi], [mi, mr]])
    k1 = np.arange(n1)
    ang = 2.0 * np.pi * ((k1[:, None] * k1[None, :]) % n1) / n1
    w2 = np.concatenate([np.cos(ang), -np.sin(ang)], axis=1) / np.sqrt(n1)
    return m1, w2.astype(np.float32)


def _fft_kernel(a_ref, b_ref, m1_ref, w2_ref, o_ref, zs_ref):
    n1, n2, c = FFT_N1, FFT_N2, FFT_CHUNK

    def stage1(s1, carry):
        rows = pl.ds(s1, n2, stride=n1)
        x = jnp.concatenate([a_ref[0, rows, :], b_ref[0, rows, :]], axis=0).astype(BF16)
        z = jnp.dot(m1_ref[s1], x, preferred_element_type=F32)
        zs_ref[pl.ds(pl.multiple_of(s1 * 2 * n2, 2 * n2), 2 * n2), :] = z
        return carry

    lax.fori_loop(0, n1, stage1, 0, unroll=4)

    def stage2(kk, carry):
        k2 = 2 * kk
        parts = []
        for d in range(2):
            zr = zs_ref[pl.ds(k2 + d, n1, stride=2 * n2), :]
            zi = zs_ref[pl.ds(n2 + k2 + d, n1, stride=2 * n2), :]
            parts.append(jnp.concatenate([zr, zi], axis=0))
        z = jnp.concatenate(parts, axis=1).astype(BF16)
        y = jnp.dot(w2_ref[...], z, preferred_element_type=F32)
        o_ref[0, pl.ds(k2, n1, stride=n2), :] = y[:, :c]
        o_ref[0, pl.ds(k2 + 1, n1, stride=n2), :] = y[:, c:]
        return carry

    lax.fori_loop(0, n2 // 2, stage2, 0, unroll=4)


def _seq_fft(ab, m1, w2):
    b, s, two_d = ab.shape
    d = two_d // 2
    c = FFT_CHUNK
    est = 2 * (s * 2 * c * 4 + m1.size * 2 + s * c * 4) + FFT_N1 * 2 * FFT_N2 * c * 4 + (4 << 20)
    return pl.pallas_call(
        _fft_kernel,
        grid=(b, d // c),
        in_specs=[pl.BlockSpec((1, s, c), lambda i, j: (i, 0, 2 * j)),
                  pl.BlockSpec((1, s, c), lambda i, j: (i, 0, 2 * j + 1)),
                  pl.BlockSpec(m1.shape, lambda i, j: (0, 0, 0)),
                  pl.BlockSpec(w2.shape, lambda i, j: (0, 0))],
        out_specs=pl.BlockSpec((1, s, c), lambda i, j: (i, 0, j)),
        out_shape=jax.ShapeDtypeStruct((b, s, d), F32),
        scratch_shapes=[pltpu.VMEM((FFT_N1 * 2 * FFT_N2, c), F32)],
        compiler_params=pltpu.CompilerParams(
            dimension_semantics=("parallel", "parallel"), vmem_limit_bytes=_vmem_limit(est)),
        name="seq_fft",
    )(ab, ab, m1, w2)


def _fourier_mixer_pre_out(x, w_in):
    b, s, d = x.shape
    g, gd, c = N_FOURIER_GROUPS, FOURIER_GROUP_DIM, FFT_CHUNK
    halves = gd // c
    cs = jnp.asarray(_fourier_channel_tables())
    w_ab = pl.pallas_call(
        functools.partial(_mm_kernel, precision=lax.Precision.HIGHEST),
        grid=(g, halves),
        in_specs=[pl.BlockSpec((d, gd), lambda i, h: (0, i)),
                  pl.BlockSpec((None, gd, 2 * c), lambda i, h: (h, 0, 0))],
        out_specs=pl.BlockSpec((d, 2 * c), lambda i, h: (0, i * halves + h)),
        out_shape=jax.ShapeDtypeStruct((d, 2 * d), BF16),
        name="fourier_weight_fold",
    )(w_in, cs)
    ab = _matmul(x.reshape(b * s, d), w_ab, F32, MM_BM, 1024).reshape(b, s, 2 * d)
    m1, w2 = _fourier_seq_tables(s)
    return _seq_fft(ab, jnp.asarray(m1, dtype=BF16), jnp.asarray(w2, dtype=BF16))


def _layer_norm_rows(z, g, b):
    mu = jnp.mean(z, axis=-1, keepdims=True)
    zc = z - mu
    var = jnp.mean(zc * zc, axis=-1, keepdims=True)
    return zc * lax.rsqrt(var + LN_EPS) * g + b


def _pack_bf16_pairs(z):
    m = z.shape[1] // 2
    zb = z.astype(BF16).astype(F32)
    bits = lax.bitcast_convert_type(zb, jnp.uint32)
    return jnp.bitwise_or(bits[:, m:], lax.shift_right_logical(bits[:, :m], jnp.uint32(16)))


def _unpack_bf16_pairs(w):
    lo = lax.bitcast_convert_type(lax.shift_left(w, jnp.uint32(16)), F32)
    hi = lax.bitcast_convert_type(jnp.bitwise_and(w, jnp.uint32(0xFFFF0000)), F32)
    return jnp.concatenate([lo, hi], axis=1)


def _mm_ln_kernel(a_ref, w_ref, x_ref, g_ref, b_ref, o_ref, opk_ref):
    y = jnp.dot(a_ref[...].astype(BF16), w_ref[...], preferred_element_type=F32)
    z = DEEPNORM_ALPHA * x_ref[...] + y
    out = _layer_norm_rows(z, g_ref[...], b_ref[...])
    o_ref[...] = out
    opk_ref[...] = _pack_bf16_pairs(out)


def _proj_residual_ln(a, w_bf16, x, ln_g, ln_b):
    t, k = a.shape
    d = w_bf16.shape[1]
    bm = LN_BM
    est = 2 * (bm * k * a.dtype.itemsize + k * d * 2 + 3 * bm * d * 4) + 6 * bm * d * 4
    return pl.pallas_call(
        _mm_ln_kernel,
        grid=(t // bm,),
        in_specs=[pl.BlockSpec((bm, k), lambda i: (i, 0)),
                  pl.BlockSpec((k, d), lambda i: (0, 0)),
                  pl.BlockSpec((bm, d), lambda i: (i, 0)),
                  pl.BlockSpec((1, d), lambda i: (0, 0)),
                  pl.BlockSpec((1, d), lambda i: (0, 0))],
        out_specs=[pl.BlockSpec((bm, d), lambda i: (i, 0)),
                   pl.BlockSpec((bm, d // 2), lambda i: (i, 0))],
        out_shape=[jax.ShapeDtypeStruct((t, d), F32), jax.ShapeDtypeStruct((t, d // 2), jnp.uint32)],
        compiler_params=pltpu.CompilerParams(
            dimension_semantics=("parallel",), vmem_limit_bytes=_vmem_limit(est)),
        name="proj_residual_ln",
    )(a, w_bf16, x, ln_g.reshape(1, d), ln_b.reshape(1, d))


def _top2_rows(v, iota):
    n_rows = v.shape[0]
    m1 = jnp.max(v, axis=0, keepdims=True)
    i1 = jnp.min(jnp.where(v == m1, iota, n_rows), axis=0, keepdims=True)
    v2 = jnp.where(iota == i1, -jnp.inf, v)
    m2 = jnp.max(v2, axis=0, keepdims=True)
    i2 = jnp.min(jnp.where(v2 == m2, iota, n_rows), axis=0, keepdims=True)
    return m1, i1, m2, i2


def _router_kernel(x_ref, wt_ref, b_ref, eidx_ref, gate_ref):
    x = x_ref[...]
    w = wt_ref[...]
    xh = x.astype(BF16)
    xl = (x - xh.astype(F32)).astype(BF16)
    wh = w.astype(BF16)
    wl = (w - wh.astype(F32)).astype(BF16)
    nt = (((1,), (1,)), ((), ()))
    logits = (lax.dot_general(wh, xh, nt, preferred_element_type=F32)
              + lax.dot_general(wh, xl, nt, preferred_element_type=F32)
              + lax.dot_general(wl, xh, nt, preferred_element_type=F32))
    scores = 1.0 / (1.0 + jnp.exp(-logits))
    sel = scores + b_ref[...]
    epg = EXPERTS_PER_GROUP
    bt = x.shape[0]
    iota = lax.broadcasted_iota(I32, (epg, bt), 0)

    best = None
    for g in range(N_GROUPS):
        m1, _, m2, _ = _top2_rows(sel[g * epg:(g + 1) * epg], iota)
        gs = m1 + m2
        if best is None:
            best, gidx = gs, jnp.zeros((1, bt), I32)
        else:
            better = gs > best
            gidx = jnp.where(better, g, gidx)
            best = jnp.where(better, gs, best)

    sel_in = sel[0:epg]
    sc_in = scores[0:epg]
    for g in range(1, N_GROUPS):
        pick = gidx == g
        sel_in = jnp.where(pick, sel[g * epg:(g + 1) * epg], sel_in)
        sc_in = jnp.where(pick, scores[g * epg:(g + 1) * epg], sc_in)
    _, i1, _, i2 = _top2_rows(sel_in, iota)
    g1 = jnp.sum(jnp.where(iota == i1, sc_in, 0.0), axis=0, keepdims=True)
    g2 = jnp.sum(jnp.where(iota == i2, sc_in, 0.0), axis=0, keepdims=True)
    denom = g1 + g2
    eidx_ref[0:1, :] = gidx * epg + i1
    eidx_ref[1:2, :] = gidx * epg + i2
    gate_ref[0:1, :] = g1 / denom
    gate_ref[1:2, :] = g2 / denom


def _router(x, w_router_t, b_router):
    t, d = x.shape
    e = w_router_t.shape[0]
    bt = ROUTER_BT
    return pl.pallas_call(
        _router_kernel,
        grid=(t // bt,),
        in_specs=[pl.BlockSpec((bt, d), lambda i: (i, 0)),
                  pl.BlockSpec((e, d), lambda i: (0, 0)),
                  pl.BlockSpec((e, 1), lambda i: (0, 0))],
        out_specs=[pl.BlockSpec((2, bt), lambda i: (0, i)),
                   pl.BlockSpec((2, bt), lambda i: (0, i))],
        out_shape=[jax.ShapeDtypeStruct((2, t), I32), jax.ShapeDtypeStruct((2, t), F32)],
        compiler_params=pltpu.CompilerParams(dimension_semantics=("parallel",)),
        name="router",
    )(x, w_router_t, b_router.reshape(e, 1))


RANK_SUB = 256
RANK_NSUB = 8


def _rank_kernel(keys_ref, tri_ref, rank_ref, counts_ref, carry_ref, *, n_classes):
    nsub, sub = RANK_NSUB, RANK_SUB

    @pl.when(pl.program_id(0) == 0)
    def _():
        carry_ref[...] = jnp.zeros_like(carry_ref)

    cls = lax.broadcasted_iota(I32, (nsub, n_classes, sub), 1)
    onehot = cls == keys_ref[...]
    oh = jnp.where(onehot, 1.0, 0.0).reshape(nsub * n_classes, sub).astype(BF16)
    pref = jnp.dot(oh, tri_ref[...], preferred_element_type=F32).reshape(nsub, n_classes, sub)
    carry = carry_ref[...]
    for j in range(nsub):
        before = pref[j] + (carry - 1.0)
        rank_ref[j] = jnp.sum(jnp.where(onehot[j], before, 0.0), axis=0, keepdims=True).astype(I32)
        carry = carry + pref[j][:, sub - 1:sub]
    carry_ref[...] = carry
    counts_ref[...] = carry.astype(I32)


def _rank_within_class(keys, n_classes):
    n = keys.shape[0]
    nsub, sub = RANK_NSUB, RANK_SUB
    tri = jnp.asarray(np.triu(np.ones((sub, sub), np.float32)), dtype=BF16)
    rank, counts = pl.pallas_call(
        functools.partial(_rank_kernel, n_classes=n_classes),
        grid=(n // (nsub * sub),),
        in_specs=[pl.BlockSpec((nsub, 1, sub), lambda i: (i, 0, 0)),
                  pl.BlockSpec((sub, sub), lambda i: (0, 0))],
        out_specs=[pl.BlockSpec((nsub, 1, sub), lambda i: (i, 0, 0)),
                   pl.BlockSpec((n_classes, 1), lambda i: (0, 0))],
        out_shape=[jax.ShapeDtypeStruct((n // sub, 1, sub), I32),
                   jax.ShapeDtypeStruct((n_classes, 1), I32)],
        scratch_shapes=[pltpu.VMEM((n_classes, 1), F32)],
        compiler_params=pltpu.CompilerParams(dimension_semantics=("arbitrary",)),
        name="rank_within_class",
    )(keys.reshape(n // sub, 1, sub), tri)
    return rank.reshape(n), counts.reshape(n_classes)


FILL_CHUNK = 1024


def _fill_slots_kernel(dest_ref, cnt_ref, pstart_ref, pend_ref, slot_ref, *, n_items, bm):
    n_classes = cnt_ref.shape[0]
    n_slots = slot_ref.shape[0]
    step = pl.program_id(0)

    @pl.when(step == 0)
    def _():
        def pad_class(k, carry):
            def pad(p, c):
                slot_ref[p] = n_items + k * bm + jnp.bitwise_and(p, bm - 1)
                return c
            lax.fori_loop(pstart_ref[k] + cnt_ref[k], pend_ref[k], pad, 0)
            return carry

        lax.fori_loop(0, n_classes, pad_class, 0)

        def zero(p, carry):
            slot_ref[p] = 0
            return carry

        lax.fori_loop(pend_ref[n_classes - 1], n_slots, zero, 0)

    base = step * FILL_CHUNK

    def place(j, carry):
        slot_ref[dest_ref[j]] = base + j
        return carry

    lax.fori_loop(0, FILL_CHUNK, place, 0, unroll=16)


def _fill_slots(dest, counts, pad_start, pad_end, n_slots, bm):
    n_items = dest.shape[0]
    assert bm & (bm - 1) == 0 and n_items % FILL_CHUNK == 0
    smem = pl.BlockSpec(memory_space=pltpu.SMEM)
    return pl.pallas_call(
        functools.partial(_fill_slots_kernel, n_items=n_items, bm=bm),
        grid=(n_items // FILL_CHUNK,),
        in_specs=[pl.BlockSpec((FILL_CHUNK,), lambda i: (i,), memory_space=pltpu.SMEM), smem, smem, smem],
        out_specs=smem,
        out_shape=jax.ShapeDtypeStruct((n_slots,), I32),
        compiler_params=pltpu.CompilerParams(dimension_semantics=("arbitrary",)),
        name="fill_slots",
    )(dest, counts, pad_start.astype(I32), pad_end.astype(I32))


def _route_slots(eidx, bm):
    t = eidx.shape[1]
    a = 2 * t
    keys = eidx.reshape(a)
    rank, counts = _rank_within_class(keys, N_EXPERTS)
    padded = (counts + bm - 1) // bm * bm
    pad_end = jnp.cumsum(padded)
    pad_start = pad_end - padded
    dest = jnp.take(pad_start, keys) + rank
    p = a + (N_EXPERTS + 2) * bm
    slot_item = _fill_slots(dest, counts, pad_start, pad_end, p, bm)
    nb = p // bm
    block_start = jnp.arange(nb, dtype=I32) * bm
    block_expert = jnp.minimum(
        jnp.sum((pad_end[None, :] <= block_start[:, None]).astype(I32), axis=1), N_EXPERTS - 1)
    n_active = (pad_end[-1] // bm).astype(I32).reshape(1)
    return slot_item, block_expert, n_active


SUBLANES = 8
SUBLANE_SHIFT = SUBLANES.bit_length() - 1


def _expert_kernel(be_ref, item_ref, nact_ref, x_hbm, wg0_ref, wu0_ref, wd0_ref, wg1_ref, wu1_ref, wd1_ref,
                   y_hbm, xa, xb, ya, yb, wg_bf, wu_bf, wd_bf, gsem, ssem, *, n_tok, n_classes):
    bm = EXPERT_BM
    tiles = bm // SUBLANES
    dw = xa.shape[-1]
    g = pl.program_id(0)
    n_blocks = 2 * pl.num_programs(0)
    nact = nact_ref[0]
    xbufs, ybufs = (xa, xb), (ya, yb)
    wrefs = ((wg0_ref, wu0_ref, wd0_ref), (wg1_ref, wu1_ref, wd1_ref))

    def row_copies(blk, par, gather, unrolled):
        base = blk * bm

        def body(rt, carry):
            for u in range(SUBLANES):
                item = item_ref[base + rt * SUBLANES + u]
                if gather:
                    tok = jnp.bitwise_and(item, n_tok - 1)
                    pltpu.make_async_copy(
                        x_hbm.at[lax.shift_right_logical(tok, SUBLANE_SHIFT),
                                 pl.ds(jnp.bitwise_and(tok, SUBLANES - 1), 1), :],
                        xbufs[par].at[rt, pl.ds(u, 1), :], gsem.at[par]).start()
                else:
                    pltpu.make_async_copy(
                        ybufs[par].at[rt, pl.ds(u, 1), :],
                        y_hbm.at[lax.shift_right_logical(item, SUBLANE_SHIFT),
                                 pl.ds(jnp.bitwise_and(item, SUBLANES - 1), 1), :],
                        ssem.at[par]).start()
            return carry

        if unrolled:
            for rt in range(tiles):
                body(rt, 0)
        else:
            lax.fori_loop(0, tiles, body, 0)

    def wait_gather(par):
        pltpu.make_async_copy(x_hbm.at[pl.ds(0, tiles)], xbufs[par], gsem.at[par]).wait()

    def wait_scatter(par):
        pltpu.make_async_copy(ybufs[par], y_hbm.at[pl.ds(0, tiles)], ssem.at[par]).wait()

    def wait_scatter_before_reuse(par):
        if par == 1:
            wait_scatter(1)
        else:
            @pl.when(g >= 1)
            def _():
                wait_scatter(0)

    @pl.when(g == 0)
    def _():
        row_copies(0, 0, True, unrolled=False)
        yb[...] = jnp.zeros(yb.shape, yb.dtype)
        spare_tile0 = 2 * n_tok // SUBLANES

        def spare_copy(k):
            return pltpu.make_async_copy(
                yb, y_hbm.at[pl.ds(pl.multiple_of(spare_tile0 + k * tiles, tiles), tiles)], ssem.at[1])

        @pl.loop(0, n_classes)
        def _(k):
            spare_copy(k).start()

        @pl.loop(0, n_classes)
        def _(k):
            spare_copy(k).wait()

    for par in range(2):
        blk = 2 * g + par
        wg_ref, wu_ref, wd_ref = wrefs[par]
        changed = jnp.logical_or(blk == 0, be_ref[blk] != be_ref[jnp.maximum(blk - 1, 0)])

        @pl.when(jnp.logical_and(changed, blk < nact))
        def _():
            wg_bf[...] = wg_ref[0].astype(BF16)
            wu_bf[...] = wu_ref[0].astype(BF16)
            wd_bf[...] = wd_ref[0].astype(BF16)

        @pl.when(blk < nact)
        def _():
            wait_gather(par)
            wait_scatter_before_reuse(par)
            row_copies(jnp.minimum(blk + 1, n_blocks - 1), 1 - par, True, unrolled=True)
            row_copies(jnp.maximum(blk - 1, 0), 1 - par, False, unrolled=True)
            x = _unpack_bf16_pairs(xbufs[par][...].reshape(bm, dw)).astype(BF16)
            gate = jnp.dot(x, wg_bf[...], preferred_element_type=F32)
            up = jnp.dot(x, wu_bf[...], preferred_element_type=F32)
            h = (gate / (1.0 + jnp.exp(-gate))) * up
            y = jnp.dot(h.astype(BF16), wd_bf[...], preferred_element_type=F32)
            ybufs[par][...] = _pack_bf16_pairs(y).reshape(tiles, SUBLANES, dw)

        @pl.when(blk == nact)
        def _():
            wait_gather(par)
            wait_scatter_before_reuse(par)
            row_copies(blk - 1, 1 - par, False, unrolled=False)
            wait_scatter(1 - par)


def _experts(x_pk, slot_item, block_expert, n_active, w_gate, w_up, w_down, layer):
    t, dw = x_pk.shape
    _, e, d, f = w_gate.shape
    bm = EXPERT_BM
    n_blocks = slot_item.shape[0] // bm
    n_rows = 2 * t + e * bm
    assert t & (t - 1) == 0 and t % SUBLANES == 0 and bm % SUBLANES == 0 and n_blocks % 2 == 0
    est = 2 * 6 * d * f * 4 + 3 * d * f * 2 + 4 * bm * d * 4 + 8 * bm * d * 4

    def w_spec(shape, par):
        return pl.BlockSpec((None, 1) + shape, lambda i, be, it, na: (layer, be[2 * i + par], 0, 0))

    buf = pltpu.VMEM((bm // SUBLANES, SUBLANES, dw), jnp.uint32)
    grid_spec = pltpu.PrefetchScalarGridSpec(
        num_scalar_prefetch=3,
        grid=(n_blocks // 2,),
        in_specs=[pl.BlockSpec(memory_space=pl.ANY),
                  w_spec((d, f), 0), w_spec((d, f), 0), w_spec((f, d), 0),
                  w_spec((d, f), 1), w_spec((d, f), 1), w_spec((f, d), 1)],
        out_specs=pl.BlockSpec(memory_space=pl.ANY),
        scratch_shapes=[buf, buf, buf, buf,
                        pltpu.VMEM((d, f), BF16), pltpu.VMEM((d, f), BF16), pltpu.VMEM((f, d), BF16),
                        pltpu.SemaphoreType.DMA((2,)), pltpu.SemaphoreType.DMA((2,))],
    )
    y = pl.pallas_call(
        functools.partial(_expert_kernel, n_tok=t, n_classes=e),
        grid_spec=grid_spec,
        out_shape=jax.ShapeDtypeStruct((n_rows // SUBLANES, SUBLANES, dw), jnp.uint32),
        compiler_params=pltpu.CompilerParams(
            dimension_semantics=("arbitrary",), vmem_limit_bytes=_vmem_limit(est)),
        name="experts",
    )(block_expert, slot_item, n_active, x_pk.reshape(t // SUBLANES, SUBLANES, dw),
      w_gate, w_up, w_down, w_gate, w_up, w_down)
    return y.reshape(n_rows, dw)


def _combine_kernel(x_ref, y0_ref, y1_ref, gate_ref, g_ref, b_ref, o_ref):
    gate = gate_ref[...]
    y = _unpack_bf16_pairs(y0_ref[...]) * gate[:, 0:1] + _unpack_bf16_pairs(y1_ref[...]) * gate[:, 1:2]
    z = DEEPNORM_ALPHA * x_ref[...] + y
    o_ref[...] = _layer_norm_rows(z, g_ref[...], b_ref[...])


def _combine_ln(x, y_items, gate_t, ln_g, ln_b):
    t, d = x.shape
    bt = COMBINE_BT
    nblk = t // bt
    return pl.pallas_call(
        _combine_kernel,
        grid=(nblk,),
        in_specs=[pl.BlockSpec((bt, d), lambda i: (i, 0)),
                  pl.BlockSpec((bt, d // 2), lambda i: (i, 0)),
                  pl.BlockSpec((bt, d // 2), lambda i: (nblk + i, 0)),
                  pl.BlockSpec((bt, 2), lambda i: (i, 0)),
                  pl.BlockSpec((1, d), lambda i: (0, 0)),
                  pl.BlockSpec((1, d), lambda i: (0, 0))],
        out_specs=pl.BlockSpec((bt, d), lambda i: (i, 0)),
        out_shape=jax.ShapeDtypeStruct((t, d), F32),
        compiler_params=pltpu.CompilerParams(
            dimension_semantics=("parallel",), vmem_limit_bytes=_vmem_limit(10 * bt * d * 4)),
        name="combine_ln",
    )(x, y_items, y_items, gate_t, ln_g.reshape(1, d), ln_b.reshape(1, d))


def _moe_residual_ln(x, x_pk, w_router_t, b_router, w_gate, w_up, w_down, layer, ln_g, ln_b):
    eidx, gate = _router(x, w_router_t, b_router)
    slot_item, block_expert, n_active = _route_slots(eidx, EXPERT_BM)
    y_items = _experts(x_pk, slot_item, block_expert, n_active, w_gate, w_up, w_down, layer)
    return _combine_ln(x, y_items, gate.T, ln_g, ln_b)


def _na_bias_tables(rpb, rows):
    w = GRID_W
    kc = min(WIN_COLS, w)
    kr = min(WIN_ROWS, rows)
    cols = np.arange(w)
    col_start = np.clip(cols - kc // 2, 0, w - kc)
    col_mask = (cols[None, :] >= col_start[:, None]) & (cols[None, :] < col_start[:, None] + kc)
    col_idx = np.clip(cols[None, :] - cols[:, None] + WIN_COLS - 1, 0, 2 * WIN_COLS - 2)
    tb = rpb.astype(F32)[:, :, col_idx]
    tb = jnp.where(col_mask[None, None], tb, NEG_BIAS)
    n_ri = 2 * WIN_ROWS - 1
    tb = jnp.concatenate([tb, jnp.full((N_HEADS, 1, w, w), NEG_BIAS, F32)], axis=1)
    n_blocks = rows // NA_RQ
    ri = np.full((3, NA_RQ, NA_KROWS), n_ri, np.int32)
    for ty, blk in enumerate((0, 1, n_blocks - 1)):
        ks = int(np.clip(blk - 1, 0, n_blocks - 3)) * NA_RQ
        for rq in range(NA_RQ):
            r = blk * NA_RQ + rq
            rs = int(np.clip(r - kr // 2, 0, rows - kr))
            for j in range(NA_KROWS):
                key_row = ks + j
                if rs <= key_row < rs + kr:
                    ri[ty, rq, j] = key_row - r + WIN_ROWS - 1
    bt = tb[:, ri]
    return bt.transpose(1, 0, 2, 4, 3, 5).reshape(3, N_HEADS, NA_RQ * w, NA_KROWS * w)


def _na_kernel(q_ref, k0_ref, k1_ref, k2_ref, v0_ref, v1_ref, v2_ref, bias_ref, o_ref):
    dh = HEAD_DIM
    nt = (((1,), (1,)), ((), ()))
    scale = HEAD_DIM ** -0.5
    outs = []
    for h in range(NA_HEADS_PER_STEP):
        cols = slice(h * dh, (h + 1) * dh)
        q = q_ref[0, :, cols] * scale
        k = jnp.concatenate([k0_ref[0, :, cols], k1_ref[0, :, cols], k2_ref[0, :, cols]], axis=0)
        v = jnp.concatenate([v0_ref[0, :, cols], v1_ref[0, :, cols], v2_ref[0, :, cols]], axis=0)
        s = lax.dot_general(q, k, nt, preferred_element_type=F32) + bias_ref[0, h]
        m = jnp.max(s, axis=-1, keepdims=True)
        p = jnp.exp(s - m)
        l = jnp.sum(p, axis=-1, keepdims=True)
        o = jnp.dot(p.astype(BF16), v, preferred_element_type=F32)
        outs.append(o / l)
    o_ref[0] = jnp.concatenate(outs, axis=-1).astype(o_ref.dtype)


def _neighbourhood_attention(qkv, bias, rows):
    b, s, _ = qkv.shape
    d = D_MODEL
    w = GRID_W
    tq = NA_RQ * w
    n_blocks = rows // NA_RQ
    hw = NA_HEADS_PER_STEP * HEAD_DIM
    n_hh = d // hw
    assert NA_KROWS * w == 3 * tq and n_blocks >= 3

    def kv_map(part, j):
        def index_map(hh, bi, i):
            return (bi, jnp.clip(i - 1, 0, n_blocks - 3) + j, part * n_hh + hh)
        return index_map

    def bias_map(hh, bi, i):
        ty = jnp.where(i == 0, 0, jnp.where(i == n_blocks - 1, 2, 1))
        return (ty, hh, 0, 0)

    blk = (1, tq, hw)
    est = 2 * (7 * tq * hw * 2 + NA_HEADS_PER_STEP * tq * 3 * tq * 4 + tq * hw * 2) + 8 * tq * 3 * tq * 4
    return pl.pallas_call(
        _na_kernel,
        grid=(n_hh, b, n_blocks),
        in_specs=[pl.BlockSpec(blk, lambda hh, bi, i: (bi, i, hh))]
                 + [pl.BlockSpec(blk, kv_map(1, j)) for j in range(3)]
                 + [pl.BlockSpec(blk, kv_map(2, j)) for j in range(3)]
                 + [pl.BlockSpec((1, NA_HEADS_PER_STEP, tq, 3 * tq), bias_map)],
        out_specs=pl.BlockSpec(blk, lambda hh, bi, i: (bi, i, hh)),
        out_shape=jax.ShapeDtypeStruct((b, s, d), BF16),
        compiler_params=pltpu.CompilerParams(
            dimension_semantics=("parallel", "parallel", "parallel"), vmem_limit_bytes=_vmem_limit(est)),
        name="neighbourhood_attention",
    )(qkv, qkv, qkv, qkv, qkv, qkv, qkv, bias)


def kernel(x, fourier_w_in, fourier_w_out, na_w_qkv, na_rpb, na_w_out, router_w, router_b,
           expert_w_gate, expert_w_up, expert_w_down, ln_g, ln_b):
    b, s, d = x.shape
    t = b * s
    rows = s // GRID_W
    w_router_t = router_w.T
    xt = x.reshape(t, d)

    f = _fourier_mixer_pre_out(x, fourier_w_in[0])
    xt, xt_pk = _proj_residual_ln(f.reshape(t, d), fourier_w_out[0].astype(BF16), xt, ln_g[0, 0], ln_b[0, 0])
    xt = _moe_residual_ln(xt, xt_pk, w_router_t, router_b, expert_w_gate, expert_w_up, expert_w_down, 0,
                          ln_g[0, 1], ln_b[0, 1])

    qkv = _matmul(xt, na_w_qkv[0].astype(BF16), BF16, MM_BM, 1024).reshape(b, s, 3 * d)
    bias = _na_bias_tables(na_rpb[0], rows)
    o = _neighbourhood_attention(qkv, bias, rows)
    xt, xt_pk = _proj_residual_ln(o.reshape(t, d), na_w_out[0].astype(BF16), xt, ln_g[1, 0], ln_b[1, 0])
    xt = _moe_residual_ln(xt, xt_pk, w_router_t, router_b, expert_w_gate, expert_w_up, expert_w_down, 1,
                          ln_g[1, 1], ln_b[1, 1])
    return xt.reshape(b, s, d)
```

```python
import functools

import jax
import jax.numpy as jnp
import numpy as np
from jax import lax
from jax.experimental import pallas as pl
from jax.experimental.pallas import tpu as pltpu

F32 = jnp.float32
BF16 = jnp.bfloat16
I32 = jnp.int32

D_MODEL = 1024
GRID_W = 64
N_FOURIER_GROUPS = 4
FOURIER_GROUP_DIM = D_MODEL // N_FOURIER_GROUPS
N_HEADS = 16
HEAD_DIM = D_MODEL // N_HEADS
WIN_ROWS = 8
WIN_COLS = 16
N_EXPERTS = 32
N_GROUPS = 4
EXPERTS_PER_GROUP = N_EXPERTS // N_GROUPS
D_EXPERT = D_MODEL // 2
DEPTH = 2
DEEPNORM_ALPHA = (2 * DEPTH) ** 0.25
LN_EPS = 1e-5

V7X_VMEM_BYTES = 64 * 1024 * 1024
LANES = 128

FFT_N1 = 64
FFT_N2 = 128
FFT_CHUNK = 128

MM_BM = 1024
LN_BM = 512
EXPERT_BM = 256
COMBINE_BT = 512
NA_RQ = 4
NA_KROWS = NA_RQ + WIN_ROWS
NA_HEADS_PER_STEP = 8
NEG_BIAS = -1e30


def _vmem_limit(nbytes):
    return int(min(max(nbytes, 32 * 1024 * 1024), V7X_VMEM_BYTES - 8 * 1024 * 1024))


def _mm_kernel(a_ref, b_ref, o_ref, *, precision):
    if precision is None:
        a = a_ref[...].astype(BF16)
        b = b_ref[...].astype(BF16)
        acc = jnp.dot(a, b, preferred_element_type=F32)
    else:
        acc = jnp.dot(a_ref[...], b_ref[...], preferred_element_type=F32, precision=precision)
    o_ref[...] = acc.astype(o_ref.dtype)


def _matmul(a, b, out_dtype, bm, bn, precision=None):
    m, k = a.shape
    _, n = b.shape
    est = 2 * (bm * k * a.dtype.itemsize + k * bn * b.dtype.itemsize + bm * bn * 4) + 3 * bm * bn * 4
    return pl.pallas_call(
        functools.partial(_mm_kernel, precision=precision),
        grid=(m // bm, n // bn),
        in_specs=[pl.BlockSpec((bm, k), lambda i, j: (i, 0)),
                  pl.BlockSpec((k, bn), lambda i, j: (0, j))],
        out_specs=pl.BlockSpec((bm, bn), lambda i, j: (i, j)),
        out_shape=jax.ShapeDtypeStruct((m, n), out_dtype),
        compiler_params=pltpu.CompilerParams(
            dimension_semantics=("parallel", "parallel"), vmem_limit_bytes=_vmem_limit(est)),
        name="matmul",
    )(a, b)


def _fourier_channel_tables():
    n = FOURIER_GROUP_DIM
    c = np.arange(n)
    ang = 2.0 * np.pi * ((c[:, None] * c[None, :]) % n) / n
    cos, sin = np.cos(ang) / np.sqrt(n), np.sin(ang) / np.sqrt(n)
    half = FFT_CHUNK
    tabs = [np.concatenate([cos[:, h * half:(h + 1) * half], sin[:, h * half:(h + 1) * half]], axis=1)
            for h in range(n // half)]
    return np.stack(tabs).astype(np.float32)


def _fourier_seq_tables(seq):
    n1, n2 = FFT_N1, FFT_N2
    assert n1 * n2 == seq
    k2 = np.arange(n2)
    s2 = np.arange(n2)
    m1 = np.empty((n1, 2 * n2, 2 * n2), np.float32)
    for s1 in range(n1):
        ang = 2.0 * np.pi * ((k2[:, None] * (s1 + n1 * s2[None, :])) % seq) / seq
        mr, mi = np.cos(ang) / np.sqrt(n2), np.sin(ang) / np.sqrt(n2)
        m1[s1] = np.block([[mr, -mi], [mi, mr]])
    k1 = np.arange(n1)
    ang = 2.0 * np.pi * ((k1[:, None] * k1[None, :]) % n1) / n1
    w2 = np.concatenate([np.cos(ang), -np.sin(ang)], axis=1) / np.sqrt(n1)
    return m1, w2.astype(np.float32)


def _fft_kernel(a_ref, b_ref, m1_ref, w2_ref, o_ref, zs_ref):
    n1, n2, c = FFT_N1, FFT_N2, FFT_CHUNK

    def stage1(s1, carry):
        rows = pl.ds(s1, n2, stride=n1)
        x = jnp.concatenate([a_ref[0, rows, :], b_ref[0, rows, :]], axis=0).astype(BF16)
        z = jnp.dot(m1_ref[s1], x, preferred_element_type=F32)
        zs_ref[pl.ds(pl.multiple_of(s1 * 2 * n2, 2 * n2), 2 * n2), :] = z
        return carry

    lax.fori_loop(0, n1, stage1, 0, unroll=8)

    def stage2(kk, carry):
        k2 = 2 * kk
        parts = []
        for d in range(2):
            zr = zs_ref[pl.ds(k2 + d, n1, stride=2 * n2), :]
            zi = zs_ref[pl.ds(n2 + k2 + d, n1, stride=2 * n2), :]
            parts.append(jnp.concatenate([zr, zi], axis=0))
        z = jnp.concatenate(parts, axis=1).astype(BF16)
        y = jnp.dot(w2_ref[...], z, preferred_element_type=F32)
        o_ref[0, pl.ds(k2, n1, stride=n2), :] = y[:, :c]
        o_ref[0, pl.ds(k2 + 1, n1, stride=n2), :] = y[:, c:]
        return carry

    lax.fori_loop(0, n2 // 2, stage2, 0, unroll=8)


def _seq_fft(ab, m1, w2):
    b, s, two_d = ab.shape
    d = two_d // 2
    c = FFT_CHUNK
    est = 2 * (s * 2 * c * 4 + m1.size * 2 + s * c * 4) + FFT_N1 * 2 * FFT_N2 * c * 4 + (4 << 20)
    return pl.pallas_call(
        _fft_kernel,
        grid=(b, d // c),
        in_specs=[pl.BlockSpec((1, s, c), lambda i, j: (i, 0, 2 * j)),
                  pl.BlockSpec((1, s, c), lambda i, j: (i, 0, 2 * j + 1)),
                  pl.BlockSpec(m1.shape, lambda i, j: (0, 0, 0)),
                  pl.BlockSpec(w2.shape, lambda i, j: (0, 0))],
        out_specs=pl.BlockSpec((1, s, c), lambda i, j: (i, 0, j)),
        out_shape=jax.ShapeDtypeStruct((b, s, d), F32),
        scratch_shapes=[pltpu.VMEM((FFT_N1 * 2 * FFT_N2, c), F32)],
        compiler_params=pltpu.CompilerParams(
            dimension_semantics=("parallel", "parallel"), vmem_limit_bytes=_vmem_limit(est)),
        name="seq_fft",
    )(ab, ab, m1, w2)


def _fourier_mixer_pre_out(x, w_in):
    b, s, d = x.shape
    g, gd, c = N_FOURIER_GROUPS, FOURIER_GROUP_DIM, FFT_CHUNK
    halves = gd // c
    cs = jnp.asarray(_fourier_channel_tables())
    w_ab = pl.pallas_call(
        functools.partial(_mm_kernel, precision=lax.Precision.HIGHEST),
        grid=(g, halves),
        in_specs=[pl.BlockSpec((d, gd), lambda i, h: (0, i)),
                  pl.BlockSpec((None, gd, 2 * c), lambda i, h: (h, 0, 0))],
        out_specs=pl.BlockSpec((d, 2 * c), lambda i, h: (0, i * halves + h)),
        out_shape=jax.ShapeDtypeStruct((d, 2 * d), BF16),
        name="fourier_weight_fold",
    )(w_in, cs)
    ab = _matmul(x.reshape(b * s, d), w_ab, F32, MM_BM, 1024).reshape(b, s, 2 * d)
    m1, w2 = _fourier_seq_tables(s)
    return _seq_fft(ab, jnp.asarray(m1, dtype=BF16), jnp.asarray(w2, dtype=BF16))


def _layer_norm_rows(z, g, b):
    mu = jnp.mean(z, axis=-1, keepdims=True)
    zc = z - mu
    var = jnp.mean(zc * zc, axis=-1, keepdims=True)
    return zc * lax.rsqrt(var + LN_EPS) * g + b


def _pack_bf16_pairs(z):
    m = z.shape[1] // 2
    zb = z.astype(BF16).astype(F32)
    bits = lax.bitcast_convert_type(zb, jnp.uint32)
    return jnp.bitwise_or(bits[:, m:], lax.shift_right_logical(bits[:, :m], jnp.uint32(16)))


def _unpack_bf16_pairs(w):
    lo = lax.bitcast_convert_type(lax.shift_left(w, jnp.uint32(16)), F32)
    hi = lax.bitcast_convert_type(jnp.bitwise_and(w, jnp.uint32(0xFFFF0000)), F32)
    return jnp.concatenate([lo, hi], axis=1)


def _mm_ln_kernel(a_ref, w_ref, x_ref, g_ref, b_ref, wr_ref, br_ref, o_ref, opk_ref, eidx_ref, gate_ref):
    y = jnp.dot(a_ref[...].astype(BF16), w_ref[...], preferred_element_type=F32)
    z = DEEPNORM_ALPHA * x_ref[...] + y
    out = _layer_norm_rows(z, g_ref[...], b_ref[...])
    o_ref[...] = out
    opk_ref[...] = _pack_bf16_pairs(out)
    eidx_ref[...], gate_ref[...] = _route_rows(out, wr_ref[...], br_ref[...])


def _proj_residual_ln(a, w_bf16, x, ln_g, ln_b, w_router_t, b_router):
    t, k = a.shape
    d = w_bf16.shape[1]
    e = w_router_t.shape[0]
    bm = LN_BM
    est = 2 * (bm * k * a.dtype.itemsize + k * d * 2 + 3 * bm * d * 4) + 8 * bm * d * 4
    return pl.pallas_call(
        _mm_ln_kernel,
        grid=(t // bm,),
        in_specs=[pl.BlockSpec((bm, k), lambda i: (i, 0)),
                  pl.BlockSpec((k, d), lambda i: (0, 0)),
                  pl.BlockSpec((bm, d), lambda i: (i, 0)),
                  pl.BlockSpec((1, d), lambda i: (0, 0)),
                  pl.BlockSpec((1, d), lambda i: (0, 0)),
                  pl.BlockSpec((e, d), lambda i: (0, 0)),
                  pl.BlockSpec((e, 1), lambda i: (0, 0))],
        out_specs=[pl.BlockSpec((bm, d), lambda i: (i, 0)),
                   pl.BlockSpec((bm, d // 2), lambda i: (i, 0)),
                   pl.BlockSpec((2, bm), lambda i: (0, i)),
                   pl.BlockSpec((2, bm), lambda i: (0, i))],
        out_shape=[jax.ShapeDtypeStruct((t, d), F32), jax.ShapeDtypeStruct((t, d // 2), jnp.uint32),
                   jax.ShapeDtypeStruct((2, t), I32), jax.ShapeDtypeStruct((2, t), F32)],
        compiler_params=pltpu.CompilerParams(
            dimension_semantics=("parallel",), vmem_limit_bytes=_vmem_limit(est)),
        name="proj_residual_ln",
    )(a, w_bf16, x, ln_g.reshape(1, d), ln_b.reshape(1, d), w_router_t, b_router.reshape(e, 1))


def _top2_rows(v, iota):
    n_rows = v.shape[0]
    m1 = jnp.max(v, axis=0, keepdims=True)
    i1 = jnp.min(jnp.where(v == m1, iota, n_rows), axis=0, keepdims=True)
    v2 = jnp.where(iota == i1, -jnp.inf, v)
    m2 = jnp.max(v2, axis=0, keepdims=True)
    i2 = jnp.min(jnp.where(v2 == m2, iota, n_rows), axis=0, keepdims=True)
    return m1, i1, m2, i2


def _route_rows(x, w, b):
    xh = x.astype(BF16)
    xl = (x - xh.astype(F32)).astype(BF16)
    wh = w.astype(BF16)
    wl = (w - wh.astype(F32)).astype(BF16)
    nt = (((1,), (1,)), ((), ()))
    logits = (lax.dot_general(wh, xh, nt, preferred_element_type=F32)
              + lax.dot_general(wh, xl, nt, preferred_element_type=F32)
              + lax.dot_general(wl, xh, nt, preferred_element_type=F32))
    scores = 1.0 / (1.0 + jnp.exp(-logits))
    sel = scores + b
    epg = EXPERTS_PER_GROUP
    bt = x.shape[0]
    iota = lax.broadcasted_iota(I32, (epg, bt), 0)

    best = None
    for g in range(N_GROUPS):
        m1, _, m2, _ = _top2_rows(sel[g * epg:(g + 1) * epg], iota)
        gs = m1 + m2
        if best is None:
            best, gidx = gs, jnp.zeros((1, bt), I32)
        else:
            better = gs > best
            gidx = jnp.where(better, g, gidx)
            best = jnp.where(better, gs, best)

    sel_in = sel[0:epg]
    sc_in = scores[0:epg]
    for g in range(1, N_GROUPS):
        pick = gidx == g
        sel_in = jnp.where(pick, sel[g * epg:(g + 1) * epg], sel_in)
        sc_in = jnp.where(pick, scores[g * epg:(g + 1) * epg], sc_in)
    _, i1, _, i2 = _top2_rows(sel_in, iota)
    g1 = jnp.sum(jnp.where(iota == i1, sc_in, 0.0), axis=0, keepdims=True)
    g2 = jnp.sum(jnp.where(iota == i2, sc_in, 0.0), axis=0, keepdims=True)
    denom = g1 + g2
    eidx = jnp.concatenate([gidx * epg + i1, gidx * epg + i2], axis=0)
    gates = jnp.concatenate([g1 / denom, g2 / denom], axis=0)
    return eidx, gates


RANK_SUB = 256
RANK_NSUB = 8


def _rank_kernel(keys_ref, tri_ref, rank_ref, counts_ref, carry_ref, *, n_classes):
    nsub, sub = RANK_NSUB, RANK_SUB

    @pl.when(pl.program_id(0) == 0)
    def _():
        carry_ref[...] = jnp.zeros_like(carry_ref)

    cls = lax.broadcasted_iota(I32, (nsub, n_classes, sub), 1)
    onehot = cls == keys_ref[...]
    oh = jnp.where(onehot, 1.0, 0.0).reshape(nsub * n_classes, sub).astype(BF16)
    pref = jnp.dot(oh, tri_ref[...], preferred_element_type=F32).reshape(nsub, n_classes, sub)
    carry = carry_ref[...]
    for j in range(nsub):
        before = pref[j] + (carry - 1.0)
        rank_ref[j] = jnp.sum(jnp.where(onehot[j], before, 0.0), axis=0, keepdims=True).astype(I32)
        carry = carry + pref[j][:, sub - 1:sub]
    carry_ref[...] = carry
    counts_ref[...] = carry.astype(I32)


def _rank_within_class(keys, n_classes):
    n = keys.shape[0]
    nsub, sub = RANK_NSUB, RANK_SUB
    tri = jnp.asarray(np.triu(np.ones((sub, sub), np.float32)), dtype=BF16)
    rank, counts = pl.pallas_call(
        functools.partial(_rank_kernel, n_classes=n_classes),
        grid=(n // (nsub * sub),),
        in_specs=[pl.BlockSpec((nsub, 1, sub), lambda i: (i, 0, 0)),
                  pl.BlockSpec((sub, sub), lambda i: (0, 0))],
        out_specs=[pl.BlockSpec((nsub, 1, sub), lambda i: (i, 0, 0)),
                   pl.BlockSpec((n_classes, 1), lambda i: (0, 0))],
        out_shape=[jax.ShapeDtypeStruct((n // sub, 1, sub), I32),
                   jax.ShapeDtypeStruct((n_classes, 1), I32)],
        scratch_shapes=[pltpu.VMEM((n_classes, 1), F32)],
        compiler_params=pltpu.CompilerParams(dimension_semantics=("arbitrary",)),
        name="rank_within_class",
    )(keys.reshape(n // sub, 1, sub), tri)
    return rank.reshape(n), counts.reshape(n_classes)


FILL_CHUNK = 1024


def _fill_slots_kernel(dest_ref, cnt_ref, pstart_ref, pend_ref, slot_ref, *, n_items, bm):
    n_classes = cnt_ref.shape[0]
    n_slots = slot_ref.shape[0]
    step = pl.program_id(0)

    @pl.when(step == 0)
    def _():
        def pad_class(k, carry):
            def pad(p, c):
                slot_ref[p] = n_items + k * bm + jnp.bitwise_and(p, bm - 1)
                return c
            lax.fori_loop(pstart_ref[k] + cnt_ref[k], pend_ref[k], pad, 0)
            return carry

        lax.fori_loop(0, n_classes, pad_class, 0)

        def zero(p, carry):
            slot_ref[p] = 0
            return carry

        lax.fori_loop(pend_ref[n_classes - 1], n_slots, zero, 0)

    base = step * FILL_CHUNK

    def place(j, carry):
        slot_ref[dest_ref[j]] = base + j
        return carry

    lax.fori_loop(0, FILL_CHUNK, place, 0, unroll=16)


def _fill_slots(dest, counts, pad_start, pad_end, n_slots, bm):
    n_items = dest.shape[0]
    assert bm & (bm - 1) == 0 and n_items % FILL_CHUNK == 0
    smem = pl.BlockSpec(memory_space=pltpu.SMEM)
    return pl.pallas_call(
        functools.partial(_fill_slots_kernel, n_items=n_items, bm=bm),
        grid=(n_items // FILL_CHUNK,),
        in_specs=[pl.BlockSpec((FILL_CHUNK,), lambda i: (i,), memory_space=pltpu.SMEM), smem, smem, smem],
        out_specs=smem,
        out_shape=jax.ShapeDtypeStruct((n_slots,), I32),
        compiler_params=pltpu.CompilerParams(dimension_semantics=("arbitrary",)),
        name="fill_slots",
    )(dest, counts, pad_start.astype(I32), pad_end.astype(I32))


def _route_slots(eidx, bm):
    t = eidx.shape[1]
    a = 2 * t
    keys = eidx.reshape(a)
    rank, counts = _rank_within_class(keys, N_EXPERTS)
    padded = (counts + bm - 1) // bm * bm
    pad_end = jnp.cumsum(padded)
    pad_start = pad_end - padded
    dest = jnp.take(pad_start, keys) + rank
    p = a + (N_EXPERTS + 2) * bm
    slot_item = _fill_slots(dest, counts, pad_start, pad_end, p, bm)
    nb = p // bm
    block_start = jnp.arange(nb, dtype=I32) * bm
    block_expert = jnp.minimum(
        jnp.sum((pad_end[None, :] <= block_start[:, None]).astype(I32), axis=1), N_EXPERTS - 1)
    n_active = (pad_end[-1] // bm).astype(I32).reshape(1)
    return slot_item, block_expert, n_active


SUBLANES = 8
SUBLANE_SHIFT = SUBLANES.bit_length() - 1


def _expert_kernel(be_ref, item_ref, nact_ref, x_hbm, wg0_ref, wu0_ref, wd0_ref, wg1_ref, wu1_ref, wd1_ref,
                   y_hbm, xa, xb, ya, yb, wg_bf, wu_bf, wd_bf, gsem, ssem, *, n_tok, n_classes):
    bm = EXPERT_BM
    tiles = bm // SUBLANES
    dw = xa.shape[-1]
    g = pl.program_id(0)
    n_blocks = 2 * pl.num_programs(0)
    nact = nact_ref[0]
    xbufs, ybufs = (xa, xb), (ya, yb)
    wrefs = ((wg0_ref, wu0_ref, wd0_ref), (wg1_ref, wu1_ref, wd1_ref))

    def row_copies(blk, par, gather, unrolled):
        base = blk * bm

        def body(rt, carry):
            for u in range(SUBLANES):
                item = item_ref[base + rt * SUBLANES + u]
                if gather:
                    tok = jnp.bitwise_and(item, n_tok - 1)
                    pltpu.make_async_copy(
                        x_hbm.at[lax.shift_right_logical(tok, SUBLANE_SHIFT),
                                 pl.ds(jnp.bitwise_and(tok, SUBLANES - 1), 1), :],
                        xbufs[par].at[rt, pl.ds(u, 1), :], gsem.at[par]).start()
                else:
                    pltpu.make_async_copy(
                        ybufs[par].at[rt, pl.ds(u, 1), :],
                        y_hbm.at[lax.shift_right_logical(item, SUBLANE_SHIFT),
                                 pl.ds(jnp.bitwise_and(item, SUBLANES - 1), 1), :],
                        ssem.at[par]).start()
            return carry

        if unrolled:
            for rt in range(tiles):
                body(rt, 0)
        else:
            lax.fori_loop(0, tiles, body, 0)

    def wait_gather(par):
        pltpu.make_async_copy(x_hbm.at[pl.ds(0, tiles)], xbufs[par], gsem.at[par]).wait()

    def wait_scatter(par):
        pltpu.make_async_copy(ybufs[par], y_hbm.at[pl.ds(0, tiles)], ssem.at[par]).wait()

    def wait_scatter_before_reuse(par):
        if par == 1:
            wait_scatter(1)
        else:
            @pl.when(g >= 1)
            def _():
                wait_scatter(0)

    @pl.when(g == 0)
    def _():
        row_copies(0, 0, True, unrolled=False)
        yb[...] = jnp.zeros(yb.shape, yb.dtype)
        spare_tile0 = 2 * n_tok // SUBLANES

        def spare_copy(k):
            return pltpu.make_async_copy(
                yb, y_hbm.at[pl.ds(pl.multiple_of(spare_tile0 + k * tiles, tiles), tiles)], ssem.at[1])

        @pl.loop(0, n_classes)
        def _(k):
            spare_copy(k).start()

        @pl.loop(0, n_classes)
        def _(k):
            spare_copy(k).wait()

    for par in range(2):
        blk = 2 * g + par
        wg_ref, wu_ref, wd_ref = wrefs[par]
        changed = jnp.logical_or(blk == 0, be_ref[blk] != be_ref[jnp.maximum(blk - 1, 0)])

        @pl.when(jnp.logical_and(changed, blk < nact))
        def _():
            wg_bf[...] = wg_ref[0].astype(BF16)
            wu_bf[...] = wu_ref[0].astype(BF16)
            wd_bf[...] = wd_ref[0].astype(BF16)

        @pl.when(blk < nact)
        def _():
            wait_gather(par)
            wait_scatter_before_reuse(par)
            row_copies(jnp.minimum(blk + 1, n_blocks - 1), 1 - par, True, unrolled=True)
            row_copies(jnp.maximum(blk - 1, 0), 1 - par, False, unrolled=True)
            x = _unpack_bf16_pairs(xbufs[par][...].reshape(bm, dw)).astype(BF16)
            gate = jnp.dot(x, wg_bf[...], preferred_element_type=F32)
            up = jnp.dot(x, wu_bf[...], preferred_element_type=F32)
            h = (gate / (1.0 + jnp.exp(-gate))) * up
            y = jnp.dot(h.astype(BF16), wd_bf[...], preferred_element_type=F32)
            ybufs[par][...] = _pack_bf16_pairs(y).reshape(tiles, SUBLANES, dw)

        @pl.when(blk == nact)
        def _():
            wait_gather(par)
            wait_scatter_before_reuse(par)
            row_copies(blk - 1, 1 - par, False, unrolled=False)
            wait_scatter(1 - par)


def _experts(x_pk, slot_item, block_expert, n_active, w_gate, w_up, w_down, layer):
    t, dw = x_pk.shape
    _, e, d, f = w_gate.shape
    bm = EXPERT_BM
    n_blocks = slot_item.shape[0] // bm
    n_rows = 2 * t + e * bm
    assert t & (t - 1) == 0 and t % SUBLANES == 0 and bm % SUBLANES == 0 and n_blocks % 2 == 0
    est = 2 * 6 * d * f * 4 + 3 * d * f * 2 + 4 * bm * d * 4 + 8 * bm * d * 4

    def w_spec(shape, par):
        return pl.BlockSpec((None, 1) + shape, lambda i, be, it, na: (layer, be[2 * i + par], 0, 0))

    buf = pltpu.VMEM((bm // SUBLANES, SUBLANES, dw), jnp.uint32)
    grid_spec = pltpu.PrefetchScalarGridSpec(
        num_scalar_prefetch=3,
        grid=(n_blocks // 2,),
        in_specs=[pl.BlockSpec(memory_space=pl.ANY),
                  w_spec((d, f), 0), w_spec((d, f), 0), w_spec((f, d), 0),
                  w_spec((d, f), 1), w_spec((d, f), 1), w_spec((f, d), 1)],
        out_specs=pl.BlockSpec(memory_space=pl.ANY),
        scratch_shapes=[buf, buf, buf, buf,
                        pltpu.VMEM((d, f), BF16), pltpu.VMEM((d, f), BF16), pltpu.VMEM((f, d), BF16),
                        pltpu.SemaphoreType.DMA((2,)), pltpu.SemaphoreType.DMA((2,))],
    )
    y = pl.pallas_call(
        functools.partial(_expert_kernel, n_tok=t, n_classes=e),
        grid_spec=grid_spec,
        out_shape=jax.ShapeDtypeStruct((n_rows // SUBLANES, SUBLANES, dw), jnp.uint32),
        compiler_params=pltpu.CompilerParams(
            dimension_semantics=("arbitrary",), vmem_limit_bytes=_vmem_limit(est)),
        name="experts",
    )(block_expert, slot_item, n_active, x_pk.reshape(t // SUBLANES, SUBLANES, dw),
      w_gate, w_up, w_down, w_gate, w_up, w_down)
    return y.reshape(n_rows, dw)


def _combine_kernel(x_ref, y0_ref, y1_ref, gate_ref, g_ref, b_ref, o_ref):
    gate = gate_ref[...]
    y = _unpack_bf16_pairs(y0_ref[...]) * gate[:, 0:1] + _unpack_bf16_pairs(y1_ref[...]) * gate[:, 1:2]
    z = DEEPNORM_ALPHA * x_ref[...] + y
    o_ref[...] = _layer_norm_rows(z, g_ref[...], b_ref[...])


def _combine_ln(x, y_items, gate_t, ln_g, ln_b):
    t, d = x.shape
    bt = COMBINE_BT
    nblk = t // bt
    return pl.pallas_call(
        _combine_kernel,
        grid=(nblk,),
        in_specs=[pl.BlockSpec((bt, d), lambda i: (i, 0)),
                  pl.BlockSpec((bt, d // 2), lambda i: (i, 0)),
                  pl.BlockSpec((bt, d // 2), lambda i: (nblk + i, 0)),
                  pl.BlockSpec((bt, 2), lambda i: (i, 0)),
                  pl.BlockSpec((1, d), lambda i: (0, 0)),
                  pl.BlockSpec((1, d), lambda i: (0, 0))],
        out_specs=pl.BlockSpec((bt, d), lambda i: (i, 0)),
        out_shape=jax.ShapeDtypeStruct((t, d), F32),
        compiler_params=pltpu.CompilerParams(
            dimension_semantics=("parallel",), vmem_limit_bytes=_vmem_limit(10 * bt * d * 4)),
        name="combine_ln",
    )(x, y_items, y_items, gate_t, ln_g.reshape(1, d), ln_b.reshape(1, d))


def _moe_residual_ln(x, x_pk, eidx, gate, w_gate, w_up, w_down, layer, ln_g, ln_b):
    slot_item, block_expert, n_active = _route_slots(eidx, EXPERT_BM)
    y_items = _experts(x_pk, slot_item, block_expert, n_active, w_gate, w_up, w_down, layer)
    return _combine_ln(x, y_items, gate.T, ln_g, ln_b)


def _na_bias_tables(rpb, rows):
    w = GRID_W
    kc = min(WIN_COLS, w)
    kr = min(WIN_ROWS, rows)
    cols = np.arange(w)
    col_start = np.clip(cols - kc // 2, 0, w - kc)
    col_mask = (cols[None, :] >= col_start[:, None]) & (cols[None, :] < col_start[:, None] + kc)
    col_idx = np.clip(cols[None, :] - cols[:, None] + WIN_COLS - 1, 0, 2 * WIN_COLS - 2)
    tb = rpb.astype(F32)[:, :, col_idx]
    tb = jnp.where(col_mask[None, None], tb, NEG_BIAS)
    n_ri = 2 * WIN_ROWS - 1
    tb = jnp.concatenate([tb, jnp.full((N_HEADS, 1, w, w), NEG_BIAS, F32)], axis=1)
    n_blocks = rows // NA_RQ
    ri = np.full((3, NA_RQ, NA_KROWS), n_ri, np.int32)
    for ty, blk in enumerate((0, 1, n_blocks - 1)):
        ks = int(np.clip(blk - 1, 0, n_blocks - 3)) * NA_RQ
        for rq in range(NA_RQ):
            r = blk * NA_RQ + rq
            rs = int(np.clip(r - kr // 2, 0, rows - kr))
            for j in range(NA_KROWS):
                key_row = ks + j
                if rs <= key_row < rs + kr:
                    ri[ty, rq, j] = key_row - r + WIN_ROWS - 1
    bt = tb[:, ri]
    return bt.transpose(1, 0, 2, 4, 3, 5).reshape(3, N_HEADS, NA_RQ * w, NA_KROWS * w)


def _na_kernel(q_ref, k0_ref, k1_ref, k2_ref, v0_ref, v1_ref, v2_ref, bias_ref, o_ref):
    dh = HEAD_DIM
    nt = (((1,), (1,)), ((), ()))
    scale = HEAD_DIM ** -0.5
    outs = []
    for h in range(NA_HEADS_PER_STEP):
        cols = slice(h * dh, (h + 1) * dh)
        q = q_ref[0, :, cols] * scale
        k = jnp.concatenate([k0_ref[0, :, cols], k1_ref[0, :, cols], k2_ref[0, :, cols]], axis=0)
        v = jnp.concatenate([v0_ref[0, :, cols], v1_ref[0, :, cols], v2_ref[0, :, cols]], axis=0)
        s = lax.dot_general(q, k, nt, preferred_element_type=F32) + bias_ref[0, h]
        m = jnp.max(s, axis=-1, keepdims=True)
        p = jnp.exp(s - m)
        l = jnp.sum(p, axis=-1, keepdims=True)
        o = jnp.dot(p.astype(BF16), v, preferred_element_type=F32)
        outs.append(o / l)
    o_ref[0] = jnp.concatenate(outs, axis=-1).astype(o_ref.dtype)


def _neighbourhood_attention(qkv, bias, rows):
    b, s, _ = qkv.shape
    d = D_MODEL
    w = GRID_W
    tq = NA_RQ * w
    n_blocks = rows // NA_RQ
    hw = NA_HEADS_PER_STEP * HEAD_DIM
    n_hh = d // hw
    assert NA_KROWS * w == 3 * tq and n_blocks >= 3

    def kv_map(part, j):
        def index_map(hh, bi, i):
            return (bi, jnp.clip(i - 1, 0, n_blocks - 3) + j, part * n_hh + hh)
        return index_map

    def bias_map(hh, bi, i):
        ty = jnp.where(i == 0, 0, jnp.where(i == n_blocks - 1, 2, 1))
        return (ty, hh, 0, 0)

    blk = (1, tq, hw)
    est = 2 * (7 * tq * hw * 2 + NA_HEADS_PER_STEP * tq * 3 * tq * 4 + tq * hw * 2) + 8 * tq * 3 * tq * 4
    return pl.pallas_call(
        _na_kernel,
        grid=(n_hh, b, n_blocks),
        in_specs=[pl.BlockSpec(blk, lambda hh, bi, i: (bi, i, hh))]
                 + [pl.BlockSpec(blk, kv_map(1, j)) for j in range(3)]
                 + [pl.BlockSpec(blk, kv_map(2, j)) for j in range(3)]
                 + [pl.BlockSpec((1, NA_HEADS_PER_STEP, tq, 3 * tq), bias_map)],
        out_specs=pl.BlockSpec(blk, lambda hh, bi, i: (bi, i, hh)),
        out_shape=jax.ShapeDtypeStruct((b, s, d), BF16),
        compiler_params=pltpu.CompilerParams(
            dimension_semantics=("parallel", "parallel", "parallel"), vmem_limit_bytes=_vmem_limit(est)),
        name="neighbourhood_attention",
    )(qkv, qkv, qkv, qkv, qkv, qkv, qkv, bias)


def kernel(x, fourier_w_in, fourier_w_out, na_w_qkv, na_rpb, na_w_out, router_w, router_b,
           expert_w_gate, expert_w_up, expert_w_down, ln_g, ln_b):
    b, s, d = x.shape
    t = b * s
    rows = s // GRID_W
    w_router_t = router_w.T
    xt = x.reshape(t, d)

    f = _fourier_mixer_pre_out(x, fourier_w_in[0])
    xt, xt_pk, eidx, gate = _proj_residual_ln(f.reshape(t, d), fourier_w_out[0].astype(BF16), xt,
                                              ln_g[0, 0], ln_b[0, 0], w_router_t, router_b)
    xt = _moe_residual_ln(xt, xt_pk, eidx, gate, expert_w_gate, expert_w_up, expert_w_down, 0,
                          ln_g[0, 1], ln_b[0, 1])

    qkv = _matmul(xt, na_w_qkv[0].astype(BF16), BF16, MM_BM, 1024).reshape(b, s, 3 * d)
    bias = _na_bias_tables(na_rpb[0], rows)
    o = _neighbourhood_attention(qkv, bias, rows)
    xt, xt_pk, eidx, gate = _proj_residual_ln(o.reshape(t, d), na_w_out[0].astype(BF16), xt,
                                              ln_g[1, 0], ln_b[1, 0], w_router_t, router_b)
    xt = _moe_residual_ln(xt, xt_pk, eidx, gate, expert_w_gate, expert_w_up, expert_w_down, 1,
                          ln_g[1, 1], ln_b[1, 1])
    return xt.reshape(b, s, d)
```

```python
import functools

import jax
import jax.numpy as jnp
import numpy as np
from jax import lax
from jax.experimental import pallas as pl
from jax.experimental.pallas import tpu as pltpu

F32 = jnp.float32
BF16 = jnp.bfloat16
I32 = jnp.int32

D_MODEL = 1024
GRID_W = 64
N_FOURIER_GROUPS = 4
FOURIER_GROUP_DIM = D_MODEL // N_FOURIER_GROUPS
N_HEADS = 16
HEAD_DIM = D_MODEL // N_HEADS
WIN_ROWS = 8
WIN_COLS = 16
N_EXPERTS = 32
N_GROUPS = 4
EXPERTS_PER_GROUP = N_EXPERTS // N_GROUPS
D_EXPERT = D_MODEL // 2
PAIRS_PER_GROUP = EXPERTS_PER_GROUP * (EXPERTS_PER_GROUP - 1) // 2
N_CLASSES = N_GROUPS * PAIRS_PER_GROUP
DEPTH = 2
DEEPNORM_ALPHA = (2 * DEPTH) ** 0.25
LN_EPS = 1e-5

V7X_VMEM_BYTES = 64 * 1024 * 1024
LANES = 128

FFT_N1 = 64
FFT_N2 = 128
FFT_CHUNK = 128

MM_BM = 1024
LN_BM = 512
EXPERT_BM = 256
COMBINE_BT = 512
NA_RQ = 4
NA_KROWS = NA_RQ + WIN_ROWS
NA_HEADS_PER_STEP = 8
NEG_BIAS = -1e30


def _vmem_limit(nbytes):
    return int(min(max(nbytes, 32 * 1024 * 1024), V7X_VMEM_BYTES - 8 * 1024 * 1024))


def _mm_kernel(a_ref, b_ref, o_ref, *, precision):
    if precision is None:
        a = a_ref[...].astype(BF16)
        b = b_ref[...].astype(BF16)
        acc = jnp.dot(a, b, preferred_element_type=F32)
    else:
        acc = jnp.dot(a_ref[...], b_ref[...], preferred_element_type=F32, precision=precision)
    o_ref[...] = acc.astype(o_ref.dtype)


def _matmul(a, b, out_dtype, bm, bn, precision=None):
    m, k = a.shape
    _, n = b.shape
    est = 2 * (bm * k * a.dtype.itemsize + k * bn * b.dtype.itemsize + bm * bn * 4) + 3 * bm * bn * 4
    return pl.pallas_call(
        functools.partial(_mm_kernel, precision=precision),
        grid=(m // bm, n // bn),
        in_specs=[pl.BlockSpec((bm, k), lambda i, j: (i, 0)),
                  pl.BlockSpec((k, bn), lambda i, j: (0, j))],
        out_specs=pl.BlockSpec((bm, bn), lambda i, j: (i, j)),
        out_shape=jax.ShapeDtypeStruct((m, n), out_dtype),
        compiler_params=pltpu.CompilerParams(
            dimension_semantics=("parallel", "parallel"), vmem_limit_bytes=_vmem_limit(est)),
        name="matmul",
    )(a, b)


def _fourier_channel_tables():
    n = FOURIER_GROUP_DIM
    c = np.arange(n)
    ang = 2.0 * np.pi * ((c[:, None] * c[None, :]) % n) / n
    cos, sin = np.cos(ang) / np.sqrt(n), np.sin(ang) / np.sqrt(n)
    half = FFT_CHUNK
    tabs = [np.concatenate([cos[:, h * half:(h + 1) * half], sin[:, h * half:(h + 1) * half]], axis=1)
            for h in range(n // half)]
    return np.stack(tabs).astype(np.float32)


def _fourier_seq_tables(seq):
    n1, n2 = FFT_N1, FFT_N2
    assert n1 * n2 == seq
    k2 = np.arange(n2)
    s2 = np.arange(n2)
    m1 = np.empty((n1, 2 * n2, 2 * n2), np.float32)
    for s1 in range(n1):
        ang = 2.0 * np.pi * ((k2[:, None] * (s1 + n1 * s2[None, :])) % seq) / seq
        mr, mi = np.cos(ang) / np.sqrt(n2), np.sin(ang) / np.sqrt(n2)
        m1[s1] = np.block([[mr, -mi], [mi, mr]])
    k1 = np.arange(n1)
    ang = 2.0 * np.pi * ((k1[:, None] * k1[None, :]) % n1) / n1
    w2 = np.concatenate([np.cos(ang), -np.sin(ang)], axis=1) / np.sqrt(n1)
    return m1, w2.astype(np.float32)


def _fft_kernel(a_ref, b_ref, m1_ref, w2_ref, o_ref, zs_ref):
    n1, n2, c = FFT_N1, FFT_N2, FFT_CHUNK

    def stage1(s1, carry):
        rows = pl.ds(s1, n2, stride=n1)
        x = jnp.concatenate([a_ref[0, rows, :], b_ref[0, rows, :]], axis=0).astype(BF16)
        z = jnp.dot(m1_ref[s1], x, preferred_element_type=F32)
        zs_ref[pl.ds(pl.multiple_of(s1 * 2 * n2, 2 * n2), 2 * n2), :] = z
        return carry

    lax.fori_loop(0, n1, stage1, 0, unroll=8)

    def stage2(kk, carry):
        k2 = 2 * kk
        parts = []
        for d in range(2):
            zr = zs_ref[pl.ds(k2 + d, n1, stride=2 * n2), :]
            zi = zs_ref[pl.ds(n2 + k2 + d, n1, stride=2 * n2), :]
            parts.append(jnp.concatenate([zr, zi], axis=0))
        z = jnp.concatenate(parts, axis=1).astype(BF16)
        y = jnp.dot(w2_ref[...], z, preferred_element_type=F32)
        o_ref[0, pl.ds(k2, n1, stride=n2), :] = y[:, :c]
        o_ref[0, pl.ds(k2 + 1, n1, stride=n2), :] = y[:, c:]
        return carry

    lax.fori_loop(0, n2 // 2, stage2, 0, unroll=8)


def _seq_fft(ab, m1, w2):
    b, s, two_d = ab.shape
    d = two_d // 2
    c = FFT_CHUNK
    est = 2 * (s * 2 * c * 4 + m1.size * 2 + s * c * 4) + FFT_N1 * 2 * FFT_N2 * c * 4 + (4 << 20)
    return pl.pallas_call(
        _fft_kernel,
        grid=(b, d // c),
        in_specs=[pl.BlockSpec((1, s, c), lambda i, j: (i, 0, 2 * j)),
                  pl.BlockSpec((1, s, c), lambda i, j: (i, 0, 2 * j + 1)),
                  pl.BlockSpec(m1.shape, lambda i, j: (0, 0, 0)),
                  pl.BlockSpec(w2.shape, lambda i, j: (0, 0))],
        out_specs=pl.BlockSpec((1, s, c), lambda i, j: (i, 0, j)),
        out_shape=jax.ShapeDtypeStruct((b, s, d), F32),
        scratch_shapes=[pltpu.VMEM((FFT_N1 * 2 * FFT_N2, c), F32)],
        compiler_params=pltpu.CompilerParams(
            dimension_semantics=("parallel", "parallel"), vmem_limit_bytes=_vmem_limit(est)),
        name="seq_fft",
    )(ab, ab, m1, w2)


def _fourier_mixer_pre_out(x, w_in):
    b, s, d = x.shape
    g, gd, c = N_FOURIER_GROUPS, FOURIER_GROUP_DIM, FFT_CHUNK
    halves = gd // c
    cs = jnp.asarray(_fourier_channel_tables())
    w_ab = pl.pallas_call(
        functools.partial(_mm_kernel, precision=lax.Precision.HIGHEST),
        grid=(g, halves),
        in_specs=[pl.BlockSpec((d, gd), lambda i, h: (0, i)),
                  pl.BlockSpec((None, gd, 2 * c), lambda i, h: (h, 0, 0))],
        out_specs=pl.BlockSpec((d, 2 * c), lambda i, h: (0, i * halves + h)),
        out_shape=jax.ShapeDtypeStruct((d, 2 * d), BF16),
        name="fourier_weight_fold",
    )(w_in, cs)
    ab = _matmul(x.reshape(b * s, d), w_ab, F32, MM_BM, 1024).reshape(b, s, 2 * d)
    m1, w2 = _fourier_seq_tables(s)
    return _seq_fft(ab, jnp.asarray(m1, dtype=BF16), jnp.asarray(w2, dtype=BF16))


def _layer_norm_rows(z, g, b):
    mu = jnp.mean(z, axis=-1, keepdims=True)
    zc = z - mu
    var = jnp.mean(zc * zc, axis=-1, keepdims=True)
    return zc * lax.rsqrt(var + LN_EPS) * g + b


def _pack_bf16_pairs(z):
    m = z.shape[1] // 2
    zb = z.astype(BF16).astype(F32)
    bits = lax.bitcast_convert_type(zb, jnp.uint32)
    return jnp.bitwise_or(bits[:, m:], lax.shift_right_logical(bits[:, :m], jnp.uint32(16)))


def _unpack_bf16_pairs(w):
    lo = lax.bitcast_convert_type(lax.shift_left(w, jnp.uint32(16)), F32)
    hi = lax.bitcast_convert_type(jnp.bitwise_and(w, jnp.uint32(0xFFFF0000)), F32)
    return jnp.concatenate([lo, hi], axis=1)


def _mm_ln_kernel(a_ref, w_ref, x_ref, g_ref, b_ref, wr_ref, br_ref, o_ref, opk_ref, cls_ref, gate_ref):
    y = jnp.dot(a_ref[...].astype(BF16), w_ref[...], preferred_element_type=F32)
    z = DEEPNORM_ALPHA * x_ref[...] + y
    out = _layer_norm_rows(z, g_ref[...], b_ref[...])
    o_ref[...] = out
    opk_ref[...] = _pack_bf16_pairs(out)
    cls_ref[...], gate_ref[...] = _route_rows(out, wr_ref[...], br_ref[...])


def _proj_residual_ln(a, w_bf16, x, ln_g, ln_b, w_router_t, b_router):
    t, k = a.shape
    d = w_bf16.shape[1]
    e = w_router_t.shape[0]
    bm = LN_BM
    est = 2 * (bm * k * a.dtype.itemsize + k * d * 2 + 3 * bm * d * 4) + 8 * bm * d * 4
    return pl.pallas_call(
        _mm_ln_kernel,
        grid=(t // bm,),
        in_specs=[pl.BlockSpec((bm, k), lambda i: (i, 0)),
                  pl.BlockSpec((k, d), lambda i: (0, 0)),
                  pl.BlockSpec((bm, d), lambda i: (i, 0)),
                  pl.BlockSpec((1, d), lambda i: (0, 0)),
                  pl.BlockSpec((1, d), lambda i: (0, 0)),
                  pl.BlockSpec((e, d), lambda i: (0, 0)),
                  pl.BlockSpec((e, 1), lambda i: (0, 0))],
        out_specs=[pl.BlockSpec((bm, d), lambda i: (i, 0)),
                   pl.BlockSpec((bm, d // 2), lambda i: (i, 0)),
                   pl.BlockSpec((1, bm), lambda i: (0, i)),
                   pl.BlockSpec((2, bm), lambda i: (0, i))],
        out_shape=[jax.ShapeDtypeStruct((t, d), F32), jax.ShapeDtypeStruct((t, d // 2), jnp.uint32),
                   jax.ShapeDtypeStruct((1, t), I32), jax.ShapeDtypeStruct((2, t), F32)],
        compiler_params=pltpu.CompilerParams(
            dimension_semantics=("parallel",), vmem_limit_bytes=_vmem_limit(est)),
        name="proj_residual_ln",
    )(a, w_bf16, x, ln_g.reshape(1, d), ln_b.reshape(1, d), w_router_t, b_router.reshape(e, 1))


def _top2_rows(v, iota):
    n_rows = v.shape[0]
    m1 = jnp.max(v, axis=0, keepdims=True)
    i1 = jnp.min(jnp.where(v == m1, iota, n_rows), axis=0, keepdims=True)
    v2 = jnp.where(iota == i1, -jnp.inf, v)
    m2 = jnp.max(v2, axis=0, keepdims=True)
    i2 = jnp.min(jnp.where(v2 == m2, iota, n_rows), axis=0, keepdims=True)
    return m1, i1, m2, i2


def _route_rows(x, w, b):
    xh = x.astype(BF16)
    xl = (x - xh.astype(F32)).astype(BF16)
    wh = w.astype(BF16)
    wl = (w - wh.astype(F32)).astype(BF16)
    nt = (((1,), (1,)), ((), ()))
    logits = (lax.dot_general(wh, xh, nt, preferred_element_type=F32)
              + lax.dot_general(wh, xl, nt, preferred_element_type=F32)
              + lax.dot_general(wl, xh, nt, preferred_element_type=F32))
    scores = 1.0 / (1.0 + jnp.exp(-logits))
    sel = scores + b
    epg = EXPERTS_PER_GROUP
    bt = x.shape[0]
    iota = lax.broadcasted_iota(I32, (epg, bt), 0)

    best = None
    for g in range(N_GROUPS):
        m1, _, m2, _ = _top2_rows(sel[g * epg:(g + 1) * epg], iota)
        gs = m1 + m2
        if best is None:
            best, gidx = gs, jnp.zeros((1, bt), I32)
        else:
            better = gs > best
            gidx = jnp.where(better, g, gidx)
            best = jnp.where(better, gs, best)

    sel_in = sel[0:epg]
    sc_in = scores[0:epg]
    for g in range(1, N_GROUPS):
        pick = gidx == g
        sel_in = jnp.where(pick, sel[g * epg:(g + 1) * epg], sel_in)
        sc_in = jnp.where(pick, scores[g * epg:(g + 1) * epg], sc_in)
    _, i1, _, i2 = _top2_rows(sel_in, iota)
    g1 = jnp.sum(jnp.where(iota == i1, sc_in, 0.0), axis=0, keepdims=True)
    g2 = jnp.sum(jnp.where(iota == i2, sc_in, 0.0), axis=0, keepdims=True)
    denom = g1 + g2
    first_lo = i1 < i2
    lo = jnp.where(first_lo, i1, i2)
    hi = jnp.where(first_lo, i2, i1)
    pair = lax.shift_right_logical(lo * (2 * epg - 1 - lo), 1) + (hi - lo - 1)
    cls = gidx * PAIRS_PER_GROUP + pair
    gates = jnp.concatenate([jnp.where(first_lo, g1, g2) / denom, jnp.where(first_lo, g2, g1) / denom], axis=0)
    return cls, gates


RANK_SUB = 256
RANK_NSUB = 8


def _rank_kernel(keys_ref, tri_ref, rank_ref, counts_ref, carry_ref, *, n_classes):
    nsub, sub = RANK_NSUB, RANK_SUB

    @pl.when(pl.program_id(0) == 0)
    def _():
        carry_ref[...] = jnp.zeros_like(carry_ref)

    cls = lax.broadcasted_iota(I32, (nsub, n_classes, sub), 1)
    onehot = cls == keys_ref[...]
    oh = jnp.where(onehot, 1.0, 0.0).reshape(nsub * n_classes, sub).astype(BF16)
    pref = jnp.dot(oh, tri_ref[...], preferred_element_type=F32).reshape(nsub, n_classes, sub)
    carry = carry_ref[...]
    for j in range(nsub):
        before = pref[j] + (carry - 1.0)
        rank_ref[j] = jnp.sum(jnp.where(onehot[j], before, 0.0), axis=0, keepdims=True).astype(I32)
        carry = carry + pref[j][:, sub - 1:sub]
    carry_ref[...] = carry
    counts_ref[...] = carry.astype(I32)


def _rank_within_class(keys, n_classes):
    n = keys.shape[0]
    nsub, sub = RANK_NSUB, RANK_SUB
    tri = jnp.asarray(np.triu(np.ones((sub, sub), np.float32)), dtype=BF16)
    rank, counts = pl.pallas_call(
        functools.partial(_rank_kernel, n_classes=n_classes),
        grid=(n // (nsub * sub),),
        in_specs=[pl.BlockSpec((nsub, 1, sub), lambda i: (i, 0, 0)),
                  pl.BlockSpec((sub, sub), lambda i: (0, 0))],
        out_specs=[pl.BlockSpec((nsub, 1, sub), lambda i: (i, 0, 0)),
                   pl.BlockSpec((n_classes, 1), lambda i: (0, 0))],
        out_shape=[jax.ShapeDtypeStruct((n // sub, 1, sub), I32),
                   jax.ShapeDtypeStruct((n_classes, 1), I32)],
        scratch_shapes=[pltpu.VMEM((n_classes, 1), F32)],
        compiler_params=pltpu.CompilerParams(dimension_semantics=("arbitrary",)),
        name="rank_within_class",
    )(keys.reshape(n // sub, 1, sub), tri)
    return rank.reshape(n), counts.reshape(n_classes)


FILL_CHUNK = 1024


def _fill_slots_kernel(dest_ref, cnt_ref, pstart_ref, pend_ref, slot_ref, *, n_items, bm):
    n_classes = cnt_ref.shape[0]
    n_slots = slot_ref.shape[0]
    step = pl.program_id(0)

    @pl.when(step == 0)
    def _():
        def pad_class(k, carry):
            def pad(p, c):
                slot_ref[p] = n_items + k * bm + jnp.bitwise_and(p, bm - 1)
                return c
            lax.fori_loop(pstart_ref[k] + cnt_ref[k], pend_ref[k], pad, 0)
            return carry

        lax.fori_loop(0, n_classes, pad_class, 0)

        def zero(p, carry):
            slot_ref[p] = 0
            return carry

        lax.fori_loop(pend_ref[n_classes - 1], n_slots, zero, 0)

    base = step * FILL_CHUNK

    def place(j, carry):
        slot_ref[dest_ref[j]] = base + j
        return carry

    lax.fori_loop(0, FILL_CHUNK, place, 0, unroll=16)


def _fill_slots(dest, counts, pad_start, pad_end, n_slots, bm):
    n_items = dest.shape[0]
    assert bm & (bm - 1) == 0 and n_items % FILL_CHUNK == 0
    smem = pl.BlockSpec(memory_space=pltpu.SMEM)
    return pl.pallas_call(
        functools.partial(_fill_slots_kernel, n_items=n_items, bm=bm),
        grid=(n_items // FILL_CHUNK,),
        in_specs=[pl.BlockSpec((FILL_CHUNK,), lambda i: (i,), memory_space=pltpu.SMEM), smem, smem, smem],
        out_specs=smem,
        out_shape=jax.ShapeDtypeStruct((n_slots,), I32),
        compiler_params=pltpu.CompilerParams(dimension_semantics=("arbitrary",)),
        name="fill_slots",
    )(dest, counts, pad_start.astype(I32), pad_end.astype(I32))


def _class_experts():
    epg = EXPERTS_PER_GROUP
    pairs = [(lo, hi) for lo in range(epg) for hi in range(lo + 1, epg)]
    return np.array([[g * epg + lo, g * epg + hi] for g in range(N_GROUPS) for lo, hi in pairs], np.int32)


def _route_slots(cls, gates, bm):
    t = cls.shape[1]
    keys = cls.reshape(t)
    rank, counts = _rank_within_class(keys, N_CLASSES)
    padded = (counts + bm - 1) // bm * bm
    pad_end = jnp.cumsum(padded)
    pad_start = pad_end - padded
    dest = jnp.take(pad_start, keys) + rank
    p = t + (N_CLASSES + 1) * bm
    slot_tok = _fill_slots(dest, counts, pad_start, pad_end, p, bm)
    nb = p // bm
    block_start = jnp.arange(nb, dtype=I32) * bm
    block_class = jnp.minimum(
        jnp.sum((pad_end[None, :] <= block_start[:, None]).astype(I32), axis=1), N_CLASSES - 1)
    block_experts = jnp.take(jnp.asarray(_class_experts()), block_class, axis=0).reshape(2 * nb)
    slot_gates = jnp.take(gates, jnp.bitwise_and(slot_tok, t - 1), axis=1)
    n_active = (pad_end[-1] // bm).astype(I32).reshape(1)
    return slot_tok, block_experts, slot_gates, n_active


SUBLANES = 8
SUBLANE_SHIFT = SUBLANES.bit_length() - 1


def _expert_kernel(be_ref, tok_ref, nact_ref, x_hbm, sg_ref, wga_ref, wua_ref, wda_ref, wgb_ref, wub_ref, wdb_ref,
                   y_hbm, xnext, xcur, ycur, yout, wga_bf, wua_bf, wda_bf, wgb_bf, wub_bf, wdb_bf, gsem, ssem,
                   *, n_tok, n_classes):
    bm = EXPERT_BM
    tiles = bm // SUBLANES
    dw = xcur.shape[-1]
    i = pl.program_id(0)
    n_blocks = pl.num_programs(0)
    nact = nact_ref[0]
    w_f32 = ((wga_ref, wua_ref, wda_ref), (wgb_ref, wub_ref, wdb_ref))
    wbf = (wga_bf, wua_bf, wda_bf, wgb_bf, wub_bf, wdb_bf)

    def row_copies(blk, gather, unrolled):
        base = blk * bm

        def body(rt, carry):
            for u in range(SUBLANES):
                row = tok_ref[base + rt * SUBLANES + u]
                if gather:
                    row = jnp.bitwise_and(row, n_tok - 1)
                    pltpu.make_async_copy(
                        x_hbm.at[lax.shift_right_logical(row, SUBLANE_SHIFT),
                                 pl.ds(jnp.bitwise_and(row, SUBLANES - 1), 1), :],
                        xnext.at[rt, pl.ds(u, 1), :], gsem.at[0]).start()
                else:
                    pltpu.make_async_copy(
                        yout.at[rt, pl.ds(u, 1), :],
                        y_hbm.at[lax.shift_right_logical(row, SUBLANE_SHIFT),
                                 pl.ds(jnp.bitwise_and(row, SUBLANES - 1), 1), :],
                        ssem.at[0]).start()
            return carry

        if unrolled is None:
            lax.fori_loop(0, tiles, body, 0)
        else:
            for rt in range(*unrolled):
                body(rt, 0)

    def wait_gather():
        pltpu.make_async_copy(x_hbm.at[pl.ds(0, tiles)], xnext, gsem.at[0]).wait()

    def wait_scatter():
        pltpu.make_async_copy(yout, y_hbm.at[pl.ds(0, tiles)], ssem.at[0]).wait()

    @pl.when(i == 0)
    def _():
        row_copies(0, True, unrolled=None)
        yout[...] = jnp.zeros(yout.shape, yout.dtype)
        ycur[...] = jnp.zeros(ycur.shape, ycur.dtype)
        spare_tile0 = n_tok // SUBLANES

        def spare_copy(k):
            return pltpu.make_async_copy(
                yout, y_hbm.at[pl.ds(pl.multiple_of(spare_tile0 + k * tiles, tiles), tiles)], ssem.at[0])

        @pl.loop(0, n_classes)
        def _(k):
            spare_copy(k).start()

        @pl.loop(0, n_classes)
        def _(k):
            spare_copy(k).wait()

    for side in range(2):
        idx = 2 * i + side
        changed = jnp.logical_or(i == 0, be_ref[idx] != be_ref[jnp.maximum(idx - 2, 0)])

        @pl.when(jnp.logical_and(changed, i < nact))
        def _():
            for m in range(3):
                wbf[3 * side + m][...] = w_f32[side][m][0].astype(BF16)

    @pl.when(i < nact)
    def _():
        wait_gather()
        xcur[...] = xnext[...]

        @pl.when(i >= 1)
        def _():
            wait_scatter()

        yout[...] = ycur[...]
        nxt = jnp.minimum(i + 1, n_blocks - 1)
        prv = jnp.maximum(i - 1, 0)
        n_chunks = 8
        per_chunk = tiles // n_chunks

        def issue_chunk(c):
            span = (c * per_chunk, (c + 1) * per_chunk)
            row_copies(nxt, True, unrolled=span)
            row_copies(prv, False, unrolled=span)

        x = _unpack_bf16_pairs(xcur[...].reshape(bm, dw)).astype(BF16)
        sg = sg_ref[...]
        on_diag = (lax.broadcasted_iota(I32, (bm, bm), 0) == lax.broadcasted_iota(I32, (bm, bm), 1))
        y = None
        for side in range(2):
            wg_bf, wu_bf, wd_bf = wbf[3 * side], wbf[3 * side + 1], wbf[3 * side + 2]
            issue_chunk(4 * side)
            gate = jnp.dot(x, wg_bf[...], preferred_element_type=F32)
            issue_chunk(4 * side + 1)
            up = jnp.dot(x, wu_bf[...], preferred_element_type=F32)
            issue_chunk(4 * side + 2)
            h = (gate / (1.0 + jnp.exp(-gate))) * up
            ys = jnp.dot(h.astype(BF16), wd_bf[...], preferred_element_type=F32)
            issue_chunk(4 * side + 3)
            col = jnp.sum(jnp.where(on_diag, sg[side:side + 1, :], 0.0), axis=1, keepdims=True)
            y = ys * col if y is None else y + ys * col
        ycur[...] = _pack_bf16_pairs(y).reshape(tiles, SUBLANES, dw)

    @pl.when(i == nact)
    def _():
        wait_gather()
        wait_scatter()
        yout[...] = ycur[...]
        row_copies(i - 1, False, unrolled=None)
        wait_scatter()


def _experts(x_pk, slot_tok, block_experts, slot_gates, n_active, w_gate, w_up, w_down, layer):
    t, dw = x_pk.shape
    _, _, d, f = w_gate.shape
    bm = EXPERT_BM
    n_blocks = slot_tok.shape[0] // bm
    n_rows = t + N_CLASSES * bm
    assert t & (t - 1) == 0 and t % SUBLANES == 0 and bm % SUBLANES == 0
    est = 2 * 6 * d * f * 4 + 6 * d * f * 2 + 4 * bm * d * 2 + 10 * bm * d * 4

    def w_spec(shape, side):
        return pl.BlockSpec((None, 1) + shape, lambda i, be, tk, na: (layer, be[2 * i + side], 0, 0))

    buf = pltpu.VMEM((bm // SUBLANES, SUBLANES, dw), jnp.uint32)
    grid_spec = pltpu.PrefetchScalarGridSpec(
        num_scalar_prefetch=3,
        grid=(n_blocks,),
        in_specs=[pl.BlockSpec(memory_space=pl.ANY),
                  pl.BlockSpec((2, bm), lambda i, be, tk, na: (0, i)),
                  w_spec((d, f), 0), w_spec((d, f), 0), w_spec((f, d), 0),
                  w_spec((d, f), 1), w_spec((d, f), 1), w_spec((f, d), 1)],
        out_specs=pl.BlockSpec(memory_space=pl.ANY),
        scratch_shapes=[buf, buf, buf, buf]
                       + [pltpu.VMEM((d, f), BF16), pltpu.VMEM((d, f), BF16), pltpu.VMEM((f, d), BF16)] * 2
                       + [pltpu.SemaphoreType.DMA((1,)), pltpu.SemaphoreType.DMA((1,))],
    )
    y = pl.pallas_call(
        functools.partial(_expert_kernel, n_tok=t, n_classes=N_CLASSES),
        grid_spec=grid_spec,
        out_shape=jax.ShapeDtypeStruct((n_rows // SUBLANES, SUBLANES, dw), jnp.uint32),
        compiler_params=pltpu.CompilerParams(
            dimension_semantics=("arbitrary",), vmem_limit_bytes=_vmem_limit(est)),
        name="experts",
    )(block_experts, slot_tok, n_active, x_pk.reshape(t // SUBLANES, SUBLANES, dw), slot_gates,
      w_gate, w_up, w_down, w_gate, w_up, w_down)
    return y.reshape(n_rows, dw)


def _combine_kernel(x_ref, y_ref, g_ref, b_ref, o_ref):
    z = DEEPNORM_ALPHA * x_ref[...] + _unpack_bf16_pairs(y_ref[...])
    o_ref[...] = _layer_norm_rows(z, g_ref[...], b_ref[...])


def _combine_ln(x, y_tok, ln_g, ln_b):
    t, d = x.shape
    bt = COMBINE_BT
    return pl.pallas_call(
        _combine_kernel,
        grid=(t // bt,),
        in_specs=[pl.BlockSpec((bt, d), lambda i: (i, 0)),
                  pl.BlockSpec((bt, d // 2), lambda i: (i, 0)),
                  pl.BlockSpec((1, d), lambda i: (0, 0)),
                  pl.BlockSpec((1, d), lambda i: (0, 0))],
        out_specs=pl.BlockSpec((bt, d), lambda i: (i, 0)),
        out_shape=jax.ShapeDtypeStruct((t, d), F32),
        compiler_params=pltpu.CompilerParams(
            dimension_semantics=("parallel",), vmem_limit_bytes=_vmem_limit(10 * bt * d * 4)),
        name="combine_ln",
    )(x, y_tok, ln_g.reshape(1, d), ln_b.reshape(1, d))


def _moe_residual_ln(x, x_pk, cls, gates, w_gate, w_up, w_down, layer, ln_g, ln_b):
    slot_tok, block_experts, slot_gates, n_active = _route_slots(cls, gates, EXPERT_BM)
    y_tok = _experts(x_pk, slot_tok, block_experts, slot_gates, n_active, w_gate, w_up, w_down, layer)
    return _combine_ln(x, y_tok, ln_g, ln_b)


def _na_bias_tables(rpb, rows):
    w = GRID_W
    kc = min(WIN_COLS, w)
    kr = min(WIN_ROWS, rows)
    cols = np.arange(w)
    col_start = np.clip(cols - kc // 2, 0, w - kc)
    col_mask = (cols[None, :] >= col_start[:, None]) & (cols[None, :] < col_start[:, None] + kc)
    col_idx = np.clip(cols[None, :] - cols[:, None] + WIN_COLS - 1, 0, 2 * WIN_COLS - 2)
    tb = rpb.astype(F32)[:, :, col_idx]
    tb = jnp.where(col_mask[None, None], tb, NEG_BIAS)
    n_ri = 2 * WIN_ROWS - 1
    tb = jnp.concatenate([tb, jnp.full((N_HEADS, 1, w, w), NEG_BIAS, F32)], axis=1)
    n_blocks = rows // NA_RQ
    ri = np.full((3, NA_RQ, NA_KROWS), n_ri, np.int32)
    for ty, blk in enumerate((0, 1, n_blocks - 1)):
        ks = int(np.clip(blk - 1, 0, n_blocks - 3)) * NA_RQ
        for rq in range(NA_RQ):
            r = blk * NA_RQ + rq
            rs = int(np.clip(r - kr // 2, 0, rows - kr))
            for j in range(NA_KROWS):
                key_row = ks + j
                if rs <= key_row < rs + kr:
                    ri[ty, rq, j] = key_row - r + WIN_ROWS - 1
    bt = tb[:, ri]
    return bt.transpose(1, 0, 2, 4, 3, 5).reshape(3, N_HEADS, NA_RQ * w, NA_KROWS * w)


def _na_kernel(q_ref, k0_ref, k1_ref, k2_ref, v0_ref, v1_ref, v2_ref, bias_ref, o_ref):
    dh = HEAD_DIM
    nt = (((1,), (1,)), ((), ()))
    scale = HEAD_DIM ** -0.5
    outs = []
    for h in range(NA_HEADS_PER_STEP):
        cols = slice(h * dh, (h + 1) * dh)
        q = q_ref[0, :, cols] * scale
        k = jnp.concatenate([k0_ref[0, :, cols], k1_ref[0, :, cols], k2_ref[0, :, cols]], axis=0)
        v = jnp.concatenate([v0_ref[0, :, cols], v1_ref[0, :, cols], v2_ref[0, :, cols]], axis=0)
        s = lax.dot_general(q, k, nt, preferred_element_type=F32) + bias_ref[0, h]
        m = jnp.max(s, axis=-1, keepdims=True)
        p = jnp.exp(s - m)
        l = jnp.sum(p, axis=-1, keepdims=True)
        o = jnp.dot(p.astype(BF16), v, preferred_element_type=F32)
        outs.append(o / l)
    o_ref[0] = jnp.concatenate(outs, axis=-1).astype(o_ref.dtype)


def _neighbourhood_attention(qkv, bias, rows):
    b, s, _ = qkv.shape
    d = D_MODEL
    w = GRID_W
    tq = NA_RQ * w
    n_blocks = rows // NA_RQ
    hw = NA_HEADS_PER_STEP * HEAD_DIM
    n_hh = d // hw
    assert NA_KROWS * w == 3 * tq and n_blocks >= 3

    def kv_map(part, j):
        def index_map(hh, bi, i):
            return (bi, jnp.clip(i - 1, 0, n_blocks - 3) + j, part * n_hh + hh)
        return index_map

    def bias_map(hh, bi, i):
        ty = jnp.where(i == 0, 0, jnp.where(i == n_blocks - 1, 2, 1))
        return (ty, hh, 0, 0)

    blk = (1, tq, hw)
    est = 2 * (7 * tq * hw * 2 + NA_HEADS_PER_STEP * tq * 3 * tq * 4 + tq * hw * 2) + 8 * tq * 3 * tq * 4
    return pl.pallas_call(
        _na_kernel,
        grid=(n_hh, b, n_blocks),
        in_specs=[pl.BlockSpec(blk, lambda hh, bi, i: (bi, i, hh))]
                 + [pl.BlockSpec(blk, kv_map(1, j)) for j in range(3)]
                 + [pl.BlockSpec(blk, kv_map(2, j)) for j in range(3)]
                 + [pl.BlockSpec((1, NA_HEADS_PER_STEP, tq, 3 * tq), bias_map)],
        out_specs=pl.BlockSpec(blk, lambda hh, bi, i: (bi, i, hh)),
        out_shape=jax.ShapeDtypeStruct((b, s, d), BF16),
        compiler_params=pltpu.CompilerParams(
            dimension_semantics=("parallel", "parallel", "parallel"), vmem_limit_bytes=_vmem_limit(est)),
        name="neighbourhood_attention",
    )(qkv, qkv, qkv, qkv, qkv, qkv, qkv, bias)


def kernel(x, fourier_w_in, fourier_w_out, na_w_qkv, na_rpb, na_w_out, router_w, router_b,
           expert_w_gate, expert_w_up, expert_w_down, ln_g, ln_b):
    b, s, d = x.shape
    t = b * s
    rows = s // GRID_W
    w_router_t = router_w.T
    xt = x.reshape(t, d)

    f = _fourier_mixer_pre_out(x, fourier_w_in[0])
    xt, xt_pk, eidx, gate = _proj_residual_ln(f.reshape(t, d), fourier_w_out[0].astype(BF16), xt,
                                              ln_g[0, 0], ln_b[0, 0], w_router_t, router_b)
    xt = _moe_residual_ln(xt, xt_pk, eidx, gate, expert_w_gate, expert_w_up, expert_w_down, 0,
                          ln_g[0, 1], ln_b[0, 1])

    qkv = _matmul(xt, na_w_qkv[0].astype(BF16), BF16, MM_BM, 1024).reshape(b, s, 3 * d)
    bias = _na_bias_tables(na_rpb[0], rows)
    o = _neighbourhood_attention(qkv, bias, rows)
    xt, xt_pk, eidx, gate = _proj_residual_ln(o.reshape(t, d), na_w_out[0].astype(BF16), xt,
                                              ln_g[1, 0], ln_b[1, 0], w_router_t, router_b)
    xt = _moe_residual_ln(xt, xt_pk, eidx, gate, expert_w_gate, expert_w_up, expert_w_down, 1,
                          ln_g[1, 1], ln_b[1, 1])
    return xt.reshape(b, s, d)
```

```python
import functools

import jax
import jax.numpy as jnp
import numpy as np
from jax import lax
from jax.experimental import pallas as pl
from jax.experimental.pallas import tpu as pltpu

F32 = jnp.float32
BF16 = jnp.bfloat16
I32 = jnp.int32

D_MODEL = 1024
GRID_W = 64
N_FOURIER_GROUPS = 4
FOURIER_GROUP_DIM = D_MODEL // N_FOURIER_GROUPS
N_HEADS = 16
HEAD_DIM = D_MODEL // N_HEADS
WIN_ROWS = 8
WIN_COLS = 16
N_EXPERTS = 32
N_GROUPS = 4
EXPERTS_PER_GROUP = N_EXPERTS // N_GROUPS
D_EXPERT = D_MODEL // 2
PAIRS_PER_GROUP = EXPERTS_PER_GROUP * (EXPERTS_PER_GROUP - 1) // 2
N_CLASSES = N_GROUPS * PAIRS_PER_GROUP
DEPTH = 2
DEEPNORM_ALPHA = (2 * DEPTH) ** 0.25
LN_EPS = 1e-5

V7X_VMEM_BYTES = 64 * 1024 * 1024
LANES = 128

FFT_N1 = 64
FFT_N2 = 128
FFT_CHUNK = 128

MM_BM = 1024
LN_BM = 512
EXPERT_BM = 256
COMBINE_BT = 512
NA_RQ = 4
NA_KROWS = NA_RQ + WIN_ROWS
NA_HEADS_PER_STEP = 8
NEG_BIAS = -1e30


def _vmem_limit(nbytes):
    return int(min(max(nbytes, 32 * 1024 * 1024), V7X_VMEM_BYTES - 8 * 1024 * 1024))


def _mm_kernel(a_ref, b_ref, o_ref, *, precision):
    if precision is None:
        a = a_ref[...].astype(BF16)
        b = b_ref[...].astype(BF16)
        acc = jnp.dot(a, b, preferred_element_type=F32)
    else:
        acc = jnp.dot(a_ref[...], b_ref[...], preferred_element_type=F32, precision=precision)
    o_ref[...] = acc.astype(o_ref.dtype)


def _matmul(a, b, out_dtype, bm, bn, precision=None):
    m, k = a.shape
    _, n = b.shape
    est = 2 * (bm * k * a.dtype.itemsize + k * bn * b.dtype.itemsize + bm * bn * 4) + 3 * bm * bn * 4
    return pl.pallas_call(
        functools.partial(_mm_kernel, precision=precision),
        grid=(m // bm, n // bn),
        in_specs=[pl.BlockSpec((bm, k), lambda i, j: (i, 0)),
                  pl.BlockSpec((k, bn), lambda i, j: (0, j))],
        out_specs=pl.BlockSpec((bm, bn), lambda i, j: (i, j)),
        out_shape=jax.ShapeDtypeStruct((m, n), out_dtype),
        compiler_params=pltpu.CompilerParams(
            dimension_semantics=("parallel", "parallel"), vmem_limit_bytes=_vmem_limit(est)),
        name="matmul",
    )(a, b)


def _fourier_channel_tables():
    n = FOURIER_GROUP_DIM
    c = np.arange(n)
    ang = 2.0 * np.pi * ((c[:, None] * c[None, :]) % n) / n
    cos, sin = np.cos(ang) / np.sqrt(n), np.sin(ang) / np.sqrt(n)
    half = FFT_CHUNK
    tabs = [np.concatenate([cos[:, h * half:(h + 1) * half], sin[:, h * half:(h + 1) * half]], axis=1)
            for h in range(n // half)]
    return np.stack(tabs).astype(np.float32)


def _fourier_seq_tables(seq):
    n1, n2 = FFT_N1, FFT_N2
    assert n1 * n2 == seq
    k2 = np.arange(n2)
    s2 = np.arange(n2)
    m1 = np.empty((n1, 2 * n2, 2 * n2), np.float32)
    for s1 in range(n1):
        ang = 2.0 * np.pi * ((k2[:, None] * (s1 + n1 * s2[None, :])) % seq) / seq
        mr, mi = np.cos(ang) / np.sqrt(n2), np.sin(ang) / np.sqrt(n2)
        m1[s1] = np.block([[mr, -mi], [mi, mr]])
    k1 = np.arange(n1)
    ang = 2.0 * np.pi * ((k1[:, None] * k1[None, :]) % n1) / n1
    w2 = np.concatenate([np.cos(ang), -np.sin(ang)], axis=1) / np.sqrt(n1)
    return m1, w2.astype(np.float32)


def _fft_kernel(a_ref, b_ref, m1_ref, w2_ref, o_ref, zs_ref):
    n1, n2, c = FFT_N1, FFT_N2, FFT_CHUNK

    def stage1(s1, carry):
        rows = pl.ds(s1, n2, stride=n1)
        x = jnp.concatenate([a_ref[0, rows, :], b_ref[0, rows, :]], axis=0).astype(BF16)
        z = jnp.dot(m1_ref[s1], x, preferred_element_type=F32)
        zs_ref[pl.ds(pl.multiple_of(s1 * 2 * n2, 2 * n2), 2 * n2), :] = z
        return carry

    lax.fori_loop(0, n1, stage1, 0, unroll=8)

    def stage2(kk, carry):
        k2 = 2 * kk
        parts = []
        for d in range(2):
            zr = zs_ref[pl.ds(k2 + d, n1, stride=2 * n2), :]
            zi = zs_ref[pl.ds(n2 + k2 + d, n1, stride=2 * n2), :]
            parts.append(jnp.concatenate([zr, zi], axis=0))
        z = jnp.concatenate(parts, axis=1).astype(BF16)
        y = jnp.dot(w2_ref[...], z, preferred_element_type=F32)
        o_ref[0, pl.ds(k2, n1, stride=n2), :] = y[:, :c]
        o_ref[0, pl.ds(k2 + 1, n1, stride=n2), :] = y[:, c:]
        return carry

    lax.fori_loop(0, n2 // 2, stage2, 0, unroll=8)


def _seq_fft(ab, m1, w2):
    b, s, two_d = ab.shape
    d = two_d // 2
    c = FFT_CHUNK
    est = 2 * (s * 2 * c * 4 + m1.size * 2 + s * c * 4) + FFT_N1 * 2 * FFT_N2 * c * 4 + (4 << 20)
    return pl.pallas_call(
        _fft_kernel,
        grid=(b, d // c),
        in_specs=[pl.BlockSpec((1, s, c), lambda i, j: (i, 0, 2 * j)),
                  pl.BlockSpec((1, s, c), lambda i, j: (i, 0, 2 * j + 1)),
                  pl.BlockSpec(m1.shape, lambda i, j: (0, 0, 0)),
                  pl.BlockSpec(w2.shape, lambda i, j: (0, 0))],
        out_specs=pl.BlockSpec((1, s, c), lambda i, j: (i, 0, j)),
        out_shape=jax.ShapeDtypeStruct((b, s, d), F32),
        scratch_shapes=[pltpu.VMEM((FFT_N1 * 2 * FFT_N2, c), F32)],
        compiler_params=pltpu.CompilerParams(
            dimension_semantics=("parallel", "parallel"), vmem_limit_bytes=_vmem_limit(est)),
        name="seq_fft",
    )(ab, ab, m1, w2)


def _fourier_mixer_pre_out(x, w_in):
    b, s, d = x.shape
    g, gd, c = N_FOURIER_GROUPS, FOURIER_GROUP_DIM, FFT_CHUNK
    halves = gd // c
    cs = jnp.asarray(_fourier_channel_tables())
    w_ab = pl.pallas_call(
        functools.partial(_mm_kernel, precision=lax.Precision.HIGHEST),
        grid=(g, halves),
        in_specs=[pl.BlockSpec((d, gd), lambda i, h: (0, i)),
                  pl.BlockSpec((None, gd, 2 * c), lambda i, h: (h, 0, 0))],
        out_specs=pl.BlockSpec((d, 2 * c), lambda i, h: (0, i * halves + h)),
        out_shape=jax.ShapeDtypeStruct((d, 2 * d), BF16),
        name="fourier_weight_fold",
    )(w_in, cs)
    ab = _matmul(x.reshape(b * s, d), w_ab, F32, MM_BM, 1024).reshape(b, s, 2 * d)
    m1, w2 = _fourier_seq_tables(s)
    return _seq_fft(ab, jnp.asarray(m1, dtype=BF16), jnp.asarray(w2, dtype=BF16))


def _layer_norm_rows(z, g, b):
    mu = jnp.mean(z, axis=-1, keepdims=True)
    zc = z - mu
    var = jnp.mean(zc * zc, axis=-1, keepdims=True)
    return zc * lax.rsqrt(var + LN_EPS) * g + b


def _pack_bf16_pairs(z):
    m = z.shape[1] // 2
    zb = z.astype(BF16).astype(F32)
    bits = lax.bitcast_convert_type(zb, jnp.uint32)
    return jnp.bitwise_or(bits[:, m:], lax.shift_right_logical(bits[:, :m], jnp.uint32(16)))


def _unpack_bf16_pairs(w):
    lo = lax.bitcast_convert_type(lax.shift_left(w, jnp.uint32(16)), F32)
    hi = lax.bitcast_convert_type(jnp.bitwise_and(w, jnp.uint32(0xFFFF0000)), F32)
    return jnp.concatenate([lo, hi], axis=1)


def _mm_ln_kernel(a_ref, w_ref, x_ref, g_ref, b_ref, wr_ref, br_ref, o_ref, opk_ref, cls_ref):
    y = jnp.dot(a_ref[...].astype(BF16), w_ref[...], preferred_element_type=F32)
    z = DEEPNORM_ALPHA * x_ref[...] + y
    out = _layer_norm_rows(z, g_ref[...], b_ref[...])
    o_ref[...] = out
    cls, gates = _route_rows(out, wr_ref[...], br_ref[...])
    cls_ref[...] = cls
    bm, dw = out.shape[0], out.shape[1] // 2
    on_diag = lax.broadcasted_iota(I32, (bm, bm), 0) == lax.broadcasted_iota(I32, (bm, bm), 1)
    lane = lax.broadcasted_iota(I32, (bm, LANES), 1)
    tile = jnp.zeros((bm, LANES), F32)
    for k in range(2):
        col = jnp.sum(jnp.where(on_diag, gates[k:k + 1, :], 0.0), axis=1, keepdims=True)
        tile = jnp.where(lane == k, col, tile)
    opk_ref[:, :dw] = _pack_bf16_pairs(out)
    opk_ref[:, dw:] = lax.bitcast_convert_type(tile, jnp.uint32)


def _proj_residual_ln(a, w_bf16, x, ln_g, ln_b, w_router_t, b_router):
    t, k = a.shape
    d = w_bf16.shape[1]
    e = w_router_t.shape[0]
    bm = LN_BM
    est = 2 * (bm * k * a.dtype.itemsize + k * d * 2 + 3 * bm * d * 4) + 8 * bm * d * 4
    return pl.pallas_call(
        _mm_ln_kernel,
        grid=(t // bm,),
        in_specs=[pl.BlockSpec((bm, k), lambda i: (i, 0)),
                  pl.BlockSpec((k, d), lambda i: (0, 0)),
                  pl.BlockSpec((bm, d), lambda i: (i, 0)),
                  pl.BlockSpec((1, d), lambda i: (0, 0)),
                  pl.BlockSpec((1, d), lambda i: (0, 0)),
                  pl.BlockSpec((e, d), lambda i: (0, 0)),
                  pl.BlockSpec((e, 1), lambda i: (0, 0))],
        out_specs=[pl.BlockSpec((bm, d), lambda i: (i, 0)),
                   pl.BlockSpec((bm, d // 2 + LANES), lambda i: (i, 0)),
                   pl.BlockSpec((1, bm), lambda i: (0, i))],
        out_shape=[jax.ShapeDtypeStruct((t, d), F32), jax.ShapeDtypeStruct((t, d // 2 + LANES), jnp.uint32),
                   jax.ShapeDtypeStruct((1, t), I32)],
        compiler_params=pltpu.CompilerParams(
            dimension_semantics=("parallel",), vmem_limit_bytes=_vmem_limit(est)),
        name="proj_residual_ln",
    )(a, w_bf16, x, ln_g.reshape(1, d), ln_b.reshape(1, d), w_router_t, b_router.reshape(e, 1))


def _top2_rows(v, iota):
    n_rows = v.shape[0]
    m1 = jnp.max(v, axis=0, keepdims=True)
    i1 = jnp.min(jnp.where(v == m1, iota, n_rows), axis=0, keepdims=True)
    v2 = jnp.where(iota == i1, -jnp.inf, v)
    m2 = jnp.max(v2, axis=0, keepdims=True)
    i2 = jnp.min(jnp.where(v2 == m2, iota, n_rows), axis=0, keepdims=True)
    return m1, i1, m2, i2


def _route_rows(x, w, b):
    xh = x.astype(BF16)
    xl = (x - xh.astype(F32)).astype(BF16)
    wh = w.astype(BF16)
    wl = (w - wh.astype(F32)).astype(BF16)
    nt = (((1,), (1,)), ((), ()))
    logits = (lax.dot_general(wh, xh, nt, preferred_element_type=F32)
              + lax.dot_general(wh, xl, nt, preferred_element_type=F32)
              + lax.dot_general(wl, xh, nt, preferred_element_type=F32))
    scores = 1.0 / (1.0 + jnp.exp(-logits))
    sel = scores + b
    epg = EXPERTS_PER_GROUP
    bt = x.shape[0]
    iota = lax.broadcasted_iota(I32, (epg, bt), 0)

    best = None
    for g in range(N_GROUPS):
        m1, _, m2, _ = _top2_rows(sel[g * epg:(g + 1) * epg], iota)
        gs = m1 + m2
        if best is None:
            best, gidx = gs, jnp.zeros((1, bt), I32)
        else:
            better = gs > best
            gidx = jnp.where(better, g, gidx)
            best = jnp.where(better, gs, best)

    sel_in = sel[0:epg]
    sc_in = scores[0:epg]
    for g in range(1, N_GROUPS):
        pick = gidx == g
        sel_in = jnp.where(pick, sel[g * epg:(g + 1) * epg], sel_in)
        sc_in = jnp.where(pick, scores[g * epg:(g + 1) * epg], sc_in)
    _, i1, _, i2 = _top2_rows(sel_in, iota)
    g1 = jnp.sum(jnp.where(iota == i1, sc_in, 0.0), axis=0, keepdims=True)
    g2 = jnp.sum(jnp.where(iota == i2, sc_in, 0.0), axis=0, keepdims=True)
    denom = g1 + g2
    first_lo = i1 < i2
    lo = jnp.where(first_lo, i1, i2)
    hi = jnp.where(first_lo, i2, i1)
    pair = lax.shift_right_logical(lo * (2 * epg - 1 - lo), 1) + (hi - lo - 1)
    cls = gidx * PAIRS_PER_GROUP + pair
    gates = jnp.concatenate([jnp.where(first_lo, g1, g2) / denom, jnp.where(first_lo, g2, g1) / denom], axis=0)
    return cls, gates


RANK_SUB = 256
RANK_NSUB = 8


def _rank_kernel(keys_ref, tri_ref, rank_ref, counts_ref, carry_ref, *, n_classes):
    nsub, sub = RANK_NSUB, RANK_SUB

    @pl.when(pl.program_id(0) == 0)
    def _():
        carry_ref[...] = jnp.zeros_like(carry_ref)

    cls = lax.broadcasted_iota(I32, (nsub, n_classes, sub), 1)
    onehot = cls == keys_ref[...]
    oh = jnp.where(onehot, 1.0, 0.0).reshape(nsub * n_classes, sub).astype(BF16)
    pref = jnp.dot(oh, tri_ref[...], preferred_element_type=F32).reshape(nsub, n_classes, sub)
    carry = carry_ref[...]
    for j in range(nsub):
        before = pref[j] + (carry - 1.0)
        rank_ref[j] = jnp.sum(jnp.where(onehot[j], before, 0.0), axis=0, keepdims=True).astype(I32)
        carry = carry + pref[j][:, sub - 1:sub]
    carry_ref[...] = carry
    counts_ref[...] = carry.astype(I32)


def _rank_within_class(keys, n_classes):
    n = keys.shape[0]
    nsub, sub = RANK_NSUB, RANK_SUB
    tri = jnp.asarray(np.triu(np.ones((sub, sub), np.float32)), dtype=BF16)
    rank, counts = pl.pallas_call(
        functools.partial(_rank_kernel, n_classes=n_classes),
        grid=(n // (nsub * sub),),
        in_specs=[pl.BlockSpec((nsub, 1, sub), lambda i: (i, 0, 0)),
                  pl.BlockSpec((sub, sub), lambda i: (0, 0))],
        out_specs=[pl.BlockSpec((nsub, 1, sub), lambda i: (i, 0, 0)),
                   pl.BlockSpec((n_classes, 1), lambda i: (0, 0))],
        out_shape=[jax.ShapeDtypeStruct((n // sub, 1, sub), I32),
                   jax.ShapeDtypeStruct((n_classes, 1), I32)],
        scratch_shapes=[pltpu.VMEM((n_classes, 1), F32)],
        compiler_params=pltpu.CompilerParams(dimension_semantics=("arbitrary",)),
        name="rank_within_class",
    )(keys.reshape(n // sub, 1, sub), tri)
    return rank.reshape(n), counts.reshape(n_classes)


FILL_CHUNK = 1024


def _fill_slots_kernel(dest_ref, cnt_ref, pend_ref, slot_ref, *, n_items, bm):
    n_classes = cnt_ref.shape[0]
    n_blocks = slot_ref.shape[0] // bm
    log_bm = bm.bit_length() - 1
    step = pl.program_id(0)

    @pl.when(step == 0)
    def _():
        def fill_block(blk, first):
            base = blk * bm

            def body(r, carry):
                slot_ref[base + r] = first + r
                return carry

            lax.fori_loop(0, bm, body, 0, unroll=16)

        def pad_class(k, carry):
            @pl.when(cnt_ref[k] > 0)
            def _():
                fill_block(lax.shift_right_logical(pend_ref[k], log_bm) - 1, n_items + k * bm)
            return carry

        lax.fori_loop(0, n_classes, pad_class, 0)

        def unused_block(blk, carry):
            fill_block(blk, 0)
            return carry

        lax.fori_loop(lax.shift_right_logical(pend_ref[n_classes - 1], log_bm), n_blocks, unused_block, 0)

    base = step * FILL_CHUNK

    def place(j, carry):
        slot_ref[dest_ref[j]] = base + j
        return carry

    lax.fori_loop(0, FILL_CHUNK, place, 0, unroll=16)


def _fill_slots(dest, counts, pad_end, n_slots, bm):
    n_items = dest.shape[0]
    assert bm & (bm - 1) == 0 and n_items % FILL_CHUNK == 0 and n_slots % bm == 0
    smem = pl.BlockSpec(memory_space=pltpu.SMEM)
    return pl.pallas_call(
        functools.partial(_fill_slots_kernel, n_items=n_items, bm=bm),
        grid=(n_items // FILL_CHUNK,),
        in_specs=[pl.BlockSpec((FILL_CHUNK,), lambda i: (i,), memory_space=pltpu.SMEM), smem, smem],
        out_specs=smem,
        out_shape=jax.ShapeDtypeStruct((n_slots,), I32),
        compiler_params=pltpu.CompilerParams(dimension_semantics=("arbitrary",)),
        name="fill_slots",
    )(dest, counts, pad_end.astype(I32))


def _class_experts():
    epg = EXPERTS_PER_GROUP
    pairs = [(lo, hi) for lo in range(epg) for hi in range(lo + 1, epg)]
    return np.array([[g * epg + lo, g * epg + hi] for g in range(N_GROUPS) for lo, hi in pairs], np.int32)


def _route_slots(cls, bm):
    t = cls.shape[1]
    keys = cls.reshape(t)
    rank, counts = _rank_within_class(keys, N_CLASSES)
    padded = (counts + bm - 1) // bm * bm
    pad_end = jnp.cumsum(padded)
    pad_start = pad_end - padded
    class_ids = jnp.arange(N_CLASSES, dtype=I32)
    dest = jnp.sum(jnp.where(keys[:, None] == class_ids[None, :], pad_start[None, :], 0), axis=1) + rank
    p = t + (N_CLASSES + 1) * bm
    slot_tok = _fill_slots(dest, counts, pad_end, p, bm)
    nb = p // bm
    block_start = jnp.arange(nb, dtype=I32) * bm
    block_class = jnp.minimum(
        jnp.sum((pad_end[None, :] <= block_start[:, None]).astype(I32), axis=1), N_CLASSES - 1)
    in_class = block_class[:, None, None] == class_ids[None, :, None]
    block_experts = jnp.sum(jnp.where(in_class, jnp.asarray(_class_experts())[None], 0), axis=1).reshape(2 * nb)
    n_active = (pad_end[-1] // bm).astype(I32).reshape(1)
    return slot_tok, block_experts, n_active


SUBLANES = 8
SUBLANE_SHIFT = SUBLANES.bit_length() - 1


def _expert_kernel(be_ref, tok_ref, nact_ref, x_hbm, wga_ref, wua_ref, wda_ref, wgb_ref, wub_ref, wdb_ref,
                   y_hbm, xnext, xcur, ycur, yout, wga_bf, wua_bf, wda_bf, wgb_bf, wub_bf, wdb_bf, gsem, ssem,
                   *, n_tok, n_classes):
    bm = EXPERT_BM
    tiles = bm // SUBLANES
    dw = ycur.shape[-1]
    i = pl.program_id(0)
    n_blocks = pl.num_programs(0)
    nact = nact_ref[0]
    w_f32 = ((wga_ref, wua_ref, wda_ref), (wgb_ref, wub_ref, wdb_ref))
    wbf = (wga_bf, wua_bf, wda_bf, wgb_bf, wub_bf, wdb_bf)

    def row_copies(blk, gather, unrolled):
        base = blk * bm

        def body(rt, carry):
            for u in range(SUBLANES):
                row = tok_ref[base + rt * SUBLANES + u]
                if gather:
                    row = jnp.bitwise_and(row, n_tok - 1)
                    pltpu.make_async_copy(
                        x_hbm.at[lax.shift_right_logical(row, SUBLANE_SHIFT),
                                 pl.ds(jnp.bitwise_and(row, SUBLANES - 1), 1), :],
                        xnext.at[rt, pl.ds(u, 1), :], gsem.at[0]).start()
                else:
                    pltpu.make_async_copy(
                        yout.at[rt, pl.ds(u, 1), :],
                        y_hbm.at[lax.shift_right_logical(row, SUBLANE_SHIFT),
                                 pl.ds(jnp.bitwise_and(row, SUBLANES - 1), 1), :],
                        ssem.at[0]).start()
            return carry

        if unrolled is None:
            lax.fori_loop(0, tiles, body, 0)
        else:
            for rt in range(*unrolled):
                body(rt, 0)

    def wait_gather():
        pltpu.make_async_copy(x_hbm.at[pl.ds(0, tiles)], xnext, gsem.at[0]).wait()

    def wait_scatter():
        pltpu.make_async_copy(yout, y_hbm.at[pl.ds(0, tiles)], ssem.at[0]).wait()

    @pl.when(i == 0)
    def _():
        row_copies(0, True, unrolled=None)
        yout[...] = jnp.zeros(yout.shape, yout.dtype)
        ycur[...] = jnp.zeros(ycur.shape, ycur.dtype)
        spare_tile0 = n_tok // SUBLANES

        def spare_copy(k):
            return pltpu.make_async_copy(
                yout, y_hbm.at[pl.ds(pl.multiple_of(spare_tile0 + k * tiles, tiles), tiles)], ssem.at[0])

        @pl.loop(0, n_classes)
        def _(k):
            spare_copy(k).start()

        @pl.loop(0, n_classes)
        def _(k):
            spare_copy(k).wait()

    for side in range(2):
        idx = 2 * i + side
        changed = jnp.logical_or(i == 0, be_ref[idx] != be_ref[jnp.maximum(idx - 2, 0)])

        @pl.when(jnp.logical_and(changed, i < nact))
        def _():
            for m in range(3):
                wbf[3 * side + m][...] = w_f32[side][m][0].astype(BF16)

    @pl.when(i < nact)
    def _():
        wait_gather()
        xcur[...] = xnext[...]

        @pl.when(i >= 1)
        def _():
            wait_scatter()

        yout[...] = ycur[...]
        row_copies(jnp.minimum(i + 1, n_blocks - 1), True, unrolled=(0, tiles))
        row_copies(jnp.maximum(i - 1, 0), False, unrolled=(0, tiles))

    @pl.when(nact - i >= 1)
    def _():
        rows = xcur[...].reshape(bm, dw + LANES)
        x = _unpack_bf16_pairs(rows[:, :dw]).astype(BF16)
        gates = lax.bitcast_convert_type(rows[:, dw:], F32)
        y = None
        for side in range(2):
            wg_bf, wu_bf, wd_bf = wbf[3 * side], wbf[3 * side + 1], wbf[3 * side + 2]
            gate = jnp.dot(x, wg_bf[...], preferred_element_type=F32)
            up = jnp.dot(x, wu_bf[...], preferred_element_type=F32)
            h = (gate / (1.0 + jnp.exp(-gate))) * up
            ys = jnp.dot(h.astype(BF16), wd_bf[...], preferred_element_type=F32) * gates[:, side:side + 1]
            y = ys if y is None else y + ys
        ycur[...] = _pack_bf16_pairs(y).reshape(tiles, SUBLANES, dw)

    @pl.when(i == nact)
    def _():
        wait_gather()
        wait_scatter()
        yout[...] = ycur[...]
        row_copies(i - 1, False, unrolled=None)
        wait_scatter()


def _experts(x_pk, slot_tok, block_experts, n_active, w_gate, w_up, w_down, layer):
    t, dw_in = x_pk.shape
    _, _, d, f = w_gate.shape
    dw = d // 2
    assert dw_in == dw + LANES
    bm = EXPERT_BM
    n_blocks = slot_tok.shape[0] // bm
    n_rows = t + N_CLASSES * bm
    assert t & (t - 1) == 0 and t % SUBLANES == 0 and bm % SUBLANES == 0
    est = 2 * 6 * d * f * 4 + 6 * d * f * 2 + 4 * bm * d * 2 + 10 * bm * d * 4

    def w_spec(shape, side):
        return pl.BlockSpec((None, 1) + shape, lambda i, be, tk, na: (layer, be[2 * i + side], 0, 0))

    xbuf = pltpu.VMEM((bm // SUBLANES, SUBLANES, dw_in), jnp.uint32)
    ybuf = pltpu.VMEM((bm // SUBLANES, SUBLANES, dw), jnp.uint32)
    grid_spec = pltpu.PrefetchScalarGridSpec(
        num_scalar_prefetch=3,
        grid=(n_blocks,),
        in_specs=[pl.BlockSpec(memory_space=pl.ANY),
                  w_spec((d, f), 0), w_spec((d, f), 0), w_spec((f, d), 0),
                  w_spec((d, f), 1), w_spec((d, f), 1), w_spec((f, d), 1)],
        out_specs=pl.BlockSpec(memory_space=pl.ANY),
        scratch_shapes=[xbuf, xbuf, ybuf, ybuf]
                       + [pltpu.VMEM((d, f), BF16), pltpu.VMEM((d, f), BF16), pltpu.VMEM((f, d), BF16)] * 2
                       + [pltpu.SemaphoreType.DMA((1,)), pltpu.SemaphoreType.DMA((1,))],
    )
    y = pl.pallas_call(
        functools.partial(_expert_kernel, n_tok=t, n_classes=N_CLASSES),
        grid_spec=grid_spec,
        out_shape=jax.ShapeDtypeStruct((n_rows // SUBLANES, SUBLANES, dw), jnp.uint32),
        compiler_params=pltpu.CompilerParams(
            dimension_semantics=("arbitrary",), vmem_limit_bytes=_vmem_limit(est)),
        name="experts",
    )(block_experts, slot_tok, n_active, x_pk.reshape(t // SUBLANES, SUBLANES, dw_in),
      w_gate, w_up, w_down, w_gate, w_up, w_down)
    return y.reshape(n_rows, dw)


def _combine_kernel(x_ref, y_ref, g_ref, b_ref, o_ref):
    z = DEEPNORM_ALPHA * x_ref[...] + _unpack_bf16_pairs(y_ref[...])
    o_ref[...] = _layer_norm_rows(z, g_ref[...], b_ref[...])


def _combine_ln(x, y_tok, ln_g, ln_b):
    t, d = x.shape
    bt = COMBINE_BT
    return pl.pallas_call(
        _combine_kernel,
        grid=(t // bt,),
        in_specs=[pl.BlockSpec((bt, d), lambda i: (i, 0)),
                  pl.BlockSpec((bt, d // 2), lambda i: (i, 0)),
                  pl.BlockSpec((1, d), lambda i: (0, 0)),
                  pl.BlockSpec((1, d), lambda i: (0, 0))],
        out_specs=pl.BlockSpec((bt, d), lambda i: (i, 0)),
        out_shape=jax.ShapeDtypeStruct((t, d), F32),
        compiler_params=pltpu.CompilerParams(
            dimension_semantics=("parallel",), vmem_limit_bytes=_vmem_limit(10 * bt * d * 4)),
        name="combine_ln",
    )(x, y_tok, ln_g.reshape(1, d), ln_b.reshape(1, d))


def _moe_residual_ln(x, x_pk, cls, w_gate, w_up, w_down, layer, ln_g, ln_b):
    slot_tok, block_experts, n_active = _route_slots(cls, EXPERT_BM)
    y_tok = _experts(x_pk, slot_tok, block_experts, n_active, w_gate, w_up, w_down, layer)
    return _combine_ln(x, y_tok, ln_g, ln_b)


def _na_bias_tables(rpb, rows):
    w = GRID_W
    kc = min(WIN_COLS, w)
    kr = min(WIN_ROWS, rows)
    cols = np.arange(w)
    col_start = np.clip(cols - kc // 2, 0, w - kc)
    col_mask = (cols[None, :] >= col_start[:, None]) & (cols[None, :] < col_start[:, None] + kc)
    col_idx = np.clip(cols[None, :] - cols[:, None] + WIN_COLS - 1, 0, 2 * WIN_COLS - 2)
    tb = rpb.astype(F32)[:, :, col_idx]
    tb = jnp.where(col_mask[None, None], tb, NEG_BIAS)
    n_ri = 2 * WIN_ROWS - 1
    tb = jnp.concatenate([tb, jnp.full((N_HEADS, 1, w, w), NEG_BIAS, F32)], axis=1)
    n_blocks = rows // NA_RQ
    ri = np.full((3, NA_RQ, NA_KROWS), n_ri, np.int32)
    for ty, blk in enumerate((0, 1, n_blocks - 1)):
        ks = int(np.clip(blk - 1, 0, n_blocks - 3)) * NA_RQ
        for rq in range(NA_RQ):
            r = blk * NA_RQ + rq
            rs = int(np.clip(r - kr // 2, 0, rows - kr))
            for j in range(NA_KROWS):
                key_row = ks + j
                if rs <= key_row < rs + kr:
                    ri[ty, rq, j] = key_row - r + WIN_ROWS - 1
    bt = tb[:, ri]
    return bt.transpose(1, 0, 2, 4, 3, 5).reshape(3, N_HEADS, NA_RQ * w, NA_KROWS * w)


def _na_kernel(q_ref, k0_ref, k1_ref, k2_ref, v0_ref, v1_ref, v2_ref, bias_ref, o_ref):
    dh = HEAD_DIM
    nt = (((1,), (1,)), ((), ()))
    scale = HEAD_DIM ** -0.5
    outs = []
    for h in range(NA_HEADS_PER_STEP):
        cols = slice(h * dh, (h + 1) * dh)
        q = q_ref[0, :, cols] * scale
        k = jnp.concatenate([k0_ref[0, :, cols], k1_ref[0, :, cols], k2_ref[0, :, cols]], axis=0)
        v = jnp.concatenate([v0_ref[0, :, cols], v1_ref[0, :, cols], v2_ref[0, :, cols]], axis=0)
        s = lax.dot_general(q, k, nt, preferred_element_type=F32) + bias_ref[0, h]
        m = jnp.max(s, axis=-1, keepdims=True)
        p = jnp.exp(s - m)
        l = jnp.sum(p, axis=-1, keepdims=True)
        o = jnp.dot(p.astype(BF16), v, preferred_element_type=F32)
        outs.append(o / l)
    o_ref[0] = jnp.concatenate(outs, axis=-1).astype(o_ref.dtype)


def _neighbourhood_attention(qkv, bias, rows):
    b, s, _ = qkv.shape
    d = D_MODEL
    w = GRID_W
    tq = NA_RQ * w
    n_blocks = rows // NA_RQ
    hw = NA_HEADS_PER_STEP * HEAD_DIM
    n_hh = d // hw
    assert NA_KROWS * w == 3 * tq and n_blocks >= 3

    def kv_map(part, j):
        def index_map(hh, bi, i):
            return (bi, jnp.clip(i - 1, 0, n_blocks - 3) + j, part * n_hh + hh)
        return index_map

    def bias_map(hh, bi, i):
        ty = jnp.where(i == 0, 0, jnp.where(i == n_blocks - 1, 2, 1))
        return (ty, hh, 0, 0)

    blk = (1, tq, hw)
    est = 2 * (7 * tq * hw * 2 + NA_HEADS_PER_STEP * tq * 3 * tq * 4 + tq * hw * 2) + 8 * tq * 3 * tq * 4
    return pl.pallas_call(
        _na_kernel,
        grid=(n_hh, b, n_blocks),
        in_specs=[pl.BlockSpec(blk, lambda hh, bi, i: (bi, i, hh))]
                 + [pl.BlockSpec(blk, kv_map(1, j)) for j in range(3)]
                 + [pl.BlockSpec(blk, kv_map(2, j)) for j in range(3)]
                 + [pl.BlockSpec((1, NA_HEADS_PER_STEP, tq, 3 * tq), bias_map)],
        out_specs=pl.BlockSpec(blk, lambda hh, bi, i: (bi, i, hh)),
        out_shape=jax.ShapeDtypeStruct((b, s, d), BF16),
        compiler_params=pltpu.CompilerParams(
            dimension_semantics=("parallel", "parallel", "parallel"), vmem_limit_bytes=_vmem_limit(est)),
        name="neighbourhood_attention",
    )(qkv, qkv, qkv, qkv, qkv, qkv, qkv, bias)


def kernel(x, fourier_w_in, fourier_w_out, na_w_qkv, na_rpb, na_w_out, router_w, router_b,
           expert_w_gate, expert_w_up, expert_w_down, ln_g, ln_b):
    b, s, d = x.shape
    t = b * s
    rows = s // GRID_W
    w_router_t = router_w.T
    xt = x.reshape(t, d)

    f = _fourier_mixer_pre_out(x, fourier_w_in[0])
    xt, xt_pk, cls = _proj_residual_ln(f.reshape(t, d), fourier_w_out[0].astype(BF16), xt,
                                       ln_g[0, 0], ln_b[0, 0], w_router_t, router_b)
    xt = _moe_residual_ln(xt, xt_pk, cls, expert_w_gate, expert_w_up, expert_w_down, 0,
                          ln_g[0, 1], ln_b[0, 1])

    qkv = _matmul(xt, na_w_qkv[0].astype(BF16), BF16, MM_BM, 1024).reshape(b, s, 3 * d)
    bias = _na_bias_tables(na_rpb[0], rows)
    o = _neighbourhood_attention(qkv, bias, rows)
    xt, xt_pk, cls = _proj_residual_ln(o.reshape(t, d), na_w_out[0].astype(BF16), xt,
                                       ln_g[1, 0], ln_b[1, 0], w_router_t, router_b)
    xt = _moe_residual_ln(xt, xt_pk, cls, expert_w_gate, expert_w_up, expert_w_down, 1,
                          ln_g[1, 1], ln_b[1, 1])
    return xt.reshape(b, s, d)
```

```python
import functools

import jax
import jax.numpy as jnp
import numpy as np
from jax import lax
from jax.experimental import pallas as pl
from jax.experimental.pallas import tpu as pltpu

F32 = jnp.float32
BF16 = jnp.bfloat16
I32 = jnp.int32

D_MODEL = 1024
GRID_W = 64
N_FOURIER_GROUPS = 4
FOURIER_GROUP_DIM = D_MODEL // N_FOURIER_GROUPS
N_HEADS = 16
HEAD_DIM = D_MODEL // N_HEADS
WIN_ROWS = 8
WIN_COLS = 16
N_EXPERTS = 32
N_GROUPS = 4
EXPERTS_PER_GROUP = N_EXPERTS // N_GROUPS
D_EXPERT = D_MODEL // 2
PAIRS_PER_GROUP = EXPERTS_PER_GROUP * (EXPERTS_PER_GROUP - 1) // 2
N_CLASSES = N_GROUPS * PAIRS_PER_GROUP
DEPTH = 2
DEEPNORM_ALPHA = (2 * DEPTH) ** 0.25
LN_EPS = 1e-5

V7X_VMEM_BYTES = 64 * 1024 * 1024
LANES = 128

FFT_N1 = 64
FFT_N2 = 128
FFT_CHUNK = 128

MM_BM = 1024
LN_BM = 512
EXPERT_BM = 384
COMBINE_BT = 512
NA_RQ = 4
NA_KROWS = NA_RQ + WIN_ROWS
NA_HEADS_PER_STEP = 8
NEG_BIAS = -1e30


def _vmem_limit(nbytes):
    return int(min(max(nbytes, 32 * 1024 * 1024), V7X_VMEM_BYTES - 8 * 1024 * 1024))


def _mm_kernel(a_ref, b_ref, o_ref, *, precision):
    if precision is None:
        a = a_ref[...].astype(BF16)
        b = b_ref[...].astype(BF16)
        acc = jnp.dot(a, b, preferred_element_type=F32)
    else:
        acc = jnp.dot(a_ref[...], b_ref[...], preferred_element_type=F32, precision=precision)
    o_ref[...] = acc.astype(o_ref.dtype)


def _matmul(a, b, out_dtype, bm, bn, precision=None):
    m, k = a.shape
    _, n = b.shape
    est = 2 * (bm * k * a.dtype.itemsize + k * bn * b.dtype.itemsize + bm * bn * 4) + 3 * bm * bn * 4
    return pl.pallas_call(
        functools.partial(_mm_kernel, precision=precision),
        grid=(m // bm, n // bn),
        in_specs=[pl.BlockSpec((bm, k), lambda i, j: (i, 0)),
                  pl.BlockSpec((k, bn), lambda i, j: (0, j))],
        out_specs=pl.BlockSpec((bm, bn), lambda i, j: (i, j)),
        out_shape=jax.ShapeDtypeStruct((m, n), out_dtype),
        compiler_params=pltpu.CompilerParams(
            dimension_semantics=("parallel", "parallel"), vmem_limit_bytes=_vmem_limit(est)),
        name="matmul",
    )(a, b)


def _fourier_channel_tables():
    n = FOURIER_GROUP_DIM
    c = np.arange(n)
    ang = 2.0 * np.pi * ((c[:, None] * c[None, :]) % n) / n
    cos, sin = np.cos(ang) / np.sqrt(n), np.sin(ang) / np.sqrt(n)
    half = FFT_CHUNK
    tabs = [np.concatenate([cos[:, h * half:(h + 1) * half], sin[:, h * half:(h + 1) * half]], axis=1)
            for h in range(n // half)]
    return np.stack(tabs).astype(np.float32)


def _fourier_seq_tables(seq):
    n1, n2 = FFT_N1, FFT_N2
    assert n1 * n2 == seq
    k2 = np.arange(n2)
    s2 = np.arange(n2)
    m1 = np.empty((n1, 2 * n2, 2 * n2), np.float32)
    for s1 in range(n1):
        ang = 2.0 * np.pi * ((k2[:, None] * (s1 + n1 * s2[None, :])) % seq) / seq
        mr, mi = np.cos(ang) / np.sqrt(n2), np.sin(ang) / np.sqrt(n2)
        m1[s1] = np.block([[mr, -mi], [mi, mr]])
    k1 = np.arange(n1)
    ang = 2.0 * np.pi * ((k1[:, None] * k1[None, :]) % n1) / n1
    w2 = np.concatenate([np.cos(ang), -np.sin(ang)], axis=1) / np.sqrt(n1)
    return m1, w2.astype(np.float32)


def _fft_kernel(a_ref, b_ref, m1_ref, w2_ref, o_ref, zs_ref):
    n1, n2, c = FFT_N1, FFT_N2, FFT_CHUNK

    def stage1(s1, carry):
        rows = pl.ds(s1, n2, stride=n1)
        x = jnp.concatenate([a_ref[0, rows, :], b_ref[0, rows, :]], axis=0).astype(BF16)
        z = jnp.dot(m1_ref[s1], x, preferred_element_type=F32)
        zs_ref[pl.ds(pl.multiple_of(s1 * 2 * n2, 2 * n2), 2 * n2), :] = z
        return carry

    lax.fori_loop(0, n1, stage1, 0, unroll=8)

    def stage2(kk, carry):
        k2 = 2 * kk
        parts = []
        for d in range(2):
            zr = zs_ref[pl.ds(k2 + d, n1, stride=2 * n2), :]
            zi = zs_ref[pl.ds(n2 + k2 + d, n1, stride=2 * n2), :]
            parts.append(jnp.concatenate([zr, zi], axis=0))
        z = jnp.concatenate(parts, axis=1).astype(BF16)
        y = jnp.dot(w2_ref[...], z, preferred_element_type=F32)
        o_ref[0, pl.ds(k2, n1, stride=n2), :] = y[:, :c]
        o_ref[0, pl.ds(k2 + 1, n1, stride=n2), :] = y[:, c:]
        return carry

    lax.fori_loop(0, n2 // 2, stage2, 0, unroll=8)


def _seq_fft(ab, m1, w2):
    b, s, two_d = ab.shape
    d = two_d // 2
    c = FFT_CHUNK
    est = 2 * (s * 2 * c * 4 + m1.size * 2 + s * c * 4) + FFT_N1 * 2 * FFT_N2 * c * 4 + (4 << 20)
    return pl.pallas_call(
        _fft_kernel,
        grid=(b, d // c),
        in_specs=[pl.BlockSpec((1, s, c), lambda i, j: (i, 0, 2 * j)),
                  pl.BlockSpec((1, s, c), lambda i, j: (i, 0, 2 * j + 1)),
                  pl.BlockSpec(m1.shape, lambda i, j: (0, 0, 0)),
                  pl.BlockSpec(w2.shape, lambda i, j: (0, 0))],
        out_specs=pl.BlockSpec((1, s, c), lambda i, j: (i, 0, j)),
        out_shape=jax.ShapeDtypeStruct((b, s, d), F32),
        scratch_shapes=[pltpu.VMEM((FFT_N1 * 2 * FFT_N2, c), F32)],
        compiler_params=pltpu.CompilerParams(
            dimension_semantics=("parallel", "parallel"), vmem_limit_bytes=_vmem_limit(est)),
        name="seq_fft",
    )(ab, ab, m1, w2)


def _fourier_mixer_pre_out(x, w_in):
    b, s, d = x.shape
    g, gd, c = N_FOURIER_GROUPS, FOURIER_GROUP_DIM, FFT_CHUNK
    halves = gd // c
    cs = jnp.asarray(_fourier_channel_tables())
    w_ab = pl.pallas_call(
        functools.partial(_mm_kernel, precision=lax.Precision.HIGHEST),
        grid=(g, halves),
        in_specs=[pl.BlockSpec((d, gd), lambda i, h: (0, i)),
                  pl.BlockSpec((None, gd, 2 * c), lambda i, h: (h, 0, 0))],
        out_specs=pl.BlockSpec((d, 2 * c), lambda i, h: (0, i * halves + h)),
        out_shape=jax.ShapeDtypeStruct((d, 2 * d), BF16),
        name="fourier_weight_fold",
    )(w_in, cs)
    ab = _matmul(x.reshape(b * s, d), w_ab, F32, MM_BM, 1024).reshape(b, s, 2 * d)
    m1, w2 = _fourier_seq_tables(s)
    return _seq_fft(ab, jnp.asarray(m1, dtype=BF16), jnp.asarray(w2, dtype=BF16))


def _layer_norm_rows(z, g, b):
    mu = jnp.mean(z, axis=-1, keepdims=True)
    zc = z - mu
    var = jnp.mean(zc * zc, axis=-1, keepdims=True)
    return zc * lax.rsqrt(var + LN_EPS) * g + b


def _pack_bf16_pairs(z):
    m = z.shape[1] // 2
    zb = z.astype(BF16).astype(F32)
    bits = lax.bitcast_convert_type(zb, jnp.uint32)
    return jnp.bitwise_or(bits[:, m:], lax.shift_right_logical(bits[:, :m], jnp.uint32(16)))


def _unpack_bf16_pairs(w):
    lo = lax.bitcast_convert_type(lax.shift_left(w, jnp.uint32(16)), F32)
    hi = lax.bitcast_convert_type(jnp.bitwise_and(w, jnp.uint32(0xFFFF0000)), F32)
    return jnp.concatenate([lo, hi], axis=1)


def _mm_ln_kernel(a_ref, w_ref, x_ref, g_ref, b_ref, wr_ref, br_ref, o_ref, opk_ref, cls_ref):
    y = jnp.dot(a_ref[...].astype(BF16), w_ref[...], preferred_element_type=F32)
    z = DEEPNORM_ALPHA * x_ref[...] + y
    out = _layer_norm_rows(z, g_ref[...], b_ref[...])
    o_ref[...] = out
    cls, gates = _route_rows(out, wr_ref[...], br_ref[...])
    cls_ref[...] = cls
    bm, dw = out.shape[0], out.shape[1] // 2
    on_diag = lax.broadcasted_iota(I32, (bm, bm), 0) == lax.broadcasted_iota(I32, (bm, bm), 1)
    lane = lax.broadcasted_iota(I32, (bm, LANES), 1)
    tile = jnp.zeros((bm, LANES), F32)
    for k in range(2):
        col = jnp.sum(jnp.where(on_diag, gates[k:k + 1, :], 0.0), axis=1, keepdims=True)
        tile = jnp.where(lane == k, col, tile)
    opk_ref[:, :dw] = _pack_bf16_pairs(out)
    opk_ref[:, dw:] = lax.bitcast_convert_type(tile, jnp.uint32)


def _proj_residual_ln(a, w_bf16, x, ln_g, ln_b, w_router_t, b_router):
    t, k = a.shape
    d = w_bf16.shape[1]
    e = w_router_t.shape[0]
    bm = LN_BM
    est = 2 * (bm * k * a.dtype.itemsize + k * d * 2 + 3 * bm * d * 4) + 8 * bm * d * 4
    return pl.pallas_call(
        _mm_ln_kernel,
        grid=(t // bm,),
        in_specs=[pl.BlockSpec((bm, k), lambda i: (i, 0)),
                  pl.BlockSpec((k, d), lambda i: (0, 0)),
                  pl.BlockSpec((bm, d), lambda i: (i, 0)),
                  pl.BlockSpec((1, d), lambda i: (0, 0)),
                  pl.BlockSpec((1, d), lambda i: (0, 0)),
                  pl.BlockSpec((e, d), lambda i: (0, 0)),
                  pl.BlockSpec((e, 1), lambda i: (0, 0))],
        out_specs=[pl.BlockSpec((bm, d), lambda i: (i, 0)),
                   pl.BlockSpec((bm, d // 2 + LANES), lambda i: (i, 0)),
                   pl.BlockSpec((1, bm), lambda i: (0, i))],
        out_shape=[jax.ShapeDtypeStruct((t, d), F32), jax.ShapeDtypeStruct((t, d // 2 + LANES), jnp.uint32),
                   jax.ShapeDtypeStruct((1, t), I32)],
        compiler_params=pltpu.CompilerParams(
            dimension_semantics=("parallel",), vmem_limit_bytes=_vmem_limit(est)),
        name="proj_residual_ln",
    )(a, w_bf16, x, ln_g.reshape(1, d), ln_b.reshape(1, d), w_router_t, b_router.reshape(e, 1))


def _top2_rows(v, iota):
    n_rows = v.shape[0]
    m1 = jnp.max(v, axis=0, keepdims=True)
    i1 = jnp.min(jnp.where(v == m1, iota, n_rows), axis=0, keepdims=True)
    v2 = jnp.where(iota == i1, -jnp.inf, v)
    m2 = jnp.max(v2, axis=0, keepdims=True)
    i2 = jnp.min(jnp.where(v2 == m2, iota, n_rows), axis=0, keepdims=True)
    return m1, i1, m2, i2


def _route_rows(x, w, b):
    xh = x.astype(BF16)
    xl = (x - xh.astype(F32)).astype(BF16)
    wh = w.astype(BF16)
    wl = (w - wh.astype(F32)).astype(BF16)
    nt = (((1,), (1,)), ((), ()))
    logits = (lax.dot_general(wh, xh, nt, preferred_element_type=F32)
              + lax.dot_general(wh, xl, nt, preferred_element_type=F32)
              + lax.dot_general(wl, xh, nt, preferred_element_type=F32))
    scores = 1.0 / (1.0 + jnp.exp(-logits))
    sel = scores + b
    epg = EXPERTS_PER_GROUP
    bt = x.shape[0]
    iota = lax.broadcasted_iota(I32, (epg, bt), 0)

    best = None
    for g in range(N_GROUPS):
        m1, _, m2, _ = _top2_rows(sel[g * epg:(g + 1) * epg], iota)
        gs = m1 + m2
        if best is None:
            best, gidx = gs, jnp.zeros((1, bt), I32)
        else:
            better = gs > best
            gidx = jnp.where(better, g, gidx)
            best = jnp.where(better, gs, best)

    sel_in = sel[0:epg]
    sc_in = scores[0:epg]
    for g in range(1, N_GROUPS):
        pick = gidx == g
        sel_in = jnp.where(pick, sel[g * epg:(g + 1) * epg], sel_in)
        sc_in = jnp.where(pick, scores[g * epg:(g + 1) * epg], sc_in)
    _, i1, _, i2 = _top2_rows(sel_in, iota)
    g1 = jnp.sum(jnp.where(iota == i1, sc_in, 0.0), axis=0, keepdims=True)
    g2 = jnp.sum(jnp.where(iota == i2, sc_in, 0.0), axis=0, keepdims=True)
    denom = g1 + g2
    first_lo = i1 < i2
    lo = jnp.where(first_lo, i1, i2)
    hi = jnp.where(first_lo, i2, i1)
    pair = lax.shift_right_logical(lo * (2 * epg - 1 - lo), 1) + (hi - lo - 1)
    cls = gidx * PAIRS_PER_GROUP + pair
    gates = jnp.concatenate([jnp.where(first_lo, g1, g2) / denom, jnp.where(first_lo, g2, g1) / denom], axis=0)
    return cls, gates


RANK_SUB = 256
RANK_NSUB = 8


def _rank_kernel(keys_ref, tri_ref, rank_ref, counts_ref, carry_ref, *, n_classes):
    nsub, sub = RANK_NSUB, RANK_SUB

    @pl.when(pl.program_id(0) == 0)
    def _():
        carry_ref[...] = jnp.zeros_like(carry_ref)

    cls = lax.broadcasted_iota(I32, (nsub, n_classes, sub), 1)
    onehot = cls == keys_ref[...]
    oh = jnp.where(onehot, 1.0, 0.0).reshape(nsub * n_classes, sub).astype(BF16)
    pref = jnp.dot(oh, tri_ref[...], preferred_element_type=F32).reshape(nsub, n_classes, sub)
    carry = carry_ref[...]
    for j in range(nsub):
        before = pref[j] + (carry - 1.0)
        rank_ref[j] = jnp.sum(jnp.where(onehot[j], before, 0.0), axis=0, keepdims=True).astype(I32)
        carry = carry + pref[j][:, sub - 1:sub]
    carry_ref[...] = carry
    counts_ref[...] = carry.astype(I32)


def _rank_within_class(keys, n_classes):
    n = keys.shape[0]
    nsub, sub = RANK_NSUB, RANK_SUB
    tri = jnp.asarray(np.triu(np.ones((sub, sub), np.float32)), dtype=BF16)
    rank, counts = pl.pallas_call(
        functools.partial(_rank_kernel, n_classes=n_classes),
        grid=(n // (nsub * sub),),
        in_specs=[pl.BlockSpec((nsub, 1, sub), lambda i: (i, 0, 0)),
                  pl.BlockSpec((sub, sub), lambda i: (0, 0))],
        out_specs=[pl.BlockSpec((nsub, 1, sub), lambda i: (i, 0, 0)),
                   pl.BlockSpec((n_classes, 1), lambda i: (0, 0))],
        out_shape=[jax.ShapeDtypeStruct((n // sub, 1, sub), I32),
                   jax.ShapeDtypeStruct((n_classes, 1), I32)],
        scratch_shapes=[pltpu.VMEM((n_classes, 1), F32)],
        compiler_params=pltpu.CompilerParams(dimension_semantics=("arbitrary",)),
        name="rank_within_class",
    )(keys.reshape(n // sub, 1, sub), tri)
    return rank.reshape(n), counts.reshape(n_classes)


FILL_CHUNK = 1024


def _fill_slots_kernel(dest_ref, cnt_ref, pend_ref, slot_ref, *, n_items, bm):
    n_classes = cnt_ref.shape[0]
    n_blocks = slot_ref.shape[0] // bm
    step = pl.program_id(0)

    @pl.when(step == 0)
    def _():
        def fill_block(blk, first):
            base = blk * bm

            def body(r, carry):
                slot_ref[base + r] = first + r
                return carry

            lax.fori_loop(0, bm, body, 0, unroll=16)

        def pad_class(k, carry):
            @pl.when(cnt_ref[k] > 0)
            def _():
                fill_block(lax.div(pend_ref[k], bm) - 1, n_items + k * bm)
            return carry

        lax.fori_loop(0, n_classes, pad_class, 0)

        def unused_block(blk, carry):
            fill_block(blk, 0)
            return carry

        lax.fori_loop(lax.div(pend_ref[n_classes - 1], bm), n_blocks, unused_block, 0)

    base = step * FILL_CHUNK

    def place(j, carry):
        slot_ref[dest_ref[j]] = base + j
        return carry

    lax.fori_loop(0, FILL_CHUNK, place, 0, unroll=16)


def _fill_slots(dest, counts, pad_end, n_slots, bm):
    n_items = dest.shape[0]
    assert n_items % FILL_CHUNK == 0 and n_slots % bm == 0
    smem = pl.BlockSpec(memory_space=pltpu.SMEM)
    return pl.pallas_call(
        functools.partial(_fill_slots_kernel, n_items=n_items, bm=bm),
        grid=(n_items // FILL_CHUNK,),
        in_specs=[pl.BlockSpec((FILL_CHUNK,), lambda i: (i,), memory_space=pltpu.SMEM), smem, smem],
        out_specs=smem,
        out_shape=jax.ShapeDtypeStruct((n_slots,), I32),
        compiler_params=pltpu.CompilerParams(dimension_semantics=("arbitrary",)),
        name="fill_slots",
    )(dest, counts, pad_end.astype(I32))


def _class_experts():
    epg = EXPERTS_PER_GROUP
    pairs = [(lo, hi) for lo in range(epg) for hi in range(lo + 1, epg)]
    return np.array([[g * epg + lo, g * epg + hi] for g in range(N_GROUPS) for lo, hi in pairs], np.int32)


def _route_slots(cls, bm):
    t = cls.shape[1]
    keys = cls.reshape(t)
    rank, counts = _rank_within_class(keys, N_CLASSES)
    padded = (counts + bm - 1) // bm * bm
    pad_end = jnp.cumsum(padded)
    pad_start = pad_end - padded
    class_ids = jnp.arange(N_CLASSES, dtype=I32)
    dest = jnp.sum(jnp.where(keys[:, None] == class_ids[None, :], pad_start[None, :], 0), axis=1) + rank
    p = (t // bm + N_CLASSES + 1) * bm
    slot_tok = _fill_slots(dest, counts, pad_end, p, bm)
    nb = p // bm
    block_start = jnp.arange(nb, dtype=I32) * bm
    block_class = jnp.minimum(
        jnp.sum((pad_end[None, :] <= block_start[:, None]).astype(I32), axis=1), N_CLASSES - 1)
    in_class = block_class[:, None, None] == class_ids[None, :, None]
    block_experts = jnp.sum(jnp.where(in_class, jnp.asarray(_class_experts())[None], 0), axis=1).reshape(2 * nb)
    n_active = (pad_end[-1] // bm).astype(I32).reshape(1)
    return slot_tok, block_experts, n_active


SUBLANES = 8
SUBLANE_SHIFT = SUBLANES.bit_length() - 1


def _expert_kernel(be_ref, tok_ref, nact_ref, x_hbm, wga_ref, wua_ref, wda_ref, wgb_ref, wub_ref, wdb_ref,
                   y_hbm, xnext, xcur, ycur, yout, wga_bf, wua_bf, wda_bf, wgb_bf, wub_bf, wdb_bf, gsem, ssem,
                   *, n_tok, n_classes):
    bm = EXPERT_BM
    tiles = bm // SUBLANES
    dw = ycur.shape[-1]
    i = pl.program_id(0)
    n_blocks = pl.num_programs(0)
    nact = nact_ref[0]
    w_f32 = ((wga_ref, wua_ref, wda_ref), (wgb_ref, wub_ref, wdb_ref))
    wbf = (wga_bf, wua_bf, wda_bf, wgb_bf, wub_bf, wdb_bf)

    def row_copies(blk, gather, unrolled):
        base = blk * bm

        def body(rt, carry):
            for u in range(SUBLANES):
                row = tok_ref[base + rt * SUBLANES + u]
                if gather:
                    row = jnp.bitwise_and(row, n_tok - 1)
                    pltpu.make_async_copy(
                        x_hbm.at[lax.shift_right_logical(row, SUBLANE_SHIFT),
                                 pl.ds(jnp.bitwise_and(row, SUBLANES - 1), 1), :],
                        xnext.at[rt, pl.ds(u, 1), :], gsem.at[0]).start()
                else:
                    pltpu.make_async_copy(
                        yout.at[rt, pl.ds(u, 1), :],
                        y_hbm.at[lax.shift_right_logical(row, SUBLANE_SHIFT),
                                 pl.ds(jnp.bitwise_and(row, SUBLANES - 1), 1), :],
                        ssem.at[0]).start()
            return carry

        if unrolled is None:
            lax.fori_loop(0, tiles, body, 0)
        else:
            for rt in range(*unrolled):
                body(rt, 0)

    def wait_gather():
        pltpu.make_async_copy(x_hbm.at[pl.ds(0, tiles)], xnext, gsem.at[0]).wait()

    def wait_scatter():
        pltpu.make_async_copy(yout, y_hbm.at[pl.ds(0, tiles)], ssem.at[0]).wait()

    @pl.when(i == 0)
    def _():
        row_copies(0, True, unrolled=None)
        yout[...] = jnp.zeros(yout.shape, yout.dtype)
        ycur[...] = jnp.zeros(ycur.shape, ycur.dtype)
        spare_tile0 = n_tok // SUBLANES

        def spare_copy(k):
            return pltpu.make_async_copy(
                yout, y_hbm.at[pl.ds(spare_tile0 + k * tiles, tiles)], ssem.at[0])

        @pl.loop(0, n_classes)
        def _(k):
            spare_copy(k).start()

        @pl.loop(0, n_classes)
        def _(k):
            spare_copy(k).wait()

    for side in range(2):
        idx = 2 * i + side
        changed = jnp.logical_or(i == 0, be_ref[idx] != be_ref[jnp.maximum(idx - 2, 0)])

        @pl.when(jnp.logical_and(changed, i < nact))
        def _():
            for m in range(3):
                wbf[3 * side + m][...] = w_f32[side][m][0].astype(BF16)

    @pl.when(i < nact)
    def _():
        wait_gather()
        xcur[...] = xnext[...]

        @pl.when(i >= 1)
        def _():
            wait_scatter()

        yout[...] = ycur[...]

    @pl.when(nact - i >= 1)
    def _():
        row_copies(jnp.minimum(i + 1, n_blocks - 1), True, unrolled=(0, tiles))
        row_copies(jnp.maximum(i - 1, 0), False, unrolled=(0, tiles))
        rows = xcur[...].reshape(bm, dw + LANES)
        x = _unpack_bf16_pairs(rows[:, :dw]).astype(BF16)
        gates = lax.bitcast_convert_type(rows[:, dw:], F32)
        y = None
        for side in range(2):
            wg_bf, wu_bf, wd_bf = wbf[3 * side], wbf[3 * side + 1], wbf[3 * side + 2]
            gate = jnp.dot(x, wg_bf[...], preferred_element_type=F32)
            up = jnp.dot(x, wu_bf[...], preferred_element_type=F32)
            h = (gate / (1.0 + jnp.exp(-gate))) * up
            ys = jnp.dot(h.astype(BF16), wd_bf[...], preferred_element_type=F32) * gates[:, side:side + 1]
            y = ys if y is None else y + ys
        ycur[...] = _pack_bf16_pairs(y).reshape(tiles, SUBLANES, dw)

    @pl.when(i == nact)
    def _():
        wait_gather()
        wait_scatter()
        yout[...] = ycur[...]
        row_copies(i - 1, False, unrolled=None)
        wait_scatter()


def _experts(x_pk, slot_tok, block_experts, n_active, w_gate, w_up, w_down, layer):
    t, dw_in = x_pk.shape
    _, _, d, f = w_gate.shape
    dw = d // 2
    assert dw_in == dw + LANES
    bm = EXPERT_BM
    n_blocks = slot_tok.shape[0] // bm
    n_rows = t + N_CLASSES * bm
    assert t & (t - 1) == 0 and t % SUBLANES == 0 and bm % SUBLANES == 0
    est = 2 * 6 * d * f * 4 + 6 * d * f * 2 + 4 * bm * d * 2 + 10 * bm * d * 4

    def w_spec(shape, side):
        return pl.BlockSpec((None, 1) + shape, lambda i, be, tk, na: (layer, be[2 * i + side], 0, 0))

    xbuf = pltpu.VMEM((bm // SUBLANES, SUBLANES, dw_in), jnp.uint32)
    ybuf = pltpu.VMEM((bm // SUBLANES, SUBLANES, dw), jnp.uint32)
    grid_spec = pltpu.PrefetchScalarGridSpec(
        num_scalar_prefetch=3,
        grid=(n_blocks,),
        in_specs=[pl.BlockSpec(memory_space=pl.ANY),
                  w_spec((d, f), 0), w_spec((d, f), 0), w_spec((f, d), 0),
                  w_spec((d, f), 1), w_spec((d, f), 1), w_spec((f, d), 1)],
        out_specs=pl.BlockSpec(memory_space=pl.ANY),
        scratch_shapes=[xbuf, xbuf, ybuf, ybuf]
                       + [pltpu.VMEM((d, f), BF16), pltpu.VMEM((d, f), BF16), pltpu.VMEM((f, d), BF16)] * 2
                       + [pltpu.SemaphoreType.DMA((1,)), pltpu.SemaphoreType.DMA((1,))],
    )
    y = pl.pallas_call(
        functools.partial(_expert_kernel, n_tok=t, n_classes=N_CLASSES),
        grid_spec=grid_spec,
        out_shape=jax.ShapeDtypeStruct((n_rows // SUBLANES, SUBLANES, dw), jnp.uint32),
        compiler_params=pltpu.CompilerParams(
            dimension_semantics=("arbitrary",), vmem_limit_bytes=_vmem_limit(est)),
        name="experts",
    )(block_experts, slot_tok, n_active, x_pk.reshape(t // SUBLANES, SUBLANES, dw_in),
      w_gate, w_up, w_down, w_gate, w_up, w_down)
    return y.reshape(n_rows, dw)


def _combine_kernel(x_ref, y_ref, g_ref, b_ref, o_ref):
    z = DEEPNORM_ALPHA * x_ref[...] + _unpack_bf16_pairs(y_ref[...])
    o_ref[...] = _layer_norm_rows(z, g_ref[...], b_ref[...])


def _combine_ln(x, y_tok, ln_g, ln_b):
    t, d = x.shape
    bt = COMBINE_BT
    return pl.pallas_call(
        _combine_kernel,
        grid=(t // bt,),
        in_specs=[pl.BlockSpec((bt, d), lambda i: (i, 0)),
                  pl.BlockSpec((bt, d // 2), lambda i: (i, 0)),
                  pl.BlockSpec((1, d), lambda i: (0, 0)),
                  pl.BlockSpec((1, d), lambda i: (0, 0))],
        out_specs=pl.BlockSpec((bt, d), lambda i: (i, 0)),
        out_shape=jax.ShapeDtypeStruct((t, d), F32),
        compiler_params=pltpu.CompilerParams(
            dimension_semantics=("parallel",), vmem_limit_bytes=_vmem_limit(10 * bt * d * 4)),
        name="combine_ln",
    )(x, y_tok, ln_g.reshape(1, d), ln_b.reshape(1, d))


def _moe_residual_ln(x, x_pk, cls, w_gate, w_up, w_down, layer, ln_g, ln_b):
    slot_tok, block_experts, n_active = _route_slots(cls, EXPERT_BM)
    y_tok = _experts(x_pk, slot_tok, block_experts, n_active, w_gate, w_up, w_down, layer)
    return _combine_ln(x, y_tok, ln_g, ln_b)


def _na_bias_tables(rpb, rows):
    w = GRID_W
    kc = min(WIN_COLS, w)
    kr = min(WIN_ROWS, rows)
    cols = np.arange(w)
    col_start = np.clip(cols - kc // 2, 0, w - kc)
    col_mask = (cols[None, :] >= col_start[:, None]) & (cols[None, :] < col_start[:, None] + kc)
    col_idx = np.clip(cols[None, :] - cols[:, None] + WIN_COLS - 1, 0, 2 * WIN_COLS - 2)
    tb = rpb.astype(F32)[:, :, col_idx]
    tb = jnp.where(col_mask[None, None], tb, NEG_BIAS)
    n_ri = 2 * WIN_ROWS - 1
    tb = jnp.concatenate([tb, jnp.full((N_HEADS, 1, w, w), NEG_BIAS, F32)], axis=1)
    n_blocks = rows // NA_RQ
    ri = np.full((3, NA_RQ, NA_KROWS), n_ri, np.int32)
    for ty, blk in enumerate((0, 1, n_blocks - 1)):
        ks = int(np.clip(blk - 1, 0, n_blocks - 3)) * NA_RQ
        for rq in range(NA_RQ):
            r = blk * NA_RQ + rq
            rs = int(np.clip(r - kr // 2, 0, rows - kr))
            for j in range(NA_KROWS):
                key_row = ks + j
                if rs <= key_row < rs + kr:
                    ri[ty, rq, j] = key_row - r + WIN_ROWS - 1
    bt = tb[:, ri]
    return bt.transpose(1, 0, 2, 4, 3, 5).reshape(3, N_HEADS, NA_RQ * w, NA_KROWS * w)


def _na_kernel(q_ref, k0_ref, k1_ref, k2_ref, v0_ref, v1_ref, v2_ref, bias_ref, o_ref):
    dh = HEAD_DIM
    nt = (((1,), (1,)), ((), ()))
    scale = HEAD_DIM ** -0.5
    outs = []
    for h in range(NA_HEADS_PER_STEP):
        cols = slice(h * dh, (h + 1) * dh)
        q = q_ref[0, :, cols] * scale
        k = jnp.concatenate([k0_ref[0, :, cols], k1_ref[0, :, cols], k2_ref[0, :, cols]], axis=0)
        v = jnp.concatenate([v0_ref[0, :, cols], v1_ref[0, :, cols], v2_ref[0, :, cols]], axis=0)
        s = lax.dot_general(q, k, nt, preferred_element_type=F32) + bias_ref[0, h]
        m = jnp.max(s, axis=-1, keepdims=True)
        p = jnp.exp(s - m)
        l = jnp.sum(p, axis=-1, keepdims=True)
        o = jnp.dot(p.astype(BF16), v, preferred_element_type=F32)
        outs.append(o / l)
    o_ref[0] = jnp.concatenate(outs, axis=-1).astype(o_ref.dtype)


def _neighbourhood_attention(qkv, bias, rows):
    b, s, _ = qkv.shape
    d = D_MODEL
    w = GRID_W
    tq = NA_RQ * w
    n_blocks = rows // NA_RQ
    hw = NA_HEADS_PER_STEP * HEAD_DIM
    n_hh = d // hw
    assert NA_KROWS * w == 3 * tq and n_blocks >= 3

    def kv_map(part, j):
        def index_map(hh, bi, i):
            return (bi, jnp.clip(i - 1, 0, n_blocks - 3) + j, part * n_hh + hh)
        return index_map

    def bias_map(hh, bi, i):
        ty = jnp.where(i == 0, 0, jnp.where(i == n_blocks - 1, 2, 1))
        return (ty, hh, 0, 0)

    blk = (1, tq, hw)
    est = 2 * (7 * tq * hw * 2 + NA_HEADS_PER_STEP * tq * 3 * tq * 4 + tq * hw * 2) + 8 * tq * 3 * tq * 4
    return pl.pallas_call(
        _na_kernel,
        grid=(n_hh, b, n_blocks),
        in_specs=[pl.BlockSpec(blk, lambda hh, bi, i: (bi, i, hh))]
                 + [pl.BlockSpec(blk, kv_map(1, j)) for j in range(3)]
                 + [pl.BlockSpec(blk, kv_map(2, j)) for j in range(3)]
                 + [pl.BlockSpec((1, NA_HEADS_PER_STEP, tq, 3 * tq), bias_map)],
        out_specs=pl.BlockSpec(blk, lambda hh, bi, i: (bi, i, hh)),
        out_shape=jax.ShapeDtypeStruct((b, s, d), BF16),
        compiler_params=pltpu.CompilerParams(
            dimension_semantics=("parallel", "parallel", "parallel"), vmem_limit_bytes=_vmem_limit(est)),
        name="neighbourhood_attention",
    )(qkv, qkv, qkv, qkv, qkv, qkv, qkv, bias)


def kernel(x, fourier_w_in, fourier_w_out, na_w_qkv, na_rpb, na_w_out, router_w, router_b,
           expert_w_gate, expert_w_up, expert_w_down, ln_g, ln_b):
    b, s, d = x.shape
    t = b * s
    rows = s // GRID_W
    w_router_t = router_w.T
    xt = x.reshape(t, d)

    f = _fourier_mixer_pre_out(x, fourier_w_in[0])
    xt, xt_pk, cls = _proj_residual_ln(f.reshape(t, d), fourier_w_out[0].astype(BF16), xt,
                                       ln_g[0, 0], ln_b[0, 0], w_router_t, router_b)
    xt = _moe_residual_ln(xt, xt_pk, cls, expert_w_gate, expert_w_up, expert_w_down, 0,
                          ln_g[0, 1], ln_b[0, 1])

    qkv = _matmul(xt, na_w_qkv[0].astype(BF16), BF16, MM_BM, 1024).reshape(b, s, 3 * d)
    bias = _na_bias_tables(na_rpb[0], rows)
    o = _neighbourhood_attention(qkv, bias, rows)
    xt, xt_pk, cls = _proj_residual_ln(o.reshape(t, d), na_w_out[0].astype(BF16), xt,
                                       ln_g[1, 0], ln_b[1, 0], w_router_t, router_b)
    xt = _moe_residual_ln(xt, xt_pk, cls, expert_w_gate, expert_w_up, expert_w_down, 1,
                          ln_g[1, 1], ln_b[1, 1])
    return xt.reshape(b, s, d)
```

```python
import functools

import jax
import jax.numpy as jnp
import numpy as np
from jax import lax
from jax.experimental import pallas as pl
from jax.experimental.pallas import tpu as pltpu

F32 = jnp.float32
BF16 = jnp.bfloat16
I32 = jnp.int32

D_MODEL = 1024
GRID_W = 64
N_FOURIER_GROUPS = 4
FOURIER_GROUP_DIM = D_MODEL // N_FOURIER_GROUPS
N_HEADS = 16
HEAD_DIM = D_MODEL // N_HEADS
WIN_ROWS = 8
WIN_COLS = 16
N_EXPERTS = 32
N_GROUPS = 4
EXPERTS_PER_GROUP = N_EXPERTS // N_GROUPS
D_EXPERT = D_MODEL // 2
PAIRS_PER_GROUP = EXPERTS_PER_GROUP * (EXPERTS_PER_GROUP - 1) // 2
N_CLASSES = N_GROUPS * PAIRS_PER_GROUP
DEPTH = 2
DEEPNORM_ALPHA = (2 * DEPTH) ** 0.25
LN_EPS = 1e-5

V7X_VMEM_BYTES = 64 * 1024 * 1024
LANES = 128

FFT_N1 = 64
FFT_N2 = 128
FFT_CHUNK = 128

MM_BM = 1024
LN_BM = 512
EXPERT_BM = 384
COMBINE_BT = 512
NA_RQ = 4
NA_KROWS = NA_RQ + WIN_ROWS
NA_HEADS_PER_STEP = 8
NEG_BIAS = -1e30


def _vmem_limit(nbytes):
    return int(min(max(nbytes, 32 * 1024 * 1024), V7X_VMEM_BYTES - 8 * 1024 * 1024))


def _mm_kernel(a_ref, b_ref, o_ref, *, precision):
    if precision is None:
        a = a_ref[...].astype(BF16)
        b = b_ref[...].astype(BF16)
        acc = jnp.dot(a, b, preferred_element_type=F32)
    else:
        acc = jnp.dot(a_ref[...], b_ref[...], preferred_element_type=F32, precision=precision)
    o_ref[...] = acc.astype(o_ref.dtype)


def _matmul(a, b, out_dtype, bm, bn, precision=None):
    m, k = a.shape
    _, n = b.shape
    est = 2 * (bm * k * a.dtype.itemsize + k * bn * b.dtype.itemsize + bm * bn * 4) + 3 * bm * bn * 4
    return pl.pallas_call(
        functools.partial(_mm_kernel, precision=precision),
        grid=(m // bm, n // bn),
        in_specs=[pl.BlockSpec((bm, k), lambda i, j: (i, 0)),
                  pl.BlockSpec((k, bn), lambda i, j: (0, j))],
        out_specs=pl.BlockSpec((bm, bn), lambda i, j: (i, j)),
        out_shape=jax.ShapeDtypeStruct((m, n), out_dtype),
        compiler_params=pltpu.CompilerParams(
            dimension_semantics=("parallel", "parallel"), vmem_limit_bytes=_vmem_limit(est)),
        name="matmul",
    )(a, b)


def _fourier_channel_tables():
    n = FOURIER_GROUP_DIM
    c = np.arange(n)
    ang = 2.0 * np.pi * ((c[:, None] * c[None, :]) % n) / n
    cos, sin = np.cos(ang) / np.sqrt(n), np.sin(ang) / np.sqrt(n)
    half = FFT_CHUNK
    tabs = [np.concatenate([cos[:, h * half:(h + 1) * half], sin[:, h * half:(h + 1) * half]], axis=1)
            for h in range(n // half)]
    return np.stack(tabs).astype(np.float32)


def _fourier_seq_tables(seq):
    n1, n2 = FFT_N1, FFT_N2
    assert n1 * n2 == seq
    k2 = np.arange(n2)
    s2 = np.arange(n2)
    m1 = np.empty((n1, 2 * n2, 2 * n2), np.float32)
    for s1 in range(n1):
        ang = 2.0 * np.pi * ((k2[:, None] * (s1 + n1 * s2[None, :])) % seq) / seq
        mr, mi = np.cos(ang) / np.sqrt(n2), np.sin(ang) / np.sqrt(n2)
        m1[s1] = np.block([[mr, -mi], [mi, mr]])
    k1 = np.arange(n1)
    ang = 2.0 * np.pi * ((k1[:, None] * k1[None, :]) % n1) / n1
    w2 = np.concatenate([np.cos(ang), -np.sin(ang)], axis=1) / np.sqrt(n1)
    return m1, w2.astype(np.float32)


def _fft_kernel(a_ref, b_ref, m1_ref, w2_ref, o_ref, zs_ref):
    n1, n2, c = FFT_N1, FFT_N2, FFT_CHUNK

    def stage1(s1, carry):
        rows = pl.ds(s1, n2, stride=n1)
        x = jnp.concatenate([a_ref[0, rows, :], b_ref[0, rows, :]], axis=0).astype(BF16)
        z = jnp.dot(m1_ref[s1], x, preferred_element_type=F32)
        zs_ref[pl.ds(pl.multiple_of(s1 * 2 * n2, 2 * n2), 2 * n2), :] = z
        return carry

    lax.fori_loop(0, n1, stage1, 0, unroll=8)

    def stage2(kk, carry):
        k2 = 2 * kk
        parts = []
        for d in range(2):
            zr = zs_ref[pl.ds(k2 + d, n1, stride=2 * n2), :]
            zi = zs_ref[pl.ds(n2 + k2 + d, n1, stride=2 * n2), :]
            parts.append(jnp.concatenate([zr, zi], axis=0))
        z = jnp.concatenate(parts, axis=1).astype(BF16)
        y = jnp.dot(w2_ref[...], z, preferred_element_type=F32)
        o_ref[0, pl.ds(k2, n1, stride=n2), :] = y[:, :c]
        o_ref[0, pl.ds(k2 + 1, n1, stride=n2), :] = y[:, c:]
        return carry

    lax.fori_loop(0, n2 // 2, stage2, 0, unroll=8)


def _seq_fft(ab, m1, w2):
    b, s, two_d = ab.shape
    d = two_d // 2
    c = FFT_CHUNK
    est = 2 * (s * 2 * c * 4 + m1.size * 2 + s * c * 4) + FFT_N1 * 2 * FFT_N2 * c * 4 + (4 << 20)
    return pl.pallas_call(
        _fft_kernel,
        grid=(b, d // c),
        in_specs=[pl.BlockSpec((1, s, c), lambda i, j: (i, 0, 2 * j)),
                  pl.BlockSpec((1, s, c), lambda i, j: (i, 0, 2 * j + 1)),
                  pl.BlockSpec(m1.shape, lambda i, j: (0, 0, 0)),
                  pl.BlockSpec(w2.shape, lambda i, j: (0, 0))],
        out_specs=pl.BlockSpec((1, s, c), lambda i, j: (i, 0, j)),
        out_shape=jax.ShapeDtypeStruct((b, s, d), F32),
        scratch_shapes=[pltpu.VMEM((FFT_N1 * 2 * FFT_N2, c), F32)],
        compiler_params=pltpu.CompilerParams(
            dimension_semantics=("parallel", "parallel"), vmem_limit_bytes=_vmem_limit(est)),
        name="seq_fft",
    )(ab, ab, m1, w2)


def _fourier_mixer_pre_out(x, w_in):
    b, s, d = x.shape
    g, gd, c = N_FOURIER_GROUPS, FOURIER_GROUP_DIM, FFT_CHUNK
    halves = gd // c
    cs = jnp.asarray(_fourier_channel_tables())
    w_ab = pl.pallas_call(
        functools.partial(_mm_kernel, precision=lax.Precision.HIGHEST),
        grid=(g, halves),
        in_specs=[pl.BlockSpec((d, gd), lambda i, h: (0, i)),
                  pl.BlockSpec((None, gd, 2 * c), lambda i, h: (h, 0, 0))],
        out_specs=pl.BlockSpec((d, 2 * c), lambda i, h: (0, i * halves + h)),
        out_shape=jax.ShapeDtypeStruct((d, 2 * d), BF16),
        name="fourier_weight_fold",
    )(w_in, cs)
    ab = _matmul(x.reshape(b * s, d), w_ab, F32, MM_BM, 1024).reshape(b, s, 2 * d)
    m1, w2 = _fourier_seq_tables(s)
    return _seq_fft(ab, jnp.asarray(m1, dtype=BF16), jnp.asarray(w2, dtype=BF16))


def _layer_norm_rows(z, g, b):
    mu = jnp.mean(z, axis=-1, keepdims=True)
    zc = z - mu
    var = jnp.mean(zc * zc, axis=-1, keepdims=True)
    return zc * lax.rsqrt(var + LN_EPS) * g + b


def _pack_bf16_pairs(z):
    m = z.shape[1] // 2
    zb = z.astype(BF16).astype(F32)
    bits = lax.bitcast_convert_type(zb, jnp.uint32)
    return jnp.bitwise_or(bits[:, m:], lax.shift_right_logical(bits[:, :m], jnp.uint32(16)))


def _unpack_bf16_pairs(w):
    lo = lax.bitcast_convert_type(lax.shift_left(w, jnp.uint32(16)), F32)
    hi = lax.bitcast_convert_type(jnp.bitwise_and(w, jnp.uint32(0xFFFF0000)), F32)
    return jnp.concatenate([lo, hi], axis=1)


def _mm_ln_kernel(a_ref, w_ref, x_ref, g_ref, b_ref, wr_ref, br_ref, o_ref, opk_ref, cls_ref):
    y = jnp.dot(a_ref[...].astype(BF16), w_ref[...], preferred_element_type=F32)
    z = DEEPNORM_ALPHA * x_ref[...] + y
    out = _layer_norm_rows(z, g_ref[...], b_ref[...])
    o_ref[...] = out
    cls, gates = _route_rows(out, wr_ref[...], br_ref[...])
    cls_ref[...] = cls
    bm, dw = out.shape[0], out.shape[1] // 2
    on_diag = lax.broadcasted_iota(I32, (bm, bm), 0) == lax.broadcasted_iota(I32, (bm, bm), 1)
    lane = lax.broadcasted_iota(I32, (bm, LANES), 1)
    tile = jnp.zeros((bm, LANES), F32)
    for k in range(2):
        col = jnp.sum(jnp.where(on_diag, gates[k:k + 1, :], 0.0), axis=1, keepdims=True)
        tile = jnp.where(lane == k, col, tile)
    opk_ref[:, :dw] = _pack_bf16_pairs(out)
    opk_ref[:, dw:] = lax.bitcast_convert_type(tile, jnp.uint32)


def _proj_residual_ln(a, w_bf16, x, ln_g, ln_b, w_router_t, b_router):
    t, k = a.shape
    d = w_bf16.shape[1]
    e = w_router_t.shape[0]
    bm = LN_BM
    est = 2 * (bm * k * a.dtype.itemsize + k * d * 2 + 3 * bm * d * 4) + 8 * bm * d * 4
    return pl.pallas_call(
        _mm_ln_kernel,
        grid=(t // bm,),
        in_specs=[pl.BlockSpec((bm, k), lambda i: (i, 0)),
                  pl.BlockSpec((k, d), lambda i: (0, 0)),
                  pl.BlockSpec((bm, d), lambda i: (i, 0)),
                  pl.BlockSpec((1, d), lambda i: (0, 0)),
                  pl.BlockSpec((1, d), lambda i: (0, 0)),
                  pl.BlockSpec((e, d), lambda i: (0, 0)),
                  pl.BlockSpec((e, 1), lambda i: (0, 0))],
        out_specs=[pl.BlockSpec((bm, d), lambda i: (i, 0)),
                   pl.BlockSpec((bm, d // 2 + LANES), lambda i: (i, 0)),
                   pl.BlockSpec((1, bm), lambda i: (0, i))],
        out_shape=[jax.ShapeDtypeStruct((t, d), F32), jax.ShapeDtypeStruct((t, d // 2 + LANES), jnp.uint32),
                   jax.ShapeDtypeStruct((1, t), I32)],
        compiler_params=pltpu.CompilerParams(
            dimension_semantics=("parallel",), vmem_limit_bytes=_vmem_limit(est)),
        name="proj_residual_ln",
    )(a, w_bf16, x, ln_g.reshape(1, d), ln_b.reshape(1, d), w_router_t, b_router.reshape(e, 1))


def _top2_rows(v, iota):
    n_rows = v.shape[0]
    m1 = jnp.max(v, axis=0, keepdims=True)
    i1 = jnp.min(jnp.where(v == m1, iota, n_rows), axis=0, keepdims=True)
    v2 = jnp.where(iota == i1, -jnp.inf, v)
    m2 = jnp.max(v2, axis=0, keepdims=True)
    i2 = jnp.min(jnp.where(v2 == m2, iota, n_rows), axis=0, keepdims=True)
    return m1, i1, m2, i2


def _route_rows(x, w, b):
    xh = x.astype(BF16)
    xl = (x - xh.astype(F32)).astype(BF16)
    wh = w.astype(BF16)
    wl = (w - wh.astype(F32)).astype(BF16)
    nt = (((1,), (1,)), ((), ()))
    logits = (lax.dot_general(wh, xh, nt, preferred_element_type=F32)
              + lax.dot_general(wh, xl, nt, preferred_element_type=F32)
              + lax.dot_general(wl, xh, nt, preferred_element_type=F32))
    scores = 1.0 / (1.0 + jnp.exp(-logits))
    sel = scores + b
    epg = EXPERTS_PER_GROUP
    bt = x.shape[0]
    iota = lax.broadcasted_iota(I32, (epg, bt), 0)

    best = None
    for g in range(N_GROUPS):
        m1, _, m2, _ = _top2_rows(sel[g * epg:(g + 1) * epg], iota)
        gs = m1 + m2
        if best is None:
            best, gidx = gs, jnp.zeros((1, bt), I32)
        else:
            better = gs > best
            gidx = jnp.where(better, g, gidx)
            best = jnp.where(better, gs, best)

    sel_in = sel[0:epg]
    sc_in = scores[0:epg]
    for g in range(1, N_GROUPS):
        pick = gidx == g
        sel_in = jnp.where(pick, sel[g * epg:(g + 1) * epg], sel_in)
        sc_in = jnp.where(pick, scores[g * epg:(g + 1) * epg], sc_in)
    _, i1, _, i2 = _top2_rows(sel_in, iota)
    g1 = jnp.sum(jnp.where(iota == i1, sc_in, 0.0), axis=0, keepdims=True)
    g2 = jnp.sum(jnp.where(iota == i2, sc_in, 0.0), axis=0, keepdims=True)
    denom = g1 + g2
    first_lo = i1 < i2
    lo = jnp.where(first_lo, i1, i2)
    hi = jnp.where(first_lo, i2, i1)
    pair = lax.shift_right_logical(lo * (2 * epg - 1 - lo), 1) + (hi - lo - 1)
    cls = gidx * PAIRS_PER_GROUP + pair
    gates = jnp.concatenate([jnp.where(first_lo, g1, g2) / denom, jnp.where(first_lo, g2, g1) / denom], axis=0)
    return cls, gates


RANK_SUB = 256
RANK_NSUB = 8


def _rank_kernel(keys_ref, tri_ref, rank_ref, counts_ref, carry_ref, *, n_classes):
    nsub, sub = RANK_NSUB, RANK_SUB

    @pl.when(pl.program_id(0) == 0)
    def _():
        carry_ref[...] = jnp.zeros_like(carry_ref)

    cls = lax.broadcasted_iota(I32, (nsub, n_classes, sub), 1)
    onehot = cls == keys_ref[...]
    oh = jnp.where(onehot, 1.0, 0.0).reshape(nsub * n_classes, sub).astype(BF16)
    pref = jnp.dot(oh, tri_ref[...], preferred_element_type=F32).reshape(nsub, n_classes, sub)
    carry = carry_ref[...]
    for j in range(nsub):
        before = pref[j] + (carry - 1.0)
        rank_ref[j] = jnp.sum(jnp.where(onehot[j], before, 0.0), axis=0, keepdims=True).astype(I32)
        carry = carry + pref[j][:, sub - 1:sub]
    carry_ref[...] = carry
    counts_ref[...] = carry.astype(I32)


def _rank_within_class(keys, n_classes):
    n = keys.shape[0]
    nsub, sub = RANK_NSUB, RANK_SUB
    tri = jnp.asarray(np.triu(np.ones((sub, sub), np.float32)), dtype=BF16)
    rank, counts = pl.pallas_call(
        functools.partial(_rank_kernel, n_classes=n_classes),
        grid=(n // (nsub * sub),),
        in_specs=[pl.BlockSpec((nsub, 1, sub), lambda i: (i, 0, 0)),
                  pl.BlockSpec((sub, sub), lambda i: (0, 0))],
        out_specs=[pl.BlockSpec((nsub, 1, sub), lambda i: (i, 0, 0)),
                   pl.BlockSpec((n_classes, 1), lambda i: (0, 0))],
        out_shape=[jax.ShapeDtypeStruct((n // sub, 1, sub), I32),
                   jax.ShapeDtypeStruct((n_classes, 1), I32)],
        scratch_shapes=[pltpu.VMEM((n_classes, 1), F32)],
        compiler_params=pltpu.CompilerParams(dimension_semantics=("arbitrary",)),
        name="rank_within_class",
    )(keys.reshape(n // sub, 1, sub), tri)
    return rank.reshape(n), counts.reshape(n_classes)


FILL_CHUNK = 1024


def _fill_slots_kernel(dest_ref, cnt_ref, pend_ref, slot_ref, *, n_items, bm):
    n_classes = cnt_ref.shape[0]
    n_blocks = slot_ref.shape[0] // bm
    step = pl.program_id(0)

    @pl.when(step == 0)
    def _():
        def fill_block(blk, first):
            base = blk * bm

            def body(r, carry):
                slot_ref[base + r] = first + r
                return carry

            lax.fori_loop(0, bm, body, 0, unroll=16)

        def pad_class(k, carry):
            @pl.when(cnt_ref[k] > 0)
            def _():
                fill_block(lax.div(pend_ref[k], bm) - 1, n_items + k * bm)
            return carry

        lax.fori_loop(0, n_classes, pad_class, 0)

        def unused_block(blk, carry):
            fill_block(blk, 0)
            return carry

        lax.fori_loop(lax.div(pend_ref[n_classes - 1], bm), n_blocks, unused_block, 0)

    base = step * FILL_CHUNK

    def place(j, carry):
        slot_ref[dest_ref[j]] = base + j
        return carry

    lax.fori_loop(0, FILL_CHUNK, place, 0, unroll=16)


def _fill_slots(dest, counts, pad_end, n_slots, bm):
    n_items = dest.shape[0]
    assert n_items % FILL_CHUNK == 0 and n_slots % bm == 0
    smem = pl.BlockSpec(memory_space=pltpu.SMEM)
    return pl.pallas_call(
        functools.partial(_fill_slots_kernel, n_items=n_items, bm=bm),
        grid=(n_items // FILL_CHUNK,),
        in_specs=[pl.BlockSpec((FILL_CHUNK,), lambda i: (i,), memory_space=pltpu.SMEM), smem, smem],
        out_specs=smem,
        out_shape=jax.ShapeDtypeStruct((n_slots,), I32),
        compiler_params=pltpu.CompilerParams(dimension_semantics=("arbitrary",)),
        name="fill_slots",
    )(dest, counts, pad_end.astype(I32))


def _class_experts():
    epg = EXPERTS_PER_GROUP
    pairs = [(lo, hi) for lo in range(epg) for hi in range(lo + 1, epg)]
    return np.array([[g * epg + lo, g * epg + hi] for g in range(N_GROUPS) for lo, hi in pairs], np.int32)


def _route_slots(cls, bm):
    t = cls.shape[1]
    keys = cls.reshape(t)
    rank, counts = _rank_within_class(keys, N_CLASSES)
    padded = (counts + bm - 1) // bm * bm
    pad_end = jnp.cumsum(padded)
    pad_start = pad_end - padded
    class_ids = jnp.arange(N_CLASSES, dtype=I32)
    dest = jnp.sum(jnp.where(keys[:, None] == class_ids[None, :], pad_start[None, :], 0), axis=1) + rank
    p = (t // bm + N_CLASSES + 1) * bm
    slot_tok = _fill_slots(dest, counts, pad_end, p, bm)
    nb = p // bm
    block_start = jnp.arange(nb, dtype=I32) * bm
    block_class = jnp.minimum(
        jnp.sum((pad_end[None, :] <= block_start[:, None]).astype(I32), axis=1), N_CLASSES - 1)
    in_class = block_class[:, None, None] == class_ids[None, :, None]
    block_experts = jnp.sum(jnp.where(in_class, jnp.asarray(_class_experts())[None], 0), axis=1).reshape(2 * nb)
    n_active = (pad_end[-1] // bm).astype(I32).reshape(1)
    return slot_tok, block_experts, n_active


SUBLANES = 8
SUBLANE_SHIFT = SUBLANES.bit_length() - 1


def _expert_kernel(be_ref, tok_ref, nact_ref, x_hbm, wga_ref, wua_ref, wda_ref, wgb_ref, wub_ref, wdb_ref,
                   y_hbm, xnext, xcur, ycur, yout, wga_bf, wua_bf, wda_bf, wgb_bf, wub_bf, wdb_bf, gsem, ssem,
                   *, n_tok, n_classes):
    bm = EXPERT_BM
    tiles = bm // SUBLANES
    dw = ycur.shape[-1]
    i = pl.program_id(0)
    n_blocks = pl.num_programs(0)
    nact = nact_ref[0]
    w_f32 = ((wga_ref, wua_ref, wda_ref), (wgb_ref, wub_ref, wdb_ref))
    wbf = (wga_bf, wua_bf, wda_bf, wgb_bf, wub_bf, wdb_bf)

    def row_copies(blk, gather, unrolled):
        base = blk * bm

        def body(rt, carry):
            for u in range(SUBLANES):
                row = tok_ref[base + rt * SUBLANES + u]
                if gather:
                    row = jnp.bitwise_and(row, n_tok - 1)
                    pltpu.make_async_copy(
                        x_hbm.at[lax.shift_right_logical(row, SUBLANE_SHIFT),
                                 pl.ds(jnp.bitwise_and(row, SUBLANES - 1), 1), :],
                        xnext.at[rt, pl.ds(u, 1), :], gsem.at[0]).start()
                else:
                    pltpu.make_async_copy(
                        yout.at[rt, pl.ds(u, 1), :],
                        y_hbm.at[lax.shift_right_logical(row, SUBLANE_SHIFT),
                                 pl.ds(jnp.bitwise_and(row, SUBLANES - 1), 1), :],
                        ssem.at[0]).start()
            return carry

        if unrolled is None:
            lax.fori_loop(0, tiles, body, 0)
        else:
            for rt in range(*unrolled):
                body(rt, 0)

    def wait_gather():
        pltpu.make_async_copy(x_hbm.at[pl.ds(0, tiles)], xnext, gsem.at[0]).wait()

    def wait_scatter():
        pltpu.make_async_copy(yout, y_hbm.at[pl.ds(0, tiles)], ssem.at[0]).wait()

    @pl.when(i == 0)
    def _():
        row_copies(0, True, unrolled=None)
        yout[...] = jnp.zeros(yout.shape, yout.dtype)
        ycur[...] = jnp.zeros(ycur.shape, ycur.dtype)
        spare_tile0 = n_tok // SUBLANES

        def spare_copy(k):
            return pltpu.make_async_copy(
                yout, y_hbm.at[pl.ds(spare_tile0 + k * tiles, tiles)], ssem.at[0])

        @pl.loop(0, n_classes)
        def _(k):
            spare_copy(k).start()

        @pl.loop(0, n_classes)
        def _(k):
            spare_copy(k).wait()

    for side in range(2):
        idx = 2 * i + side
        changed = jnp.logical_or(i == 0, be_ref[idx] != be_ref[jnp.maximum(idx - 2, 0)])

        @pl.when(jnp.logical_and(changed, i < nact))
        def _():
            for m in range(3):
                wbf[3 * side + m][...] = w_f32[side][m][0].astype(BF16)

    @pl.when(i < nact)
    def _():
        wait_gather()
        xcur[...] = xnext[...]

        @pl.when(i >= 1)
        def _():
            wait_scatter()

        yout[...] = ycur[...]

    @pl.when(nact - i >= 1)
    def _():
        row_copies(jnp.minimum(i + 1, n_blocks - 1), True, unrolled=(0, tiles))
        row_copies(jnp.maximum(i - 1, 0), False, unrolled=(0, tiles))
        rows = xcur[...].reshape(bm, dw + LANES)
        x = _unpack_bf16_pairs(rows[:, :dw]).astype(BF16)
        gates = lax.bitcast_convert_type(rows[:, dw:], F32)
        y = None
        for side in range(2):
            wg_bf, wu_bf, wd_bf = wbf[3 * side], wbf[3 * side + 1], wbf[3 * side + 2]
            gate = jnp.dot(x, wg_bf[...], preferred_element_type=F32)
            up = jnp.dot(x, wu_bf[...], preferred_element_type=F32)
            h = (gate / (1.0 + jnp.exp(-gate))) * up
            ys = jnp.dot(h.astype(BF16), wd_bf[...], preferred_element_type=F32) * gates[:, side:side + 1]
            y = ys if y is None else y + ys
        ycur[...] = _pack_bf16_pairs(y).reshape(tiles, SUBLANES, dw)

    @pl.when(i == nact)
    def _():
        wait_gather()
        wait_scatter()
        yout[...] = ycur[...]
        row_copies(i - 1, False, unrolled=None)
        wait_scatter()


def _experts(x_pk, slot_tok, block_experts, n_active, w_gate, w_up, w_down, layer):
    t, dw_in = x_pk.shape
    _, _, d, f = w_gate.shape
    dw = d // 2
    assert dw_in == dw + LANES
    bm = EXPERT_BM
    n_blocks = slot_tok.shape[0] // bm
    n_rows = t + N_CLASSES * bm
    assert t & (t - 1) == 0 and t % SUBLANES == 0 and bm % SUBLANES == 0
    est = 2 * 6 * d * f * 4 + 6 * d * f * 2 + 4 * bm * d * 2 + 10 * bm * d * 4

    def w_spec(shape, side):
        return pl.BlockSpec((None, 1) + shape, lambda i, be, tk, na: (layer, be[2 * i + side], 0, 0))

    xbuf = pltpu.VMEM((bm // SUBLANES, SUBLANES, dw_in), jnp.uint32)
    ybuf = pltpu.VMEM((bm // SUBLANES, SUBLANES, dw), jnp.uint32)
    grid_spec = pltpu.PrefetchScalarGridSpec(
        num_scalar_prefetch=3,
        grid=(n_blocks,),
        in_specs=[pl.BlockSpec(memory_space=pl.ANY),
                  w_spec((d, f), 0), w_spec((d, f), 0), w_spec((f, d), 0),
                  w_spec((d, f), 1), w_spec((d, f), 1), w_spec((f, d), 1)],
        out_specs=pl.BlockSpec(memory_space=pl.ANY),
        scratch_shapes=[xbuf, xbuf, ybuf, ybuf]
                       + [pltpu.VMEM((d, f), BF16), pltpu.VMEM((d, f), BF16), pltpu.VMEM((f, d), BF16)] * 2
                       + [pltpu.SemaphoreType.DMA((1,)), pltpu.SemaphoreType.DMA((1,))],
    )
    y = pl.pallas_call(
        functools.partial(_expert_kernel, n_tok=t, n_classes=N_CLASSES),
        grid_spec=grid_spec,
        out_shape=jax.ShapeDtypeStruct((n_rows // SUBLANES, SUBLANES, dw), jnp.uint32),
        compiler_params=pltpu.CompilerParams(
            dimension_semantics=("arbitrary",), vmem_limit_bytes=_vmem_limit(est)),
        name="experts",
    )(block_experts, slot_tok, n_active, x_pk.reshape(t // SUBLANES, SUBLANES, dw_in),
      w_gate, w_up, w_down, w_gate, w_up, w_down)
    return y.reshape(n_rows, dw)


def _combine_kernel(x_ref, y_ref, g_ref, b_ref, o_ref):
    z = DEEPNORM_ALPHA * x_ref[...] + _unpack_bf16_pairs(y_ref[...])
    o_ref[...] = _layer_norm_rows(z, g_ref[...], b_ref[...])


def _combine_ln(x, y_tok, ln_g, ln_b):
    t, d = x.shape
    bt = COMBINE_BT
    return pl.pallas_call(
        _combine_kernel,
        grid=(t // bt,),
        in_specs=[pl.BlockSpec((bt, d), lambda i: (i, 0)),
                  pl.BlockSpec((bt, d // 2), lambda i: (i, 0)),
                  pl.BlockSpec((1, d), lambda i: (0, 0)),
                  pl.BlockSpec((1, d), lambda i: (0, 0))],
        out_specs=pl.BlockSpec((bt, d), lambda i: (i, 0)),
        out_shape=jax.ShapeDtypeStruct((t, d), F32),
        compiler_params=pltpu.CompilerParams(
            dimension_semantics=("parallel",), vmem_limit_bytes=_vmem_limit(10 * bt * d * 4)),
        name="combine_ln",
    )(x, y_tok, ln_g.reshape(1, d), ln_b.reshape(1, d))


def _moe_residual_ln(x, x_pk, cls, w_gate, w_up, w_down, layer, ln_g, ln_b):
    slot_tok, block_experts, n_active = _route_slots(cls, EXPERT_BM)
    y_tok = _experts(x_pk, slot_tok, block_experts, n_active, w_gate, w_up, w_down, layer)
    return _combine_ln(x, y_tok, ln_g, ln_b)


def _na_bias_pairs(rpb):
    w = GRID_W
    kc = min(WIN_COLS, w)
    cols = np.arange(w)
    col_start = np.clip(cols - kc // 2, 0, w - kc)
    col_mask = (cols[None, :] >= col_start[:, None]) & (cols[None, :] < col_start[:, None] + kc)
    col_idx = np.clip(cols[None, :] - cols[:, None] + WIN_COLS - 1, 0, 2 * WIN_COLS - 2)
    n_ci = 2 * WIN_COLS - 1
    pick = (col_idx[None] == np.arange(n_ci)[:, None, None]).astype(np.float32)
    tb = jnp.einsum("hrc,cqk->hrqk", rpb.astype(F32), jnp.asarray(pick), precision=lax.Precision.HIGHEST)
    tb = jnp.where(col_mask[None, None], tb, NEG_BIAS)
    neg = jnp.full((N_HEADS, 1, w, w), NEG_BIAS, F32)
    ext = jnp.concatenate([neg, tb, neg], axis=1)
    return jnp.concatenate([ext[:, :-1], ext[:, 1:]], axis=-1)


def _na_tile_plan(rows):
    kr = min(WIN_ROWS, rows)
    n_blocks = rows // NA_RQ
    plan = []
    for blk in (0, 1, n_blocks - 1):
        ks = int(np.clip(blk - 1, 0, n_blocks - 3)) * NA_RQ
        per_rq = []
        for rq in range(NA_RQ):
            r = blk * NA_RQ + rq
            rs = int(np.clip(r - kr // 2, 0, rows - kr))
            ri = [ks + j - r + WIN_ROWS - 1 if rs <= ks + j < rs + kr else None for j in range(NA_KROWS)]
            tiles = []
            for jp in range(NA_KROWS // 2):
                a, b = ri[2 * jp], ri[2 * jp + 1]
                if a is not None and b is not None:
                    tiles.append(("both", b))
                elif b is not None:
                    tiles.append(("hi", b))
                elif a is not None:
                    tiles.append(("lo", a + 1))
                else:
                    tiles.append(("none", 0))
            per_rq.append(tiles)
        plan.append(per_rq)
    return plan


def _na_kernel(q_ref, k0_ref, k1_ref, k2_ref, v0_ref, v1_ref, v2_ref, pair_ref, o_ref, bias_ref, *, plan):
    dh = HEAD_DIM
    w = GRID_W
    nt = (((1,), (1,)), ((), ()))
    scale = HEAD_DIM ** -0.5
    i = pl.program_id(2)
    n_blocks = pl.num_programs(2)

    def rebuild(per_rq):
        upper = lax.broadcasted_iota(I32, (w, 2 * w), 1) >= w
        for rq, tiles in enumerate(per_rq):
            for jp, (kind, k) in enumerate(tiles):
                for h in range(NA_HEADS_PER_STEP):
                    if kind == "none":
                        tile = jnp.full((w, 2 * w), NEG_BIAS, F32)
                    elif kind == "both":
                        tile = pair_ref[h, k]
                    elif kind == "hi":
                        tile = jnp.where(upper, pair_ref[h, k], NEG_BIAS)
                    else:
                        tile = jnp.where(upper, NEG_BIAS, pair_ref[h, k])
                    bias_ref[h, rq * w:(rq + 1) * w, jp * 2 * w:(jp + 1) * 2 * w] = tile

    for ty, first_step in enumerate((0, 1, n_blocks - 1)):
        @pl.when(i == first_step)
        def _():
            rebuild(plan[ty])

    outs = []
    for h in range(NA_HEADS_PER_STEP):
        cols = slice(h * dh, (h + 1) * dh)
        q = q_ref[0, :, cols] * scale
        k = jnp.concatenate([k0_ref[0, :, cols], k1_ref[0, :, cols], k2_ref[0, :, cols]], axis=0)
        v = jnp.concatenate([v0_ref[0, :, cols], v1_ref[0, :, cols], v2_ref[0, :, cols]], axis=0)
        s = lax.dot_general(q, k, nt, preferred_element_type=F32) + bias_ref[h]
        m = jnp.max(s, axis=-1, keepdims=True)
        p = jnp.exp(s - m)
        l = jnp.sum(p, axis=-1, keepdims=True)
        o = jnp.dot(p.astype(BF16), v, preferred_element_type=F32)
        outs.append(o / l)
    o_ref[0] = jnp.concatenate(outs, axis=-1).astype(o_ref.dtype)


def _neighbourhood_attention(qkv, bias_pairs, rows):
    b, s, _ = qkv.shape
    d = D_MODEL
    w = GRID_W
    tq = NA_RQ * w
    n_blocks = rows // NA_RQ
    hps = NA_HEADS_PER_STEP
    hw = hps * HEAD_DIM
    n_hh = d // hw
    assert NA_KROWS * w == 3 * tq and n_blocks >= 3 and NA_KROWS % 2 == 0

    def kv_map(part, j):
        def index_map(hh, bi, i):
            return (bi, jnp.clip(i - 1, 0, n_blocks - 3) + j, part * n_hh + hh)
        return index_map

    blk = (1, tq, hw)
    pair_blk = (hps,) + bias_pairs.shape[1:]
    est = (2 * (7 * tq * hw * 2 + tq * hw * 2) + 2 * 4 * int(np.prod(pair_blk))
           + hps * tq * 3 * tq * 4 + 8 * tq * 3 * tq * 4)
    return pl.pallas_call(
        functools.partial(_na_kernel, plan=_na_tile_plan(rows)),
        grid=(n_hh, b, n_blocks),
        in_specs=[pl.BlockSpec(blk, lambda hh, bi, i: (bi, i, hh))]
                 + [pl.BlockSpec(blk, kv_map(1, j)) for j in range(3)]
                 + [pl.BlockSpec(blk, kv_map(2, j)) for j in range(3)]
                 + [pl.BlockSpec(pair_blk, lambda hh, bi, i: (hh, 0, 0, 0))],
        out_specs=pl.BlockSpec(blk, lambda hh, bi, i: (bi, i, hh)),
        out_shape=jax.ShapeDtypeStruct((b, s, d), BF16),
        scratch_shapes=[pltpu.VMEM((hps, tq, 3 * tq), F32)],
        compiler_params=pltpu.CompilerParams(
            dimension_semantics=("arbitrary", "arbitrary", "arbitrary"), vmem_limit_bytes=_vmem_limit(est)),
        name="neighbourhood_attention",
    )(qkv, qkv, qkv, qkv, qkv, qkv, qkv, bias_pairs)


def kernel(x, fourier_w_in, fourier_w_out, na_w_qkv, na_rpb, na_w_out, router_w, router_b,
           expert_w_gate, expert_w_up, expert_w_down, ln_g, ln_b):
    b, s, d = x.shape
    t = b * s
    rows = s // GRID_W
    w_router_t = router_w.T
    xt = x.reshape(t, d)

    f = _fourier_mixer_pre_out(x, fourier_w_in[0])
    xt, xt_pk, cls = _proj_residual_ln(f.reshape(t, d), fourier_w_out[0].astype(BF16), xt,
                                       ln_g[0, 0], ln_b[0, 0], w_router_t, router_b)
    xt = _moe_residual_ln(xt, xt_pk, cls, expert_w_gate, expert_w_up, expert_w_down, 0,
                          ln_g[0, 1], ln_b[0, 1])

    qkv = _matmul(xt, na_w_qkv[0].astype(BF16), BF16, MM_BM, 1024).reshape(b, s, 3 * d)
    bias = _na_bias_pairs(na_rpb[0])
    o = _neighbourhood_attention(qkv, bias, rows)
    xt, xt_pk, cls = _proj_residual_ln(o.reshape(t, d), na_w_out[0].astype(BF16), xt,
                                       ln_g[1, 0], ln_b[1, 0], w_router_t, router_b)
    xt = _moe_residual_ln(xt, xt_pk, cls, expert_w_gate, expert_w_up, expert_w_down, 1,
                          ln_g[1, 1], ln_b[1, 1])
    return xt.reshape(b, s, d)
```

```python
import functools

import jax
import jax.numpy as jnp
import numpy as np
from jax import lax
from jax.experimental import pallas as pl
from jax.experimental.pallas import tpu as pltpu

F32 = jnp.float32
BF16 = jnp.bfloat16
I32 = jnp.int32

D_MODEL = 1024
GRID_W = 64
N_FOURIER_GROUPS = 4
FOURIER_GROUP_DIM = D_MODEL // N_FOURIER_GROUPS
N_HEADS = 16
HEAD_DIM = D_MODEL // N_HEADS
WIN_ROWS = 8
WIN_COLS = 16
N_EXPERTS = 32
N_GROUPS = 4
EXPERTS_PER_GROUP = N_EXPERTS // N_GROUPS
D_EXPERT = D_MODEL // 2
PAIRS_PER_GROUP = EXPERTS_PER_GROUP * (EXPERTS_PER_GROUP - 1) // 2
N_CLASSES = N_GROUPS * PAIRS_PER_GROUP
DEPTH = 2
DEEPNORM_ALPHA = (2 * DEPTH) ** 0.25
LN_EPS = 1e-5

V7X_VMEM_BYTES = 64 * 1024 * 1024
LANES = 128

FFT_N1 = 64
FFT_N2 = 128
FFT_CHUNK = 128

MM_BM = 1024
LN_BM = 512
EXPERT_BM = 384
COMBINE_BT = 512
NA_RQ = 4
NA_KROWS = NA_RQ + WIN_ROWS
NA_HEADS_PER_STEP = 8
NEG_BIAS = -1e30


def _vmem_limit(nbytes):
    return int(min(max(nbytes, 32 * 1024 * 1024), V7X_VMEM_BYTES - 8 * 1024 * 1024))


def _mm_kernel(a_ref, b_ref, o_ref, *, precision):
    if precision is None:
        a = a_ref[...].astype(BF16)
        b = b_ref[...].astype(BF16)
        acc = jnp.dot(a, b, preferred_element_type=F32)
    else:
        acc = jnp.dot(a_ref[...], b_ref[...], preferred_element_type=F32, precision=precision)
    o_ref[...] = acc.astype(o_ref.dtype)


def _matmul(a, b, out_dtype, bm, bn, precision=None):
    m, k = a.shape
    _, n = b.shape
    est = 2 * (bm * k * a.dtype.itemsize + k * bn * b.dtype.itemsize + bm * bn * 4) + 3 * bm * bn * 4
    return pl.pallas_call(
        functools.partial(_mm_kernel, precision=precision),
        grid=(m // bm, n // bn),
        in_specs=[pl.BlockSpec((bm, k), lambda i, j: (i, 0)),
                  pl.BlockSpec((k, bn), lambda i, j: (0, j))],
        out_specs=pl.BlockSpec((bm, bn), lambda i, j: (i, j)),
        out_shape=jax.ShapeDtypeStruct((m, n), out_dtype),
        compiler_params=pltpu.CompilerParams(
            dimension_semantics=("parallel", "parallel"), vmem_limit_bytes=_vmem_limit(est)),
        name="matmul",
    )(a, b)


def _fourier_channel_tables():
    n = FOURIER_GROUP_DIM
    c = np.arange(n)
    ang = 2.0 * np.pi * ((c[:, None] * c[None, :]) % n) / n
    cos, sin = np.cos(ang) / np.sqrt(n), np.sin(ang) / np.sqrt(n)
    half = FFT_CHUNK
    tabs = [np.concatenate([cos[:, h * half:(h + 1) * half], sin[:, h * half:(h + 1) * half]], axis=1)
            for h in range(n // half)]
    return np.stack(tabs).astype(np.float32)


def _fourier_seq_tables(seq):
    n1, n2 = FFT_N1, FFT_N2
    assert n1 * n2 == seq
    k2 = np.arange(n2)
    s2 = np.arange(n2)
    m1 = np.empty((n1, 2 * n2, 2 * n2), np.float32)
    for s1 in range(n1):
        ang = 2.0 * np.pi * ((k2[:, None] * (s1 + n1 * s2[None, :])) % seq) / seq
        mr, mi = np.cos(ang) / np.sqrt(n2), np.sin(ang) / np.sqrt(n2)
        m1[s1] = np.block([[mr, -mi], [mi, mr]])
    k1 = np.arange(n1)
    ang = 2.0 * np.pi * ((k1[:, None] * k1[None, :]) % n1) / n1
    w2 = np.concatenate([np.cos(ang), -np.sin(ang)], axis=1) / np.sqrt(n1)
    return m1, w2.astype(np.float32)


def _fft_kernel(a_ref, b_ref, m1_ref, w2_ref, o_ref, zs_ref):
    n1, n2, c = FFT_N1, FFT_N2, FFT_CHUNK

    def stage1(s1, carry):
        rows = pl.ds(s1, n2, stride=n1)
        x = jnp.concatenate([a_ref[0, rows, :], b_ref[0, rows, :]], axis=0).astype(BF16)
        z = jnp.dot(m1_ref[s1], x, preferred_element_type=F32)
        zs_ref[pl.ds(pl.multiple_of(s1 * 2 * n2, 2 * n2), 2 * n2), :] = z
        return carry

    lax.fori_loop(0, n1, stage1, 0, unroll=8)

    def stage2(kk, carry):
        k2 = 2 * kk
        parts = []
        for d in range(2):
            zr = zs_ref[pl.ds(k2 + d, n1, stride=2 * n2), :]
            zi = zs_ref[pl.ds(n2 + k2 + d, n1, stride=2 * n2), :]
            parts.append(jnp.concatenate([zr, zi], axis=0))
        z = jnp.concatenate(parts, axis=1).astype(BF16)
        y = jnp.dot(w2_ref[...], z, preferred_element_type=F32)
        o_ref[0, pl.ds(k2, n1, stride=n2), :] = y[:, :c]
        o_ref[0, pl.ds(k2 + 1, n1, stride=n2), :] = y[:, c:]
        return carry

    lax.fori_loop(0, n2 // 2, stage2, 0, unroll=8)


def _seq_fft(ab, m1, w2):
    b, s, two_d = ab.shape
    d = two_d // 2
    c = FFT_CHUNK
    est = 2 * (s * 2 * c * 4 + m1.size * 2 + s * c * 4) + FFT_N1 * 2 * FFT_N2 * c * 4 + (4 << 20)
    return pl.pallas_call(
        _fft_kernel,
        grid=(b, d // c),
        in_specs=[pl.BlockSpec((1, s, c), lambda i, j: (i, 0, 2 * j)),
                  pl.BlockSpec((1, s, c), lambda i, j: (i, 0, 2 * j + 1)),
                  pl.BlockSpec(m1.shape, lambda i, j: (0, 0, 0)),
                  pl.BlockSpec(w2.shape, lambda i, j: (0, 0))],
        out_specs=pl.BlockSpec((1, s, c), lambda i, j: (i, 0, j)),
        out_shape=jax.ShapeDtypeStruct((b, s, d), F32),
        scratch_shapes=[pltpu.VMEM((FFT_N1 * 2 * FFT_N2, c), F32)],
        compiler_params=pltpu.CompilerParams(
            dimension_semantics=("parallel", "parallel"), vmem_limit_bytes=_vmem_limit(est)),
        name="seq_fft",
    )(ab, ab, m1, w2)


def _fourier_mixer_pre_out(x, w_in):
    b, s, d = x.shape
    g, gd, c = N_FOURIER_GROUPS, FOURIER_GROUP_DIM, FFT_CHUNK
    halves = gd // c
    cs = jnp.asarray(_fourier_channel_tables())
    w_ab = pl.pallas_call(
        functools.partial(_mm_kernel, precision=lax.Precision.HIGHEST),
        grid=(g, halves),
        in_specs=[pl.BlockSpec((d, gd), lambda i, h: (0, i)),
                  pl.BlockSpec((None, gd, 2 * c), lambda i, h: (h, 0, 0))],
        out_specs=pl.BlockSpec((d, 2 * c), lambda i, h: (0, i * halves + h)),
        out_shape=jax.ShapeDtypeStruct((d, 2 * d), BF16),
        name="fourier_weight_fold",
    )(w_in, cs)
    ab = _matmul(x.reshape(b * s, d), w_ab, F32, MM_BM, 1024).reshape(b, s, 2 * d)
    m1, w2 = _fourier_seq_tables(s)
    return _seq_fft(ab, jnp.asarray(m1, dtype=BF16), jnp.asarray(w2, dtype=BF16))


def _layer_norm_rows(z, g, b):
    mu = jnp.mean(z, axis=-1, keepdims=True)
    zc = z - mu
    var = jnp.mean(zc * zc, axis=-1, keepdims=True)
    return zc * lax.rsqrt(var + LN_EPS) * g + b


def _pack_bf16_pairs(z):
    m = z.shape[1] // 2
    zb = z.astype(BF16).astype(F32)
    bits = lax.bitcast_convert_type(zb, jnp.uint32)
    return jnp.bitwise_or(bits[:, m:], lax.shift_right_logical(bits[:, :m], jnp.uint32(16)))


def _unpack_bf16_pairs(w):
    lo = lax.bitcast_convert_type(lax.shift_left(w, jnp.uint32(16)), F32)
    hi = lax.bitcast_convert_type(jnp.bitwise_and(w, jnp.uint32(0xFFFF0000)), F32)
    return jnp.concatenate([lo, hi], axis=1)


def _mm_ln_kernel(a_ref, w_ref, x_ref, g_ref, b_ref, wr_ref, br_ref, o_ref, opk_ref, cls_ref):
    y = jnp.dot(a_ref[...].astype(BF16), w_ref[...], preferred_element_type=F32)
    z = DEEPNORM_ALPHA * x_ref[...] + y
    out = _layer_norm_rows(z, g_ref[...], b_ref[...])
    o_ref[...] = out
    cls, gates = _route_rows(out, wr_ref[...], br_ref[...])
    cls_ref[...] = cls
    bm, dw = out.shape[0], out.shape[1] // 2
    on_diag = lax.broadcasted_iota(I32, (bm, bm), 0) == lax.broadcasted_iota(I32, (bm, bm), 1)
    lane = lax.broadcasted_iota(I32, (bm, LANES), 1)
    tile = jnp.zeros((bm, LANES), F32)
    for k in range(2):
        col = jnp.sum(jnp.where(on_diag, gates[k:k + 1, :], 0.0), axis=1, keepdims=True)
        tile = jnp.where(lane == k, col, tile)
    opk_ref[:, :dw] = _pack_bf16_pairs(out)
    opk_ref[:, dw:] = lax.bitcast_convert_type(tile, jnp.uint32)


def _proj_residual_ln(a, w_bf16, x, ln_g, ln_b, w_router_t, b_router):
    t, k = a.shape
    d = w_bf16.shape[1]
    e = w_router_t.shape[0]
    bm = LN_BM
    est = 2 * (bm * k * a.dtype.itemsize + k * d * 2 + 3 * bm * d * 4) + 8 * bm * d * 4
    return pl.pallas_call(
        _mm_ln_kernel,
        grid=(t // bm,),
        in_specs=[pl.BlockSpec((bm, k), lambda i: (i, 0)),
                  pl.BlockSpec((k, d), lambda i: (0, 0)),
                  pl.BlockSpec((bm, d), lambda i: (i, 0)),
                  pl.BlockSpec((1, d), lambda i: (0, 0)),
                  pl.BlockSpec((1, d), lambda i: (0, 0)),
                  pl.BlockSpec((e, d), lambda i: (0, 0)),
                  pl.BlockSpec((e, 1), lambda i: (0, 0))],
        out_specs=[pl.BlockSpec((bm, d), lambda i: (i, 0)),
                   pl.BlockSpec((bm, d // 2 + LANES), lambda i: (i, 0)),
                   pl.BlockSpec((1, bm), lambda i: (0, i))],
        out_shape=[jax.ShapeDtypeStruct((t, d), F32), jax.ShapeDtypeStruct((t, d // 2 + LANES), jnp.uint32),
                   jax.ShapeDtypeStruct((1, t), I32)],
        compiler_params=pltpu.CompilerParams(
            dimension_semantics=("parallel",), vmem_limit_bytes=_vmem_limit(est)),
        name="proj_residual_ln",
    )(a, w_bf16, x, ln_g.reshape(1, d), ln_b.reshape(1, d), w_router_t, b_router.reshape(e, 1))


def _top2_rows(v, iota):
    n_rows = v.shape[0]
    m1 = jnp.max(v, axis=0, keepdims=True)
    i1 = jnp.min(jnp.where(v == m1, iota, n_rows), axis=0, keepdims=True)
    v2 = jnp.where(iota == i1, -jnp.inf, v)
    m2 = jnp.max(v2, axis=0, keepdims=True)
    i2 = jnp.min(jnp.where(v2 == m2, iota, n_rows), axis=0, keepdims=True)
    return m1, i1, m2, i2


def _route_rows(x, w, b):
    xh = x.astype(BF16)
    xl = (x - xh.astype(F32)).astype(BF16)
    wh = w.astype(BF16)
    wl = (w - wh.astype(F32)).astype(BF16)
    nt = (((1,), (1,)), ((), ()))
    logits = (lax.dot_general(wh, xh, nt, preferred_element_type=F32)
              + lax.dot_general(wh, xl, nt, preferred_element_type=F32)
              + lax.dot_general(wl, xh, nt, preferred_element_type=F32))
    scores = 1.0 / (1.0 + jnp.exp(-logits))
    sel = scores + b
    epg = EXPERTS_PER_GROUP
    bt = x.shape[0]
    iota = lax.broadcasted_iota(I32, (epg, bt), 0)

    best = None
    for g in range(N_GROUPS):
        m1, _, m2, _ = _top2_rows(sel[g * epg:(g + 1) * epg], iota)
        gs = m1 + m2
        if best is None:
            best, gidx = gs, jnp.zeros((1, bt), I32)
        else:
            better = gs > best
            gidx = jnp.where(better, g, gidx)
            best = jnp.where(better, gs, best)

    sel_in = sel[0:epg]
    sc_in = scores[0:epg]
    for g in range(1, N_GROUPS):
        pick = gidx == g
        sel_in = jnp.where(pick, sel[g * epg:(g + 1) * epg], sel_in)
        sc_in = jnp.where(pick, scores[g * epg:(g + 1) * epg], sc_in)
    _, i1, _, i2 = _top2_rows(sel_in, iota)
    g1 = jnp.sum(jnp.where(iota == i1, sc_in, 0.0), axis=0, keepdims=True)
    g2 = jnp.sum(jnp.where(iota == i2, sc_in, 0.0), axis=0, keepdims=True)
    denom = g1 + g2
    first_lo = i1 < i2
    lo = jnp.where(first_lo, i1, i2)
    hi = jnp.where(first_lo, i2, i1)
    pair = lax.shift_right_logical(lo * (2 * epg - 1 - lo), 1) + (hi - lo - 1)
    cls = gidx * PAIRS_PER_GROUP + pair
    gates = jnp.concatenate([jnp.where(first_lo, g1, g2) / denom, jnp.where(first_lo, g2, g1) / denom], axis=0)
    return cls, gates


RANK_SUB = 256
RANK_NSUB = 8


def _rank_kernel(keys_ref, tri_ref, rank_ref, counts_ref, carry_ref, *, n_classes):
    nsub, sub = RANK_NSUB, RANK_SUB

    @pl.when(pl.program_id(0) == 0)
    def _():
        carry_ref[...] = jnp.zeros_like(carry_ref)

    cls = lax.broadcasted_iota(I32, (nsub, n_classes, sub), 1)
    onehot = cls == keys_ref[...]
    oh = jnp.where(onehot, 1.0, 0.0).reshape(nsub * n_classes, sub).astype(BF16)
    pref = jnp.dot(oh, tri_ref[...], preferred_element_type=F32).reshape(nsub, n_classes, sub)
    carry = carry_ref[...]
    for j in range(nsub):
        before = pref[j] + (carry - 1.0)
        rank_ref[j] = jnp.sum(jnp.where(onehot[j], before, 0.0), axis=0, keepdims=True).astype(I32)
        carry = carry + pref[j][:, sub - 1:sub]
    carry_ref[...] = carry
    counts_ref[...] = carry.astype(I32)


def _rank_within_class(keys, n_classes):
    n = keys.shape[0]
    nsub, sub = RANK_NSUB, RANK_SUB
    tri = jnp.asarray(np.triu(np.ones((sub, sub), np.float32)), dtype=BF16)
    rank, counts = pl.pallas_call(
        functools.partial(_rank_kernel, n_classes=n_classes),
        grid=(n // (nsub * sub),),
        in_specs=[pl.BlockSpec((nsub, 1, sub), lambda i: (i, 0, 0)),
                  pl.BlockSpec((sub, sub), lambda i: (0, 0))],
        out_specs=[pl.BlockSpec((nsub, 1, sub), lambda i: (i, 0, 0)),
                   pl.BlockSpec((n_classes, 1), lambda i: (0, 0))],
        out_shape=[jax.ShapeDtypeStruct((n // sub, 1, sub), I32),
                   jax.ShapeDtypeStruct((n_classes, 1), I32)],
        scratch_shapes=[pltpu.VMEM((n_classes, 1), F32)],
        compiler_params=pltpu.CompilerParams(dimension_semantics=("arbitrary",)),
        name="rank_within_class",
    )(keys.reshape(n // sub, 1, sub), tri)
    return rank.reshape(n), counts.reshape(n_classes)


FILL_CHUNK = 1024


def _fill_slots_kernel(dest_ref, cnt_ref, pend_ref, slot_ref, *, n_items, bm):
    n_classes = cnt_ref.shape[0]
    n_blocks = slot_ref.shape[0] // bm
    step = pl.program_id(0)

    @pl.when(step == 0)
    def _():
        def fill_block(blk, first):
            base = blk * bm

            def body(r, carry):
                slot_ref[base + r] = first + r
                return carry

            lax.fori_loop(0, bm, body, 0, unroll=16)

        def pad_class(k, carry):
            @pl.when(cnt_ref[k] > 0)
            def _():
                fill_block(lax.div(pend_ref[k], bm) - 1, n_items + k * bm)
            return carry

        lax.fori_loop(0, n_classes, pad_class, 0)

        def unused_block(blk, carry):
            fill_block(blk, 0)
            return carry

        lax.fori_loop(lax.div(pend_ref[n_classes - 1], bm), n_blocks, unused_block, 0)

    base = step * FILL_CHUNK

    def place(j, carry):
        slot_ref[dest_ref[j]] = base + j
        return carry

    lax.fori_loop(0, FILL_CHUNK, place, 0, unroll=16)


def _fill_slots(dest, counts, pad_end, n_slots, bm):
    n_items = dest.shape[0]
    assert n_items % FILL_CHUNK == 0 and n_slots % bm == 0
    smem = pl.BlockSpec(memory_space=pltpu.SMEM)
    return pl.pallas_call(
        functools.partial(_fill_slots_kernel, n_items=n_items, bm=bm),
        grid=(n_items // FILL_CHUNK,),
        in_specs=[pl.BlockSpec((FILL_CHUNK,), lambda i: (i,), memory_space=pltpu.SMEM), smem, smem],
        out_specs=smem,
        out_shape=jax.ShapeDtypeStruct((n_slots,), I32),
        compiler_params=pltpu.CompilerParams(dimension_semantics=("arbitrary",)),
        name="fill_slots",
    )(dest, counts, pad_end.astype(I32))


def _class_experts():
    epg = EXPERTS_PER_GROUP
    pairs = [(lo, hi) for lo in range(epg) for hi in range(lo + 1, epg)]
    return np.array([[g * epg + lo, g * epg + hi] for g in range(N_GROUPS) for lo, hi in pairs], np.int32)


def _route_slots(cls, bm):
    t = cls.shape[1]
    keys = cls.reshape(t)
    rank, counts = _rank_within_class(keys, N_CLASSES)
    padded = (counts + bm - 1) // bm * bm
    pad_end = jnp.cumsum(padded)
    pad_start = pad_end - padded
    class_ids = jnp.arange(N_CLASSES, dtype=I32)
    dest = jnp.sum(jnp.where(keys[:, None] == class_ids[None, :], pad_start[None, :], 0), axis=1) + rank
    p = (t // bm + N_CLASSES + 1) * bm
    slot_tok = _fill_slots(dest, counts, pad_end, p, bm)
    nb = p // bm
    block_start = jnp.arange(nb, dtype=I32) * bm
    block_class = jnp.minimum(
        jnp.sum((pad_end[None, :] <= block_start[:, None]).astype(I32), axis=1), N_CLASSES - 1)
    in_class = block_class[:, None, None] == class_ids[None, :, None]
    block_experts = jnp.sum(jnp.where(in_class, jnp.asarray(_class_experts())[None], 0), axis=1).reshape(2 * nb)
    n_active = (pad_end[-1] // bm).astype(I32).reshape(1)
    return slot_tok, block_experts, n_active


SUBLANES = 8
SUBLANE_SHIFT = SUBLANES.bit_length() - 1


def _expert_kernel(be_ref, tok_ref, nact_ref, x_hbm, wga_ref, wua_ref, wda_ref, wgb_ref, wub_ref, wdb_ref,
                   y_hbm, xnext, xcur, ycur, yout, wga_bf, wua_bf, wda_bf, wgb_bf, wub_bf, wdb_bf, gsem, ssem,
                   *, n_tok, n_classes):
    bm = EXPERT_BM
    tiles = bm // SUBLANES
    dw = ycur.shape[-1]
    i = pl.program_id(0)
    n_blocks = pl.num_programs(0)
    nact = nact_ref[0]
    w_f32 = ((wga_ref, wua_ref, wda_ref), (wgb_ref, wub_ref, wdb_ref))
    wbf = (wga_bf, wua_bf, wda_bf, wgb_bf, wub_bf, wdb_bf)

    def row_copies(blk, gather, unrolled):
        base = blk * bm

        def body(rt, carry):
            for u in range(SUBLANES):
                row = tok_ref[base + rt * SUBLANES + u]
                if gather:
                    row = jnp.bitwise_and(row, n_tok - 1)
                    pltpu.make_async_copy(
                        x_hbm.at[lax.shift_right_logical(row, SUBLANE_SHIFT),
                                 pl.ds(jnp.bitwise_and(row, SUBLANES - 1), 1), :],
                        xnext.at[rt, pl.ds(u, 1), :], gsem.at[0]).start()
                else:
                    pltpu.make_async_copy(
                        yout.at[rt, pl.ds(u, 1), :],
                        y_hbm.at[lax.shift_right_logical(row, SUBLANE_SHIFT),
                                 pl.ds(jnp.bitwise_and(row, SUBLANES - 1), 1), :],
                        ssem.at[0]).start()
            return carry

        if unrolled is None:
            lax.fori_loop(0, tiles, body, 0)
        else:
            for rt in range(*unrolled):
                body(rt, 0)

    def wait_gather():
        pltpu.make_async_copy(x_hbm.at[pl.ds(0, tiles)], xnext, gsem.at[0]).wait()

    def wait_scatter():
        pltpu.make_async_copy(yout, y_hbm.at[pl.ds(0, tiles)], ssem.at[0]).wait()

    @pl.when(i == 0)
    def _():
        row_copies(0, True, unrolled=None)
        yout[...] = jnp.zeros(yout.shape, yout.dtype)
        ycur[...] = jnp.zeros(ycur.shape, ycur.dtype)
        spare_tile0 = n_tok // SUBLANES

        def spare_copy(k):
            return pltpu.make_async_copy(
                yout, y_hbm.at[pl.ds(spare_tile0 + k * tiles, tiles)], ssem.at[0])

        @pl.loop(0, n_classes)
        def _(k):
            spare_copy(k).start()

        @pl.loop(0, n_classes)
        def _(k):
            spare_copy(k).wait()

    for side in range(2):
        idx = 2 * i + side
        changed = jnp.logical_or(i == 0, be_ref[idx] != be_ref[jnp.maximum(idx - 2, 0)])

        @pl.when(jnp.logical_and(changed, i < nact))
        def _():
            for m in range(3):
                wbf[3 * side + m][...] = w_f32[side][m][0].astype(BF16)

    @pl.when(i < nact)
    def _():
        wait_gather()
        xcur[...] = xnext[...]

        @pl.when(i >= 1)
        def _():
            wait_scatter()

        yout[...] = ycur[...]

    @pl.when(nact - i >= 1)
    def _():
        row_copies(jnp.minimum(i + 1, n_blocks - 1), True, unrolled=(0, tiles))
        row_copies(jnp.maximum(i - 1, 0), False, unrolled=(0, tiles))
        rows = xcur[...].reshape(bm, dw + LANES)
        x = _unpack_bf16_pairs(rows[:, :dw]).astype(BF16)
        gates = lax.bitcast_convert_type(rows[:, dw:], F32)
        y = None
        for side in range(2):
            wg_bf, wu_bf, wd_bf = wbf[3 * side], wbf[3 * side + 1], wbf[3 * side + 2]
            gate = jnp.dot(x, wg_bf[...], preferred_element_type=F32)
            up = jnp.dot(x, wu_bf[...], preferred_element_type=F32)
            h = (gate / (1.0 + jnp.exp(-gate))) * up
            ys = jnp.dot(h.astype(BF16), wd_bf[...], preferred_element_type=F32) * gates[:, side:side + 1]
            y = ys if y is None else y + ys
        ycur[...] = _pack_bf16_pairs(y).reshape(tiles, SUBLANES, dw)

    @pl.when(i == nact)
    def _():
        wait_gather()
        wait_scatter()
        yout[...] = ycur[...]
        row_copies(i - 1, False, unrolled=None)
        wait_scatter()


def _experts(x_pk, slot_tok, block_experts, n_active, w_gate, w_up, w_down, layer):
    t, dw_in = x_pk.shape
    _, _, d, f = w_gate.shape
    dw = d // 2
    assert dw_in == dw + LANES
    bm = EXPERT_BM
    n_blocks = slot_tok.shape[0] // bm
    n_rows = t + N_CLASSES * bm
    assert t & (t - 1) == 0 and t % SUBLANES == 0 and bm % SUBLANES == 0
    est = 2 * 6 * d * f * 4 + 6 * d * f * 2 + 4 * bm * d * 2 + 10 * bm * d * 4

    def w_spec(shape, side):
        return pl.BlockSpec((None, 1) + shape, lambda i, be, tk, na: (layer, be[2 * i + side], 0, 0))

    xbuf = pltpu.VMEM((bm // SUBLANES, SUBLANES, dw_in), jnp.uint32)
    ybuf = pltpu.VMEM((bm // SUBLANES, SUBLANES, dw), jnp.uint32)
    grid_spec = pltpu.PrefetchScalarGridSpec(
        num_scalar_prefetch=3,
        grid=(n_blocks,),
        in_specs=[pl.BlockSpec(memory_space=pl.ANY),
                  w_spec((d, f), 0), w_spec((d, f), 0), w_spec((f, d), 0),
                  w_spec((d, f), 1), w_spec((d, f), 1), w_spec((f, d), 1)],
        out_specs=pl.BlockSpec(memory_space=pl.ANY),
        scratch_shapes=[xbuf, xbuf, ybuf, ybuf]
                       + [pltpu.VMEM((d, f), BF16), pltpu.VMEM((d, f), BF16), pltpu.VMEM((f, d), BF16)] * 2
                       + [pltpu.SemaphoreType.DMA((1,)), pltpu.SemaphoreType.DMA((1,))],
    )
    y = pl.pallas_call(
        functools.partial(_expert_kernel, n_tok=t, n_classes=N_CLASSES),
        grid_spec=grid_spec,
        out_shape=jax.ShapeDtypeStruct((n_rows // SUBLANES, SUBLANES, dw), jnp.uint32),
        compiler_params=pltpu.CompilerParams(
            dimension_semantics=("arbitrary",), vmem_limit_bytes=_vmem_limit(est)),
        name="experts",
    )(block_experts, slot_tok, n_active, x_pk.reshape(t // SUBLANES, SUBLANES, dw_in),
      w_gate, w_up, w_down, w_gate, w_up, w_down)
    return y.reshape(n_rows, dw)


def _combine_kernel(x_ref, y_ref, g_ref, b_ref, o_ref):
    z = DEEPNORM_ALPHA * x_ref[...] + _unpack_bf16_pairs(y_ref[...])
    o_ref[...] = _layer_norm_rows(z, g_ref[...], b_ref[...])


def _combine_ln(x, y_tok, ln_g, ln_b):
    t, d = x.shape
    bt = COMBINE_BT
    return pl.pallas_call(
        _combine_kernel,
        grid=(t // bt,),
        in_specs=[pl.BlockSpec((bt, d), lambda i: (i, 0)),
                  pl.BlockSpec((bt, d // 2), lambda i: (i, 0)),
                  pl.BlockSpec((1, d), lambda i: (0, 0)),
                  pl.BlockSpec((1, d), lambda i: (0, 0))],
        out_specs=pl.BlockSpec((bt, d), lambda i: (i, 0)),
        out_shape=jax.ShapeDtypeStruct((t, d), F32),
        compiler_params=pltpu.CompilerParams(
            dimension_semantics=("parallel",), vmem_limit_bytes=_vmem_limit(10 * bt * d * 4)),
        name="combine_ln",
    )(x, y_tok, ln_g.reshape(1, d), ln_b.reshape(1, d))


def _moe_residual_ln(x, x_pk, cls, w_gate, w_up, w_down, layer, ln_g, ln_b):
    slot_tok, block_experts, n_active = _route_slots(cls, EXPERT_BM)
    y_tok = _experts(x_pk, slot_tok, block_experts, n_active, w_gate, w_up, w_down, layer)
    return _combine_ln(x, y_tok, ln_g, ln_b)


def _na_bias_pairs(rpb):
    w = GRID_W
    kc = min(WIN_COLS, w)
    cols = np.arange(w)
    col_start = np.clip(cols - kc // 2, 0, w - kc)
    col_mask = (cols[None, :] >= col_start[:, None]) & (cols[None, :] < col_start[:, None] + kc)
    col_idx = np.clip(cols[None, :] - cols[:, None] + WIN_COLS - 1, 0, 2 * WIN_COLS - 2)
    n_ci = 2 * WIN_COLS - 1
    pick = (col_idx[None] == np.arange(n_ci)[:, None, None]).astype(np.float32)
    tb = jnp.einsum("hrc,cqk->hrqk", rpb.astype(F32), jnp.asarray(pick), precision=lax.Precision.HIGHEST)
    tb = jnp.where(col_mask[None, None], tb, NEG_BIAS)
    neg = jnp.full((N_HEADS, 1, w, w), NEG_BIAS, F32)
    ext = jnp.concatenate([neg, tb, neg], axis=1)
    return jnp.concatenate([ext[:, :-1], ext[:, 1:]], axis=-1)


def _na_tile_plan(rows):
    kr = min(WIN_ROWS, rows)
    n_blocks = rows // NA_RQ
    plan = []
    for blk in (0, 1, n_blocks - 1):
        ks = int(np.clip(blk - 1, 0, n_blocks - 3)) * NA_RQ
        per_rq = []
        for rq in range(NA_RQ):
            r = blk * NA_RQ + rq
            rs = int(np.clip(r - kr // 2, 0, rows - kr))
            ri = [ks + j - r + WIN_ROWS - 1 if rs <= ks + j < rs + kr else None for j in range(NA_KROWS)]
            tiles = []
            for jp in range(NA_KROWS // 2):
                a, b = ri[2 * jp], ri[2 * jp + 1]
                if a is not None and b is not None:
                    tiles.append(("both", b))
                elif b is not None:
                    tiles.append(("hi", b))
                elif a is not None:
                    tiles.append(("lo", a + 1))
                else:
                    tiles.append(("none", 0))
            per_rq.append(tiles)
        plan.append(per_rq)
    return plan


def _na_kernel(q_ref, k0_ref, k1_ref, k2_ref, v0_ref, v1_ref, v2_ref, pair_ref, o_ref, bias_ref, *, plan):
    dh = HEAD_DIM
    w = GRID_W
    nt = (((1,), (1,)), ((), ()))
    scale = HEAD_DIM ** -0.5
    i = pl.program_id(2)
    n_blocks = pl.num_programs(2)

    def rebuild(per_rq):
        upper = lax.broadcasted_iota(I32, (w, 2 * w), 1) >= w
        for rq, tiles in enumerate(per_rq):
            for jp, (kind, k) in enumerate(tiles):
                for h in range(NA_HEADS_PER_STEP):
                    if kind == "none":
                        tile = jnp.full((w, 2 * w), NEG_BIAS, F32)
                    elif kind == "both":
                        tile = pair_ref[h, k]
                    elif kind == "hi":
                        tile = jnp.where(upper, pair_ref[h, k], NEG_BIAS)
                    else:
                        tile = jnp.where(upper, NEG_BIAS, pair_ref[h, k])
                    bias_ref[h, rq * w:(rq + 1) * w, jp * 2 * w:(jp + 1) * 2 * w] = tile

    for ty, first_step in enumerate((0, 1, n_blocks - 1)):
        @pl.when(i == first_step)
        def _():
            rebuild(plan[ty])

    assert 2 * dh == LANES
    tq = q_ref.shape[1]
    low = lax.broadcasted_iota(I32, (tq, LANES), 1) < dh
    low_k = lax.broadcasted_iota(I32, (3 * tq, LANES), 1) < dh
    outs = []
    for hp in range(NA_HEADS_PER_STEP // 2):
        cols = slice(hp * LANES, (hp + 1) * LANES)
        q2 = q_ref[0, :, cols] * scale
        k2 = jnp.concatenate([k0_ref[0, :, cols], k1_ref[0, :, cols], k2_ref[0, :, cols]], axis=0)
        v2 = jnp.concatenate([v0_ref[0, :, cols], v1_ref[0, :, cols], v2_ref[0, :, cols]], axis=0)
        halves = []
        for sub in range(2):
            mine, mine_k = (low, low_k) if sub == 0 else (~low, ~low_k)
            q = jnp.where(mine, q2, jnp.zeros_like(q2))
            v = jnp.where(mine_k, v2, jnp.ones_like(v2))
            s = lax.dot_general(q, k2, nt, preferred_element_type=F32) + bias_ref[2 * hp + sub]
            m = jnp.max(s, axis=-1, keepdims=True)
            p = jnp.exp((s - m).astype(BF16))
            o = jnp.dot(p, v, preferred_element_type=F32)
            denom = o[:, dh:dh + 1] if sub == 0 else o[:, 0:1]
            halves.append(o * (1.0 / denom))
        outs.append(jnp.where(low, halves[0], halves[1]))
    o_ref[0] = jnp.concatenate(outs, axis=-1).astype(o_ref.dtype)


def _neighbourhood_attention(qkv, bias_pairs, rows):
    b, s, _ = qkv.shape
    d = D_MODEL
    w = GRID_W
    tq = NA_RQ * w
    n_blocks = rows // NA_RQ
    hps = NA_HEADS_PER_STEP
    hw = hps * HEAD_DIM
    n_hh = d // hw
    assert NA_KROWS * w == 3 * tq and n_blocks >= 3 and NA_KROWS % 2 == 0

    def kv_map(part, j):
        def index_map(hh, bi, i):
            return (bi, jnp.clip(i - 1, 0, n_blocks - 3) + j, part * n_hh + hh)
        return index_map

    blk = (1, tq, hw)
    pair_blk = (hps,) + bias_pairs.shape[1:]
    est = (2 * (7 * tq * hw * 2 + tq * hw * 2) + 2 * 4 * int(np.prod(pair_blk))
           + hps * tq * 3 * tq * 4 + 8 * tq * 3 * tq * 4)
    return pl.pallas_call(
        functools.partial(_na_kernel, plan=_na_tile_plan(rows)),
        grid=(n_hh, b, n_blocks),
        in_specs=[pl.BlockSpec(blk, lambda hh, bi, i: (bi, i, hh))]
                 + [pl.BlockSpec(blk, kv_map(1, j)) for j in range(3)]
                 + [pl.BlockSpec(blk, kv_map(2, j)) for j in range(3)]
                 + [pl.BlockSpec(pair_blk, lambda hh, bi, i: (hh, 0, 0, 0))],
        out_specs=pl.BlockSpec(blk, lambda hh, bi, i: (bi, i, hh)),
        out_shape=jax.ShapeDtypeStruct((b, s, d), BF16),
        scratch_shapes=[pltpu.VMEM((hps, tq, 3 * tq), F32)],
        compiler_params=pltpu.CompilerParams(
            dimension_semantics=("arbitrary", "arbitrary", "arbitrary"), vmem_limit_bytes=_vmem_limit(est)),
        name="neighbourhood_attention",
    )(qkv, qkv, qkv, qkv, qkv, qkv, qkv, bias_pairs)


def kernel(x, fourier_w_in, fourier_w_out, na_w_qkv, na_rpb, na_w_out, router_w, router_b,
           expert_w_gate, expert_w_up, expert_w_down, ln_g, ln_b):
    b, s, d = x.shape
    t = b * s
    rows = s // GRID_W
    w_router_t = router_w.T
    xt = x.reshape(t, d)

    f = _fourier_mixer_pre_out(x, fourier_w_in[0])
    xt, xt_pk, cls = _proj_residual_ln(f.reshape(t, d), fourier_w_out[0].astype(BF16), xt,
                                       ln_g[0, 0], ln_b[0, 0], w_router_t, router_b)
    xt = _moe_residual_ln(xt, xt_pk, cls, expert_w_gate, expert_w_up, expert_w_down, 0,
                          ln_g[0, 1], ln_b[0, 1])

    qkv = _matmul(xt, na_w_qkv[0].astype(BF16), BF16, MM_BM, 1024).reshape(b, s, 3 * d)
    bias = _na_bias_pairs(na_rpb[0])
    o = _neighbourhood_attention(qkv, bias, rows)
    xt, xt_pk, cls = _proj_residual_ln(o.reshape(t, d), na_w_out[0].astype(BF16), xt,
                                       ln_g[1, 0], ln_b[1, 0], w_router_t, router_b)
    xt = _moe_residual_ln(xt, xt_pk, cls, expert_w_gate, expert_w_up, expert_w_down, 1,
                          ln_g[1, 1], ln_b[1, 1])
    return xt.reshape(b, s, d)
```

```python
import functools

import jax
import jax.numpy as jnp
import numpy as np
from jax import lax
from jax.experimental import pallas as pl
from jax.experimental.pallas import tpu as pltpu

F32 = jnp.float32
BF16 = jnp.bfloat16
I32 = jnp.int32

D_MODEL = 1024
GRID_W = 64
N_FOURIER_GROUPS = 4
FOURIER_GROUP_DIM = D_MODEL // N_FOURIER_GROUPS
N_HEADS = 16
HEAD_DIM = D_MODEL // N_HEADS
WIN_ROWS = 8
WIN_COLS = 16
N_EXPERTS = 32
N_GROUPS = 4
EXPERTS_PER_GROUP = N_EXPERTS // N_GROUPS
D_EXPERT = D_MODEL // 2
PAIRS_PER_GROUP = EXPERTS_PER_GROUP * (EXPERTS_PER_GROUP - 1) // 2
N_CLASSES = N_GROUPS * PAIRS_PER_GROUP
DEPTH = 2
DEEPNORM_ALPHA = (2 * DEPTH) ** 0.25
LN_EPS = 1e-5

V7X_VMEM_BYTES = 64 * 1024 * 1024
LANES = 128

FFT_N1 = 64
FFT_N2 = 128
FFT_CHUNK = 128

MM_BM = 1024
LN_BM = 512
EXPERT_BM = 384
COMBINE_BT = 512
NA_RQ = 4
NA_KROWS = NA_RQ + WIN_ROWS
NA_HEADS_PER_STEP = 8
NEG_BIAS = -1e30


def _vmem_limit(nbytes):
    return int(min(max(nbytes, 32 * 1024 * 1024), V7X_VMEM_BYTES - 8 * 1024 * 1024))


def _mm_kernel(a_ref, b_ref, o_ref, *, precision):
    if precision is None:
        a = a_ref[...].astype(BF16)
        b = b_ref[...].astype(BF16)
        acc = jnp.dot(a, b, preferred_element_type=F32)
    else:
        acc = jnp.dot(a_ref[...], b_ref[...], preferred_element_type=F32, precision=precision)
    o_ref[...] = acc.astype(o_ref.dtype)


def _matmul(a, b, out_dtype, bm, bn, precision=None):
    m, k = a.shape
    _, n = b.shape
    est = 2 * (bm * k * a.dtype.itemsize + k * bn * b.dtype.itemsize + bm * bn * 4) + 3 * bm * bn * 4
    return pl.pallas_call(
        functools.partial(_mm_kernel, precision=precision),
        grid=(m // bm, n // bn),
        in_specs=[pl.BlockSpec((bm, k), lambda i, j: (i, 0)),
                  pl.BlockSpec((k, bn), lambda i, j: (0, j))],
        out_specs=pl.BlockSpec((bm, bn), lambda i, j: (i, j)),
        out_shape=jax.ShapeDtypeStruct((m, n), out_dtype),
        compiler_params=pltpu.CompilerParams(
            dimension_semantics=("parallel", "parallel"), vmem_limit_bytes=_vmem_limit(est)),
        name="matmul",
    )(a, b)


def _fourier_channel_tables():
    n = FOURIER_GROUP_DIM
    c = np.arange(n)
    ang = 2.0 * np.pi * ((c[:, None] * c[None, :]) % n) / n
    cos, sin = np.cos(ang) / np.sqrt(n), np.sin(ang) / np.sqrt(n)
    half = FFT_CHUNK
    tabs = [np.concatenate([cos[:, h * half:(h + 1) * half], sin[:, h * half:(h + 1) * half]], axis=1)
            for h in range(n // half)]
    return np.stack(tabs).astype(np.float32)


def _fourier_seq_tables(seq):
    n1, n2 = FFT_N1, FFT_N2
    assert n1 * n2 == seq
    k2 = np.arange(n2)
    s2 = np.arange(n2)
    m1 = np.empty((n1, 2 * n2, 2 * n2), np.float32)
    for s1 in range(n1):
        ang = 2.0 * np.pi * ((k2[:, None] * (s1 + n1 * s2[None, :])) % seq) / seq
        mr, mi = np.cos(ang) / np.sqrt(n2), np.sin(ang) / np.sqrt(n2)
        m1[s1] = np.block([[mr, -mi], [mi, mr]])
    k1 = np.arange(n1)
    ang = 2.0 * np.pi * ((k1[:, None] * k1[None, :]) % n1) / n1
    w2 = np.concatenate([np.cos(ang), -np.sin(ang)], axis=1) / np.sqrt(n1)
    return m1, w2.astype(np.float32)


def _fft_kernel(a_ref, b_ref, m1_ref, w2_ref, o_ref, zs_ref):
    n1, n2, c = FFT_N1, FFT_N2, FFT_CHUNK

    def stage1(s1, carry):
        rows = pl.ds(s1, n2, stride=n1)
        x = jnp.concatenate([a_ref[0, rows, :], b_ref[0, rows, :]], axis=0).astype(BF16)
        z = jnp.dot(m1_ref[s1], x, preferred_element_type=F32)
        zs_ref[pl.ds(pl.multiple_of(s1 * 2 * n2, 2 * n2), 2 * n2), :] = z
        return carry

    lax.fori_loop(0, n1, stage1, 0, unroll=8)

    def stage2(kk, carry):
        k2 = 2 * kk
        parts = []
        for d in range(2):
            zr = zs_ref[pl.ds(k2 + d, n1, stride=2 * n2), :]
            zi = zs_ref[pl.ds(n2 + k2 + d, n1, stride=2 * n2), :]
            parts.append(jnp.concatenate([zr, zi], axis=0))
        z = jnp.concatenate(parts, axis=1).astype(BF16)
        y = jnp.dot(w2_ref[...], z, preferred_element_type=F32)
        o_ref[0, pl.ds(k2, n1, stride=n2), :] = y[:, :c]
        o_ref[0, pl.ds(k2 + 1, n1, stride=n2), :] = y[:, c:]
        return carry

    lax.fori_loop(0, n2 // 2, stage2, 0, unroll=8)


def _seq_fft(ab, m1, w2):
    b, s, two_d = ab.shape
    d = two_d // 2
    c = FFT_CHUNK
    est = 2 * (s * 2 * c * 4 + m1.size * 2 + s * c * 4) + FFT_N1 * 2 * FFT_N2 * c * 4 + (4 << 20)
    return pl.pallas_call(
        _fft_kernel,
        grid=(b, d // c),
        in_specs=[pl.BlockSpec((1, s, c), lambda i, j: (i, 0, 2 * j)),
                  pl.BlockSpec((1, s, c), lambda i, j: (i, 0, 2 * j + 1)),
                  pl.BlockSpec(m1.shape, lambda i, j: (0, 0, 0)),
                  pl.BlockSpec(w2.shape, lambda i, j: (0, 0))],
        out_specs=pl.BlockSpec((1, s, c), lambda i, j: (i, 0, j)),
        out_shape=jax.ShapeDtypeStruct((b, s, d), F32),
        scratch_shapes=[pltpu.VMEM((FFT_N1 * 2 * FFT_N2, c), F32)],
        compiler_params=pltpu.CompilerParams(
            dimension_semantics=("parallel", "parallel"), vmem_limit_bytes=_vmem_limit(est)),
        name="seq_fft",
    )(ab, ab, m1, w2)


def _fourier_mixer_pre_out(x, w_in):
    b, s, d = x.shape
    g, gd, c = N_FOURIER_GROUPS, FOURIER_GROUP_DIM, FFT_CHUNK
    halves = gd // c
    cs = jnp.asarray(_fourier_channel_tables())
    w_ab = pl.pallas_call(
        functools.partial(_mm_kernel, precision=lax.Precision.HIGHEST),
        grid=(g, halves),
        in_specs=[pl.BlockSpec((d, gd), lambda i, h: (0, i)),
                  pl.BlockSpec((None, gd, 2 * c), lambda i, h: (h, 0, 0))],
        out_specs=pl.BlockSpec((d, 2 * c), lambda i, h: (0, i * halves + h)),
        out_shape=jax.ShapeDtypeStruct((d, 2 * d), BF16),
        name="fourier_weight_fold",
    )(w_in, cs)
    ab = _matmul(x.reshape(b * s, d), w_ab, F32, MM_BM, 2 * d).reshape(b, s, 2 * d)
    m1, w2 = _fourier_seq_tables(s)
    return _seq_fft(ab, jnp.asarray(m1, dtype=BF16), jnp.asarray(w2, dtype=BF16))


def _layer_norm_rows(z, g, b):
    mu = jnp.mean(z, axis=-1, keepdims=True)
    zc = z - mu
    var = jnp.mean(zc * zc, axis=-1, keepdims=True)
    return zc * lax.rsqrt(var + LN_EPS) * g + b


def _pack_bf16_pairs(z):
    m = z.shape[1] // 2
    zb = z.astype(BF16).astype(F32)
    bits = lax.bitcast_convert_type(zb, jnp.uint32)
    return jnp.bitwise_or(bits[:, m:], lax.shift_right_logical(bits[:, :m], jnp.uint32(16)))


def _unpack_bf16_pairs(w):
    lo = lax.bitcast_convert_type(lax.shift_left(w, jnp.uint32(16)), F32)
    hi = lax.bitcast_convert_type(jnp.bitwise_and(w, jnp.uint32(0xFFFF0000)), F32)
    return jnp.concatenate([lo, hi], axis=1)


def _mm_ln_kernel(a_ref, w_ref, x_ref, g_ref, b_ref, wr_ref, br_ref, o_ref, opk_ref, cls_ref):
    y = jnp.dot(a_ref[...].astype(BF16), w_ref[...], preferred_element_type=F32)
    z = DEEPNORM_ALPHA * x_ref[...] + y
    out = _layer_norm_rows(z, g_ref[...], b_ref[...])
    o_ref[...] = out
    cls, gates = _route_rows(out, wr_ref[...], br_ref[...])
    cls_ref[...] = cls
    bm, dw = out.shape[0], out.shape[1] // 2
    on_diag = lax.broadcasted_iota(I32, (bm, bm), 0) == lax.broadcasted_iota(I32, (bm, bm), 1)
    lane = lax.broadcasted_iota(I32, (bm, LANES), 1)
    tile = jnp.zeros((bm, LANES), F32)
    for k in range(2):
        col = jnp.sum(jnp.where(on_diag, gates[k:k + 1, :], 0.0), axis=1, keepdims=True)
        tile = jnp.where(lane == k, col, tile)
    opk_ref[:, :dw] = _pack_bf16_pairs(out)
    opk_ref[:, dw:] = lax.bitcast_convert_type(tile, jnp.uint32)


def _proj_residual_ln(a, w_bf16, x, ln_g, ln_b, w_router_t, b_router):
    t, k = a.shape
    d = w_bf16.shape[1]
    e = w_router_t.shape[0]
    bm = LN_BM
    est = 2 * (bm * k * a.dtype.itemsize + k * d * 2 + 3 * bm * d * 4) + 8 * bm * d * 4
    return pl.pallas_call(
        _mm_ln_kernel,
        grid=(t // bm,),
        in_specs=[pl.BlockSpec((bm, k), lambda i: (i, 0)),
                  pl.BlockSpec((k, d), lambda i: (0, 0)),
                  pl.BlockSpec((bm, d), lambda i: (i, 0)),
                  pl.BlockSpec((1, d), lambda i: (0, 0)),
                  pl.BlockSpec((1, d), lambda i: (0, 0)),
                  pl.BlockSpec((e, d), lambda i: (0, 0)),
                  pl.BlockSpec((e, 1), lambda i: (0, 0))],
        out_specs=[pl.BlockSpec((bm, d), lambda i: (i, 0)),
                   pl.BlockSpec((bm, d // 2 + LANES), lambda i: (i, 0)),
                   pl.BlockSpec((1, bm), lambda i: (0, i))],
        out_shape=[jax.ShapeDtypeStruct((t, d), F32), jax.ShapeDtypeStruct((t, d // 2 + LANES), jnp.uint32),
                   jax.ShapeDtypeStruct((1, t), I32)],
        compiler_params=pltpu.CompilerParams(
            dimension_semantics=("parallel",), vmem_limit_bytes=_vmem_limit(est)),
        name="proj_residual_ln",
    )(a, w_bf16, x, ln_g.reshape(1, d), ln_b.reshape(1, d), w_router_t, b_router.reshape(e, 1))


def _top2_rows(v, iota):
    n_rows = v.shape[0]
    m1 = jnp.max(v, axis=0, keepdims=True)
    i1 = jnp.min(jnp.where(v == m1, iota, n_rows), axis=0, keepdims=True)
    v2 = jnp.where(iota == i1, -jnp.inf, v)
    m2 = jnp.max(v2, axis=0, keepdims=True)
    i2 = jnp.min(jnp.where(v2 == m2, iota, n_rows), axis=0, keepdims=True)
    return m1, i1, m2, i2


def _route_rows(x, w, b):
    xh = x.astype(BF16)
    xl = (x - xh.astype(F32)).astype(BF16)
    wh = w.astype(BF16)
    wl = (w - wh.astype(F32)).astype(BF16)
    nt = (((1,), (1,)), ((), ()))
    logits = (lax.dot_general(wh, xh, nt, preferred_element_type=F32)
              + lax.dot_general(wh, xl, nt, preferred_element_type=F32)
              + lax.dot_general(wl, xh, nt, preferred_element_type=F32))
    scores = 1.0 / (1.0 + jnp.exp(-logits))
    sel = scores + b
    epg = EXPERTS_PER_GROUP
    bt = x.shape[0]
    iota = lax.broadcasted_iota(I32, (epg, bt), 0)

    best = None
    for g in range(N_GROUPS):
        m1, _, m2, _ = _top2_rows(sel[g * epg:(g + 1) * epg], iota)
        gs = m1 + m2
        if best is None:
            best, gidx = gs, jnp.zeros((1, bt), I32)
        else:
            better = gs > best
            gidx = jnp.where(better, g, gidx)
            best = jnp.where(better, gs, best)

    sel_in = sel[0:epg]
    sc_in = scores[0:epg]
    for g in range(1, N_GROUPS):
        pick = gidx == g
        sel_in = jnp.where(pick, sel[g * epg:(g + 1) * epg], sel_in)
        sc_in = jnp.where(pick, scores[g * epg:(g + 1) * epg], sc_in)
    _, i1, _, i2 = _top2_rows(sel_in, iota)
    g1 = jnp.sum(jnp.where(iota == i1, sc_in, 0.0), axis=0, keepdims=True)
    g2 = jnp.sum(jnp.where(iota == i2, sc_in, 0.0), axis=0, keepdims=True)
    denom = g1 + g2
    first_lo = i1 < i2
    lo = jnp.where(first_lo, i1, i2)
    hi = jnp.where(first_lo, i2, i1)
    pair = lax.shift_right_logical(lo * (2 * epg - 1 - lo), 1) + (hi - lo - 1)
    cls = gidx * PAIRS_PER_GROUP + pair
    gates = jnp.concatenate([jnp.where(first_lo, g1, g2) / denom, jnp.where(first_lo, g2, g1) / denom], axis=0)
    return cls, gates


RANK_SUB = 256
RANK_NSUB = 8


def _rank_kernel(keys_ref, tri_ref, rank_ref, counts_ref, carry_ref, *, n_classes):
    nsub, sub = RANK_NSUB, RANK_SUB

    @pl.when(pl.program_id(0) == 0)
    def _():
        carry_ref[...] = jnp.zeros_like(carry_ref)

    cls = lax.broadcasted_iota(I32, (nsub, n_classes, sub), 1)
    onehot = cls == keys_ref[...]
    oh = jnp.where(onehot, 1.0, 0.0).reshape(nsub * n_classes, sub).astype(BF16)
    pref = jnp.dot(oh, tri_ref[...], preferred_element_type=F32).reshape(nsub, n_classes, sub)
    carry = carry_ref[...]
    for j in range(nsub):
        before = pref[j] + (carry - 1.0)
        rank_ref[j] = jnp.sum(jnp.where(onehot[j], before, 0.0), axis=0, keepdims=True).astype(I32)
        carry = carry + pref[j][:, sub - 1:sub]
    carry_ref[...] = carry
    counts_ref[...] = carry.astype(I32)


def _rank_within_class(keys, n_classes):
    n = keys.shape[0]
    nsub, sub = RANK_NSUB, RANK_SUB
    tri = jnp.asarray(np.triu(np.ones((sub, sub), np.float32)), dtype=BF16)
    rank, counts = pl.pallas_call(
        functools.partial(_rank_kernel, n_classes=n_classes),
        grid=(n // (nsub * sub),),
        in_specs=[pl.BlockSpec((nsub, 1, sub), lambda i: (i, 0, 0)),
                  pl.BlockSpec((sub, sub), lambda i: (0, 0))],
        out_specs=[pl.BlockSpec((nsub, 1, sub), lambda i: (i, 0, 0)),
                   pl.BlockSpec((n_classes, 1), lambda i: (0, 0))],
        out_shape=[jax.ShapeDtypeStruct((n // sub, 1, sub), I32),
                   jax.ShapeDtypeStruct((n_classes, 1), I32)],
        scratch_shapes=[pltpu.VMEM((n_classes, 1), F32)],
        compiler_params=pltpu.CompilerParams(dimension_semantics=("arbitrary",)),
        name="rank_within_class",
    )(keys.reshape(n // sub, 1, sub), tri)
    return rank.reshape(n), counts.reshape(n_classes)


FILL_CHUNK = 1024


def _fill_slots_kernel(dest_ref, cnt_ref, pend_ref, slot_ref, *, n_items, bm):
    n_classes = cnt_ref.shape[0]
    n_blocks = slot_ref.shape[0] // bm
    step = pl.program_id(0)

    @pl.when(step == 0)
    def _():
        def fill_block(blk, first):
            base = blk * bm

            def body(r, carry):
                slot_ref[base + r] = first + r
                return carry

            lax.fori_loop(0, bm, body, 0, unroll=16)

        def pad_class(k, carry):
            @pl.when(cnt_ref[k] > 0)
            def _():
                fill_block(lax.div(pend_ref[k], bm) - 1, n_items + k * bm)
            return carry

        lax.fori_loop(0, n_classes, pad_class, 0)

        def unused_block(blk, carry):
            fill_block(blk, 0)
            return carry

        lax.fori_loop(lax.div(pend_ref[n_classes - 1], bm), n_blocks, unused_block, 0)

    base = step * FILL_CHUNK

    def place(j, carry):
        slot_ref[dest_ref[j]] = base + j
        return carry

    lax.fori_loop(0, FILL_CHUNK, place, 0, unroll=16)


def _fill_slots(dest, counts, pad_end, n_slots, bm):
    n_items = dest.shape[0]
    assert n_items % FILL_CHUNK == 0 and n_slots % bm == 0
    smem = pl.BlockSpec(memory_space=pltpu.SMEM)
    return pl.pallas_call(
        functools.partial(_fill_slots_kernel, n_items=n_items, bm=bm),
        grid=(n_items // FILL_CHUNK,),
        in_specs=[pl.BlockSpec((FILL_CHUNK,), lambda i: (i,), memory_space=pltpu.SMEM), smem, smem],
        out_specs=smem,
        out_shape=jax.ShapeDtypeStruct((n_slots,), I32),
        compiler_params=pltpu.CompilerParams(dimension_semantics=("arbitrary",)),
        name="fill_slots",
    )(dest, counts, pad_end.astype(I32))


def _class_experts():
    epg = EXPERTS_PER_GROUP
    pairs = [(lo, hi) for lo in range(epg) for hi in range(lo + 1, epg)]
    return np.array([[g * epg + lo, g * epg + hi] for g in range(N_GROUPS) for lo, hi in pairs], np.int32)


def _route_slots(cls, bm):
    t = cls.shape[1]
    keys = cls.reshape(t)
    rank, counts = _rank_within_class(keys, N_CLASSES)
    padded = (counts + bm - 1) // bm * bm
    pad_end = jnp.cumsum(padded)
    pad_start = pad_end - padded
    class_ids = jnp.arange(N_CLASSES, dtype=I32)
    dest = jnp.sum(jnp.where(keys[:, None] == class_ids[None, :], pad_start[None, :], 0), axis=1) + rank
    p = (t // bm + N_CLASSES + 1) * bm
    slot_tok = _fill_slots(dest, counts, pad_end, p, bm)
    nb = p // bm
    block_start = jnp.arange(nb, dtype=I32) * bm
    block_class = jnp.minimum(
        jnp.sum((pad_end[None, :] <= block_start[:, None]).astype(I32), axis=1), N_CLASSES - 1)
    in_class = block_class[:, None, None] == class_ids[None, :, None]
    block_experts = jnp.sum(jnp.where(in_class, jnp.asarray(_class_experts())[None], 0), axis=1).reshape(2 * nb)
    n_active = (pad_end[-1] // bm).astype(I32).reshape(1)
    return slot_tok, block_experts, n_active


SUBLANES = 8
SUBLANE_SHIFT = SUBLANES.bit_length() - 1


def _expert_kernel(be_ref, tok_ref, nact_ref, x_hbm, wga_ref, wua_ref, wda_ref, wgb_ref, wub_ref, wdb_ref,
                   y_hbm, xnext, xcur, ycur, yout, wga_bf, wua_bf, wda_bf, wgb_bf, wub_bf, wdb_bf, gsem, ssem,
                   *, n_tok, n_classes):
    bm = EXPERT_BM
    tiles = bm // SUBLANES
    dw = ycur.shape[-1]
    i = pl.program_id(0)
    n_blocks = pl.num_programs(0)
    nact = nact_ref[0]
    w_f32 = ((wga_ref, wua_ref, wda_ref), (wgb_ref, wub_ref, wdb_ref))
    wbf = (wga_bf, wua_bf, wda_bf, wgb_bf, wub_bf, wdb_bf)

    def row_copies(blk, gather, unrolled):
        base = blk * bm

        def body(rt, carry):
            for u in range(SUBLANES):
                row = tok_ref[base + rt * SUBLANES + u]
                if gather:
                    row = jnp.bitwise_and(row, n_tok - 1)
                    pltpu.make_async_copy(
                        x_hbm.at[lax.shift_right_logical(row, SUBLANE_SHIFT),
                                 pl.ds(jnp.bitwise_and(row, SUBLANES - 1), 1), :],
                        xnext.at[rt, pl.ds(u, 1), :], gsem.at[0]).start()
                else:
                    pltpu.make_async_copy(
                        yout.at[rt, pl.ds(u, 1), :],
                        y_hbm.at[lax.shift_right_logical(row, SUBLANE_SHIFT),
                                 pl.ds(jnp.bitwise_and(row, SUBLANES - 1), 1), :],
                        ssem.at[0]).start()
            return carry

        if unrolled is None:
            lax.fori_loop(0, tiles, body, 0)
        else:
            for rt in range(*unrolled):
                body(rt, 0)

    def wait_gather():
        pltpu.make_async_copy(x_hbm.at[pl.ds(0, tiles)], xnext, gsem.at[0]).wait()

    def wait_scatter():
        pltpu.make_async_copy(yout, y_hbm.at[pl.ds(0, tiles)], ssem.at[0]).wait()

    @pl.when(i == 0)
    def _():
        row_copies(0, True, unrolled=None)
        yout[...] = jnp.zeros(yout.shape, yout.dtype)
        ycur[...] = jnp.zeros(ycur.shape, ycur.dtype)
        spare_tile0 = n_tok // SUBLANES

        def spare_copy(k):
            return pltpu.make_async_copy(
                yout, y_hbm.at[pl.ds(spare_tile0 + k * tiles, tiles)], ssem.at[0])

        @pl.loop(0, n_classes)
        def _(k):
            spare_copy(k).start()

        @pl.loop(0, n_classes)
        def _(k):
            spare_copy(k).wait()

    for side in range(2):
        idx = 2 * i + side
        changed = jnp.logical_or(i == 0, be_ref[idx] != be_ref[jnp.maximum(idx - 2, 0)])

        @pl.when(jnp.logical_and(changed, i < nact))
        def _():
            for m in range(3):
                wbf[3 * side + m][...] = w_f32[side][m][0].astype(BF16)

    @pl.when(i < nact)
    def _():
        wait_gather()
        xcur[...] = xnext[...]

        @pl.when(i >= 1)
        def _():
            wait_scatter()

        yout[...] = ycur[...]

    @pl.when(nact - i >= 1)
    def _():
        row_copies(jnp.minimum(i + 1, n_blocks - 1), True, unrolled=(0, tiles))
        row_copies(jnp.maximum(i - 1, 0), False, unrolled=(0, tiles))
        rows = xcur[...].reshape(bm, dw + LANES)
        x = _unpack_bf16_pairs(rows[:, :dw]).astype(BF16)
        gates = lax.bitcast_convert_type(rows[:, dw:], F32)
        y = None
        for side in range(2):
            wg_bf, wu_bf, wd_bf = wbf[3 * side], wbf[3 * side + 1], wbf[3 * side + 2]
            gate = jnp.dot(x, wg_bf[...], preferred_element_type=F32)
            up = jnp.dot(x, wu_bf[...], preferred_element_type=F32)
            h = (gate / (1.0 + jnp.exp(-gate))) * up
            ys = jnp.dot(h.astype(BF16), wd_bf[...], preferred_element_type=F32) * gates[:, side:side + 1]
            y = ys if y is None else y + ys
        ycur[...] = _pack_bf16_pairs(y).reshape(tiles, SUBLANES, dw)

    @pl.when(i == nact)
    def _():
        wait_gather()
        wait_scatter()
        yout[...] = ycur[...]
        row_copies(i - 1, False, unrolled=None)
        wait_scatter()


def _experts(x_pk, slot_tok, block_experts, n_active, w_gate, w_up, w_down, layer):
    t, dw_in = x_pk.shape
    _, _, d, f = w_gate.shape
    dw = d // 2
    assert dw_in == dw + LANES
    bm = EXPERT_BM
    n_blocks = slot_tok.shape[0] // bm
    n_rows = t + N_CLASSES * bm
    assert t & (t - 1) == 0 and t % SUBLANES == 0 and bm % SUBLANES == 0
    est = 2 * 6 * d * f * 4 + 6 * d * f * 2 + 4 * bm * d * 2 + 10 * bm * d * 4

    def w_spec(shape, side):
        return pl.BlockSpec((None, 1) + shape, lambda i, be, tk, na: (layer, be[2 * i + side], 0, 0))

    xbuf = pltpu.VMEM((bm // SUBLANES, SUBLANES, dw_in), jnp.uint32)
    ybuf = pltpu.VMEM((bm // SUBLANES, SUBLANES, dw), jnp.uint32)
    grid_spec = pltpu.PrefetchScalarGridSpec(
        num_scalar_prefetch=3,
        grid=(n_blocks,),
        in_specs=[pl.BlockSpec(memory_space=pl.ANY),
                  w_spec((d, f), 0), w_spec((d, f), 0), w_spec((f, d), 0),
                  w_spec((d, f), 1), w_spec((d, f), 1), w_spec((f, d), 1)],
        out_specs=pl.BlockSpec(memory_space=pl.ANY),
        scratch_shapes=[xbuf, xbuf, ybuf, ybuf]
                       + [pltpu.VMEM((d, f), BF16), pltpu.VMEM((d, f), BF16), pltpu.VMEM((f, d), BF16)] * 2
                       + [pltpu.SemaphoreType.DMA((1,)), pltpu.SemaphoreType.DMA((1,))],
    )
    y = pl.pallas_call(
        functools.partial(_expert_kernel, n_tok=t, n_classes=N_CLASSES),
        grid_spec=grid_spec,
        out_shape=jax.ShapeDtypeStruct((n_rows // SUBLANES, SUBLANES, dw), jnp.uint32),
        compiler_params=pltpu.CompilerParams(
            dimension_semantics=("arbitrary",), vmem_limit_bytes=_vmem_limit(est)),
        name="experts",
    )(block_experts, slot_tok, n_active, x_pk.reshape(t // SUBLANES, SUBLANES, dw_in),
      w_gate, w_up, w_down, w_gate, w_up, w_down)
    return y.reshape(n_rows, dw)


def _combine_kernel(x_ref, y_ref, g_ref, b_ref, o_ref):
    z = DEEPNORM_ALPHA * x_ref[...] + _unpack_bf16_pairs(y_ref[...])
    o_ref[...] = _layer_norm_rows(z, g_ref[...], b_ref[...])


def _combine_ln(x, y_tok, ln_g, ln_b):
    t, d = x.shape
    bt = COMBINE_BT
    return pl.pallas_call(
        _combine_kernel,
        grid=(t // bt,),
        in_specs=[pl.BlockSpec((bt, d), lambda i: (i, 0)),
                  pl.BlockSpec((bt, d // 2), lambda i: (i, 0)),
                  pl.BlockSpec((1, d), lambda i: (0, 0)),
                  pl.BlockSpec((1, d), lambda i: (0, 0))],
        out_specs=pl.BlockSpec((bt, d), lambda i: (i, 0)),
        out_shape=jax.ShapeDtypeStruct((t, d), F32),
        compiler_params=pltpu.CompilerParams(
            dimension_semantics=("parallel",), vmem_limit_bytes=_vmem_limit(10 * bt * d * 4)),
        name="combine_ln",
    )(x, y_tok, ln_g.reshape(1, d), ln_b.reshape(1, d))


def _moe_residual_ln(x, x_pk, cls, w_gate, w_up, w_down, layer, ln_g, ln_b):
    slot_tok, block_experts, n_active = _route_slots(cls, EXPERT_BM)
    y_tok = _experts(x_pk, slot_tok, block_experts, n_active, w_gate, w_up, w_down, layer)
    return _combine_ln(x, y_tok, ln_g, ln_b)


def _na_bias_pairs(rpb):
    w = GRID_W
    kc = min(WIN_COLS, w)
    cols = np.arange(w)
    col_start = np.clip(cols - kc // 2, 0, w - kc)
    col_mask = (cols[None, :] >= col_start[:, None]) & (cols[None, :] < col_start[:, None] + kc)
    col_idx = np.clip(cols[None, :] - cols[:, None] + WIN_COLS - 1, 0, 2 * WIN_COLS - 2)
    n_ci = 2 * WIN_COLS - 1
    pick = (col_idx[None] == np.arange(n_ci)[:, None, None]).astype(np.float32)
    tb = jnp.einsum("hrc,cqk->hrqk", rpb.astype(F32), jnp.asarray(pick), precision=lax.Precision.HIGHEST)
    tb = jnp.where(col_mask[None, None], tb, NEG_BIAS)
    neg = jnp.full((N_HEADS, 1, w, w), NEG_BIAS, F32)
    ext = jnp.concatenate([neg, tb, neg], axis=1)
    return jnp.concatenate([ext[:, :-1], ext[:, 1:]], axis=-1)


def _na_tile_plan(rows):
    kr = min(WIN_ROWS, rows)
    n_blocks = rows // NA_RQ
    plan = []
    for blk in (0, 1, n_blocks - 1):
        ks = int(np.clip(blk - 1, 0, n_blocks - 3)) * NA_RQ
        per_rq = []
        for rq in range(NA_RQ):
            r = blk * NA_RQ + rq
            rs = int(np.clip(r - kr // 2, 0, rows - kr))
            ri = [ks + j - r + WIN_ROWS - 1 if rs <= ks + j < rs + kr else None for j in range(NA_KROWS)]
            tiles = []
            for jp in range(NA_KROWS // 2):
                a, b = ri[2 * jp], ri[2 * jp + 1]
                if a is not None and b is not None:
                    tiles.append(("both", b))
                elif b is not None:
                    tiles.append(("hi", b))
                elif a is not None:
                    tiles.append(("lo", a + 1))
                else:
                    tiles.append(("none", 0))
            per_rq.append(tiles)
        plan.append(per_rq)
    return plan


def _na_kernel(q_ref, k0_ref, k1_ref, k2_ref, v0_ref, v1_ref, v2_ref, pair_ref, o_ref, bias_ref, *, plan):
    dh = HEAD_DIM
    w = GRID_W
    nt = (((1,), (1,)), ((), ()))
    scale = HEAD_DIM ** -0.5
    i = pl.program_id(2)
    n_blocks = pl.num_programs(2)

    def rebuild(per_rq):
        upper = lax.broadcasted_iota(I32, (w, 2 * w), 1) >= w
        for rq, tiles in enumerate(per_rq):
            for jp, (kind, k) in enumerate(tiles):
                for h in range(NA_HEADS_PER_STEP):
                    if kind == "none":
                        tile = jnp.full((w, 2 * w), NEG_BIAS, F32)
                    elif kind == "both":
                        tile = pair_ref[h, k]
                    elif kind == "hi":
                        tile = jnp.where(upper, pair_ref[h, k], NEG_BIAS)
                    else:
                        tile = jnp.where(upper, NEG_BIAS, pair_ref[h, k])
                    bias_ref[h, rq * w:(rq + 1) * w, jp * 2 * w:(jp + 1) * 2 * w] = tile

    for ty, first_step in enumerate((0, 1, n_blocks - 1)):
        @pl.when(i == first_step)
        def _():
            rebuild(plan[ty])

    assert 2 * dh == LANES
    tq = q_ref.shape[1]
    low = lax.broadcasted_iota(I32, (tq, LANES), 1) < dh
    low_k = lax.broadcasted_iota(I32, (3 * tq, LANES), 1) < dh
    outs = []
    for hp in range(NA_HEADS_PER_STEP // 2):
        cols = slice(hp * LANES, (hp + 1) * LANES)
        q2 = q_ref[0, :, cols] * scale
        k2 = jnp.concatenate([k0_ref[0, :, cols], k1_ref[0, :, cols], k2_ref[0, :, cols]], axis=0)
        v2 = jnp.concatenate([v0_ref[0, :, cols], v1_ref[0, :, cols], v2_ref[0, :, cols]], axis=0)
        halves = []
        for sub in range(2):
            mine, mine_k = (low, low_k) if sub == 0 else (~low, ~low_k)
            q = jnp.where(mine, q2, jnp.zeros_like(q2))
            v = jnp.where(mine_k, v2, jnp.ones_like(v2))
            s = lax.dot_general(q, k2, nt, preferred_element_type=F32) + bias_ref[2 * hp + sub]
            m = jnp.max(s, axis=-1, keepdims=True)
            p = jnp.exp((s - m).astype(BF16))
            o = jnp.dot(p, v, preferred_element_type=F32)
            denom = o[:, dh:dh + 1] if sub == 0 else o[:, 0:1]
            halves.append(o * (1.0 / denom))
        outs.append(jnp.where(low, halves[0], halves[1]))
    o_ref[0] = jnp.concatenate(outs, axis=-1).astype(o_ref.dtype)


def _neighbourhood_attention(qkv, bias_pairs, rows):
    b, s, _ = qkv.shape
    d = D_MODEL
    w = GRID_W
    tq = NA_RQ * w
    n_blocks = rows // NA_RQ
    hps = NA_HEADS_PER_STEP
    hw = hps * HEAD_DIM
    n_hh = d // hw
    assert NA_KROWS * w == 3 * tq and n_blocks >= 3 and NA_KROWS % 2 == 0

    def kv_map(part, j):
        def index_map(hh, bi, i):
            return (bi, jnp.clip(i - 1, 0, n_blocks - 3) + j, part * n_hh + hh)
        return index_map

    blk = (1, tq, hw)
    pair_blk = (hps,) + bias_pairs.shape[1:]
    est = (2 * (7 * tq * hw * 2 + tq * hw * 2) + 2 * 4 * int(np.prod(pair_blk))
           + hps * tq * 3 * tq * 4 + 8 * tq * 3 * tq * 4)
    return pl.pallas_call(
        functools.partial(_na_kernel, plan=_na_tile_plan(rows)),
        grid=(n_hh, b, n_blocks),
        in_specs=[pl.BlockSpec(blk, lambda hh, bi, i: (bi, i, hh))]
                 + [pl.BlockSpec(blk, kv_map(1, j)) for j in range(3)]
                 + [pl.BlockSpec(blk, kv_map(2, j)) for j in range(3)]
                 + [pl.BlockSpec(pair_blk, lambda hh, bi, i: (hh, 0, 0, 0))],
        out_specs=pl.BlockSpec(blk, lambda hh, bi, i: (bi, i, hh)),
        out_shape=jax.ShapeDtypeStruct((b, s, d), BF16),
        scratch_shapes=[pltpu.VMEM((hps, tq, 3 * tq), F32)],
        compiler_params=pltpu.CompilerParams(
            dimension_semantics=("arbitrary", "arbitrary", "arbitrary"), vmem_limit_bytes=_vmem_limit(est)),
        name="neighbourhood_attention",
    )(qkv, qkv, qkv, qkv, qkv, qkv, qkv, bias_pairs)


def kernel(x, fourier_w_in, fourier_w_out, na_w_qkv, na_rpb, na_w_out, router_w, router_b,
           expert_w_gate, expert_w_up, expert_w_down, ln_g, ln_b):
    b, s, d = x.shape
    t = b * s
    rows = s // GRID_W
    w_router_t = router_w.T
    xt = x.reshape(t, d)

    f = _fourier_mixer_pre_out(x, fourier_w_in[0])
    xt, xt_pk, cls = _proj_residual_ln(f.reshape(t, d), fourier_w_out[0].astype(BF16), xt,
                                       ln_g[0, 0], ln_b[0, 0], w_router_t, router_b)
    xt = _moe_residual_ln(xt, xt_pk, cls, expert_w_gate, expert_w_up, expert_w_down, 0,
                          ln_g[0, 1], ln_b[0, 1])

    qkv = _matmul(xt, na_w_qkv[0].astype(BF16), BF16, MM_BM, 3 * d // 2).reshape(b, s, 3 * d)
    bias = _na_bias_pairs(na_rpb[0])
    o = _neighbourhood_attention(qkv, bias, rows)
    xt, xt_pk, cls = _proj_residual_ln(o.reshape(t, d), na_w_out[0].astype(BF16), xt,
                                       ln_g[1, 0], ln_b[1, 0], w_router_t, router_b)
    xt = _moe_residual_ln(xt, xt_pk, cls, expert_w_gate, expert_w_up, expert_w_down, 1,
                          ln_g[1, 1], ln_b[1, 1])
    return xt.reshape(b, s, d)
```

```python
import functools

import jax
import jax.numpy as jnp
import numpy as np
from jax import lax
from jax.experimental import pallas as pl
from jax.experimental.pallas import tpu as pltpu

F32 = jnp.float32
BF16 = jnp.bfloat16
I32 = jnp.int32

D_MODEL = 1024
GRID_W = 64
N_FOURIER_GROUPS = 4
FOURIER_GROUP_DIM = D_MODEL // N_FOURIER_GROUPS
N_HEADS = 16
HEAD_DIM = D_MODEL // N_HEADS
WIN_ROWS = 8
WIN_COLS = 16
N_EXPERTS = 32
N_GROUPS = 4
EXPERTS_PER_GROUP = N_EXPERTS // N_GROUPS
D_EXPERT = D_MODEL // 2
PAIRS_PER_GROUP = EXPERTS_PER_GROUP * (EXPERTS_PER_GROUP - 1) // 2
N_CLASSES = N_GROUPS * PAIRS_PER_GROUP
DEPTH = 2
DEEPNORM_ALPHA = (2 * DEPTH) ** 0.25
LN_EPS = 1e-5

V7X_VMEM_BYTES = 64 * 1024 * 1024
LANES = 128

FFT_N1 = 64
FFT_N2 = 128
FFT_CHUNK = 128

MM_BM = 1024
LN_BM = 512
EXPERT_BM = 384
COMBINE_BT = 512
NA_RQ = 4
NA_KROWS = NA_RQ + WIN_ROWS
NA_HEADS_PER_STEP = 16
NEG_BIAS = -1e30


def _vmem_limit(nbytes):
    return int(min(max(nbytes, 32 * 1024 * 1024), V7X_VMEM_BYTES - 8 * 1024 * 1024))


def _mm_kernel(a_ref, b_ref, o_ref, *, precision):
    if precision is None:
        a = a_ref[...].astype(BF16)
        b = b_ref[...].astype(BF16)
        acc = jnp.dot(a, b, preferred_element_type=F32)
    else:
        acc = jnp.dot(a_ref[...], b_ref[...], preferred_element_type=F32, precision=precision)
    o_ref[...] = acc.astype(o_ref.dtype)


def _matmul(a, b, out_dtype, bm, bn, precision=None):
    m, k = a.shape
    _, n = b.shape
    est = 2 * (bm * k * a.dtype.itemsize + k * bn * b.dtype.itemsize + bm * bn * 4) + 3 * bm * bn * 4
    return pl.pallas_call(
        functools.partial(_mm_kernel, precision=precision),
        grid=(m // bm, n // bn),
        in_specs=[pl.BlockSpec((bm, k), lambda i, j: (i, 0)),
                  pl.BlockSpec((k, bn), lambda i, j: (0, j))],
        out_specs=pl.BlockSpec((bm, bn), lambda i, j: (i, j)),
        out_shape=jax.ShapeDtypeStruct((m, n), out_dtype),
        compiler_params=pltpu.CompilerParams(
            dimension_semantics=("parallel", "parallel"), vmem_limit_bytes=_vmem_limit(est)),
        name="matmul",
    )(a, b)


def _fourier_channel_tables():
    n = FOURIER_GROUP_DIM
    c = np.arange(n)
    ang = 2.0 * np.pi * ((c[:, None] * c[None, :]) % n) / n
    cos, sin = np.cos(ang) / np.sqrt(n), np.sin(ang) / np.sqrt(n)
    half = FFT_CHUNK
    tabs = [np.concatenate([cos[:, h * half:(h + 1) * half], sin[:, h * half:(h + 1) * half]], axis=1)
            for h in range(n // half)]
    return np.stack(tabs).astype(np.float32)


def _fourier_seq_tables(seq):
    n1, n2 = FFT_N1, FFT_N2
    assert n1 * n2 == seq
    k2 = np.arange(n2)
    s2 = np.arange(n2)
    m1 = np.empty((n1, 2 * n2, 2 * n2), np.float32)
    for s1 in range(n1):
        ang = 2.0 * np.pi * ((k2[:, None] * (s1 + n1 * s2[None, :])) % seq) / seq
        mr, mi = np.cos(ang) / np.sqrt(n2), np.sin(ang) / np.sqrt(n2)
        m1[s1] = np.block([[mr, -mi], [mi, mr]])
    k1 = np.arange(n1)
    ang = 2.0 * np.pi * ((k1[:, None] * k1[None, :]) % n1) / n1
    w2 = np.concatenate([np.cos(ang), -np.sin(ang)], axis=1) / np.sqrt(n1)
    return m1, w2.astype(np.float32)


def _fft_kernel(a_ref, b_ref, m1_ref, w2_ref, o_ref, zs_ref):
    n1, n2, c = FFT_N1, FFT_N2, FFT_CHUNK

    def stage1(s1, carry):
        rows = pl.ds(s1, n2, stride=n1)
        x = jnp.concatenate([a_ref[0, rows, :], b_ref[0, rows, :]], axis=0).astype(BF16)
        z = jnp.dot(m1_ref[s1], x, preferred_element_type=F32)
        zs_ref[pl.ds(pl.multiple_of(s1 * 2 * n2, 2 * n2), 2 * n2), :] = z
        return carry

    lax.fori_loop(0, n1, stage1, 0, unroll=16)

    def stage2(kk, carry):
        k2 = 2 * kk
        parts = []
        for d in range(2):
            zr = zs_ref[pl.ds(k2 + d, n1, stride=2 * n2), :]
            zi = zs_ref[pl.ds(n2 + k2 + d, n1, stride=2 * n2), :]
            parts.append(jnp.concatenate([zr, zi], axis=0))
        z = jnp.concatenate(parts, axis=1).astype(BF16)
        y = jnp.dot(w2_ref[...], z, preferred_element_type=F32)
        o_ref[0, pl.ds(k2, n1, stride=n2), :] = y[:, :c]
        o_ref[0, pl.ds(k2 + 1, n1, stride=n2), :] = y[:, c:]
        return carry

    lax.fori_loop(0, n2 // 2, stage2, 0, unroll=16)


def _seq_fft(ab, m1, w2):
    b, s, two_d = ab.shape
    d = two_d // 2
    c = FFT_CHUNK
    est = 2 * (s * 2 * c * 4 + m1.size * 2 + s * c * 4) + FFT_N1 * 2 * FFT_N2 * c * 4 + (4 << 20)
    return pl.pallas_call(
        _fft_kernel,
        grid=(b, d // c),
        in_specs=[pl.BlockSpec((1, s, c), lambda i, j: (i, 0, 2 * j)),
                  pl.BlockSpec((1, s, c), lambda i, j: (i, 0, 2 * j + 1)),
                  pl.BlockSpec(m1.shape, lambda i, j: (0, 0, 0)),
                  pl.BlockSpec(w2.shape, lambda i, j: (0, 0))],
        out_specs=pl.BlockSpec((1, s, c), lambda i, j: (i, 0, j)),
        out_shape=jax.ShapeDtypeStruct((b, s, d), F32),
        scratch_shapes=[pltpu.VMEM((FFT_N1 * 2 * FFT_N2, c), F32)],
        compiler_params=pltpu.CompilerParams(
            dimension_semantics=("parallel", "parallel"), vmem_limit_bytes=_vmem_limit(est)),
        name="seq_fft",
    )(ab, ab, m1, w2)


def _fourier_mixer_pre_out(x, w_in):
    b, s, d = x.shape
    g, gd, c = N_FOURIER_GROUPS, FOURIER_GROUP_DIM, FFT_CHUNK
    halves = gd // c
    cs = jnp.asarray(_fourier_channel_tables())
    w_ab = pl.pallas_call(
        functools.partial(_mm_kernel, precision=lax.Precision.HIGHEST),
        grid=(g, halves),
        in_specs=[pl.BlockSpec((d, gd), lambda i, h: (0, i)),
                  pl.BlockSpec((None, gd, 2 * c), lambda i, h: (h, 0, 0))],
        out_specs=pl.BlockSpec((d, 2 * c), lambda i, h: (0, i * halves + h)),
        out_shape=jax.ShapeDtypeStruct((d, 2 * d), BF16),
        name="fourier_weight_fold",
    )(w_in, cs)
    ab = _matmul(x.reshape(b * s, d), w_ab, F32, MM_BM, 2 * d).reshape(b, s, 2 * d)
    m1, w2 = _fourier_seq_tables(s)
    return _seq_fft(ab, jnp.asarray(m1, dtype=BF16), jnp.asarray(w2, dtype=BF16))


def _layer_norm_rows(z, g, b):
    mu = jnp.mean(z, axis=-1, keepdims=True)
    zc = z - mu
    var = jnp.mean(zc * zc, axis=-1, keepdims=True)
    return zc * lax.rsqrt(var + LN_EPS) * g + b


def _pack_bf16_pairs(z):
    m = z.shape[1] // 2
    zb = z.astype(BF16).astype(F32)
    bits = lax.bitcast_convert_type(zb, jnp.uint32)
    return jnp.bitwise_or(bits[:, m:], lax.shift_right_logical(bits[:, :m], jnp.uint32(16)))


def _unpack_bf16_pairs(w):
    lo = lax.bitcast_convert_type(lax.shift_left(w, jnp.uint32(16)), F32)
    hi = lax.bitcast_convert_type(jnp.bitwise_and(w, jnp.uint32(0xFFFF0000)), F32)
    return jnp.concatenate([lo, hi], axis=1)


def _mm_ln_kernel(a_ref, w_ref, x_ref, g_ref, b_ref, wr_ref, br_ref, o_ref, opk_ref, cls_ref):
    y = jnp.dot(a_ref[...].astype(BF16), w_ref[...], preferred_element_type=F32)
    z = DEEPNORM_ALPHA * x_ref[...] + y
    out = _layer_norm_rows(z, g_ref[...], b_ref[...])
    o_ref[...] = out
    cls, gates = _route_rows(out, wr_ref[...], br_ref[...])
    cls_ref[...] = cls
    bm, dw = out.shape[0], out.shape[1] // 2
    on_diag = lax.broadcasted_iota(I32, (bm, bm), 0) == lax.broadcasted_iota(I32, (bm, bm), 1)
    lane = lax.broadcasted_iota(I32, (bm, LANES), 1)
    tile = jnp.zeros((bm, LANES), F32)
    for k in range(2):
        col = jnp.sum(jnp.where(on_diag, gates[k:k + 1, :], 0.0), axis=1, keepdims=True)
        tile = jnp.where(lane == k, col, tile)
    opk_ref[:, :dw] = _pack_bf16_pairs(out)
    opk_ref[:, dw:] = lax.bitcast_convert_type(tile, jnp.uint32)


def _proj_residual_ln(a, w_bf16, x, ln_g, ln_b, w_router_t, b_router):
    t, k = a.shape
    d = w_bf16.shape[1]
    e = w_router_t.shape[0]
    bm = LN_BM
    est = 2 * (bm * k * a.dtype.itemsize + k * d * 2 + 3 * bm * d * 4) + 8 * bm * d * 4
    return pl.pallas_call(
        _mm_ln_kernel,
        grid=(t // bm,),
        in_specs=[pl.BlockSpec((bm, k), lambda i: (i, 0)),
                  pl.BlockSpec((k, d), lambda i: (0, 0)),
                  pl.BlockSpec((bm, d), lambda i: (i, 0)),
                  pl.BlockSpec((1, d), lambda i: (0, 0)),
                  pl.BlockSpec((1, d), lambda i: (0, 0)),
                  pl.BlockSpec((e, d), lambda i: (0, 0)),
                  pl.BlockSpec((e, 1), lambda i: (0, 0))],
        out_specs=[pl.BlockSpec((bm, d), lambda i: (i, 0)),
                   pl.BlockSpec((bm, d // 2 + LANES), lambda i: (i, 0)),
                   pl.BlockSpec((1, bm), lambda i: (0, i))],
        out_shape=[jax.ShapeDtypeStruct((t, d), F32), jax.ShapeDtypeStruct((t, d // 2 + LANES), jnp.uint32),
                   jax.ShapeDtypeStruct((1, t), I32)],
        compiler_params=pltpu.CompilerParams(
            dimension_semantics=("parallel",), vmem_limit_bytes=_vmem_limit(est)),
        name="proj_residual_ln",
    )(a, w_bf16, x, ln_g.reshape(1, d), ln_b.reshape(1, d), w_router_t, b_router.reshape(e, 1))


def _top2_rows(v, iota):
    n_rows = v.shape[0]
    m1 = jnp.max(v, axis=0, keepdims=True)
    i1 = jnp.min(jnp.where(v == m1, iota, n_rows), axis=0, keepdims=True)
    v2 = jnp.where(iota == i1, -jnp.inf, v)
    m2 = jnp.max(v2, axis=0, keepdims=True)
    i2 = jnp.min(jnp.where(v2 == m2, iota, n_rows), axis=0, keepdims=True)
    return m1, i1, m2, i2


def _route_rows(x, w, b):
    xh = x.astype(BF16)
    xl = (x - xh.astype(F32)).astype(BF16)
    wh = w.astype(BF16)
    wl = (w - wh.astype(F32)).astype(BF16)
    nt = (((1,), (1,)), ((), ()))
    logits = (lax.dot_general(wh, xh, nt, preferred_element_type=F32)
              + lax.dot_general(wh, xl, nt, preferred_element_type=F32)
              + lax.dot_general(wl, xh, nt, preferred_element_type=F32))
    scores = 1.0 / (1.0 + jnp.exp(-logits))
    sel = scores + b
    epg = EXPERTS_PER_GROUP
    bt = x.shape[0]
    iota = lax.broadcasted_iota(I32, (epg, bt), 0)

    best = None
    for g in range(N_GROUPS):
        m1, _, m2, _ = _top2_rows(sel[g * epg:(g + 1) * epg], iota)
        gs = m1 + m2
        if best is None:
            best, gidx = gs, jnp.zeros((1, bt), I32)
        else:
            better = gs > best
            gidx = jnp.where(better, g, gidx)
            best = jnp.where(better, gs, best)

    sel_in = sel[0:epg]
    sc_in = scores[0:epg]
    for g in range(1, N_GROUPS):
        pick = gidx == g
        sel_in = jnp.where(pick, sel[g * epg:(g + 1) * epg], sel_in)
        sc_in = jnp.where(pick, scores[g * epg:(g + 1) * epg], sc_in)
    _, i1, _, i2 = _top2_rows(sel_in, iota)
    g1 = jnp.sum(jnp.where(iota == i1, sc_in, 0.0), axis=0, keepdims=True)
    g2 = jnp.sum(jnp.where(iota == i2, sc_in, 0.0), axis=0, keepdims=True)
    denom = g1 + g2
    first_lo = i1 < i2
    lo = jnp.where(first_lo, i1, i2)
    hi = jnp.where(first_lo, i2, i1)
    pair = lax.shift_right_logical(lo * (2 * epg - 1 - lo), 1) + (hi - lo - 1)
    cls = gidx * PAIRS_PER_GROUP + pair
    gates = jnp.concatenate([jnp.where(first_lo, g1, g2) / denom, jnp.where(first_lo, g2, g1) / denom], axis=0)
    return cls, gates


RANK_SUB = 256
RANK_NSUB = 8


def _rank_kernel(keys_ref, tri_ref, rank_ref, counts_ref, carry_ref, *, n_classes):
    nsub, sub = RANK_NSUB, RANK_SUB

    @pl.when(pl.program_id(0) == 0)
    def _():
        carry_ref[...] = jnp.zeros_like(carry_ref)

    cls = lax.broadcasted_iota(I32, (nsub, n_classes, sub), 1)
    onehot = cls == keys_ref[...]
    oh = jnp.where(onehot, 1.0, 0.0).reshape(nsub * n_classes, sub).astype(BF16)
    pref = jnp.dot(oh, tri_ref[...], preferred_element_type=F32).reshape(nsub, n_classes, sub)
    carry = carry_ref[...]
    for j in range(nsub):
        before = pref[j] + (carry - 1.0)
        rank_ref[j] = jnp.sum(jnp.where(onehot[j], before, 0.0), axis=0, keepdims=True).astype(I32)
        carry = carry + pref[j][:, sub - 1:sub]
    carry_ref[...] = carry
    counts_ref[...] = carry.astype(I32)


def _rank_within_class(keys, n_classes):
    n = keys.shape[0]
    nsub, sub = RANK_NSUB, RANK_SUB
    tri = jnp.asarray(np.triu(np.ones((sub, sub), np.float32)), dtype=BF16)
    rank, counts = pl.pallas_call(
        functools.partial(_rank_kernel, n_classes=n_classes),
        grid=(n // (nsub * sub),),
        in_specs=[pl.BlockSpec((nsub, 1, sub), lambda i: (i, 0, 0)),
                  pl.BlockSpec((sub, sub), lambda i: (0, 0))],
        out_specs=[pl.BlockSpec((nsub, 1, sub), lambda i: (i, 0, 0)),
                   pl.BlockSpec((n_classes, 1), lambda i: (0, 0))],
        out_shape=[jax.ShapeDtypeStruct((n // sub, 1, sub), I32),
                   jax.ShapeDtypeStruct((n_classes, 1), I32)],
        scratch_shapes=[pltpu.VMEM((n_classes, 1), F32)],
        compiler_params=pltpu.CompilerParams(dimension_semantics=("arbitrary",)),
        name="rank_within_class",
    )(keys.reshape(n // sub, 1, sub), tri)
    return rank.reshape(n), counts.reshape(n_classes)


FILL_CHUNK = 1024


def _fill_slots_kernel(dest_ref, cnt_ref, pend_ref, slot_ref, *, n_items, bm):
    n_classes = cnt_ref.shape[0]
    n_blocks = slot_ref.shape[0] // bm
    step = pl.program_id(0)

    @pl.when(step == 0)
    def _():
        def fill_block(blk, first):
            base = blk * bm

            def body(r, carry):
                slot_ref[base + r] = first + r
                return carry

            lax.fori_loop(0, bm, body, 0, unroll=16)

        def pad_class(k, carry):
            @pl.when(cnt_ref[k] > 0)
            def _():
                fill_block(lax.div(pend_ref[k], bm) - 1, n_items + k * bm)
            return carry

        lax.fori_loop(0, n_classes, pad_class, 0)

        def unused_block(blk, carry):
            fill_block(blk, 0)
            return carry

        lax.fori_loop(lax.div(pend_ref[n_classes - 1], bm), n_blocks, unused_block, 0)

    base = step * FILL_CHUNK

    def place(j, carry):
        slot_ref[dest_ref[j]] = base + j
        return carry

    lax.fori_loop(0, FILL_CHUNK, place, 0, unroll=16)


def _fill_slots(dest, counts, pad_end, n_slots, bm):
    n_items = dest.shape[0]
    assert n_items % FILL_CHUNK == 0 and n_slots % bm == 0
    smem = pl.BlockSpec(memory_space=pltpu.SMEM)
    return pl.pallas_call(
        functools.partial(_fill_slots_kernel, n_items=n_items, bm=bm),
        grid=(n_items // FILL_CHUNK,),
        in_specs=[pl.BlockSpec((FILL_CHUNK,), lambda i: (i,), memory_space=pltpu.SMEM), smem, smem],
        out_specs=smem,
        out_shape=jax.ShapeDtypeStruct((n_slots,), I32),
        compiler_params=pltpu.CompilerParams(dimension_semantics=("arbitrary",)),
        name="fill_slots",
    )(dest, counts, pad_end.astype(I32))


def _class_experts():
    epg = EXPERTS_PER_GROUP
    pairs = [(lo, hi) for lo in range(epg) for hi in range(lo + 1, epg)]
    return np.array([[g * epg + lo, g * epg + hi] for g in range(N_GROUPS) for lo, hi in pairs], np.int32)


def _route_slots(cls, bm):
    t = cls.shape[1]
    keys = cls.reshape(t)
    rank, counts = _rank_within_class(keys, N_CLASSES)
    padded = (counts + bm - 1) // bm * bm
    pad_end = jnp.cumsum(padded)
    pad_start = pad_end - padded
    class_ids = jnp.arange(N_CLASSES, dtype=I32)
    dest = jnp.sum(jnp.where(keys[:, None] == class_ids[None, :], pad_start[None, :], 0), axis=1) + rank
    p = (t // bm + N_CLASSES + 1) * bm
    slot_tok = _fill_slots(dest, counts, pad_end, p, bm)
    nb = p // bm
    block_start = jnp.arange(nb, dtype=I32) * bm
    block_class = jnp.minimum(
        jnp.sum((pad_end[None, :] <= block_start[:, None]).astype(I32), axis=1), N_CLASSES - 1)
    in_class = block_class[:, None, None] == class_ids[None, :, None]
    block_experts = jnp.sum(jnp.where(in_class, jnp.asarray(_class_experts())[None], 0), axis=1).reshape(2 * nb)
    n_active = (pad_end[-1] // bm).astype(I32).reshape(1)
    return slot_tok, block_experts, n_active


SUBLANES = 8
SUBLANE_SHIFT = SUBLANES.bit_length() - 1


def _expert_kernel(be_ref, tok_ref, nact_ref, x_hbm, wga_ref, wua_ref, wda_ref, wgb_ref, wub_ref, wdb_ref,
                   y_hbm, xnext, xcur, ycur, yout, wga_bf, wua_bf, wda_bf, wgb_bf, wub_bf, wdb_bf, gsem, ssem,
                   *, n_tok, n_classes):
    bm = EXPERT_BM
    tiles = bm // SUBLANES
    dw = ycur.shape[-1]
    i = pl.program_id(0)
    n_blocks = pl.num_programs(0)
    nact = nact_ref[0]
    w_f32 = ((wga_ref, wua_ref, wda_ref), (wgb_ref, wub_ref, wdb_ref))
    wbf = (wga_bf, wua_bf, wda_bf, wgb_bf, wub_bf, wdb_bf)

    def row_copies(blk, gather, unrolled):
        base = blk * bm

        def body(rt, carry):
            for u in range(SUBLANES):
                row = tok_ref[base + rt * SUBLANES + u]
                if gather:
                    row = jnp.bitwise_and(row, n_tok - 1)
                    pltpu.make_async_copy(
                        x_hbm.at[lax.shift_right_logical(row, SUBLANE_SHIFT),
                                 pl.ds(jnp.bitwise_and(row, SUBLANES - 1), 1), :],
                        xnext.at[rt, pl.ds(u, 1), :], gsem.at[0]).start()
                else:
                    pltpu.make_async_copy(
                        yout.at[rt, pl.ds(u, 1), :],
                        y_hbm.at[lax.shift_right_logical(row, SUBLANE_SHIFT),
                                 pl.ds(jnp.bitwise_and(row, SUBLANES - 1), 1), :],
                        ssem.at[0]).start()
            return carry

        if unrolled is None:
            lax.fori_loop(0, tiles, body, 0)
        else:
            for rt in range(*unrolled):
                body(rt, 0)

    def wait_gather():
        pltpu.make_async_copy(x_hbm.at[pl.ds(0, tiles)], xnext, gsem.at[0]).wait()

    def wait_scatter():
        pltpu.make_async_copy(yout, y_hbm.at[pl.ds(0, tiles)], ssem.at[0]).wait()

    @pl.when(i == 0)
    def _():
        row_copies(0, True, unrolled=None)
        yout[...] = jnp.zeros(yout.shape, yout.dtype)
        ycur[...] = jnp.zeros(ycur.shape, ycur.dtype)
        spare_tile0 = n_tok // SUBLANES

        def spare_copy(k):
            return pltpu.make_async_copy(
                yout, y_hbm.at[pl.ds(spare_tile0 + k * tiles, tiles)], ssem.at[0])

        @pl.loop(0, n_classes)
        def _(k):
            spare_copy(k).start()

        @pl.loop(0, n_classes)
        def _(k):
            spare_copy(k).wait()

    for side in range(2):
        idx = 2 * i + side
        changed = jnp.logical_or(i == 0, be_ref[idx] != be_ref[jnp.maximum(idx - 2, 0)])

        @pl.when(jnp.logical_and(changed, i < nact))
        def _():
            for m in range(3):
                wbf[3 * side + m][...] = w_f32[side][m][0].astype(BF16)

    @pl.when(i < nact)
    def _():
        wait_gather()
        xcur[...] = xnext[...]

        @pl.when(i >= 1)
        def _():
            wait_scatter()

        yout[...] = ycur[...]

    @pl.when(nact - i >= 1)
    def _():
        row_copies(jnp.minimum(i + 1, n_blocks - 1), True, unrolled=(0, tiles))
        row_copies(jnp.maximum(i - 1, 0), False, unrolled=(0, tiles))
        rows = xcur[...].reshape(bm, dw + LANES)
        x = _unpack_bf16_pairs(rows[:, :dw]).astype(BF16)
        gates = lax.bitcast_convert_type(rows[:, dw:], F32)
        y = None
        for side in range(2):
            wg_bf, wu_bf, wd_bf = wbf[3 * side], wbf[3 * side + 1], wbf[3 * side + 2]
            gate = jnp.dot(x, wg_bf[...], preferred_element_type=F32)
            up = jnp.dot(x, wu_bf[...], preferred_element_type=F32)
            h = (gate / (1.0 + jnp.exp(-gate))) * up
            ys = jnp.dot(h.astype(BF16), wd_bf[...], preferred_element_type=F32) * gates[:, side:side + 1]
            y = ys if y is None else y + ys
        ycur[...] = _pack_bf16_pairs(y).reshape(tiles, SUBLANES, dw)

    @pl.when(i == nact)
    def _():
        wait_gather()
        wait_scatter()
        yout[...] = ycur[...]
        row_copies(i - 1, False, unrolled=None)
        wait_scatter()


def _experts(x_pk, slot_tok, block_experts, n_active, w_gate, w_up, w_down, layer):
    t, dw_in = x_pk.shape
    _, _, d, f = w_gate.shape
    dw = d // 2
    assert dw_in == dw + LANES
    bm = EXPERT_BM
    n_blocks = slot_tok.shape[0] // bm
    n_rows = t + N_CLASSES * bm
    assert t & (t - 1) == 0 and t % SUBLANES == 0 and bm % SUBLANES == 0
    est = 2 * 6 * d * f * 4 + 6 * d * f * 2 + 4 * bm * d * 2 + 10 * bm * d * 4

    def w_spec(shape, side):
        return pl.BlockSpec((None, 1) + shape, lambda i, be, tk, na: (layer, be[2 * i + side], 0, 0))

    xbuf = pltpu.VMEM((bm // SUBLANES, SUBLANES, dw_in), jnp.uint32)
    ybuf = pltpu.VMEM((bm // SUBLANES, SUBLANES, dw), jnp.uint32)
    grid_spec = pltpu.PrefetchScalarGridSpec(
        num_scalar_prefetch=3,
        grid=(n_blocks,),
        in_specs=[pl.BlockSpec(memory_space=pl.ANY),
                  w_spec((d, f), 0), w_spec((d, f), 0), w_spec((f, d), 0),
                  w_spec((d, f), 1), w_spec((d, f), 1), w_spec((f, d), 1)],
        out_specs=pl.BlockSpec(memory_space=pl.ANY),
        scratch_shapes=[xbuf, xbuf, ybuf, ybuf]
                       + [pltpu.VMEM((d, f), BF16), pltpu.VMEM((d, f), BF16), pltpu.VMEM((f, d), BF16)] * 2
                       + [pltpu.SemaphoreType.DMA((1,)), pltpu.SemaphoreType.DMA((1,))],
    )
    y = pl.pallas_call(
        functools.partial(_expert_kernel, n_tok=t, n_classes=N_CLASSES),
        grid_spec=grid_spec,
        out_shape=jax.ShapeDtypeStruct((n_rows // SUBLANES, SUBLANES, dw), jnp.uint32),
        compiler_params=pltpu.CompilerParams(
            dimension_semantics=("arbitrary",), vmem_limit_bytes=_vmem_limit(est)),
        name="experts",
    )(block_experts, slot_tok, n_active, x_pk.reshape(t // SUBLANES, SUBLANES, dw_in),
      w_gate, w_up, w_down, w_gate, w_up, w_down)
    return y.reshape(n_rows, dw)


def _combine_kernel(x_ref, y_ref, g_ref, b_ref, o_ref):
    z = DEEPNORM_ALPHA * x_ref[...] + _unpack_bf16_pairs(y_ref[...])
    o_ref[...] = _layer_norm_rows(z, g_ref[...], b_ref[...])


def _combine_ln(x, y_tok, ln_g, ln_b):
    t, d = x.shape
    bt = COMBINE_BT
    return pl.pallas_call(
        _combine_kernel,
        grid=(t // bt,),
        in_specs=[pl.BlockSpec((bt, d), lambda i: (i, 0)),
                  pl.BlockSpec((bt, d // 2), lambda i: (i, 0)),
                  pl.BlockSpec((1, d), lambda i: (0, 0)),
                  pl.BlockSpec((1, d), lambda i: (0, 0))],
        out_specs=pl.BlockSpec((bt, d), lambda i: (i, 0)),
        out_shape=jax.ShapeDtypeStruct((t, d), F32),
        compiler_params=pltpu.CompilerParams(
            dimension_semantics=("parallel",), vmem_limit_bytes=_vmem_limit(10 * bt * d * 4)),
        name="combine_ln",
    )(x, y_tok, ln_g.reshape(1, d), ln_b.reshape(1, d))


def _moe_residual_ln(x, x_pk, cls, w_gate, w_up, w_down, layer, ln_g, ln_b):
    slot_tok, block_experts, n_active = _route_slots(cls, EXPERT_BM)
    y_tok = _experts(x_pk, slot_tok, block_experts, n_active, w_gate, w_up, w_down, layer)
    return _combine_ln(x, y_tok, ln_g, ln_b)


def _na_bias_pairs(rpb):
    w = GRID_W
    kc = min(WIN_COLS, w)
    cols = np.arange(w)
    col_start = np.clip(cols - kc // 2, 0, w - kc)
    col_mask = (cols[None, :] >= col_start[:, None]) & (cols[None, :] < col_start[:, None] + kc)
    col_idx = np.clip(cols[None, :] - cols[:, None] + WIN_COLS - 1, 0, 2 * WIN_COLS - 2)
    n_ci = 2 * WIN_COLS - 1
    pick = (col_idx[None] == np.arange(n_ci)[:, None, None]).astype(np.float32)
    tb = jnp.einsum("hrc,cqk->hrqk", rpb.astype(F32), jnp.asarray(pick), precision=lax.Precision.HIGHEST)
    tb = jnp.where(col_mask[None, None], tb, NEG_BIAS)
    neg = jnp.full((N_HEADS, 1, w, w), NEG_BIAS, F32)
    ext = jnp.concatenate([neg, tb, neg], axis=1)
    return jnp.concatenate([ext[:, :-1], ext[:, 1:]], axis=-1)


def _na_tile_plan(rows):
    kr = min(WIN_ROWS, rows)
    n_blocks = rows // NA_RQ
    plan = []
    for blk in (0, 1, n_blocks - 1):
        ks = int(np.clip(blk - 1, 0, n_blocks - 3)) * NA_RQ
        per_rq = []
        for rq in range(NA_RQ):
            r = blk * NA_RQ + rq
            rs = int(np.clip(r - kr // 2, 0, rows - kr))
            ri = [ks + j - r + WIN_ROWS - 1 if rs <= ks + j < rs + kr else None for j in range(NA_KROWS)]
            tiles = []
            for jp in range(NA_KROWS // 2):
                a, b = ri[2 * jp], ri[2 * jp + 1]
                if a is not None and b is not None:
                    tiles.append(("both", b))
                elif b is not None:
                    tiles.append(("hi", b))
                elif a is not None:
                    tiles.append(("lo", a + 1))
                else:
                    tiles.append(("none", 0))
            per_rq.append(tiles)
        plan.append(per_rq)
    return plan


def _na_kernel(q_ref, k0_ref, k1_ref, k2_ref, v0_ref, v1_ref, v2_ref, pair_ref, o_ref, bias_ref, *, plan):
    dh = HEAD_DIM
    w = GRID_W
    nt = (((1,), (1,)), ((), ()))
    scale = HEAD_DIM ** -0.5
    i = pl.program_id(2)
    n_blocks = pl.num_programs(2)

    def rebuild(per_rq):
        upper = lax.broadcasted_iota(I32, (w, 2 * w), 1) >= w
        for rq, tiles in enumerate(per_rq):
            for jp, (kind, k) in enumerate(tiles):
                for h in range(NA_HEADS_PER_STEP):
                    if kind == "none":
                        tile = jnp.full((w, 2 * w), NEG_BIAS, F32)
                    elif kind == "both":
                        tile = pair_ref[h, k]
                    elif kind == "hi":
                        tile = jnp.where(upper, pair_ref[h, k], NEG_BIAS)
                    else:
                        tile = jnp.where(upper, NEG_BIAS, pair_ref[h, k])
                    bias_ref[h, rq * w:(rq + 1) * w, jp * 2 * w:(jp + 1) * 2 * w] = tile

    for ty, first_step in enumerate((0, 1, n_blocks - 1)):
        @pl.when(i == first_step)
        def _():
            rebuild(plan[ty])

    assert 2 * dh == LANES
    tq = q_ref.shape[1]
    low = lax.broadcasted_iota(I32, (tq, LANES), 1) < dh
    low_k = lax.broadcasted_iota(I32, (3 * tq, LANES), 1) < dh
    outs = []
    for hp in range(NA_HEADS_PER_STEP // 2):
        cols = slice(hp * LANES, (hp + 1) * LANES)
        q2 = q_ref[0, :, cols] * scale
        k2 = jnp.concatenate([k0_ref[0, :, cols], k1_ref[0, :, cols], k2_ref[0, :, cols]], axis=0)
        v2 = jnp.concatenate([v0_ref[0, :, cols], v1_ref[0, :, cols], v2_ref[0, :, cols]], axis=0)
        halves = []
        for sub in range(2):
            mine, mine_k = (low, low_k) if sub == 0 else (~low, ~low_k)
            q = jnp.where(mine, q2, jnp.zeros_like(q2))
            v = jnp.where(mine_k, v2, jnp.ones_like(v2))
            s = lax.dot_general(q, k2, nt, preferred_element_type=F32) + bias_ref[2 * hp + sub]
            m = jnp.max(s, axis=-1, keepdims=True)
            p = jnp.exp((s - m).astype(BF16))
            o = jnp.dot(p, v, preferred_element_type=F32)
            denom = o[:, dh:dh + 1] if sub == 0 else o[:, 0:1]
            halves.append(o * (1.0 / denom))
        outs.append(jnp.where(low, halves[0], halves[1]))
    o_ref[0] = jnp.concatenate(outs, axis=-1).astype(o_ref.dtype)


def _neighbourhood_attention(qkv, bias_pairs, rows):
    b, s, _ = qkv.shape
    d = D_MODEL
    w = GRID_W
    tq = NA_RQ * w
    n_blocks = rows // NA_RQ
    hps = NA_HEADS_PER_STEP
    hw = hps * HEAD_DIM
    n_hh = d // hw
    assert NA_KROWS * w == 3 * tq and n_blocks >= 3 and NA_KROWS % 2 == 0

    def kv_map(part, j):
        def index_map(hh, bi, i):
            return (bi, jnp.clip(i - 1, 0, n_blocks - 3) + j, part * n_hh + hh)
        return index_map

    blk = (1, tq, hw)
    pair_blk = (hps,) + bias_pairs.shape[1:]
    est = (2 * (7 * tq * hw * 2 + tq * hw * 2) + 2 * 4 * int(np.prod(pair_blk))
           + hps * tq * 3 * tq * 4 + 8 * tq * 3 * tq * 4)
    return pl.pallas_call(
        functools.partial(_na_kernel, plan=_na_tile_plan(rows)),
        grid=(n_hh, b, n_blocks),
        in_specs=[pl.BlockSpec(blk, lambda hh, bi, i: (bi, i, hh))]
                 + [pl.BlockSpec(blk, kv_map(1, j)) for j in range(3)]
                 + [pl.BlockSpec(blk, kv_map(2, j)) for j in range(3)]
                 + [pl.BlockSpec(pair_blk, lambda hh, bi, i: (hh, 0, 0, 0))],
        out_specs=pl.BlockSpec(blk, lambda hh, bi, i: (bi, i, hh)),
        out_shape=jax.ShapeDtypeStruct((b, s, d), BF16),
        scratch_shapes=[pltpu.VMEM((hps, tq, 3 * tq), F32)],
        compiler_params=pltpu.CompilerParams(
            dimension_semantics=("arbitrary", "arbitrary", "arbitrary"), vmem_limit_bytes=_vmem_limit(est)),
        name="neighbourhood_attention",
    )(qkv, qkv, qkv, qkv, qkv, qkv, qkv, bias_pairs)


def kernel(x, fourier_w_in, fourier_w_out, na_w_qkv, na_rpb, na_w_out, router_w, router_b,
           expert_w_gate, expert_w_up, expert_w_down, ln_g, ln_b):
    b, s, d = x.shape
    t = b * s
    rows = s // GRID_W
    w_router_t = router_w.T
    xt = x.reshape(t, d)

    f = _fourier_mixer_pre_out(x, fourier_w_in[0])
    xt, xt_pk, cls = _proj_residual_ln(f.reshape(t, d), fourier_w_out[0].astype(BF16), xt,
                                       ln_g[0, 0], ln_b[0, 0], w_router_t, router_b)
    xt = _moe_residual_ln(xt, xt_pk, cls, expert_w_gate, expert_w_up, expert_w_down, 0,
                          ln_g[0, 1], ln_b[0, 1])

    qkv = _matmul(xt, na_w_qkv[0].astype(BF16), BF16, MM_BM, 3 * d // 2).reshape(b, s, 3 * d)
    bias = _na_bias_pairs(na_rpb[0])
    o = _neighbourhood_attention(qkv, bias, rows)
    xt, xt_pk, cls = _proj_residual_ln(o.reshape(t, d), na_w_out[0].astype(BF16), xt,
                                       ln_g[1, 0], ln_b[1, 0], w_router_t, router_b)
    xt = _moe_residual_ln(xt, xt_pk, cls, expert_w_gate, expert_w_up, expert_w_down, 1,
                          ln_g[1, 1], ln_b[1, 1])
    return xt.reshape(b, s, d)
```

```python
import functools

import jax
import jax.numpy as jnp
import numpy as np
from jax import lax
from jax.experimental import pallas as pl
from jax.experimental.pallas import tpu as pltpu

F32 = jnp.float32
BF16 = jnp.bfloat16
I32 = jnp.int32

D_MODEL = 1024
GRID_W = 64
N_FOURIER_GROUPS = 4
FOURIER_GROUP_DIM = D_MODEL // N_FOURIER_GROUPS
N_HEADS = 16
HEAD_DIM = D_MODEL // N_HEADS
WIN_ROWS = 8
WIN_COLS = 16
N_EXPERTS = 32
N_GROUPS = 4
EXPERTS_PER_GROUP = N_EXPERTS // N_GROUPS
D_EXPERT = D_MODEL // 2
PAIRS_PER_GROUP = EXPERTS_PER_GROUP * (EXPERTS_PER_GROUP - 1) // 2
N_CLASSES = N_GROUPS * PAIRS_PER_GROUP
DEPTH = 2
DEEPNORM_ALPHA = (2 * DEPTH) ** 0.25
LN_EPS = 1e-5

V7X_VMEM_BYTES = 64 * 1024 * 1024
LANES = 128

FFT_N1 = 64
FFT_N2 = 128
FFT_CHUNK = 128

MM_BM = 1024
LN_BM = 512
EXPERT_BM = 384
COMBINE_BT = 512
NA_RQ = 4
NA_KROWS = NA_RQ + WIN_ROWS
NA_HEADS_PER_STEP = 16
NEG_BIAS = -1e30


def _vmem_limit(nbytes):
    return int(min(max(nbytes, 32 * 1024 * 1024), V7X_VMEM_BYTES - 8 * 1024 * 1024))


def _mm_kernel(a_ref, b_ref, o_ref, *, precision):
    if precision is None:
        a = a_ref[...].astype(BF16)
        b = b_ref[...].astype(BF16)
        acc = jnp.dot(a, b, preferred_element_type=F32)
    else:
        acc = jnp.dot(a_ref[...], b_ref[...], preferred_element_type=F32, precision=precision)
    o_ref[...] = acc.astype(o_ref.dtype)


def _matmul(a, b, out_dtype, bm, bn, precision=None):
    m, k = a.shape
    _, n = b.shape
    est = 2 * (bm * k * a.dtype.itemsize + k * bn * b.dtype.itemsize + bm * bn * 4) + 3 * bm * bn * 4
    return pl.pallas_call(
        functools.partial(_mm_kernel, precision=precision),
        grid=(m // bm, n // bn),
        in_specs=[pl.BlockSpec((bm, k), lambda i, j: (i, 0)),
                  pl.BlockSpec((k, bn), lambda i, j: (0, j))],
        out_specs=pl.BlockSpec((bm, bn), lambda i, j: (i, j)),
        out_shape=jax.ShapeDtypeStruct((m, n), out_dtype),
        compiler_params=pltpu.CompilerParams(
            dimension_semantics=("parallel", "parallel"), vmem_limit_bytes=_vmem_limit(est)),
        name="matmul",
    )(a, b)


def _fourier_channel_tables():
    n = FOURIER_GROUP_DIM
    c = np.arange(n)
    ang = 2.0 * np.pi * ((c[:, None] * c[None, :]) % n) / n
    cos, sin = np.cos(ang) / np.sqrt(n), np.sin(ang) / np.sqrt(n)
    half = FFT_CHUNK
    tabs = [np.concatenate([cos[:, h * half:(h + 1) * half], sin[:, h * half:(h + 1) * half]], axis=1)
            for h in range(n // half)]
    return np.stack(tabs).astype(np.float32)


def _fourier_seq_tables(seq):
    n1, n2 = FFT_N1, FFT_N2
    assert n1 * n2 == seq
    k2 = np.arange(n2)
    s2 = np.arange(n2)
    m1 = np.empty((n1, 2 * n2, 2 * n2), np.float32)
    for s1 in range(n1):
        ang = 2.0 * np.pi * ((k2[:, None] * (s1 + n1 * s2[None, :])) % seq) / seq
        mr, mi = np.cos(ang) / np.sqrt(n2), np.sin(ang) / np.sqrt(n2)
        m1[s1] = np.block([[mr, -mi], [mi, mr]])
    k1 = np.arange(n1)
    ang = 2.0 * np.pi * ((k1[:, None] * k1[None, :]) % n1) / n1
    w2 = np.concatenate([np.cos(ang), -np.sin(ang)], axis=1) / np.sqrt(n1)
    return m1, w2.astype(np.float32)


def _fft_kernel(a_ref, b_ref, m1_ref, w2_ref, o_ref, zs_ref):
    n1, n2, c = FFT_N1, FFT_N2, FFT_CHUNK

    def stage1(s1, carry):
        rows = pl.ds(s1, n2, stride=n1)
        x = jnp.concatenate([a_ref[0, rows, :], b_ref[0, rows, :]], axis=0).astype(BF16)
        z = jnp.dot(m1_ref[s1], x, preferred_element_type=F32)
        zs_ref[pl.ds(pl.multiple_of(s1 * 2 * n2, 2 * n2), 2 * n2), :] = z
        return carry

    lax.fori_loop(0, n1, stage1, 0, unroll=16)

    def stage2(kk, carry):
        k2 = 2 * kk
        parts = []
        for d in range(2):
            zr = zs_ref[pl.ds(k2 + d, n1, stride=2 * n2), :]
            zi = zs_ref[pl.ds(n2 + k2 + d, n1, stride=2 * n2), :]
            parts.append(jnp.concatenate([zr, zi], axis=0))
        z = jnp.concatenate(parts, axis=1).astype(BF16)
        y = jnp.dot(w2_ref[...], z, preferred_element_type=F32)
        o_ref[0, pl.ds(k2, n1, stride=n2), :] = y[:, :c]
        o_ref[0, pl.ds(k2 + 1, n1, stride=n2), :] = y[:, c:]
        return carry

    lax.fori_loop(0, n2 // 2, stage2, 0, unroll=16)


def _seq_fft(ab, m1, w2):
    b, s, two_d = ab.shape
    d = two_d // 2
    c = FFT_CHUNK
    est = 2 * (s * 2 * c * 4 + m1.size * 2 + s * c * 4) + FFT_N1 * 2 * FFT_N2 * c * 4 + (4 << 20)
    return pl.pallas_call(
        _fft_kernel,
        grid=(b, d // c),
        in_specs=[pl.BlockSpec((1, s, c), lambda i, j: (i, 0, 2 * j)),
                  pl.BlockSpec((1, s, c), lambda i, j: (i, 0, 2 * j + 1)),
                  pl.BlockSpec(m1.shape, lambda i, j: (0, 0, 0)),
                  pl.BlockSpec(w2.shape, lambda i, j: (0, 0))],
        out_specs=pl.BlockSpec((1, s, c), lambda i, j: (i, 0, j)),
        out_shape=jax.ShapeDtypeStruct((b, s, d), F32),
        scratch_shapes=[pltpu.VMEM((FFT_N1 * 2 * FFT_N2, c), F32)],
        compiler_params=pltpu.CompilerParams(
            dimension_semantics=("parallel", "parallel"), vmem_limit_bytes=_vmem_limit(est)),
        name="seq_fft",
    )(ab, ab, m1, w2)


def _fourier_mixer_pre_out(x, w_in):
    b, s, d = x.shape
    g, gd, c = N_FOURIER_GROUPS, FOURIER_GROUP_DIM, FFT_CHUNK
    halves = gd // c
    cs = jnp.asarray(_fourier_channel_tables())
    w_ab = pl.pallas_call(
        functools.partial(_mm_kernel, precision=lax.Precision.HIGHEST),
        grid=(g, halves),
        in_specs=[pl.BlockSpec((d, gd), lambda i, h: (0, i)),
                  pl.BlockSpec((None, gd, 2 * c), lambda i, h: (h, 0, 0))],
        out_specs=pl.BlockSpec((d, 2 * c), lambda i, h: (0, i * halves + h)),
        out_shape=jax.ShapeDtypeStruct((d, 2 * d), BF16),
        name="fourier_weight_fold",
    )(w_in, cs)
    ab = _matmul(x.reshape(b * s, d), w_ab, F32, MM_BM, 2 * d).reshape(b, s, 2 * d)
    m1, w2 = _fourier_seq_tables(s)
    return _seq_fft(ab, jnp.asarray(m1, dtype=BF16), jnp.asarray(w2, dtype=BF16))


def _layer_norm_rows(z, g, b):
    mu = jnp.mean(z, axis=-1, keepdims=True)
    zc = z - mu
    var = jnp.mean(zc * zc, axis=-1, keepdims=True)
    return zc * lax.rsqrt(var + LN_EPS) * g + b


def _pack_bf16_pairs(z):
    m = z.shape[1] // 2
    zb = z.astype(BF16).astype(F32)
    bits = lax.bitcast_convert_type(zb, jnp.uint32)
    return jnp.bitwise_or(bits[:, m:], lax.shift_right_logical(bits[:, :m], jnp.uint32(16)))


def _unpack_bf16_pairs(w):
    lo = lax.bitcast_convert_type(lax.shift_left(w, jnp.uint32(16)), F32)
    hi = lax.bitcast_convert_type(jnp.bitwise_and(w, jnp.uint32(0xFFFF0000)), F32)
    return jnp.concatenate([lo, hi], axis=1)


def _mm_ln_kernel(a_ref, w_ref, x_ref, g_ref, b_ref, wr_ref, br_ref, o_ref, opk_ref, cls_ref):
    y = jnp.dot(a_ref[...].astype(BF16), w_ref[...], preferred_element_type=F32)
    z = DEEPNORM_ALPHA * x_ref[...] + y
    out = _layer_norm_rows(z, g_ref[...], b_ref[...])
    o_ref[...] = out
    cls, gates = _route_rows(out, wr_ref[...], br_ref[...])
    cls_ref[...] = cls
    bm, dw = out.shape[0], out.shape[1] // 2
    on_diag = lax.broadcasted_iota(I32, (LANES, LANES), 0) == lax.broadcasted_iota(I32, (LANES, LANES), 1)
    lane = lax.broadcasted_iota(I32, (LANES, LANES), 1)
    tiles = []
    for c in range(bm // LANES):
        tile = jnp.zeros((LANES, LANES), F32)
        for k in range(2):
            g = gates[k:k + 1, c * LANES:(c + 1) * LANES]
            col = jnp.sum(jnp.where(on_diag, g, 0.0), axis=1, keepdims=True)
            tile = jnp.where(lane == k, col, tile)
        tiles.append(tile)
    opk_ref[:, :dw] = _pack_bf16_pairs(out)
    opk_ref[:, dw:] = lax.bitcast_convert_type(jnp.concatenate(tiles, axis=0), jnp.uint32)


def _proj_residual_ln(a, w_bf16, x, ln_g, ln_b, w_router_t, b_router):
    t, k = a.shape
    d = w_bf16.shape[1]
    e = w_router_t.shape[0]
    bm = LN_BM
    est = 2 * (bm * k * a.dtype.itemsize + k * d * 2 + 3 * bm * d * 4) + 8 * bm * d * 4
    return pl.pallas_call(
        _mm_ln_kernel,
        grid=(t // bm,),
        in_specs=[pl.BlockSpec((bm, k), lambda i: (i, 0)),
                  pl.BlockSpec((k, d), lambda i: (0, 0)),
                  pl.BlockSpec((bm, d), lambda i: (i, 0)),
                  pl.BlockSpec((1, d), lambda i: (0, 0)),
                  pl.BlockSpec((1, d), lambda i: (0, 0)),
                  pl.BlockSpec((e, d), lambda i: (0, 0)),
                  pl.BlockSpec((e, 1), lambda i: (0, 0))],
        out_specs=[pl.BlockSpec((bm, d), lambda i: (i, 0)),
                   pl.BlockSpec((bm, d // 2 + LANES), lambda i: (i, 0)),
                   pl.BlockSpec((1, bm), lambda i: (0, i))],
        out_shape=[jax.ShapeDtypeStruct((t, d), F32), jax.ShapeDtypeStruct((t, d // 2 + LANES), jnp.uint32),
                   jax.ShapeDtypeStruct((1, t), I32)],
        compiler_params=pltpu.CompilerParams(
            dimension_semantics=("parallel",), vmem_limit_bytes=_vmem_limit(est)),
        name="proj_residual_ln",
    )(a, w_bf16, x, ln_g.reshape(1, d), ln_b.reshape(1, d), w_router_t, b_router.reshape(e, 1))


def _top2_rows(v, iota):
    n_rows = v.shape[0]
    m1 = jnp.max(v, axis=0, keepdims=True)
    i1 = jnp.min(jnp.where(v == m1, iota, n_rows), axis=0, keepdims=True)
    v2 = jnp.where(iota == i1, -jnp.inf, v)
    m2 = jnp.max(v2, axis=0, keepdims=True)
    i2 = jnp.min(jnp.where(v2 == m2, iota, n_rows), axis=0, keepdims=True)
    return m1, i1, m2, i2


def _route_rows(x, w, b):
    xh = x.astype(BF16)
    xl = (x - xh.astype(F32)).astype(BF16)
    wh = w.astype(BF16)
    wl = (w - wh.astype(F32)).astype(BF16)
    nt = (((1,), (1,)), ((), ()))
    logits = (lax.dot_general(wh, xh, nt, preferred_element_type=F32)
              + lax.dot_general(wh, xl, nt, preferred_element_type=F32)
              + lax.dot_general(wl, xh, nt, preferred_element_type=F32))
    scores = 1.0 / (1.0 + jnp.exp(-logits))
    sel = scores + b
    epg = EXPERTS_PER_GROUP
    bt = x.shape[0]
    iota = lax.broadcasted_iota(I32, (epg, bt), 0)

    best = None
    for g in range(N_GROUPS):
        m1, _, m2, _ = _top2_rows(sel[g * epg:(g + 1) * epg], iota)
        gs = m1 + m2
        if best is None:
            best, gidx = gs, jnp.zeros((1, bt), I32)
        else:
            better = gs > best
            gidx = jnp.where(better, g, gidx)
            best = jnp.where(better, gs, best)

    sel_in = sel[0:epg]
    sc_in = scores[0:epg]
    for g in range(1, N_GROUPS):
        pick = gidx == g
        sel_in = jnp.where(pick, sel[g * epg:(g + 1) * epg], sel_in)
        sc_in = jnp.where(pick, scores[g * epg:(g + 1) * epg], sc_in)
    _, i1, _, i2 = _top2_rows(sel_in, iota)
    g1 = jnp.sum(jnp.where(iota == i1, sc_in, 0.0), axis=0, keepdims=True)
    g2 = jnp.sum(jnp.where(iota == i2, sc_in, 0.0), axis=0, keepdims=True)
    denom = g1 + g2
    first_lo = i1 < i2
    lo = jnp.where(first_lo, i1, i2)
    hi = jnp.where(first_lo, i2, i1)
    pair = lax.shift_right_logical(lo * (2 * epg - 1 - lo), 1) + (hi - lo - 1)
    cls = gidx * PAIRS_PER_GROUP + pair
    gates = jnp.concatenate([jnp.where(first_lo, g1, g2) / denom, jnp.where(first_lo, g2, g1) / denom], axis=0)
    return cls, gates


RANK_SUB = 256
RANK_NSUB = 8


def _rank_kernel(keys_ref, tri_ref, rank_ref, counts_ref, carry_ref, *, n_classes):
    nsub, sub = RANK_NSUB, RANK_SUB

    @pl.when(pl.program_id(0) == 0)
    def _():
        carry_ref[...] = jnp.zeros_like(carry_ref)

    cls = lax.broadcasted_iota(I32, (nsub, n_classes, sub), 1)
    onehot = cls == keys_ref[...]
    oh = jnp.where(onehot, 1.0, 0.0).reshape(nsub * n_classes, sub).astype(BF16)
    pref = jnp.dot(oh, tri_ref[...], preferred_element_type=F32).reshape(nsub, n_classes, sub)
    carry = carry_ref[...]
    for j in range(nsub):
        before = pref[j] + (carry - 1.0)
        rank_ref[j] = jnp.sum(jnp.where(onehot[j], before, 0.0), axis=0, keepdims=True).astype(I32)
        carry = carry + pref[j][:, sub - 1:sub]
    carry_ref[...] = carry
    counts_ref[...] = carry.astype(I32)


def _rank_within_class(keys, n_classes):
    n = keys.shape[0]
    nsub, sub = RANK_NSUB, RANK_SUB
    tri = jnp.asarray(np.triu(np.ones((sub, sub), np.float32)), dtype=BF16)
    rank, counts = pl.pallas_call(
        functools.partial(_rank_kernel, n_classes=n_classes),
        grid=(n // (nsub * sub),),
        in_specs=[pl.BlockSpec((nsub, 1, sub), lambda i: (i, 0, 0)),
                  pl.BlockSpec((sub, sub), lambda i: (0, 0))],
        out_specs=[pl.BlockSpec((nsub, 1, sub), lambda i: (i, 0, 0)),
                   pl.BlockSpec((n_classes, 1), lambda i: (0, 0))],
        out_shape=[jax.ShapeDtypeStruct((n // sub, 1, sub), I32),
                   jax.ShapeDtypeStruct((n_classes, 1), I32)],
        scratch_shapes=[pltpu.VMEM((n_classes, 1), F32)],
        compiler_params=pltpu.CompilerParams(dimension_semantics=("arbitrary",)),
        name="rank_within_class",
    )(keys.reshape(n // sub, 1, sub), tri)
    return rank.reshape(n), counts.reshape(n_classes)


FILL_CHUNK = 4096


def _fill_slots_kernel(dest_ref, cnt_ref, pend_ref, slot_ref, *, n_items, bm):
    n_classes = cnt_ref.shape[0]
    n_blocks = slot_ref.shape[0] // bm
    step = pl.program_id(0)

    @pl.when(step == 0)
    def _():
        def fill_block(blk, first):
            base = blk * bm

            def body(r, carry):
                slot_ref[base + r] = first + r
                return carry

            lax.fori_loop(0, bm, body, 0, unroll=16)

        def pad_class(k, carry):
            @pl.when(cnt_ref[k] > 0)
            def _():
                fill_block(lax.div(pend_ref[k], bm) - 1, n_items + k * bm)
            return carry

        lax.fori_loop(0, n_classes, pad_class, 0)

        def unused_block(blk, carry):
            fill_block(blk, 0)
            return carry

        lax.fori_loop(lax.div(pend_ref[n_classes - 1], bm), n_blocks, unused_block, 0)

    base = step * FILL_CHUNK

    def place(j, carry):
        slot_ref[dest_ref[j]] = base + j
        return carry

    lax.fori_loop(0, FILL_CHUNK, place, 0, unroll=16)


def _fill_slots(dest, counts, pad_end, n_slots, bm):
    n_items = dest.shape[0]
    assert n_items % FILL_CHUNK == 0 and n_slots % bm == 0
    smem = pl.BlockSpec(memory_space=pltpu.SMEM)
    return pl.pallas_call(
        functools.partial(_fill_slots_kernel, n_items=n_items, bm=bm),
        grid=(n_items // FILL_CHUNK,),
        in_specs=[pl.BlockSpec((FILL_CHUNK,), lambda i: (i,), memory_space=pltpu.SMEM), smem, smem],
        out_specs=smem,
        out_shape=jax.ShapeDtypeStruct((n_slots,), I32),
        compiler_params=pltpu.CompilerParams(dimension_semantics=("arbitrary",)),
        name="fill_slots",
    )(dest, counts, pad_end.astype(I32))


def _class_experts():
    epg = EXPERTS_PER_GROUP
    pairs = [(lo, hi) for lo in range(epg) for hi in range(lo + 1, epg)]
    return np.array([[g * epg + lo, g * epg + hi] for g in range(N_GROUPS) for lo, hi in pairs], np.int32)


def _route_slots(cls, bm):
    t = cls.shape[1]
    keys = cls.reshape(t)
    rank, counts = _rank_within_class(keys, N_CLASSES)
    padded = (counts + bm - 1) // bm * bm
    pad_end = jnp.cumsum(padded)
    pad_start = pad_end - padded
    class_ids = jnp.arange(N_CLASSES, dtype=I32)
    dest = jnp.sum(jnp.where(keys[:, None] == class_ids[None, :], pad_start[None, :], 0), axis=1) + rank
    p = (t // bm + N_CLASSES + 1) * bm
    slot_tok = _fill_slots(dest, counts, pad_end, p, bm)
    nb = p // bm
    block_start = jnp.arange(nb, dtype=I32) * bm
    block_class = jnp.minimum(
        jnp.sum((pad_end[None, :] <= block_start[:, None]).astype(I32), axis=1), N_CLASSES - 1)
    in_class = block_class[:, None, None] == class_ids[None, :, None]
    block_experts = jnp.sum(jnp.where(in_class, jnp.asarray(_class_experts())[None], 0), axis=1).reshape(2 * nb)
    n_active = (pad_end[-1] // bm).astype(I32).reshape(1)
    return slot_tok, block_experts, n_active


SUBLANES = 8
SUBLANE_SHIFT = SUBLANES.bit_length() - 1


def _expert_kernel(be_ref, tok_ref, nact_ref, x_hbm, wga_ref, wua_ref, wda_ref, wgb_ref, wub_ref, wdb_ref,
                   y_hbm, xnext, xcur, ycur, yout, wga_bf, wua_bf, wda_bf, wgb_bf, wub_bf, wdb_bf, gsem, ssem,
                   *, n_tok, n_classes):
    bm = EXPERT_BM
    tiles = bm // SUBLANES
    dw = ycur.shape[-1]
    i = pl.program_id(0)
    n_blocks = pl.num_programs(0)
    nact = nact_ref[0]
    w_f32 = ((wga_ref, wua_ref, wda_ref), (wgb_ref, wub_ref, wdb_ref))
    wbf = (wga_bf, wua_bf, wda_bf, wgb_bf, wub_bf, wdb_bf)

    def row_copies(blk, gather, unrolled):
        base = blk * bm

        def body(rt, carry):
            for u in range(SUBLANES):
                row = tok_ref[base + rt * SUBLANES + u]
                if gather:
                    row = jnp.bitwise_and(row, n_tok - 1)
                    pltpu.make_async_copy(
                        x_hbm.at[lax.shift_right_logical(row, SUBLANE_SHIFT),
                                 pl.ds(jnp.bitwise_and(row, SUBLANES - 1), 1), :],
                        xnext.at[rt, pl.ds(u, 1), :], gsem.at[0]).start()
                else:
                    pltpu.make_async_copy(
                        yout.at[rt, pl.ds(u, 1), :],
                        y_hbm.at[lax.shift_right_logical(row, SUBLANE_SHIFT),
                                 pl.ds(jnp.bitwise_and(row, SUBLANES - 1), 1), :],
                        ssem.at[0]).start()
            return carry

        if unrolled is None:
            lax.fori_loop(0, tiles, body, 0)
        else:
            for rt in range(*unrolled):
                body(rt, 0)

    def wait_gather():
        pltpu.make_async_copy(x_hbm.at[pl.ds(0, tiles)], xnext, gsem.at[0]).wait()

    def wait_scatter():
        pltpu.make_async_copy(yout, y_hbm.at[pl.ds(0, tiles)], ssem.at[0]).wait()

    @pl.when(i == 0)
    def _():
        row_copies(0, True, unrolled=None)
        yout[...] = jnp.zeros(yout.shape, yout.dtype)
        ycur[...] = jnp.zeros(ycur.shape, ycur.dtype)
        spare_tile0 = n_tok // SUBLANES

        def spare_copy(k):
            return pltpu.make_async_copy(
                yout, y_hbm.at[pl.ds(spare_tile0 + k * tiles, tiles)], ssem.at[0])

        @pl.loop(0, n_classes)
        def _(k):
            spare_copy(k).start()

        @pl.loop(0, n_classes)
        def _(k):
            spare_copy(k).wait()

    for side in range(2):
        idx = 2 * i + side
        changed = jnp.logical_or(i == 0, be_ref[idx] != be_ref[jnp.maximum(idx - 2, 0)])

        @pl.when(jnp.logical_and(changed, i < nact))
        def _():
            for m in range(3):
                wbf[3 * side + m][...] = w_f32[side][m][0].astype(BF16)

    @pl.when(i < nact)
    def _():
        wait_gather()
        xcur[...] = xnext[...]

        @pl.when(i >= 1)
        def _():
            wait_scatter()

        yout[...] = ycur[...]

    @pl.when(nact - i >= 1)
    def _():
        row_copies(jnp.minimum(i + 1, n_blocks - 1), True, unrolled=(0, tiles))
        row_copies(jnp.maximum(i - 1, 0), False, unrolled=(0, tiles))
        rows = xcur[...].reshape(bm, dw + LANES)
        x = _unpack_bf16_pairs(rows[:, :dw]).astype(BF16)
        gates = lax.bitcast_convert_type(rows[:, dw:], F32)
        y = None
        for side in range(2):
            wg_bf, wu_bf, wd_bf = wbf[3 * side], wbf[3 * side + 1], wbf[3 * side + 2]
            gate = jnp.dot(x, wg_bf[...], preferred_element_type=F32)
            up = jnp.dot(x, wu_bf[...], preferred_element_type=F32)
            h = (gate / (1.0 + jnp.exp(-gate))) * up
            ys = jnp.dot(h.astype(BF16), wd_bf[...], preferred_element_type=F32) * gates[:, side:side + 1]
            y = ys if y is None else y + ys
        ycur[...] = _pack_bf16_pairs(y).reshape(tiles, SUBLANES, dw)

    @pl.when(i == nact)
    def _():
        wait_gather()
        wait_scatter()
        yout[...] = ycur[...]
        row_copies(i - 1, False, unrolled=None)
        wait_scatter()


def _experts(x_pk, slot_tok, block_experts, n_active, w_gate, w_up, w_down, layer):
    t, dw_in = x_pk.shape
    _, _, d, f = w_gate.shape
    dw = d // 2
    assert dw_in == dw + LANES
    bm = EXPERT_BM
    n_blocks = slot_tok.shape[0] // bm
    n_rows = t + N_CLASSES * bm
    assert t & (t - 1) == 0 and t % SUBLANES == 0 and bm % SUBLANES == 0
    est = 2 * 6 * d * f * 4 + 6 * d * f * 2 + 4 * bm * d * 2 + 10 * bm * d * 4

    def w_spec(shape, side):
        return pl.BlockSpec((None, 1) + shape, lambda i, be, tk, na: (layer, be[2 * i + side], 0, 0))

    xbuf = pltpu.VMEM((bm // SUBLANES, SUBLANES, dw_in), jnp.uint32)
    ybuf = pltpu.VMEM((bm // SUBLANES, SUBLANES, dw), jnp.uint32)
    grid_spec = pltpu.PrefetchScalarGridSpec(
        num_scalar_prefetch=3,
        grid=(n_blocks,),
        in_specs=[pl.BlockSpec(memory_space=pl.ANY),
                  w_spec((d, f), 0), w_spec((d, f), 0), w_spec((f, d), 0),
                  w_spec((d, f), 1), w_spec((d, f), 1), w_spec((f, d), 1)],
        out_specs=pl.BlockSpec(memory_space=pl.ANY),
        scratch_shapes=[xbuf, xbuf, ybuf, ybuf]
                       + [pltpu.VMEM((d, f), BF16), pltpu.VMEM((d, f), BF16), pltpu.VMEM((f, d), BF16)] * 2
                       + [pltpu.SemaphoreType.DMA((1,)), pltpu.SemaphoreType.DMA((1,))],
    )
    y = pl.pallas_call(
        functools.partial(_expert_kernel, n_tok=t, n_classes=N_CLASSES),
        grid_spec=grid_spec,
        out_shape=jax.ShapeDtypeStruct((n_rows // SUBLANES, SUBLANES, dw), jnp.uint32),
        compiler_params=pltpu.CompilerParams(
            dimension_semantics=("arbitrary",), vmem_limit_bytes=_vmem_limit(est)),
        name="experts",
    )(block_experts, slot_tok, n_active, x_pk.reshape(t // SUBLANES, SUBLANES, dw_in),
      w_gate, w_up, w_down, w_gate, w_up, w_down)
    return y.reshape(n_rows, dw)


def _combine_kernel(x_ref, y_ref, g_ref, b_ref, o_ref):
    z = DEEPNORM_ALPHA * x_ref[...] + _unpack_bf16_pairs(y_ref[...])
    o_ref[...] = _layer_norm_rows(z, g_ref[...], b_ref[...])


def _combine_ln(x, y_tok, ln_g, ln_b):
    t, d = x.shape
    bt = COMBINE_BT
    return pl.pallas_call(
        _combine_kernel,
        grid=(t // bt,),
        in_specs=[pl.BlockSpec((bt, d), lambda i: (i, 0)),
                  pl.BlockSpec((bt, d // 2), lambda i: (i, 0)),
                  pl.BlockSpec((1, d), lambda i: (0, 0)),
                  pl.BlockSpec((1, d), lambda i: (0, 0))],
        out_specs=pl.BlockSpec((bt, d), lambda i: (i, 0)),
        out_shape=jax.ShapeDtypeStruct((t, d), F32),
        compiler_params=pltpu.CompilerParams(
            dimension_semantics=("parallel",), vmem_limit_bytes=_vmem_limit(10 * bt * d * 4)),
        name="combine_ln",
    )(x, y_tok, ln_g.reshape(1, d), ln_b.reshape(1, d))


def _moe_residual_ln(x, x_pk, cls, w_gate, w_up, w_down, layer, ln_g, ln_b):
    slot_tok, block_experts, n_active = _route_slots(cls, EXPERT_BM)
    y_tok = _experts(x_pk, slot_tok, block_experts, n_active, w_gate, w_up, w_down, layer)
    return _combine_ln(x, y_tok, ln_g, ln_b)


def _na_bias_pairs(rpb):
    w = GRID_W
    kc = min(WIN_COLS, w)
    cols = np.arange(w)
    col_start = np.clip(cols - kc // 2, 0, w - kc)
    col_mask = (cols[None, :] >= col_start[:, None]) & (cols[None, :] < col_start[:, None] + kc)
    col_idx = np.clip(cols[None, :] - cols[:, None] + WIN_COLS - 1, 0, 2 * WIN_COLS - 2)
    n_ci = 2 * WIN_COLS - 1
    pick = (col_idx[None] == np.arange(n_ci)[:, None, None]).astype(np.float32)
    tb = jnp.einsum("hrc,cqk->hrqk", rpb.astype(F32), jnp.asarray(pick), precision=lax.Precision.HIGHEST)
    tb = jnp.where(col_mask[None, None], tb, NEG_BIAS)
    neg = jnp.full((N_HEADS, 1, w, w), NEG_BIAS, F32)
    ext = jnp.concatenate([neg, tb, neg], axis=1)
    return jnp.concatenate([ext[:, :-1], ext[:, 1:]], axis=-1)


def _na_tile_plan(rows):
    kr = min(WIN_ROWS, rows)
    n_blocks = rows // NA_RQ
    plan = []
    for blk in (0, 1, n_blocks - 1):
        ks = int(np.clip(blk - 1, 0, n_blocks - 3)) * NA_RQ
        per_rq = []
        for rq in range(NA_RQ):
            r = blk * NA_RQ + rq
            rs = int(np.clip(r - kr // 2, 0, rows - kr))
            ri = [ks + j - r + WIN_ROWS - 1 if rs <= ks + j < rs + kr else None for j in range(NA_KROWS)]
            tiles = []
            for jp in range(NA_KROWS // 2):
                a, b = ri[2 * jp], ri[2 * jp + 1]
                if a is not None and b is not None:
                    tiles.append(("both", b))
                elif b is not None:
                    tiles.append(("hi", b))
                elif a is not None:
                    tiles.append(("lo", a + 1))
                else:
                    tiles.append(("none", 0))
            per_rq.append(tiles)
        plan.append(per_rq)
    return plan


def _na_kernel(q_ref, k0_ref, k1_ref, k2_ref, v0_ref, v1_ref, v2_ref, pair_ref, o_ref, bias_ref, *, plan):
    dh = HEAD_DIM
    w = GRID_W
    nt = (((1,), (1,)), ((), ()))
    scale = HEAD_DIM ** -0.5
    i = pl.program_id(2)
    n_blocks = pl.num_programs(2)

    def rebuild(per_rq):
        upper = lax.broadcasted_iota(I32, (w, 2 * w), 1) >= w
        for rq, tiles in enumerate(per_rq):
            for jp, (kind, k) in enumerate(tiles):
                for h in range(NA_HEADS_PER_STEP):
                    if kind == "none":
                        tile = jnp.full((w, 2 * w), NEG_BIAS, F32)
                    elif kind == "both":
                        tile = pair_ref[h, k]
                    elif kind == "hi":
                        tile = jnp.where(upper, pair_ref[h, k], NEG_BIAS)
                    else:
                        tile = jnp.where(upper, NEG_BIAS, pair_ref[h, k])
                    bias_ref[h, rq * w:(rq + 1) * w, jp * 2 * w:(jp + 1) * 2 * w] = tile

    for ty, first_step in enumerate((0, 1, n_blocks - 1)):
        @pl.when(i == first_step)
        def _():
            rebuild(plan[ty])

    assert 2 * dh == LANES
    tq = q_ref.shape[1]
    low = lax.broadcasted_iota(I32, (tq, LANES), 1) < dh
    low_k = lax.broadcasted_iota(I32, (3 * tq, LANES), 1) < dh
    outs = []
    for hp in range(NA_HEADS_PER_STEP // 2):
        cols = slice(hp * LANES, (hp + 1) * LANES)
        q2 = q_ref[0, :, cols] * scale
        k2 = jnp.concatenate([k0_ref[0, :, cols], k1_ref[0, :, cols], k2_ref[0, :, cols]], axis=0)
        v2 = jnp.concatenate([v0_ref[0, :, cols], v1_ref[0, :, cols], v2_ref[0, :, cols]], axis=0)
        halves = []
        for sub in range(2):
            mine, mine_k = (low, low_k) if sub == 0 else (~low, ~low_k)
            q = jnp.where(mine, q2, jnp.zeros_like(q2))
            v = jnp.where(mine_k, v2, jnp.ones_like(v2))
            s = lax.dot_general(q, k2, nt, preferred_element_type=F32) + bias_ref[2 * hp + sub]
            m = jnp.max(s, axis=-1, keepdims=True)
            p = jnp.exp((s - m).astype(BF16))
            o = jnp.dot(p, v, preferred_element_type=F32)
            denom = o[:, dh:dh + 1] if sub == 0 else o[:, 0:1]
            halves.append(o * (1.0 / denom))
        outs.append(jnp.where(low, halves[0], halves[1]))
    o_ref[0] = jnp.concatenate(outs, axis=-1).astype(o_ref.dtype)


def _neighbourhood_attention(qkv, bias_pairs, rows):
    b, s, _ = qkv.shape
    d = D_MODEL
    w = GRID_W
    tq = NA_RQ * w
    n_blocks = rows // NA_RQ
    hps = NA_HEADS_PER_STEP
    hw = hps * HEAD_DIM
    n_hh = d // hw
    assert NA_KROWS * w == 3 * tq and n_blocks >= 3 and NA_KROWS % 2 == 0

    def kv_map(part, j):
        def index_map(hh, bi, i):
            return (bi, jnp.clip(i - 1, 0, n_blocks - 3) + j, part * n_hh + hh)
        return index_map

    blk = (1, tq, hw)
    pair_blk = (hps,) + bias_pairs.shape[1:]
    est = (2 * (7 * tq * hw * 2 + tq * hw * 2) + 2 * 4 * int(np.prod(pair_blk))
           + hps * tq * 3 * tq * 4 + 8 * tq * 3 * tq * 4)
    return pl.pallas_call(
        functools.partial(_na_kernel, plan=_na_tile_plan(rows)),
        grid=(n_hh, b, n_blocks),
        in_specs=[pl.BlockSpec(blk, lambda hh, bi, i: (bi, i, hh))]
                 + [pl.BlockSpec(blk, kv_map(1, j)) for j in range(3)]
                 + [pl.BlockSpec(blk, kv_map(2, j)) for j in range(3)]
                 + [pl.BlockSpec(pair_blk, lambda hh, bi, i: (hh, 0, 0, 0))],
        out_specs=pl.BlockSpec(blk, lambda hh, bi, i: (bi, i, hh)),
        out_shape=jax.ShapeDtypeStruct((b, s, d), BF16),
        scratch_shapes=[pltpu.VMEM((hps, tq, 3 * tq), F32)],
        compiler_params=pltpu.CompilerParams(
            dimension_semantics=("arbitrary", "arbitrary", "arbitrary"), vmem_limit_bytes=_vmem_limit(est)),
        name="neighbourhood_attention",
    )(qkv, qkv, qkv, qkv, qkv, qkv, qkv, bias_pairs)


def kernel(x, fourier_w_in, fourier_w_out, na_w_qkv, na_rpb, na_w_out, router_w, router_b,
           expert_w_gate, expert_w_up, expert_w_down, ln_g, ln_b):
    b, s, d = x.shape
    t = b * s
    rows = s // GRID_W
    w_router_t = router_w.T
    xt = x.reshape(t, d)

    f = _fourier_mixer_pre_out(x, fourier_w_in[0])
    xt, xt_pk, cls = _proj_residual_ln(f.reshape(t, d), fourier_w_out[0].astype(BF16), xt,
                                       ln_g[0, 0], ln_b[0, 0], w_router_t, router_b)
    xt = _moe_residual_ln(xt, xt_pk, cls, expert_w_gate, expert_w_up, expert_w_down, 0,
                          ln_g[0, 1], ln_b[0, 1])

    qkv = _matmul(xt, na_w_qkv[0].astype(BF16), BF16, MM_BM, 3 * d // 2).reshape(b, s, 3 * d)
    bias = _na_bias_pairs(na_rpb[0])
    o = _neighbourhood_attention(qkv, bias, rows)
    xt, xt_pk, cls = _proj_residual_ln(o.reshape(t, d), na_w_out[0].astype(BF16), xt,
                                       ln_g[1, 0], ln_b[1, 0], w_router_t, router_b)
    xt = _moe_residual_ln(xt, xt_pk, cls, expert_w_gate, expert_w_up, expert_w_down, 1,
                          ln_g[1, 1], ln_b[1, 1])
    return xt.reshape(b, s, d)
```

```python
import functools

import jax
import jax.numpy as jnp
import numpy as np
from jax import lax
from jax.experimental import pallas as pl
from jax.experimental.pallas import tpu as pltpu

F32 = jnp.float32
BF16 = jnp.bfloat16
I32 = jnp.int32

D_MODEL = 1024
GRID_W = 64
N_FOURIER_GROUPS = 4
FOURIER_GROUP_DIM = D_MODEL // N_FOURIER_GROUPS
N_HEADS = 16
HEAD_DIM = D_MODEL // N_HEADS
WIN_ROWS = 8
WIN_COLS = 16
N_EXPERTS = 32
N_GROUPS = 4
EXPERTS_PER_GROUP = N_EXPERTS // N_GROUPS
D_EXPERT = D_MODEL // 2
PAIRS_PER_GROUP = EXPERTS_PER_GROUP * (EXPERTS_PER_GROUP - 1) // 2
N_CLASSES = N_GROUPS * PAIRS_PER_GROUP
DEPTH = 2
DEEPNORM_ALPHA = (2 * DEPTH) ** 0.25
LN_EPS = 1e-5

V7X_VMEM_BYTES = 64 * 1024 * 1024
LANES = 128

FFT_N1 = 64
FFT_N2 = 128
FFT_CHUNK = 128

MM_BM = 1024
LN_BM = 512
EXPERT_BM = 384
COMBINE_BT = 512
NA_RQ = 4
NA_KROWS = NA_RQ + WIN_ROWS
NA_HEADS_PER_STEP = 16
NEG_BIAS = -1e30


def _vmem_limit(nbytes):
    return int(min(max(nbytes, 32 * 1024 * 1024), V7X_VMEM_BYTES - 8 * 1024 * 1024))


def _mm_kernel(a_ref, b_ref, o_ref, *, precision):
    if precision is None:
        a = a_ref[...].astype(BF16)
        b = b_ref[...].astype(BF16)
        acc = jnp.dot(a, b, preferred_element_type=F32)
    else:
        acc = jnp.dot(a_ref[...], b_ref[...], preferred_element_type=F32, precision=precision)
    o_ref[...] = acc.astype(o_ref.dtype)


def _matmul(a, b, out_dtype, bm, bn, precision=None):
    m, k = a.shape
    _, n = b.shape
    est = 2 * (bm * k * a.dtype.itemsize + k * bn * b.dtype.itemsize + bm * bn * 4) + 3 * bm * bn * 4
    return pl.pallas_call(
        functools.partial(_mm_kernel, precision=precision),
        grid=(m // bm, n // bn),
        in_specs=[pl.BlockSpec((bm, k), lambda i, j: (i, 0)),
                  pl.BlockSpec((k, bn), lambda i, j: (0, j))],
        out_specs=pl.BlockSpec((bm, bn), lambda i, j: (i, j)),
        out_shape=jax.ShapeDtypeStruct((m, n), out_dtype),
        compiler_params=pltpu.CompilerParams(
            dimension_semantics=("parallel", "parallel"), vmem_limit_bytes=_vmem_limit(est)),
        name="matmul",
    )(a, b)


def _fourier_channel_tables():
    n = FOURIER_GROUP_DIM
    c = np.arange(n)
    ang = 2.0 * np.pi * ((c[:, None] * c[None, :]) % n) / n
    cos, sin = np.cos(ang) / np.sqrt(n), np.sin(ang) / np.sqrt(n)
    half = FFT_CHUNK
    tabs = [np.concatenate([cos[:, h * half:(h + 1) * half], sin[:, h * half:(h + 1) * half]], axis=1)
            for h in range(n // half)]
    return np.stack(tabs).astype(np.float32)


def _fourier_seq_tables(seq):
    n1, n2 = FFT_N1, FFT_N2
    assert n1 * n2 == seq
    k2 = np.arange(n2)
    s2 = np.arange(n2)
    m1 = np.empty((n1, 2 * n2, 2 * n2), np.float32)
    for s1 in range(n1):
        ang = 2.0 * np.pi * ((k2[:, None] * (s1 + n1 * s2[None, :])) % seq) / seq
        mr, mi = np.cos(ang) / np.sqrt(n2), np.sin(ang) / np.sqrt(n2)
        m1[s1] = np.block([[mr, -mi], [mi, mr]])
    k1 = np.arange(n1)
    ang = 2.0 * np.pi * ((k1[:, None] * k1[None, :]) % n1) / n1
    w2 = np.concatenate([np.cos(ang), -np.sin(ang)], axis=1) / np.sqrt(n1)
    return m1, w2.astype(np.float32)


def _fft_kernel(a_ref, b_ref, m1_ref, w2_ref, o_ref, zs_ref):
    n1, n2, c = FFT_N1, FFT_N2, FFT_CHUNK

    def stage1(s1, carry):
        rows = pl.ds(s1, n2, stride=n1)
        x = jnp.concatenate([a_ref[0, rows, :], b_ref[0, rows, :]], axis=0).astype(BF16)
        z = jnp.dot(m1_ref[s1], x, preferred_element_type=F32)
        zs_ref[pl.ds(pl.multiple_of(s1 * 2 * n2, 2 * n2), 2 * n2), :] = z
        return carry

    lax.fori_loop(0, n1, stage1, 0, unroll=16)

    def stage2(kk, carry):
        k2 = 2 * kk
        parts = []
        for d in range(2):
            zr = zs_ref[pl.ds(k2 + d, n1, stride=2 * n2), :]
            zi = zs_ref[pl.ds(n2 + k2 + d, n1, stride=2 * n2), :]
            parts.append(jnp.concatenate([zr, zi], axis=0))
        z = jnp.concatenate(parts, axis=1).astype(BF16)
        y = jnp.dot(w2_ref[...], z, preferred_element_type=F32)
        o_ref[0, pl.ds(k2, n1, stride=n2), :] = y[:, :c]
        o_ref[0, pl.ds(k2 + 1, n1, stride=n2), :] = y[:, c:]
        return carry

    lax.fori_loop(0, n2 // 2, stage2, 0, unroll=16)


def _seq_fft(ab, m1, w2):
    b, s, two_d = ab.shape
    d = two_d // 2
    c = FFT_CHUNK
    est = 2 * (s * 2 * c * 4 + m1.size * 2 + s * c * 4) + FFT_N1 * 2 * FFT_N2 * c * 4 + (4 << 20)
    return pl.pallas_call(
        _fft_kernel,
        grid=(b, d // c),
        in_specs=[pl.BlockSpec((1, s, c), lambda i, j: (i, 0, 2 * j)),
                  pl.BlockSpec((1, s, c), lambda i, j: (i, 0, 2 * j + 1)),
                  pl.BlockSpec(m1.shape, lambda i, j: (0, 0, 0)),
                  pl.BlockSpec(w2.shape, lambda i, j: (0, 0))],
        out_specs=pl.BlockSpec((1, s, c), lambda i, j: (i, 0, j)),
        out_shape=jax.ShapeDtypeStruct((b, s, d), F32),
        scratch_shapes=[pltpu.VMEM((FFT_N1 * 2 * FFT_N2, c), F32)],
        compiler_params=pltpu.CompilerParams(
            dimension_semantics=("parallel", "parallel"), vmem_limit_bytes=_vmem_limit(est)),
        name="seq_fft",
    )(ab, ab, m1, w2)


def _fourier_mixer_pre_out(x, w_in):
    b, s, d = x.shape
    g, gd, c = N_FOURIER_GROUPS, FOURIER_GROUP_DIM, FFT_CHUNK
    halves = gd // c
    cs = jnp.asarray(_fourier_channel_tables())
    w_ab = pl.pallas_call(
        functools.partial(_mm_kernel, precision=lax.Precision.HIGHEST),
        grid=(g, halves),
        in_specs=[pl.BlockSpec((d, gd), lambda i, h: (0, i)),
                  pl.BlockSpec((None, gd, 2 * c), lambda i, h: (h, 0, 0))],
        out_specs=pl.BlockSpec((d, 2 * c), lambda i, h: (0, i * halves + h)),
        out_shape=jax.ShapeDtypeStruct((d, 2 * d), BF16),
        name="fourier_weight_fold",
    )(w_in, cs)
    ab = _matmul(x.reshape(b * s, d), w_ab, F32, MM_BM, 2 * d).reshape(b, s, 2 * d)
    m1, w2 = _fourier_seq_tables(s)
    return _seq_fft(ab, jnp.asarray(m1, dtype=BF16), jnp.asarray(w2, dtype=BF16))


def _layer_norm_rows(z, g, b):
    mu = jnp.mean(z, axis=-1, keepdims=True)
    zc = z - mu
    var = jnp.mean(zc * zc, axis=-1, keepdims=True)
    return zc * lax.rsqrt(var + LN_EPS) * g + b


def _pack_bf16_pairs(z):
    m = z.shape[1] // 2
    zb = z.astype(BF16).astype(F32)
    bits = lax.bitcast_convert_type(zb, jnp.uint32)
    return jnp.bitwise_or(bits[:, m:], lax.shift_right_logical(bits[:, :m], jnp.uint32(16)))


def _unpack_bf16_pairs(w):
    lo = lax.bitcast_convert_type(lax.shift_left(w, jnp.uint32(16)), F32)
    hi = lax.bitcast_convert_type(jnp.bitwise_and(w, jnp.uint32(0xFFFF0000)), F32)
    return jnp.concatenate([lo, hi], axis=1)


def _mm_ln_kernel(a_ref, w_ref, x_ref, g_ref, b_ref, wr_ref, br_ref, o_ref, opk_ref, cls_ref):
    y = jnp.dot(a_ref[...].astype(BF16), w_ref[...], preferred_element_type=F32)
    z = DEEPNORM_ALPHA * x_ref[...] + y
    out = _layer_norm_rows(z, g_ref[...], b_ref[...])
    o_ref[...] = out
    cls, gates = _route_rows(out, wr_ref[...], br_ref[...])
    cls_ref[...] = cls
    bm, dw = out.shape[0], out.shape[1] // 2
    on_diag = lax.broadcasted_iota(I32, (LANES, LANES), 0) == lax.broadcasted_iota(I32, (LANES, LANES), 1)
    lane = lax.broadcasted_iota(I32, (LANES, LANES), 1)
    tiles = []
    for c in range(bm // LANES):
        tile = jnp.zeros((LANES, LANES), F32)
        for k in range(2):
            g = gates[k:k + 1, c * LANES:(c + 1) * LANES]
            col = jnp.sum(jnp.where(on_diag, g, 0.0), axis=1, keepdims=True)
            tile = jnp.where(lane == k, col, tile)
        tiles.append(tile)
    opk_ref[:, :dw] = _pack_bf16_pairs(out)
    opk_ref[:, dw:] = lax.bitcast_convert_type(jnp.concatenate(tiles, axis=0), jnp.uint32)


def _proj_residual_ln(a, w_bf16, x, ln_g, ln_b, w_router_t, b_router):
    t, k = a.shape
    d = w_bf16.shape[1]
    e = w_router_t.shape[0]
    bm = LN_BM
    est = 2 * (bm * k * a.dtype.itemsize + k * d * 2 + 3 * bm * d * 4) + 8 * bm * d * 4
    return pl.pallas_call(
        _mm_ln_kernel,
        grid=(t // bm,),
        in_specs=[pl.BlockSpec((bm, k), lambda i: (i, 0)),
                  pl.BlockSpec((k, d), lambda i: (0, 0)),
                  pl.BlockSpec((bm, d), lambda i: (i, 0)),
                  pl.BlockSpec((1, d), lambda i: (0, 0)),
                  pl.BlockSpec((1, d), lambda i: (0, 0)),
                  pl.BlockSpec((e, d), lambda i: (0, 0)),
                  pl.BlockSpec((e, 1), lambda i: (0, 0))],
        out_specs=[pl.BlockSpec((bm, d), lambda i: (i, 0)),
                   pl.BlockSpec((bm, d // 2 + LANES), lambda i: (i, 0)),
                   pl.BlockSpec((1, bm), lambda i: (0, i))],
        out_shape=[jax.ShapeDtypeStruct((t, d), F32), jax.ShapeDtypeStruct((t, d // 2 + LANES), jnp.uint32),
                   jax.ShapeDtypeStruct((1, t), I32)],
        compiler_params=pltpu.CompilerParams(
            dimension_semantics=("parallel",), vmem_limit_bytes=_vmem_limit(est)),
        name="proj_residual_ln",
    )(a, w_bf16, x, ln_g.reshape(1, d), ln_b.reshape(1, d), w_router_t, b_router.reshape(e, 1))


def _top2_rows(v, iota):
    n_rows = v.shape[0]
    m1 = jnp.max(v, axis=0, keepdims=True)
    i1 = jnp.min(jnp.where(v == m1, iota, n_rows), axis=0, keepdims=True)
    v2 = jnp.where(iota == i1, -jnp.inf, v)
    m2 = jnp.max(v2, axis=0, keepdims=True)
    i2 = jnp.min(jnp.where(v2 == m2, iota, n_rows), axis=0, keepdims=True)
    return m1, i1, m2, i2


def _route_rows(x, w, b):
    xh = x.astype(BF16)
    xl = (x - xh.astype(F32)).astype(BF16)
    wh = w.astype(BF16)
    wl = (w - wh.astype(F32)).astype(BF16)
    nt = (((1,), (1,)), ((), ()))
    logits = (lax.dot_general(wh, xh, nt, preferred_element_type=F32)
              + lax.dot_general(wh, xl, nt, preferred_element_type=F32)
              + lax.dot_general(wl, xh, nt, preferred_element_type=F32))
    scores = 1.0 / (1.0 + jnp.exp(-logits))
    sel = scores + b
    epg = EXPERTS_PER_GROUP
    bt = x.shape[0]
    iota = lax.broadcasted_iota(I32, (epg, bt), 0)

    best = None
    for g in range(N_GROUPS):
        m1, _, m2, _ = _top2_rows(sel[g * epg:(g + 1) * epg], iota)
        gs = m1 + m2
        if best is None:
            best, gidx = gs, jnp.zeros((1, bt), I32)
        else:
            better = gs > best
            gidx = jnp.where(better, g, gidx)
            best = jnp.where(better, gs, best)

    sel_in = sel[0:epg]
    sc_in = scores[0:epg]
    for g in range(1, N_GROUPS):
        pick = gidx == g
        sel_in = jnp.where(pick, sel[g * epg:(g + 1) * epg], sel_in)
        sc_in = jnp.where(pick, scores[g * epg:(g + 1) * epg], sc_in)
    _, i1, _, i2 = _top2_rows(sel_in, iota)
    g1 = jnp.sum(jnp.where(iota == i1, sc_in, 0.0), axis=0, keepdims=True)
    g2 = jnp.sum(jnp.where(iota == i2, sc_in, 0.0), axis=0, keepdims=True)
    denom = g1 + g2
    first_lo = i1 < i2
    lo = jnp.where(first_lo, i1, i2)
    hi = jnp.where(first_lo, i2, i1)
    pair = lax.shift_right_logical(lo * (2 * epg - 1 - lo), 1) + (hi - lo - 1)
    cls = gidx * PAIRS_PER_GROUP + pair
    gates = jnp.concatenate([jnp.where(first_lo, g1, g2) / denom, jnp.where(first_lo, g2, g1) / denom], axis=0)
    return cls, gates


RANK_SUB = 256
RANK_NSUB = 8


def _rank_kernel(keys_ref, tri_ref, rank_ref, counts_ref, carry_ref, *, n_classes):
    nsub, sub = RANK_NSUB, RANK_SUB

    @pl.when(pl.program_id(0) == 0)
    def _():
        carry_ref[...] = jnp.zeros_like(carry_ref)

    cls = lax.broadcasted_iota(I32, (nsub, n_classes, sub), 1)
    onehot = cls == keys_ref[...]
    oh = jnp.where(onehot, 1.0, 0.0).reshape(nsub * n_classes, sub).astype(BF16)
    pref = jnp.dot(oh, tri_ref[...], preferred_element_type=F32).reshape(nsub, n_classes, sub)
    carry = carry_ref[...]
    for j in range(nsub):
        before = pref[j] + (carry - 1.0)
        rank_ref[j] = jnp.sum(jnp.where(onehot[j], before, 0.0), axis=0, keepdims=True).astype(I32)
        carry = carry + pref[j][:, sub - 1:sub]
    carry_ref[...] = carry
    counts_ref[...] = carry.astype(I32)


def _rank_within_class(keys, n_classes):
    n = keys.shape[0]
    nsub, sub = RANK_NSUB, RANK_SUB
    tri = jnp.asarray(np.triu(np.ones((sub, sub), np.float32)), dtype=BF16)
    rank, counts = pl.pallas_call(
        functools.partial(_rank_kernel, n_classes=n_classes),
        grid=(n // (nsub * sub),),
        in_specs=[pl.BlockSpec((nsub, 1, sub), lambda i: (i, 0, 0)),
                  pl.BlockSpec((sub, sub), lambda i: (0, 0))],
        out_specs=[pl.BlockSpec((nsub, 1, sub), lambda i: (i, 0, 0)),
                   pl.BlockSpec((n_classes, 1), lambda i: (0, 0))],
        out_shape=[jax.ShapeDtypeStruct((n // sub, 1, sub), I32),
                   jax.ShapeDtypeStruct((n_classes, 1), I32)],
        scratch_shapes=[pltpu.VMEM((n_classes, 1), F32)],
        compiler_params=pltpu.CompilerParams(dimension_semantics=("arbitrary",)),
        name="rank_within_class",
    )(keys.reshape(n // sub, 1, sub), tri)
    return rank.reshape(n), counts.reshape(n_classes)


FILL_CHUNK = 4096


def _fill_slots_kernel(dest_ref, default_hbm, slot_ref, sem):
    step = pl.program_id(0)

    @pl.when(step == 0)
    def _():
        copy = pltpu.make_async_copy(default_hbm, slot_ref, sem.at[0])
        copy.start()
        copy.wait()

    base = step * FILL_CHUNK

    def place(j, carry):
        slot_ref[dest_ref[j]] = base + j
        return carry

    lax.fori_loop(0, FILL_CHUNK, place, 0, unroll=16)


def _fill_slots(dest, default):
    n_items = dest.shape[0]
    assert n_items % FILL_CHUNK == 0
    return pl.pallas_call(
        _fill_slots_kernel,
        grid=(n_items // FILL_CHUNK,),
        in_specs=[pl.BlockSpec((FILL_CHUNK,), lambda i: (i,), memory_space=pltpu.SMEM),
                  pl.BlockSpec(memory_space=pl.ANY)],
        out_specs=pl.BlockSpec(memory_space=pltpu.SMEM),
        out_shape=jax.ShapeDtypeStruct(default.shape, I32),
        scratch_shapes=[pltpu.SemaphoreType.DMA((1,))],
        compiler_params=pltpu.CompilerParams(dimension_semantics=("arbitrary",)),
        name="fill_slots",
    )(dest, default)


def _class_experts():
    epg = EXPERTS_PER_GROUP
    pairs = [(lo, hi) for lo in range(epg) for hi in range(lo + 1, epg)]
    return np.array([[g * epg + lo, g * epg + hi] for g in range(N_GROUPS) for lo, hi in pairs], np.int32)


def _route_slots(cls, bm):
    t = cls.shape[1]
    keys = cls.reshape(t)
    rank, counts = _rank_within_class(keys, N_CLASSES)
    padded = (counts + bm - 1) // bm * bm
    pad_end = jnp.cumsum(padded)
    pad_start = pad_end - padded
    class_ids = jnp.arange(N_CLASSES, dtype=I32)
    dest = jnp.sum(jnp.where(keys[:, None] == class_ids[None, :], pad_start[None, :], 0), axis=1) + rank
    nb = t // bm + N_CLASSES + 1
    block_start = jnp.arange(nb, dtype=I32) * bm
    block_class = jnp.minimum(
        jnp.sum((pad_end[None, :] <= block_start[:, None]).astype(I32), axis=1), N_CLASSES - 1)
    row = jnp.arange(bm, dtype=I32)[None, :]
    in_use = (block_start < pad_end[-1])[:, None]
    default = jnp.where(in_use, t + block_class[:, None] * bm + row, row).reshape(nb * bm)
    slot_tok = _fill_slots(dest, default)
    in_class = block_class[:, None, None] == class_ids[None, :, None]
    block_experts = jnp.sum(jnp.where(in_class, jnp.asarray(_class_experts())[None], 0), axis=1).reshape(2 * nb)
    n_active = (pad_end[-1] // bm).astype(I32).reshape(1)
    return slot_tok, block_experts, n_active


SUBLANES = 8
SUBLANE_SHIFT = SUBLANES.bit_length() - 1


def _expert_kernel(be_ref, tok_ref, nact_ref, x_hbm, wga_ref, wua_ref, wda_ref, wgb_ref, wub_ref, wdb_ref,
                   y_hbm, xnext, xcur, ycur, yout, wga_bf, wua_bf, wda_bf, wgb_bf, wub_bf, wdb_bf, gsem, ssem,
                   *, n_tok, n_classes):
    bm = EXPERT_BM
    tiles = bm // SUBLANES
    dw = ycur.shape[-1]
    i = pl.program_id(0)
    n_blocks = pl.num_programs(0)
    nact = nact_ref[0]
    w_f32 = ((wga_ref, wua_ref, wda_ref), (wgb_ref, wub_ref, wdb_ref))
    wbf = (wga_bf, wua_bf, wda_bf, wgb_bf, wub_bf, wdb_bf)

    def row_copies(blk, gather, unrolled):
        base = blk * bm

        def body(rt, carry):
            for u in range(SUBLANES):
                row = tok_ref[base + rt * SUBLANES + u]
                if gather:
                    row = jnp.bitwise_and(row, n_tok - 1)
                    pltpu.make_async_copy(
                        x_hbm.at[lax.shift_right_logical(row, SUBLANE_SHIFT),
                                 pl.ds(jnp.bitwise_and(row, SUBLANES - 1), 1), :],
                        xnext.at[rt, pl.ds(u, 1), :], gsem.at[0]).start()
                else:
                    pltpu.make_async_copy(
                        yout.at[rt, pl.ds(u, 1), :],
                        y_hbm.at[lax.shift_right_logical(row, SUBLANE_SHIFT),
                                 pl.ds(jnp.bitwise_and(row, SUBLANES - 1), 1), :],
                        ssem.at[0]).start()
            return carry

        if unrolled is None:
            lax.fori_loop(0, tiles, body, 0)
        else:
            for rt in range(*unrolled):
                body(rt, 0)

    def wait_gather():
        pltpu.make_async_copy(x_hbm.at[pl.ds(0, tiles)], xnext, gsem.at[0]).wait()

    def wait_scatter():
        pltpu.make_async_copy(yout, y_hbm.at[pl.ds(0, tiles)], ssem.at[0]).wait()

    @pl.when(i == 0)
    def _():
        row_copies(0, True, unrolled=None)
        yout[...] = jnp.zeros(yout.shape, yout.dtype)
        ycur[...] = jnp.zeros(ycur.shape, ycur.dtype)
        spare_tile0 = n_tok // SUBLANES

        def spare_copy(k):
            return pltpu.make_async_copy(
                yout, y_hbm.at[pl.ds(spare_tile0 + k * tiles, tiles)], ssem.at[0])

        @pl.loop(0, n_classes)
        def _(k):
            spare_copy(k).start()

        @pl.loop(0, n_classes)
        def _(k):
            spare_copy(k).wait()

    for side in range(2):
        idx = 2 * i + side
        changed = jnp.logical_or(i == 0, be_ref[idx] != be_ref[jnp.maximum(idx - 2, 0)])

        @pl.when(jnp.logical_and(changed, i < nact))
        def _():
            for m in range(3):
                wbf[3 * side + m][...] = w_f32[side][m][0].astype(BF16)

    @pl.when(i < nact)
    def _():
        wait_gather()
        xcur[...] = xnext[...]

        @pl.when(i >= 1)
        def _():
            wait_scatter()

        yout[...] = ycur[...]

    @pl.when(nact - i >= 1)
    def _():
        row_copies(jnp.minimum(i + 1, n_blocks - 1), True, unrolled=(0, tiles))
        row_copies(jnp.maximum(i - 1, 0), False, unrolled=(0, tiles))
        rows = xcur[...].reshape(bm, dw + LANES)
        x = _unpack_bf16_pairs(rows[:, :dw]).astype(BF16)
        gates = lax.bitcast_convert_type(rows[:, dw:], F32)
        y = None
        for side in range(2):
            wg_bf, wu_bf, wd_bf = wbf[3 * side], wbf[3 * side + 1], wbf[3 * side + 2]
            gate = jnp.dot(x, wg_bf[...], preferred_element_type=F32)
            up = jnp.dot(x, wu_bf[...], preferred_element_type=F32)
            h = (gate / (1.0 + jnp.exp(-gate))) * up
            ys = jnp.dot(h.astype(BF16), wd_bf[...], preferred_element_type=F32) * gates[:, side:side + 1]
            y = ys if y is None else y + ys
        ycur[...] = _pack_bf16_pairs(y).reshape(tiles, SUBLANES, dw)

    @pl.when(i == nact)
    def _():
        wait_gather()
        wait_scatter()
        yout[...] = ycur[...]
        row_copies(i - 1, False, unrolled=None)
        wait_scatter()


def _experts(x_pk, slot_tok, block_experts, n_active, w_gate, w_up, w_down, layer):
    t, dw_in = x_pk.shape
    _, _, d, f = w_gate.shape
    dw = d // 2
    assert dw_in == dw + LANES
    bm = EXPERT_BM
    n_blocks = slot_tok.shape[0] // bm
    n_rows = t + N_CLASSES * bm
    assert t & (t - 1) == 0 and t % SUBLANES == 0 and bm % SUBLANES == 0
    est = 2 * 6 * d * f * 4 + 6 * d * f * 2 + 4 * bm * d * 2 + 10 * bm * d * 4

    def w_spec(shape, side):
        return pl.BlockSpec((None, 1) + shape, lambda i, be, tk, na: (layer, be[2 * i + side], 0, 0))

    xbuf = pltpu.VMEM((bm // SUBLANES, SUBLANES, dw_in), jnp.uint32)
    ybuf = pltpu.VMEM((bm // SUBLANES, SUBLANES, dw), jnp.uint32)
    grid_spec = pltpu.PrefetchScalarGridSpec(
        num_scalar_prefetch=3,
        grid=(n_blocks,),
        in_specs=[pl.BlockSpec(memory_space=pl.ANY),
                  w_spec((d, f), 0), w_spec((d, f), 0), w_spec((f, d), 0),
                  w_spec((d, f), 1), w_spec((d, f), 1), w_spec((f, d), 1)],
        out_specs=pl.BlockSpec(memory_space=pl.ANY),
        scratch_shapes=[xbuf, xbuf, ybuf, ybuf]
                       + [pltpu.VMEM((d, f), BF16), pltpu.VMEM((d, f), BF16), pltpu.VMEM((f, d), BF16)] * 2
                       + [pltpu.SemaphoreType.DMA((1,)), pltpu.SemaphoreType.DMA((1,))],
    )
    y = pl.pallas_call(
        functools.partial(_expert_kernel, n_tok=t, n_classes=N_CLASSES),
        grid_spec=grid_spec,
        out_shape=jax.ShapeDtypeStruct((n_rows // SUBLANES, SUBLANES, dw), jnp.uint32),
        compiler_params=pltpu.CompilerParams(
            dimension_semantics=("arbitrary",), vmem_limit_bytes=_vmem_limit(est)),
        name="experts",
    )(block_experts, slot_tok, n_active, x_pk.reshape(t // SUBLANES, SUBLANES, dw_in),
      w_gate, w_up, w_down, w_gate, w_up, w_down)
    return y.reshape(n_rows, dw)


def _combine_kernel(x_ref, y_ref, g_ref, b_ref, o_ref):
    z = DEEPNORM_ALPHA * x_ref[...] + _unpack_bf16_pairs(y_ref[...])
    o_ref[...] = _layer_norm_rows(z, g_ref[...], b_ref[...])


def _combine_ln(x, y_tok, ln_g, ln_b):
    t, d = x.shape
    bt = COMBINE_BT
    return pl.pallas_call(
        _combine_kernel,
        grid=(t // bt,),
        in_specs=[pl.BlockSpec((bt, d), lambda i: (i, 0)),
                  pl.BlockSpec((bt, d // 2), lambda i: (i, 0)),
                  pl.BlockSpec((1, d), lambda i: (0, 0)),
                  pl.BlockSpec((1, d), lambda i: (0, 0))],
        out_specs=pl.BlockSpec((bt, d), lambda i: (i, 0)),
        out_shape=jax.ShapeDtypeStruct((t, d), F32),
        compiler_params=pltpu.CompilerParams(
            dimension_semantics=("parallel",), vmem_limit_bytes=_vmem_limit(10 * bt * d * 4)),
        name="combine_ln",
    )(x, y_tok, ln_g.reshape(1, d), ln_b.reshape(1, d))


def _moe_residual_ln(x, x_pk, cls, w_gate, w_up, w_down, layer, ln_g, ln_b):
    slot_tok, block_experts, n_active = _route_slots(cls, EXPERT_BM)
    y_tok = _experts(x_pk, slot_tok, block_experts, n_active, w_gate, w_up, w_down, layer)
    return _combine_ln(x, y_tok, ln_g, ln_b)


def _na_bias_pairs(rpb):
    w = GRID_W
    kc = min(WIN_COLS, w)
    cols = np.arange(w)
    col_start = np.clip(cols - kc // 2, 0, w - kc)
    col_mask = (cols[None, :] >= col_start[:, None]) & (cols[None, :] < col_start[:, None] + kc)
    col_idx = np.clip(cols[None, :] - cols[:, None] + WIN_COLS - 1, 0, 2 * WIN_COLS - 2)
    n_ci = 2 * WIN_COLS - 1
    pick = (col_idx[None] == np.arange(n_ci)[:, None, None]).astype(np.float32)
    tb = jnp.einsum("hrc,cqk->hrqk", rpb.astype(F32), jnp.asarray(pick), precision=lax.Precision.HIGHEST)
    tb = jnp.where(col_mask[None, None], tb, NEG_BIAS)
    neg = jnp.full((N_HEADS, 1, w, w), NEG_BIAS, F32)
    ext = jnp.concatenate([neg, tb, neg], axis=1)
    return jnp.concatenate([ext[:, :-1], ext[:, 1:]], axis=-1)


def _na_tile_plan(rows):
    kr = min(WIN_ROWS, rows)
    n_blocks = rows // NA_RQ
    plan = []
    for blk in (0, 1, n_blocks - 1):
        ks = int(np.clip(blk - 1, 0, n_blocks - 3)) * NA_RQ
        per_rq = []
        for rq in range(NA_RQ):
            r = blk * NA_RQ + rq
            rs = int(np.clip(r - kr // 2, 0, rows - kr))
            ri = [ks + j - r + WIN_ROWS - 1 if rs <= ks + j < rs + kr else None for j in range(NA_KROWS)]
            tiles = []
            for jp in range(NA_KROWS // 2):
                a, b = ri[2 * jp], ri[2 * jp + 1]
                if a is not None and b is not None:
                    tiles.append(("both", b))
                elif b is not None:
                    tiles.append(("hi", b))
                elif a is not None:
                    tiles.append(("lo", a + 1))
                else:
                    tiles.append(("none", 0))
            per_rq.append(tiles)
        plan.append(per_rq)
    return plan


def _na_kernel(q_ref, k0_ref, k1_ref, k2_ref, v0_ref, v1_ref, v2_ref, pair_ref, o_ref, bias_ref, *, plan):
    dh = HEAD_DIM
    w = GRID_W
    nt = (((1,), (1,)), ((), ()))
    scale = HEAD_DIM ** -0.5
    i = pl.program_id(2)
    n_blocks = pl.num_programs(2)

    def rebuild(per_rq):
        upper = lax.broadcasted_iota(I32, (w, 2 * w), 1) >= w
        for rq, tiles in enumerate(per_rq):
            for jp, (kind, k) in enumerate(tiles):
                for h in range(NA_HEADS_PER_STEP):
                    if kind == "none":
                        tile = jnp.full((w, 2 * w), NEG_BIAS, F32)
                    elif kind == "both":
                        tile = pair_ref[h, k]
                    elif kind == "hi":
                        tile = jnp.where(upper, pair_ref[h, k], NEG_BIAS)
                    else:
                        tile = jnp.where(upper, NEG_BIAS, pair_ref[h, k])
                    bias_ref[h, rq * w:(rq + 1) * w, jp * 2 * w:(jp + 1) * 2 * w] = tile

    for ty, first_step in enumerate((0, 1, n_blocks - 1)):
        @pl.when(i == first_step)
        def _():
            rebuild(plan[ty])

    assert 2 * dh == LANES
    tq = q_ref.shape[1]
    low = lax.broadcasted_iota(I32, (tq, LANES), 1) < dh
    low_k = lax.broadcasted_iota(I32, (3 * tq, LANES), 1) < dh
    outs = []
    for hp in range(NA_HEADS_PER_STEP // 2):
        cols = slice(hp * LANES, (hp + 1) * LANES)
        q2 = q_ref[0, :, cols] * scale
        k2 = jnp.concatenate([k0_ref[0, :, cols], k1_ref[0, :, cols], k2_ref[0, :, cols]], axis=0)
        v2 = jnp.concatenate([v0_ref[0, :, cols], v1_ref[0, :, cols], v2_ref[0, :, cols]], axis=0)
        halves = []
        for sub in range(2):
            mine, mine_k = (low, low_k) if sub == 0 else (~low, ~low_k)
            q = jnp.where(mine, q2, jnp.zeros_like(q2))
            v = jnp.where(mine_k, v2, jnp.ones_like(v2))
            s = lax.dot_general(q, k2, nt, preferred_element_type=F32) + bias_ref[2 * hp + sub]
            m = jnp.max(s, axis=-1, keepdims=True)
            p = jnp.exp((s - m).astype(BF16))
            o = jnp.dot(p, v, preferred_element_type=F32)
            denom = o[:, dh:dh + 1] if sub == 0 else o[:, 0:1]
            halves.append(o * (1.0 / denom))
        outs.append(jnp.where(low, halves[0], halves[1]))
    o_ref[0] = jnp.concatenate(outs, axis=-1).astype(o_ref.dtype)


def _neighbourhood_attention(qkv, bias_pairs, rows):
    b, s, _ = qkv.shape
    d = D_MODEL
    w = GRID_W
    tq = NA_RQ * w
    n_blocks = rows // NA_RQ
    hps = NA_HEADS_PER_STEP
    hw = hps * HEAD_DIM
    n_hh = d // hw
    assert NA_KROWS * w == 3 * tq and n_blocks >= 3 and NA_KROWS % 2 == 0

    def kv_map(part, j):
        def index_map(hh, bi, i):
            return (bi, jnp.clip(i - 1, 0, n_blocks - 3) + j, part * n_hh + hh)
        return index_map

    blk = (1, tq, hw)
    pair_blk = (hps,) + bias_pairs.shape[1:]
    est = (2 * (7 * tq * hw * 2 + tq * hw * 2) + 2 * 4 * int(np.prod(pair_blk))
           + hps * tq * 3 * tq * 4 + 8 * tq * 3 * tq * 4)
    return pl.pallas_call(
        functools.partial(_na_kernel, plan=_na_tile_plan(rows)),
        grid=(n_hh, b, n_blocks),
        in_specs=[pl.BlockSpec(blk, lambda hh, bi, i: (bi, i, hh))]
                 + [pl.BlockSpec(blk, kv_map(1, j)) for j in range(3)]
                 + [pl.BlockSpec(blk, kv_map(2, j)) for j in range(3)]
                 + [pl.BlockSpec(pair_blk, lambda hh, bi, i: (hh, 0, 0, 0))],
        out_specs=pl.BlockSpec(blk, lambda hh, bi, i: (bi, i, hh)),
        out_shape=jax.ShapeDtypeStruct((b, s, d), BF16),
        scratch_shapes=[pltpu.VMEM((hps, tq, 3 * tq), F32)],
        compiler_params=pltpu.CompilerParams(
            dimension_semantics=("arbitrary", "arbitrary", "arbitrary"), vmem_limit_bytes=_vmem_limit(est)),
        name="neighbourhood_attention",
    )(qkv, qkv, qkv, qkv, qkv, qkv, qkv, bias_pairs)


def kernel(x, fourier_w_in, fourier_w_out, na_w_qkv, na_rpb, na_w_out, router_w, router_b,
           expert_w_gate, expert_w_up, expert_w_down, ln_g, ln_b):
    b, s, d = x.shape
    t = b * s
    rows = s // GRID_W
    w_router_t = router_w.T
    xt = x.reshape(t, d)

    f = _fourier_mixer_pre_out(x, fourier_w_in[0])
    xt, xt_pk, cls = _proj_residual_ln(f.reshape(t, d), fourier_w_out[0].astype(BF16), xt,
                                       ln_g[0, 0], ln_b[0, 0], w_router_t, router_b)
    xt = _moe_residual_ln(xt, xt_pk, cls, expert_w_gate, expert_w_up, expert_w_down, 0,
                          ln_g[0, 1], ln_b[0, 1])

    qkv = _matmul(xt, na_w_qkv[0].astype(BF16), BF16, MM_BM, 3 * d // 2).reshape(b, s, 3 * d)
    bias = _na_bias_pairs(na_rpb[0])
    o = _neighbourhood_attention(qkv, bias, rows)
    xt, xt_pk, cls = _proj_residual_ln(o.reshape(t, d), na_w_out[0].astype(BF16), xt,
                                       ln_g[1, 0], ln_b[1, 0], w_router_t, router_b)
    xt = _moe_residual_ln(xt, xt_pk, cls, expert_w_gate, expert_w_up, expert_w_down, 1,
                          ln_g[1, 1], ln_b[1, 1])
    return xt.reshape(b, s, d)
```

```python
import functools

import jax
import jax.numpy as jnp
import numpy as np
from jax import lax
from jax.experimental import pallas as pl
from jax.experimental.pallas import tpu as pltpu

F32 = jnp.float32
BF16 = jnp.bfloat16
I32 = jnp.int32

D_MODEL = 1024
GRID_W = 64
N_FOURIER_GROUPS = 4
FOURIER_GROUP_DIM = D_MODEL // N_FOURIER_GROUPS
N_HEADS = 16
HEAD_DIM = D_MODEL // N_HEADS
WIN_ROWS = 8
WIN_COLS = 16
N_EXPERTS = 32
N_GROUPS = 4
EXPERTS_PER_GROUP = N_EXPERTS // N_GROUPS
D_EXPERT = D_MODEL // 2
PAIRS_PER_GROUP = EXPERTS_PER_GROUP * (EXPERTS_PER_GROUP - 1) // 2
N_CLASSES = N_GROUPS * PAIRS_PER_GROUP
DEPTH = 2
DEEPNORM_ALPHA = (2 * DEPTH) ** 0.25
LN_EPS = 1e-5

V7X_VMEM_BYTES = 64 * 1024 * 1024
LANES = 128

FFT_N1 = 64
FFT_N2 = 128
FFT_CHUNK = 128

MM_BM = 1024
LN_BM = 512
EXPERT_BM = 384
COMBINE_BT = 1024
NA_RQ = 4
NA_KROWS = NA_RQ + WIN_ROWS
NA_HEADS_PER_STEP = 16
NEG_BIAS = -1e30


def _vmem_limit(nbytes):
    return int(min(max(nbytes, 32 * 1024 * 1024), V7X_VMEM_BYTES - 8 * 1024 * 1024))


def _mm_kernel(a_ref, b_ref, o_ref, *, precision):
    if precision is None:
        a = a_ref[...].astype(BF16)
        b = b_ref[...].astype(BF16)
        acc = jnp.dot(a, b, preferred_element_type=F32)
    else:
        acc = jnp.dot(a_ref[...], b_ref[...], preferred_element_type=F32, precision=precision)
    o_ref[...] = acc.astype(o_ref.dtype)


def _matmul(a, b, out_dtype, bm, bn, precision=None):
    m, k = a.shape
    _, n = b.shape
    est = 2 * (bm * k * a.dtype.itemsize + k * bn * b.dtype.itemsize + bm * bn * 4) + 3 * bm * bn * 4
    return pl.pallas_call(
        functools.partial(_mm_kernel, precision=precision),
        grid=(m // bm, n // bn),
        in_specs=[pl.BlockSpec((bm, k), lambda i, j: (i, 0)),
                  pl.BlockSpec((k, bn), lambda i, j: (0, j))],
        out_specs=pl.BlockSpec((bm, bn), lambda i, j: (i, j)),
        out_shape=jax.ShapeDtypeStruct((m, n), out_dtype),
        compiler_params=pltpu.CompilerParams(
            dimension_semantics=("parallel", "parallel"), vmem_limit_bytes=_vmem_limit(est)),
        name="matmul",
    )(a, b)


def _fourier_channel_tables():
    n = FOURIER_GROUP_DIM
    c = np.arange(n)
    ang = 2.0 * np.pi * ((c[:, None] * c[None, :]) % n) / n
    cos, sin = np.cos(ang) / np.sqrt(n), np.sin(ang) / np.sqrt(n)
    half = FFT_CHUNK
    tabs = [np.concatenate([cos[:, h * half:(h + 1) * half], sin[:, h * half:(h + 1) * half]], axis=1)
            for h in range(n // half)]
    return np.stack(tabs).astype(np.float32)


def _fourier_seq_tables(seq):
    n1, n2 = FFT_N1, FFT_N2
    assert n1 * n2 == seq
    k2 = np.arange(n2)
    s2 = np.arange(n2)
    m1 = np.empty((n1, 2 * n2, 2 * n2), np.float32)
    for s1 in range(n1):
        ang = 2.0 * np.pi * ((k2[:, None] * (s1 + n1 * s2[None, :])) % seq) / seq
        mr, mi = np.cos(ang) / np.sqrt(n2), np.sin(ang) / np.sqrt(n2)
        m1[s1] = np.block([[mr, -mi], [mi, mr]])
    k1 = np.arange(n1)
    ang = 2.0 * np.pi * ((k1[:, None] * k1[None, :]) % n1) / n1
    w2 = np.concatenate([np.cos(ang), -np.sin(ang)], axis=1) / np.sqrt(n1)
    return m1, w2.astype(np.float32)


def _fft_kernel(a_ref, b_ref, m1_ref, w2_ref, o_ref, zs_ref):
    n1, n2, c = FFT_N1, FFT_N2, FFT_CHUNK

    def stage1(s1, carry):
        rows = pl.ds(s1, n2, stride=n1)
        x = jnp.concatenate([a_ref[0, rows, :], b_ref[0, rows, :]], axis=0).astype(BF16)
        z = jnp.dot(m1_ref[s1], x, preferred_element_type=F32)
        zs_ref[pl.ds(pl.multiple_of(s1 * 2 * n2, 2 * n2), 2 * n2), :] = z
        return carry

    lax.fori_loop(0, n1, stage1, 0, unroll=16)

    def stage2(kk, carry):
        k2 = 2 * kk
        parts = []
        for d in range(2):
            zr = zs_ref[pl.ds(k2 + d, n1, stride=2 * n2), :]
            zi = zs_ref[pl.ds(n2 + k2 + d, n1, stride=2 * n2), :]
            parts.append(jnp.concatenate([zr, zi], axis=0))
        z = jnp.concatenate(parts, axis=1).astype(BF16)
        y = jnp.dot(w2_ref[...], z, preferred_element_type=F32)
        o_ref[0, pl.ds(k2, n1, stride=n2), :] = y[:, :c]
        o_ref[0, pl.ds(k2 + 1, n1, stride=n2), :] = y[:, c:]
        return carry

    lax.fori_loop(0, n2 // 2, stage2, 0, unroll=16)


def _seq_fft(ab, m1, w2):
    b, s, two_d = ab.shape
    d = two_d // 2
    c = FFT_CHUNK
    est = 2 * (s * 2 * c * 4 + m1.size * 2 + s * c * 4) + FFT_N1 * 2 * FFT_N2 * c * 4 + (4 << 20)
    return pl.pallas_call(
        _fft_kernel,
        grid=(b, d // c),
        in_specs=[pl.BlockSpec((1, s, c), lambda i, j: (i, 0, 2 * j)),
                  pl.BlockSpec((1, s, c), lambda i, j: (i, 0, 2 * j + 1)),
                  pl.BlockSpec(m1.shape, lambda i, j: (0, 0, 0)),
                  pl.BlockSpec(w2.shape, lambda i, j: (0, 0))],
        out_specs=pl.BlockSpec((1, s, c), lambda i, j: (i, 0, j)),
        out_shape=jax.ShapeDtypeStruct((b, s, d), F32),
        scratch_shapes=[pltpu.VMEM((FFT_N1 * 2 * FFT_N2, c), F32)],
        compiler_params=pltpu.CompilerParams(
            dimension_semantics=("parallel", "parallel"), vmem_limit_bytes=_vmem_limit(est)),
        name="seq_fft",
    )(ab, ab, m1, w2)


def _fourier_mixer_pre_out(x, w_in):
    b, s, d = x.shape
    g, gd, c = N_FOURIER_GROUPS, FOURIER_GROUP_DIM, FFT_CHUNK
    halves = gd // c
    cs = jnp.asarray(_fourier_channel_tables())
    w_ab = pl.pallas_call(
        functools.partial(_mm_kernel, precision=lax.Precision.HIGHEST),
        grid=(g, halves),
        in_specs=[pl.BlockSpec((d, gd), lambda i, h: (0, i)),
                  pl.BlockSpec((None, gd, 2 * c), lambda i, h: (h, 0, 0))],
        out_specs=pl.BlockSpec((d, 2 * c), lambda i, h: (0, i * halves + h)),
        out_shape=jax.ShapeDtypeStruct((d, 2 * d), BF16),
        name="fourier_weight_fold",
    )(w_in, cs)
    ab = _matmul(x.reshape(b * s, d), w_ab, F32, MM_BM, 2 * d).reshape(b, s, 2 * d)
    m1, w2 = _fourier_seq_tables(s)
    return _seq_fft(ab, jnp.asarray(m1, dtype=BF16), jnp.asarray(w2, dtype=BF16))


def _layer_norm_rows(z, g, b):
    mu = jnp.mean(z, axis=-1, keepdims=True)
    zc = z - mu
    var = jnp.mean(zc * zc, axis=-1, keepdims=True)
    return zc * lax.rsqrt(var + LN_EPS) * g + b


def _pack_bf16_pairs(z):
    m = z.shape[1] // 2
    zb = z.astype(BF16).astype(F32)
    bits = lax.bitcast_convert_type(zb, jnp.uint32)
    return jnp.bitwise_or(bits[:, m:], lax.shift_right_logical(bits[:, :m], jnp.uint32(16)))


def _unpack_bf16_pairs(w):
    lo = lax.bitcast_convert_type(lax.shift_left(w, jnp.uint32(16)), F32)
    hi = lax.bitcast_convert_type(jnp.bitwise_and(w, jnp.uint32(0xFFFF0000)), F32)
    return jnp.concatenate([lo, hi], axis=1)


def _mm_ln_kernel(a_ref, w_ref, x_ref, g_ref, b_ref, wr_ref, br_ref, o_ref, opk_ref, cls_ref):
    y = jnp.dot(a_ref[...].astype(BF16), w_ref[...], preferred_element_type=F32)
    z = DEEPNORM_ALPHA * x_ref[...] + y
    out = _layer_norm_rows(z, g_ref[...], b_ref[...])
    o_ref[...] = out
    cls, gates = _route_rows(out, wr_ref[...], br_ref[...])
    cls_ref[...] = cls
    bm, dw = out.shape[0], out.shape[1] // 2
    on_diag = lax.broadcasted_iota(I32, (LANES, LANES), 0) == lax.broadcasted_iota(I32, (LANES, LANES), 1)
    lane = lax.broadcasted_iota(I32, (LANES, LANES), 1)
    tiles = []
    for c in range(bm // LANES):
        tile = jnp.zeros((LANES, LANES), F32)
        for k in range(2):
            g = gates[k:k + 1, c * LANES:(c + 1) * LANES]
            col = jnp.sum(jnp.where(on_diag, g, 0.0), axis=1, keepdims=True)
            tile = jnp.where(lane == k, col, tile)
        tiles.append(tile)
    opk_ref[:, :dw] = _pack_bf16_pairs(out)
    opk_ref[:, dw:] = lax.bitcast_convert_type(jnp.concatenate(tiles, axis=0), jnp.uint32)


def _proj_residual_ln(a, w_bf16, x, ln_g, ln_b, w_router_t, b_router):
    t, k = a.shape
    d = w_bf16.shape[1]
    e = w_router_t.shape[0]
    bm = LN_BM
    est = 2 * (bm * k * a.dtype.itemsize + k * d * 2 + 3 * bm * d * 4) + 8 * bm * d * 4
    return pl.pallas_call(
        _mm_ln_kernel,
        grid=(t // bm,),
        in_specs=[pl.BlockSpec((bm, k), lambda i: (i, 0)),
                  pl.BlockSpec((k, d), lambda i: (0, 0)),
                  pl.BlockSpec((bm, d), lambda i: (i, 0)),
                  pl.BlockSpec((1, d), lambda i: (0, 0)),
                  pl.BlockSpec((1, d), lambda i: (0, 0)),
                  pl.BlockSpec((e, d), lambda i: (0, 0)),
                  pl.BlockSpec((e, 1), lambda i: (0, 0))],
        out_specs=[pl.BlockSpec((bm, d), lambda i: (i, 0)),
                   pl.BlockSpec((bm, d // 2 + LANES), lambda i: (i, 0)),
                   pl.BlockSpec((1, bm), lambda i: (0, i))],
        out_shape=[jax.ShapeDtypeStruct((t, d), F32), jax.ShapeDtypeStruct((t, d // 2 + LANES), jnp.uint32),
                   jax.ShapeDtypeStruct((1, t), I32)],
        compiler_params=pltpu.CompilerParams(
            dimension_semantics=("parallel",), vmem_limit_bytes=_vmem_limit(est)),
        name="proj_residual_ln",
    )(a, w_bf16, x, ln_g.reshape(1, d), ln_b.reshape(1, d), w_router_t, b_router.reshape(e, 1))


def _top2_rows(v, iota):
    n_rows = v.shape[0]
    m1 = jnp.max(v, axis=0, keepdims=True)
    i1 = jnp.min(jnp.where(v == m1, iota, n_rows), axis=0, keepdims=True)
    v2 = jnp.where(iota == i1, -jnp.inf, v)
    m2 = jnp.max(v2, axis=0, keepdims=True)
    i2 = jnp.min(jnp.where(v2 == m2, iota, n_rows), axis=0, keepdims=True)
    return m1, i1, m2, i2


def _route_rows(x, w, b):
    xh = x.astype(BF16)
    xl = (x - xh.astype(F32)).astype(BF16)
    wh = w.astype(BF16)
    wl = (w - wh.astype(F32)).astype(BF16)
    nt = (((1,), (1,)), ((), ()))
    logits = (lax.dot_general(wh, xh, nt, preferred_element_type=F32)
              + lax.dot_general(wh, xl, nt, preferred_element_type=F32)
              + lax.dot_general(wl, xh, nt, preferred_element_type=F32))
    scores = 1.0 / (1.0 + jnp.exp(-logits))
    sel = scores + b
    epg = EXPERTS_PER_GROUP
    bt = x.shape[0]
    iota = lax.broadcasted_iota(I32, (epg, bt), 0)

    best = None
    for g in range(N_GROUPS):
        m1, _, m2, _ = _top2_rows(sel[g * epg:(g + 1) * epg], iota)
        gs = m1 + m2
        if best is None:
            best, gidx = gs, jnp.zeros((1, bt), I32)
        else:
            better = gs > best
            gidx = jnp.where(better, g, gidx)
            best = jnp.where(better, gs, best)

    sel_in = sel[0:epg]
    sc_in = scores[0:epg]
    for g in range(1, N_GROUPS):
        pick = gidx == g
        sel_in = jnp.where(pick, sel[g * epg:(g + 1) * epg], sel_in)
        sc_in = jnp.where(pick, scores[g * epg:(g + 1) * epg], sc_in)
    _, i1, _, i2 = _top2_rows(sel_in, iota)
    g1 = jnp.sum(jnp.where(iota == i1, sc_in, 0.0), axis=0, keepdims=True)
    g2 = jnp.sum(jnp.where(iota == i2, sc_in, 0.0), axis=0, keepdims=True)
    denom = g1 + g2
    first_lo = i1 < i2
    lo = jnp.where(first_lo, i1, i2)
    hi = jnp.where(first_lo, i2, i1)
    pair = lax.shift_right_logical(lo * (2 * epg - 1 - lo), 1) + (hi - lo - 1)
    cls = gidx * PAIRS_PER_GROUP + pair
    gates = jnp.concatenate([jnp.where(first_lo, g1, g2) / denom, jnp.where(first_lo, g2, g1) / denom], axis=0)
    return cls, gates


RANK_SUB = 256
RANK_NSUB = 8


def _rank_kernel(keys_ref, tri_ref, rank_ref, counts_ref, carry_ref, *, n_classes):
    nsub, sub = RANK_NSUB, RANK_SUB

    @pl.when(pl.program_id(0) == 0)
    def _():
        carry_ref[...] = jnp.zeros_like(carry_ref)

    cls = lax.broadcasted_iota(I32, (nsub, n_classes, sub), 1)
    onehot = cls == keys_ref[...]
    oh = jnp.where(onehot, 1.0, 0.0).reshape(nsub * n_classes, sub).astype(BF16)
    pref = jnp.dot(oh, tri_ref[...], preferred_element_type=F32).reshape(nsub, n_classes, sub)
    carry = carry_ref[...]
    for j in range(nsub):
        before = pref[j] + (carry - 1.0)
        rank_ref[j] = jnp.sum(jnp.where(onehot[j], before, 0.0), axis=0, keepdims=True).astype(I32)
        carry = carry + pref[j][:, sub - 1:sub]
    carry_ref[...] = carry
    counts_ref[...] = carry.astype(I32)


def _rank_within_class(keys, n_classes):
    n = keys.shape[0]
    nsub, sub = RANK_NSUB, RANK_SUB
    tri = jnp.asarray(np.triu(np.ones((sub, sub), np.float32)), dtype=BF16)
    rank, counts = pl.pallas_call(
        functools.partial(_rank_kernel, n_classes=n_classes),
        grid=(n // (nsub * sub),),
        in_specs=[pl.BlockSpec((nsub, 1, sub), lambda i: (i, 0, 0)),
                  pl.BlockSpec((sub, sub), lambda i: (0, 0))],
        out_specs=[pl.BlockSpec((nsub, 1, sub), lambda i: (i, 0, 0)),
                   pl.BlockSpec((n_classes, 1), lambda i: (0, 0))],
        out_shape=[jax.ShapeDtypeStruct((n // sub, 1, sub), I32),
                   jax.ShapeDtypeStruct((n_classes, 1), I32)],
        scratch_shapes=[pltpu.VMEM((n_classes, 1), F32)],
        compiler_params=pltpu.CompilerParams(dimension_semantics=("arbitrary",)),
        name="rank_within_class",
    )(keys.reshape(n // sub, 1, sub), tri)
    return rank.reshape(n), counts.reshape(n_classes)


FILL_CHUNK = 4096


def _fill_slots_kernel(dest_ref, default_hbm, slot_ref, sem):
    step = pl.program_id(0)

    @pl.when(step == 0)
    def _():
        copy = pltpu.make_async_copy(default_hbm, slot_ref, sem.at[0])
        copy.start()
        copy.wait()

    base = step * FILL_CHUNK

    def place(j, carry):
        slot_ref[dest_ref[j]] = base + j
        return carry

    lax.fori_loop(0, FILL_CHUNK, place, 0, unroll=32)


def _fill_slots(dest, default):
    n_items = dest.shape[0]
    assert n_items % FILL_CHUNK == 0
    return pl.pallas_call(
        _fill_slots_kernel,
        grid=(n_items // FILL_CHUNK,),
        in_specs=[pl.BlockSpec((FILL_CHUNK,), lambda i: (i,), memory_space=pltpu.SMEM),
                  pl.BlockSpec(memory_space=pl.ANY)],
        out_specs=pl.BlockSpec(memory_space=pltpu.SMEM),
        out_shape=jax.ShapeDtypeStruct(default.shape, I32),
        scratch_shapes=[pltpu.SemaphoreType.DMA((1,))],
        compiler_params=pltpu.CompilerParams(dimension_semantics=("arbitrary",)),
        name="fill_slots",
    )(dest, default)


def _class_experts():
    epg = EXPERTS_PER_GROUP
    pairs = [(lo, hi) for lo in range(epg) for hi in range(lo + 1, epg)]
    return np.array([[g * epg + lo, g * epg + hi] for g in range(N_GROUPS) for lo, hi in pairs], np.int32)


def _route_slots(cls, bm):
    t = cls.shape[1]
    keys = cls.reshape(t)
    rank, counts = _rank_within_class(keys, N_CLASSES)
    padded = (counts + bm - 1) // bm * bm
    pad_end = jnp.cumsum(padded)
    pad_start = pad_end - padded
    class_ids = jnp.arange(N_CLASSES, dtype=I32)
    dest = jnp.sum(jnp.where(keys[:, None] == class_ids[None, :], pad_start[None, :], 0), axis=1) + rank
    nb = t // bm + N_CLASSES + 1
    block_start = jnp.arange(nb, dtype=I32) * bm
    block_class = jnp.minimum(
        jnp.sum((pad_end[None, :] <= block_start[:, None]).astype(I32), axis=1), N_CLASSES - 1)
    row = jnp.arange(bm, dtype=I32)[None, :]
    in_use = (block_start < pad_end[-1])[:, None]
    default = jnp.where(in_use, t + block_class[:, None] * bm + row, row).reshape(nb * bm)
    slot_tok = _fill_slots(dest, default)
    in_class = block_class[:, None, None] == class_ids[None, :, None]
    block_experts = jnp.sum(jnp.where(in_class, jnp.asarray(_class_experts())[None], 0), axis=1).reshape(2 * nb)
    n_active = (pad_end[-1] // bm).astype(I32).reshape(1)
    return slot_tok, block_experts, n_active


SUBLANES = 8
SUBLANE_SHIFT = SUBLANES.bit_length() - 1


def _expert_kernel(be_ref, tok_ref, nact_ref, x_hbm, wga_ref, wua_ref, wda_ref, wgb_ref, wub_ref, wdb_ref,
                   y_hbm, xnext, xcur, ycur, yout, wga_bf, wua_bf, wda_bf, wgb_bf, wub_bf, wdb_bf, gsem, ssem,
                   *, n_tok, n_classes):
    bm = EXPERT_BM
    tiles = bm // SUBLANES
    dw = ycur.shape[-1]
    i = pl.program_id(0)
    n_blocks = pl.num_programs(0)
    nact = nact_ref[0]
    w_f32 = ((wga_ref, wua_ref, wda_ref), (wgb_ref, wub_ref, wdb_ref))
    wbf = (wga_bf, wua_bf, wda_bf, wgb_bf, wub_bf, wdb_bf)

    def row_copies(blk, gather, unrolled):
        base = blk * bm

        def body(rt, carry):
            for u in range(SUBLANES):
                row = tok_ref[base + rt * SUBLANES + u]
                if gather:
                    row = jnp.bitwise_and(row, n_tok - 1)
                    pltpu.make_async_copy(
                        x_hbm.at[lax.shift_right_logical(row, SUBLANE_SHIFT),
                                 pl.ds(jnp.bitwise_and(row, SUBLANES - 1), 1), :],
                        xnext.at[rt, pl.ds(u, 1), :], gsem.at[0]).start()
                else:
                    pltpu.make_async_copy(
                        yout.at[rt, pl.ds(u, 1), :],
                        y_hbm.at[lax.shift_right_logical(row, SUBLANE_SHIFT),
                                 pl.ds(jnp.bitwise_and(row, SUBLANES - 1), 1), :],
                        ssem.at[0]).start()
            return carry

        if unrolled is None:
            lax.fori_loop(0, tiles, body, 0)
        else:
            for rt in range(*unrolled):
                body(rt, 0)

    def wait_gather():
        pltpu.make_async_copy(x_hbm.at[pl.ds(0, tiles)], xnext, gsem.at[0]).wait()

    def wait_scatter():
        pltpu.make_async_copy(yout, y_hbm.at[pl.ds(0, tiles)], ssem.at[0]).wait()

    @pl.when(i == 0)
    def _():
        row_copies(0, True, unrolled=None)
        yout[...] = jnp.zeros(yout.shape, yout.dtype)
        ycur[...] = jnp.zeros(ycur.shape, ycur.dtype)
        spare_tile0 = n_tok // SUBLANES

        def spare_copy(k):
            return pltpu.make_async_copy(
                yout, y_hbm.at[pl.ds(spare_tile0 + k * tiles, tiles)], ssem.at[0])

        @pl.loop(0, n_classes)
        def _(k):
            spare_copy(k).start()

        @pl.loop(0, n_classes)
        def _(k):
            spare_copy(k).wait()

    for side in range(2):
        idx = 2 * i + side
        changed = jnp.logical_or(i == 0, be_ref[idx] != be_ref[jnp.maximum(idx - 2, 0)])

        @pl.when(jnp.logical_and(changed, i < nact))
        def _():
            for m in range(3):
                wbf[3 * side + m][...] = w_f32[side][m][0].astype(BF16)

    @pl.when(i < nact)
    def _():
        wait_gather()
        xcur[...] = xnext[...]

        @pl.when(i >= 1)
        def _():
            wait_scatter()

        yout[...] = ycur[...]

    @pl.when(nact - i >= 1)
    def _():
        row_copies(jnp.minimum(i + 1, n_blocks - 1), True, unrolled=(0, tiles))
        row_copies(jnp.maximum(i - 1, 0), False, unrolled=(0, tiles))
        rows = xcur[...].reshape(bm, dw + LANES)
        x = _unpack_bf16_pairs(rows[:, :dw]).astype(BF16)
        gates = lax.bitcast_convert_type(rows[:, dw:], F32)
        y = None
        for side in range(2):
            wg_bf, wu_bf, wd_bf = wbf[3 * side], wbf[3 * side + 1], wbf[3 * side + 2]
            gate = jnp.dot(x, wg_bf[...], preferred_element_type=F32)
            up = jnp.dot(x, wu_bf[...], preferred_element_type=F32)
            h = (gate / (1.0 + jnp.exp(-gate))) * up
            ys = jnp.dot(h.astype(BF16), wd_bf[...], preferred_element_type=F32) * gates[:, side:side + 1]
            y = ys if y is None else y + ys
        ycur[...] = _pack_bf16_pairs(y).reshape(tiles, SUBLANES, dw)

    @pl.when(i == nact)
    def _():
        wait_gather()
        wait_scatter()
        yout[...] = ycur[...]
        row_copies(i - 1, False, unrolled=None)
        wait_scatter()


def _experts(x_pk, slot_tok, block_experts, n_active, w_gate, w_up, w_down, layer):
    t, dw_in = x_pk.shape
    _, _, d, f = w_gate.shape
    dw = d // 2
    assert dw_in == dw + LANES
    bm = EXPERT_BM
    n_blocks = slot_tok.shape[0] // bm
    n_rows = t + N_CLASSES * bm
    assert t & (t - 1) == 0 and t % SUBLANES == 0 and bm % SUBLANES == 0
    est = 2 * 6 * d * f * 4 + 6 * d * f * 2 + 4 * bm * d * 2 + 10 * bm * d * 4

    def w_spec(shape, side):
        return pl.BlockSpec((None, 1) + shape, lambda i, be, tk, na: (layer, be[2 * i + side], 0, 0))

    xbuf = pltpu.VMEM((bm // SUBLANES, SUBLANES, dw_in), jnp.uint32)
    ybuf = pltpu.VMEM((bm // SUBLANES, SUBLANES, dw), jnp.uint32)
    grid_spec = pltpu.PrefetchScalarGridSpec(
        num_scalar_prefetch=3,
        grid=(n_blocks,),
        in_specs=[pl.BlockSpec(memory_space=pl.ANY),
                  w_spec((d, f), 0), w_spec((d, f), 0), w_spec((f, d), 0),
                  w_spec((d, f), 1), w_spec((d, f), 1), w_spec((f, d), 1)],
        out_specs=pl.BlockSpec(memory_space=pl.ANY),
        scratch_shapes=[xbuf, xbuf, ybuf, ybuf]
                       + [pltpu.VMEM((d, f), BF16), pltpu.VMEM((d, f), BF16), pltpu.VMEM((f, d), BF16)] * 2
                       + [pltpu.SemaphoreType.DMA((1,)), pltpu.SemaphoreType.DMA((1,))],
    )
    y = pl.pallas_call(
        functools.partial(_expert_kernel, n_tok=t, n_classes=N_CLASSES),
        grid_spec=grid_spec,
        out_shape=jax.ShapeDtypeStruct((n_rows // SUBLANES, SUBLANES, dw), jnp.uint32),
        compiler_params=pltpu.CompilerParams(
            dimension_semantics=("arbitrary",), vmem_limit_bytes=_vmem_limit(est)),
        name="experts",
    )(block_experts, slot_tok, n_active, x_pk.reshape(t // SUBLANES, SUBLANES, dw_in),
      w_gate, w_up, w_down, w_gate, w_up, w_down)
    return y.reshape(n_rows, dw)


def _combine_kernel(x_ref, y_ref, g_ref, b_ref, o_ref):
    z = DEEPNORM_ALPHA * x_ref[...] + _unpack_bf16_pairs(y_ref[...])
    o_ref[...] = _layer_norm_rows(z, g_ref[...], b_ref[...])


def _combine_ln(x, y_tok, ln_g, ln_b):
    t, d = x.shape
    bt = COMBINE_BT
    return pl.pallas_call(
        _combine_kernel,
        grid=(t // bt,),
        in_specs=[pl.BlockSpec((bt, d), lambda i: (i, 0)),
                  pl.BlockSpec((bt, d // 2), lambda i: (i, 0)),
                  pl.BlockSpec((1, d), lambda i: (0, 0)),
                  pl.BlockSpec((1, d), lambda i: (0, 0))],
        out_specs=pl.BlockSpec((bt, d), lambda i: (i, 0)),
        out_shape=jax.ShapeDtypeStruct((t, d), F32),
        compiler_params=pltpu.CompilerParams(
            dimension_semantics=("parallel",), vmem_limit_bytes=_vmem_limit(10 * bt * d * 4)),
        name="combine_ln",
    )(x, y_tok, ln_g.reshape(1, d), ln_b.reshape(1, d))


def _moe_residual_ln(x, x_pk, cls, w_gate, w_up, w_down, layer, ln_g, ln_b):
    slot_tok, block_experts, n_active = _route_slots(cls, EXPERT_BM)
    y_tok = _experts(x_pk, slot_tok, block_experts, n_active, w_gate, w_up, w_down, layer)
    return _combine_ln(x, y_tok, ln_g, ln_b)


def _na_bias_pairs(rpb):
    w = GRID_W
    kc = min(WIN_COLS, w)
    cols = np.arange(w)
    col_start = np.clip(cols - kc // 2, 0, w - kc)
    col_mask = (cols[None, :] >= col_start[:, None]) & (cols[None, :] < col_start[:, None] + kc)
    col_idx = np.clip(cols[None, :] - cols[:, None] + WIN_COLS - 1, 0, 2 * WIN_COLS - 2)
    n_ci = 2 * WIN_COLS - 1
    pick = (col_idx[None] == np.arange(n_ci)[:, None, None]).astype(np.float32)
    tb = jnp.einsum("hrc,cqk->hrqk", rpb.astype(F32), jnp.asarray(pick), precision=lax.Precision.HIGHEST)
    tb = jnp.where(col_mask[None, None], tb, NEG_BIAS)
    neg = jnp.full((N_HEADS, 1, w, w), NEG_BIAS, F32)
    ext = jnp.concatenate([neg, tb, neg], axis=1)
    return jnp.concatenate([ext[:, :-1], ext[:, 1:]], axis=-1)


def _na_tile_plan(rows):
    kr = min(WIN_ROWS, rows)
    n_blocks = rows // NA_RQ
    plan = []
    for blk in (0, 1, n_blocks - 1):
        ks = int(np.clip(blk - 1, 0, n_blocks - 3)) * NA_RQ
        per_rq = []
        for rq in range(NA_RQ):
            r = blk * NA_RQ + rq
            rs = int(np.clip(r - kr // 2, 0, rows - kr))
            ri = [ks + j - r + WIN_ROWS - 1 if rs <= ks + j < rs + kr else None for j in range(NA_KROWS)]
            tiles = []
            for jp in range(NA_KROWS // 2):
                a, b = ri[2 * jp], ri[2 * jp + 1]
                if a is not None and b is not None:
                    tiles.append(("both", b))
                elif b is not None:
                    tiles.append(("hi", b))
                elif a is not None:
                    tiles.append(("lo", a + 1))
                else:
                    tiles.append(("none", 0))
            per_rq.append(tiles)
        plan.append(per_rq)
    return plan


def _na_kernel(q_ref, k0_ref, k1_ref, k2_ref, v0_ref, v1_ref, v2_ref, pair_ref, o_ref, bias_ref, *, plan):
    dh = HEAD_DIM
    w = GRID_W
    nt = (((1,), (1,)), ((), ()))
    scale = HEAD_DIM ** -0.5
    i = pl.program_id(2)
    n_blocks = pl.num_programs(2)

    def rebuild(per_rq):
        upper = lax.broadcasted_iota(I32, (w, 2 * w), 1) >= w
        for rq, tiles in enumerate(per_rq):
            for jp, (kind, k) in enumerate(tiles):
                for h in range(NA_HEADS_PER_STEP):
                    if kind == "none":
                        tile = jnp.full((w, 2 * w), NEG_BIAS, F32)
                    elif kind == "both":
                        tile = pair_ref[h, k]
                    elif kind == "hi":
                        tile = jnp.where(upper, pair_ref[h, k], NEG_BIAS)
                    else:
                        tile = jnp.where(upper, NEG_BIAS, pair_ref[h, k])
                    bias_ref[h, rq * w:(rq + 1) * w, jp * 2 * w:(jp + 1) * 2 * w] = tile

    for ty, first_step in enumerate((0, 1, n_blocks - 1)):
        @pl.when(i == first_step)
        def _():
            rebuild(plan[ty])

    assert 2 * dh == LANES
    tq = q_ref.shape[1]
    low = lax.broadcasted_iota(I32, (tq, LANES), 1) < dh
    low_k = lax.broadcasted_iota(I32, (3 * tq, LANES), 1) < dh
    outs = []
    for hp in range(NA_HEADS_PER_STEP // 2):
        cols = slice(hp * LANES, (hp + 1) * LANES)
        q2 = q_ref[0, :, cols] * scale
        k2 = jnp.concatenate([k0_ref[0, :, cols], k1_ref[0, :, cols], k2_ref[0, :, cols]], axis=0)
        v2 = jnp.concatenate([v0_ref[0, :, cols], v1_ref[0, :, cols], v2_ref[0, :, cols]], axis=0)
        halves = []
        for sub in range(2):
            mine, mine_k = (low, low_k) if sub == 0 else (~low, ~low_k)
            q = jnp.where(mine, q2, jnp.zeros_like(q2))
            v = jnp.where(mine_k, v2, jnp.ones_like(v2))
            s = lax.dot_general(q, k2, nt, preferred_element_type=F32) + bias_ref[2 * hp + sub]
            m = jnp.max(s, axis=-1, keepdims=True)
            p = jnp.exp((s - m).astype(BF16))
            o = jnp.dot(p, v, preferred_element_type=F32)
            denom = o[:, dh:dh + 1] if sub == 0 else o[:, 0:1]
            halves.append(o * (1.0 / denom))
        outs.append(jnp.where(low, halves[0], halves[1]))
    o_ref[0] = jnp.concatenate(outs, axis=-1).astype(o_ref.dtype)


def _neighbourhood_attention(qkv, bias_pairs, rows):
    b, s, _ = qkv.shape
    d = D_MODEL
    w = GRID_W
    tq = NA_RQ * w
    n_blocks = rows // NA_RQ
    hps = NA_HEADS_PER_STEP
    hw = hps * HEAD_DIM
    n_hh = d // hw
    assert NA_KROWS * w == 3 * tq and n_blocks >= 3 and NA_KROWS % 2 == 0

    def kv_map(part, j):
        def index_map(hh, bi, i):
            return (bi, jnp.clip(i - 1, 0, n_blocks - 3) + j, part * n_hh + hh)
        return index_map

    blk = (1, tq, hw)
    pair_blk = (hps,) + bias_pairs.shape[1:]
    est = (2 * (7 * tq * hw * 2 + tq * hw * 2) + 2 * 4 * int(np.prod(pair_blk))
           + hps * tq * 3 * tq * 4 + 8 * tq * 3 * tq * 4)
    return pl.pallas_call(
        functools.partial(_na_kernel, plan=_na_tile_plan(rows)),
        grid=(n_hh, b, n_blocks),
        in_specs=[pl.BlockSpec(blk, lambda hh, bi, i: (bi, i, hh))]
                 + [pl.BlockSpec(blk, kv_map(1, j)) for j in range(3)]
                 + [pl.BlockSpec(blk, kv_map(2, j)) for j in range(3)]
                 + [pl.BlockSpec(pair_blk, lambda hh, bi, i: (hh, 0, 0, 0))],
        out_specs=pl.BlockSpec(blk, lambda hh, bi, i: (bi, i, hh)),
        out_shape=jax.ShapeDtypeStruct((b, s, d), BF16),
        scratch_shapes=[pltpu.VMEM((hps, tq, 3 * tq), F32)],
        compiler_params=pltpu.CompilerParams(
            dimension_semantics=("arbitrary", "arbitrary", "arbitrary"), vmem_limit_bytes=_vmem_limit(est)),
        name="neighbourhood_attention",
    )(qkv, qkv, qkv, qkv, qkv, qkv, qkv, bias_pairs)


def kernel(x, fourier_w_in, fourier_w_out, na_w_qkv, na_rpb, na_w_out, router_w, router_b,
           expert_w_gate, expert_w_up, expert_w_down, ln_g, ln_b):
    b, s, d = x.shape
    t = b * s
    rows = s // GRID_W
    w_router_t = router_w.T
    xt = x.reshape(t, d)

    f = _fourier_mixer_pre_out(x, fourier_w_in[0])
    xt, xt_pk, cls = _proj_residual_ln(f.reshape(t, d), fourier_w_out[0].astype(BF16), xt,
                                       ln_g[0, 0], ln_b[0, 0], w_router_t, router_b)
    xt = _moe_residual_ln(xt, xt_pk, cls, expert_w_gate, expert_w_up, expert_w_down, 0,
                          ln_g[0, 1], ln_b[0, 1])

    qkv = _matmul(xt, na_w_qkv[0].astype(BF16), BF16, MM_BM, 3 * d // 2).reshape(b, s, 3 * d)
    bias = _na_bias_pairs(na_rpb[0])
    o = _neighbourhood_attention(qkv, bias, rows)
    xt, xt_pk, cls = _proj_residual_ln(o.reshape(t, d), na_w_out[0].astype(BF16), xt,
                                       ln_g[1, 0], ln_b[1, 0], w_router_t, router_b)
    xt = _moe_residual_ln(xt, xt_pk, cls, expert_w_gate, expert_w_up, expert_w_down, 1,
                          ln_g[1, 1], ln_b[1, 1])
    return xt.reshape(b, s, d)
```

```python
import functools

import jax
import jax.numpy as jnp
import numpy as np
from jax import lax
from jax.experimental import pallas as pl
from jax.experimental.pallas import tpu as pltpu

F32 = jnp.float32
BF16 = jnp.bfloat16
I32 = jnp.int32

D_MODEL = 1024
GRID_W = 64
N_FOURIER_GROUPS = 4
FOURIER_GROUP_DIM = D_MODEL // N_FOURIER_GROUPS
N_HEADS = 16
HEAD_DIM = D_MODEL // N_HEADS
WIN_ROWS = 8
WIN_COLS = 16
N_EXPERTS = 32
N_GROUPS = 4
EXPERTS_PER_GROUP = N_EXPERTS // N_GROUPS
D_EXPERT = D_MODEL // 2
PAIRS_PER_GROUP = EXPERTS_PER_GROUP * (EXPERTS_PER_GROUP - 1) // 2
N_CLASSES = N_GROUPS * PAIRS_PER_GROUP
DEPTH = 2
DEEPNORM_ALPHA = (2 * DEPTH) ** 0.25
LN_EPS = 1e-5

V7X_VMEM_BYTES = 64 * 1024 * 1024
LANES = 128

FFT_N1 = 64
FFT_N2 = 128
FFT_CHUNK = 128

MM_BM = 1024
LN_BM = 256
EXPERT_BM = 384
COMBINE_BT = 1024
NA_RQ = 4
NA_KROWS = NA_RQ + WIN_ROWS
NA_HEADS_PER_STEP = 16
NEG_BIAS = -1e30


def _vmem_limit(nbytes):
    return int(min(max(nbytes, 32 * 1024 * 1024), V7X_VMEM_BYTES - 8 * 1024 * 1024))


def _mm_kernel(a_ref, b_ref, o_ref, *, precision):
    if precision is None:
        a = a_ref[...].astype(BF16)
        b = b_ref[...].astype(BF16)
        acc = jnp.dot(a, b, preferred_element_type=F32)
    else:
        acc = jnp.dot(a_ref[...], b_ref[...], preferred_element_type=F32, precision=precision)
    o_ref[...] = acc.astype(o_ref.dtype)


def _matmul(a, b, out_dtype, bm, bn, precision=None):
    m, k = a.shape
    _, n = b.shape
    est = 2 * (bm * k * a.dtype.itemsize + k * bn * b.dtype.itemsize + bm * bn * 4) + 3 * bm * bn * 4
    return pl.pallas_call(
        functools.partial(_mm_kernel, precision=precision),
        grid=(m // bm, n // bn),
        in_specs=[pl.BlockSpec((bm, k), lambda i, j: (i, 0)),
                  pl.BlockSpec((k, bn), lambda i, j: (0, j))],
        out_specs=pl.BlockSpec((bm, bn), lambda i, j: (i, j)),
        out_shape=jax.ShapeDtypeStruct((m, n), out_dtype),
        compiler_params=pltpu.CompilerParams(
            dimension_semantics=("parallel", "parallel"), vmem_limit_bytes=_vmem_limit(est)),
        name="matmul",
    )(a, b)


def _fourier_channel_tables():
    n = FOURIER_GROUP_DIM
    c = np.arange(n)
    ang = 2.0 * np.pi * ((c[:, None] * c[None, :]) % n) / n
    cos, sin = np.cos(ang) / np.sqrt(n), np.sin(ang) / np.sqrt(n)
    half = FFT_CHUNK
    tabs = [np.concatenate([cos[:, h * half:(h + 1) * half], sin[:, h * half:(h + 1) * half]], axis=1)
            for h in range(n // half)]
    return np.stack(tabs).astype(np.float32)


def _fourier_seq_tables(seq):
    n1, n2 = FFT_N1, FFT_N2
    assert n1 * n2 == seq
    k2 = np.arange(n2)
    s2 = np.arange(n2)
    m1 = np.empty((n1, 2 * n2, 2 * n2), np.float32)
    for s1 in range(n1):
        ang = 2.0 * np.pi * ((k2[:, None] * (s1 + n1 * s2[None, :])) % seq) / seq
        mr, mi = np.cos(ang) / np.sqrt(n2), np.sin(ang) / np.sqrt(n2)
        m1[s1] = np.block([[mr, -mi], [mi, mr]])
    k1 = np.arange(n1)
    ang = 2.0 * np.pi * ((k1[:, None] * k1[None, :]) % n1) / n1
    w2 = np.concatenate([np.cos(ang), -np.sin(ang)], axis=1) / np.sqrt(n1)
    return m1, w2.astype(np.float32)


def _fft_kernel(a_ref, b_ref, m1_ref, w2_ref, o_ref, zs_ref):
    n1, n2, c = FFT_N1, FFT_N2, FFT_CHUNK

    def stage1(s1, carry):
        rows = pl.ds(s1, n2, stride=n1)
        x = jnp.concatenate([a_ref[0, rows, :], b_ref[0, rows, :]], axis=0).astype(BF16)
        z = jnp.dot(m1_ref[s1], x, preferred_element_type=F32)
        zs_ref[pl.ds(pl.multiple_of(s1 * 2 * n2, 2 * n2), 2 * n2), :] = z
        return carry

    lax.fori_loop(0, n1, stage1, 0, unroll=16)

    def stage2(kk, carry):
        k2 = 2 * kk
        parts = []
        for d in range(2):
            zr = zs_ref[pl.ds(k2 + d, n1, stride=2 * n2), :]
            zi = zs_ref[pl.ds(n2 + k2 + d, n1, stride=2 * n2), :]
            parts.append(jnp.concatenate([zr, zi], axis=0))
        z = jnp.concatenate(parts, axis=1).astype(BF16)
        y = jnp.dot(w2_ref[...], z, preferred_element_type=F32)
        o_ref[0, pl.ds(k2, n1, stride=n2), :] = y[:, :c]
        o_ref[0, pl.ds(k2 + 1, n1, stride=n2), :] = y[:, c:]
        return carry

    lax.fori_loop(0, n2 // 2, stage2, 0, unroll=16)


def _seq_fft(ab, m1, w2):
    b, s, two_d = ab.shape
    d = two_d // 2
    c = FFT_CHUNK
    est = 2 * (s * 2 * c * 4 + m1.size * 2 + s * c * 4) + FFT_N1 * 2 * FFT_N2 * c * 4 + (4 << 20)
    return pl.pallas_call(
        _fft_kernel,
        grid=(b, d // c),
        in_specs=[pl.BlockSpec((1, s, c), lambda i, j: (i, 0, 2 * j)),
                  pl.BlockSpec((1, s, c), lambda i, j: (i, 0, 2 * j + 1)),
                  pl.BlockSpec(m1.shape, lambda i, j: (0, 0, 0)),
                  pl.BlockSpec(w2.shape, lambda i, j: (0, 0))],
        out_specs=pl.BlockSpec((1, s, c), lambda i, j: (i, 0, j)),
        out_shape=jax.ShapeDtypeStruct((b, s, d), F32),
        scratch_shapes=[pltpu.VMEM((FFT_N1 * 2 * FFT_N2, c), F32)],
        compiler_params=pltpu.CompilerParams(
            dimension_semantics=("parallel", "parallel"), vmem_limit_bytes=_vmem_limit(est)),
        name="seq_fft",
    )(ab, ab, m1, w2)


def _fourier_mixer_pre_out(x, w_in):
    b, s, d = x.shape
    g, gd, c = N_FOURIER_GROUPS, FOURIER_GROUP_DIM, FFT_CHUNK
    halves = gd // c
    cs = jnp.asarray(_fourier_channel_tables())
    w_ab = pl.pallas_call(
        functools.partial(_mm_kernel, precision=lax.Precision.HIGHEST),
        grid=(g, halves),
        in_specs=[pl.BlockSpec((d, gd), lambda i, h: (0, i)),
                  pl.BlockSpec((None, gd, 2 * c), lambda i, h: (h, 0, 0))],
        out_specs=pl.BlockSpec((d, 2 * c), lambda i, h: (0, i * halves + h)),
        out_shape=jax.ShapeDtypeStruct((d, 2 * d), BF16),
        name="fourier_weight_fold",
    )(w_in, cs)
    ab = _matmul(x.reshape(b * s, d), w_ab, F32, MM_BM, 2 * d).reshape(b, s, 2 * d)
    m1, w2 = _fourier_seq_tables(s)
    return _seq_fft(ab, jnp.asarray(m1, dtype=BF16), jnp.asarray(w2, dtype=BF16))


def _layer_norm_rows(z, g, b):
    mu = jnp.mean(z, axis=-1, keepdims=True)
    zc = z - mu
    var = jnp.mean(zc * zc, axis=-1, keepdims=True)
    return zc * lax.rsqrt(var + LN_EPS) * g + b


def _pack_bf16_pairs(z):
    m = z.shape[1] // 2
    zb = z.astype(BF16).astype(F32)
    bits = lax.bitcast_convert_type(zb, jnp.uint32)
    return jnp.bitwise_or(bits[:, m:], lax.shift_right_logical(bits[:, :m], jnp.uint32(16)))


def _unpack_bf16_pairs(w):
    lo = lax.bitcast_convert_type(lax.shift_left(w, jnp.uint32(16)), F32)
    hi = lax.bitcast_convert_type(jnp.bitwise_and(w, jnp.uint32(0xFFFF0000)), F32)
    return jnp.concatenate([lo, hi], axis=1)


def _mm_ln_kernel(a_ref, w_ref, x_ref, g_ref, b_ref, wr_ref, br_ref, o_ref, opk_ref, cls_ref):
    y = jnp.dot(a_ref[...].astype(BF16), w_ref[...], preferred_element_type=F32)
    z = DEEPNORM_ALPHA * x_ref[...] + y
    out = _layer_norm_rows(z, g_ref[...], b_ref[...])
    o_ref[...] = out
    cls, gates = _route_rows(out, wr_ref[...], br_ref[...])
    cls_ref[...] = cls
    bm, dw = out.shape[0], out.shape[1] // 2
    on_diag = lax.broadcasted_iota(I32, (LANES, LANES), 0) == lax.broadcasted_iota(I32, (LANES, LANES), 1)
    lane = lax.broadcasted_iota(I32, (LANES, LANES), 1)
    tiles = []
    for c in range(bm // LANES):
        tile = jnp.zeros((LANES, LANES), F32)
        for k in range(2):
            g = gates[k:k + 1, c * LANES:(c + 1) * LANES]
            col = jnp.sum(jnp.where(on_diag, g, 0.0), axis=1, keepdims=True)
            tile = jnp.where(lane == k, col, tile)
        tiles.append(tile)
    opk_ref[:, :dw] = _pack_bf16_pairs(out)
    opk_ref[:, dw:] = lax.bitcast_convert_type(jnp.concatenate(tiles, axis=0), jnp.uint32)


def _proj_residual_ln(a, w_bf16, x, ln_g, ln_b, w_router_t, b_router):
    t, k = a.shape
    d = w_bf16.shape[1]
    e = w_router_t.shape[0]
    bm = LN_BM
    est = 2 * (bm * k * a.dtype.itemsize + k * d * 2 + 3 * bm * d * 4) + 8 * bm * d * 4
    return pl.pallas_call(
        _mm_ln_kernel,
        grid=(t // bm,),
        in_specs=[pl.BlockSpec((bm, k), lambda i: (i, 0)),
                  pl.BlockSpec((k, d), lambda i: (0, 0)),
                  pl.BlockSpec((bm, d), lambda i: (i, 0)),
                  pl.BlockSpec((1, d), lambda i: (0, 0)),
                  pl.BlockSpec((1, d), lambda i: (0, 0)),
                  pl.BlockSpec((e, d), lambda i: (0, 0)),
                  pl.BlockSpec((e, 1), lambda i: (0, 0))],
        out_specs=[pl.BlockSpec((bm, d), lambda i: (i, 0)),
                   pl.BlockSpec((bm, d // 2 + LANES), lambda i: (i, 0)),
                   pl.BlockSpec((1, bm), lambda i: (0, i))],
        out_shape=[jax.ShapeDtypeStruct((t, d), F32), jax.ShapeDtypeStruct((t, d // 2 + LANES), jnp.uint32),
                   jax.ShapeDtypeStruct((1, t), I32)],
        compiler_params=pltpu.CompilerParams(
            dimension_semantics=("parallel",), vmem_limit_bytes=_vmem_limit(est)),
        name="proj_residual_ln",
    )(a, w_bf16, x, ln_g.reshape(1, d), ln_b.reshape(1, d), w_router_t, b_router.reshape(e, 1))


def _top2_rows(v, iota):
    n_rows = v.shape[0]
    m1 = jnp.max(v, axis=0, keepdims=True)
    i1 = jnp.min(jnp.where(v == m1, iota, n_rows), axis=0, keepdims=True)
    v2 = jnp.where(iota == i1, -jnp.inf, v)
    m2 = jnp.max(v2, axis=0, keepdims=True)
    i2 = jnp.min(jnp.where(v2 == m2, iota, n_rows), axis=0, keepdims=True)
    return m1, i1, m2, i2


def _route_rows(x, w, b):
    xh = x.astype(BF16)
    xl = (x - xh.astype(F32)).astype(BF16)
    wh = w.astype(BF16)
    wl = (w - wh.astype(F32)).astype(BF16)
    nt = (((1,), (1,)), ((), ()))
    logits = (lax.dot_general(wh, xh, nt, preferred_element_type=F32)
              + lax.dot_general(wh, xl, nt, preferred_element_type=F32)
              + lax.dot_general(wl, xh, nt, preferred_element_type=F32))
    scores = 1.0 / (1.0 + jnp.exp(-logits))
    sel = scores + b
    epg = EXPERTS_PER_GROUP
    bt = x.shape[0]
    iota = lax.broadcasted_iota(I32, (epg, bt), 0)

    best = None
    for g in range(N_GROUPS):
        m1, _, m2, _ = _top2_rows(sel[g * epg:(g + 1) * epg], iota)
        gs = m1 + m2
        if best is None:
            best, gidx = gs, jnp.zeros((1, bt), I32)
        else:
            better = gs > best
            gidx = jnp.where(better, g, gidx)
            best = jnp.where(better, gs, best)

    sel_in = sel[0:epg]
    sc_in = scores[0:epg]
    for g in range(1, N_GROUPS):
        pick = gidx == g
        sel_in = jnp.where(pick, sel[g * epg:(g + 1) * epg], sel_in)
        sc_in = jnp.where(pick, scores[g * epg:(g + 1) * epg], sc_in)
    _, i1, _, i2 = _top2_rows(sel_in, iota)
    g1 = jnp.sum(jnp.where(iota == i1, sc_in, 0.0), axis=0, keepdims=True)
    g2 = jnp.sum(jnp.where(iota == i2, sc_in, 0.0), axis=0, keepdims=True)
    denom = g1 + g2
    first_lo = i1 < i2
    lo = jnp.where(first_lo, i1, i2)
    hi = jnp.where(first_lo, i2, i1)
    pair = lax.shift_right_logical(lo * (2 * epg - 1 - lo), 1) + (hi - lo - 1)
    cls = gidx * PAIRS_PER_GROUP + pair
    gates = jnp.concatenate([jnp.where(first_lo, g1, g2) / denom, jnp.where(first_lo, g2, g1) / denom], axis=0)
    return cls, gates


RANK_SUB = 256
RANK_NSUB = 8


def _rank_kernel(keys_ref, tri_ref, rank_ref, counts_ref, carry_ref, *, n_classes):
    nsub, sub = RANK_NSUB, RANK_SUB

    @pl.when(pl.program_id(0) == 0)
    def _():
        carry_ref[...] = jnp.zeros_like(carry_ref)

    cls = lax.broadcasted_iota(I32, (nsub, n_classes, sub), 1)
    onehot = cls == keys_ref[...]
    oh = jnp.where(onehot, 1.0, 0.0).reshape(nsub * n_classes, sub).astype(BF16)
    pref = jnp.dot(oh, tri_ref[...], preferred_element_type=F32).reshape(nsub, n_classes, sub)
    carry = carry_ref[...]
    for j in range(nsub):
        before = pref[j] + (carry - 1.0)
        rank_ref[j] = jnp.sum(jnp.where(onehot[j], before, 0.0), axis=0, keepdims=True).astype(I32)
        carry = carry + pref[j][:, sub - 1:sub]
    carry_ref[...] = carry
    counts_ref[...] = carry.astype(I32)


def _rank_within_class(keys, n_classes):
    n = keys.shape[0]
    nsub, sub = RANK_NSUB, RANK_SUB
    tri = jnp.asarray(np.triu(np.ones((sub, sub), np.float32)), dtype=BF16)
    rank, counts = pl.pallas_call(
        functools.partial(_rank_kernel, n_classes=n_classes),
        grid=(n // (nsub * sub),),
        in_specs=[pl.BlockSpec((nsub, 1, sub), lambda i: (i, 0, 0)),
                  pl.BlockSpec((sub, sub), lambda i: (0, 0))],
        out_specs=[pl.BlockSpec((nsub, 1, sub), lambda i: (i, 0, 0)),
                   pl.BlockSpec((n_classes, 1), lambda i: (0, 0))],
        out_shape=[jax.ShapeDtypeStruct((n // sub, 1, sub), I32),
                   jax.ShapeDtypeStruct((n_classes, 1), I32)],
        scratch_shapes=[pltpu.VMEM((n_classes, 1), F32)],
        compiler_params=pltpu.CompilerParams(dimension_semantics=("arbitrary",)),
        name="rank_within_class",
    )(keys.reshape(n // sub, 1, sub), tri)
    return rank.reshape(n), counts.reshape(n_classes)


FILL_CHUNK = 4096


def _fill_slots_kernel(dest_ref, default_hbm, slot_ref, sem):
    step = pl.program_id(0)

    @pl.when(step == 0)
    def _():
        copy = pltpu.make_async_copy(default_hbm, slot_ref, sem.at[0])
        copy.start()
        copy.wait()

    base = step * FILL_CHUNK

    def place(j, carry):
        slot_ref[dest_ref[j]] = base + j
        return carry

    lax.fori_loop(0, FILL_CHUNK, place, 0, unroll=32)


def _fill_slots(dest, default):
    n_items = dest.shape[0]
    assert n_items % FILL_CHUNK == 0
    return pl.pallas_call(
        _fill_slots_kernel,
        grid=(n_items // FILL_CHUNK,),
        in_specs=[pl.BlockSpec((FILL_CHUNK,), lambda i: (i,), memory_space=pltpu.SMEM),
                  pl.BlockSpec(memory_space=pl.ANY)],
        out_specs=pl.BlockSpec(memory_space=pltpu.SMEM),
        out_shape=jax.ShapeDtypeStruct(default.shape, I32),
        scratch_shapes=[pltpu.SemaphoreType.DMA((1,))],
        compiler_params=pltpu.CompilerParams(dimension_semantics=("arbitrary",)),
        name="fill_slots",
    )(dest, default)


def _class_experts():
    epg = EXPERTS_PER_GROUP
    pairs = [(lo, hi) for lo in range(epg) for hi in range(lo + 1, epg)]
    return np.array([[g * epg + lo, g * epg + hi] for g in range(N_GROUPS) for lo, hi in pairs], np.int32)


def _route_slots(cls, bm):
    t = cls.shape[1]
    keys = cls.reshape(t)
    rank, counts = _rank_within_class(keys, N_CLASSES)
    padded = (counts + bm - 1) // bm * bm
    pad_end = jnp.cumsum(padded)
    pad_start = pad_end - padded
    class_ids = jnp.arange(N_CLASSES, dtype=I32)
    dest = jnp.sum(jnp.where(keys[:, None] == class_ids[None, :], pad_start[None, :], 0), axis=1) + rank
    nb = t // bm + N_CLASSES + 1
    block_start = jnp.arange(nb, dtype=I32) * bm
    block_class = jnp.minimum(
        jnp.sum((pad_end[None, :] <= block_start[:, None]).astype(I32), axis=1), N_CLASSES - 1)
    row = jnp.arange(bm, dtype=I32)[None, :]
    in_use = (block_start < pad_end[-1])[:, None]
    default = jnp.where(in_use, t + block_class[:, None] * bm + row, row).reshape(nb * bm)
    slot_tok = _fill_slots(dest, default)
    in_class = block_class[:, None, None] == class_ids[None, :, None]
    block_experts = jnp.sum(jnp.where(in_class, jnp.asarray(_class_experts())[None], 0), axis=1).reshape(2 * nb)
    n_active = (pad_end[-1] // bm).astype(I32).reshape(1)
    return slot_tok, block_experts, n_active


SUBLANES = 8
SUBLANE_SHIFT = SUBLANES.bit_length() - 1


def _expert_kernel(be_ref, tok_ref, nact_ref, x_hbm, wga_ref, wua_ref, wda_ref, wgb_ref, wub_ref, wdb_ref,
                   y_hbm, xnext, xcur, ycur, yout, wga_bf, wua_bf, wda_bf, wgb_bf, wub_bf, wdb_bf, gsem, ssem,
                   *, n_tok, n_classes):
    bm = EXPERT_BM
    tiles = bm // SUBLANES
    dw = ycur.shape[-1]
    i = pl.program_id(0)
    n_blocks = pl.num_programs(0)
    nact = nact_ref[0]
    w_f32 = ((wga_ref, wua_ref, wda_ref), (wgb_ref, wub_ref, wdb_ref))
    wbf = (wga_bf, wua_bf, wda_bf, wgb_bf, wub_bf, wdb_bf)

    def row_copies(blk, gather, unrolled):
        base = blk * bm

        def body(rt, carry):
            for u in range(SUBLANES):
                row = tok_ref[base + rt * SUBLANES + u]
                if gather:
                    row = jnp.bitwise_and(row, n_tok - 1)
                    pltpu.make_async_copy(
                        x_hbm.at[lax.shift_right_logical(row, SUBLANE_SHIFT),
                                 pl.ds(jnp.bitwise_and(row, SUBLANES - 1), 1), :],
                        xnext.at[rt, pl.ds(u, 1), :], gsem.at[0]).start()
                else:
                    pltpu.make_async_copy(
                        yout.at[rt, pl.ds(u, 1), :],
                        y_hbm.at[lax.shift_right_logical(row, SUBLANE_SHIFT),
                                 pl.ds(jnp.bitwise_and(row, SUBLANES - 1), 1), :],
                        ssem.at[0]).start()
            return carry

        if unrolled is None:
            lax.fori_loop(0, tiles, body, 0)
        else:
            for rt in range(*unrolled):
                body(rt, 0)

    def wait_gather():
        pltpu.make_async_copy(x_hbm.at[pl.ds(0, tiles)], xnext, gsem.at[0]).wait()

    def wait_scatter():
        pltpu.make_async_copy(yout, y_hbm.at[pl.ds(0, tiles)], ssem.at[0]).wait()

    @pl.when(i == 0)
    def _():
        row_copies(0, True, unrolled=None)
        yout[...] = jnp.zeros(yout.shape, yout.dtype)
        ycur[...] = jnp.zeros(ycur.shape, ycur.dtype)
        spare_tile0 = n_tok // SUBLANES

        def spare_copy(k):
            return pltpu.make_async_copy(
                yout, y_hbm.at[pl.ds(spare_tile0 + k * tiles, tiles)], ssem.at[0])

        @pl.loop(0, n_classes)
        def _(k):
            spare_copy(k).start()

        @pl.loop(0, n_classes)
        def _(k):
            spare_copy(k).wait()

    for side in range(2):
        idx = 2 * i + side
        changed = jnp.logical_or(i == 0, be_ref[idx] != be_ref[jnp.maximum(idx - 2, 0)])

        @pl.when(jnp.logical_and(changed, i < nact))
        def _():
            for m in range(3):
                wbf[3 * side + m][...] = w_f32[side][m][0].astype(BF16)

    @pl.when(i < nact)
    def _():
        wait_gather()
        xcur[...] = xnext[...]

        @pl.when(i >= 1)
        def _():
            wait_scatter()

        yout[...] = ycur[...]

    @pl.when(nact - i >= 1)
    def _():
        row_copies(jnp.minimum(i + 1, n_blocks - 1), True, unrolled=(0, tiles))
        row_copies(jnp.maximum(i - 1, 0), False, unrolled=(0, tiles))
        rows = xcur[...].reshape(bm, dw + LANES)
        x = _unpack_bf16_pairs(rows[:, :dw]).astype(BF16)
        gates = lax.bitcast_convert_type(rows[:, dw:], F32)
        y = None
        for side in range(2):
            wg_bf, wu_bf, wd_bf = wbf[3 * side], wbf[3 * side + 1], wbf[3 * side + 2]
            gate = jnp.dot(x, wg_bf[...], preferred_element_type=F32)
            up = jnp.dot(x, wu_bf[...], preferred_element_type=F32)
            h = (gate / (1.0 + jnp.exp(-gate))) * up
            ys = jnp.dot(h.astype(BF16), wd_bf[...], preferred_element_type=F32) * gates[:, side:side + 1]
            y = ys if y is None else y + ys
        ycur[...] = _pack_bf16_pairs(y).reshape(tiles, SUBLANES, dw)

    @pl.when(i == nact)
    def _():
        wait_gather()
        wait_scatter()
        yout[...] = ycur[...]
        row_copies(i - 1, False, unrolled=None)
        wait_scatter()


def _experts(x_pk, slot_tok, block_experts, n_active, w_gate, w_up, w_down, layer):
    t, dw_in = x_pk.shape
    _, _, d, f = w_gate.shape
    dw = d // 2
    assert dw_in == dw + LANES
    bm = EXPERT_BM
    n_blocks = slot_tok.shape[0] // bm
    n_rows = t + N_CLASSES * bm
    assert t & (t - 1) == 0 and t % SUBLANES == 0 and bm % SUBLANES == 0
    est = 2 * 6 * d * f * 4 + 6 * d * f * 2 + 4 * bm * d * 2 + 10 * bm * d * 4

    def w_spec(shape, side):
        return pl.BlockSpec((None, 1) + shape, lambda i, be, tk, na: (layer, be[2 * i + side], 0, 0))

    xbuf = pltpu.VMEM((bm // SUBLANES, SUBLANES, dw_in), jnp.uint32)
    ybuf = pltpu.VMEM((bm // SUBLANES, SUBLANES, dw), jnp.uint32)
    grid_spec = pltpu.PrefetchScalarGridSpec(
        num_scalar_prefetch=3,
        grid=(n_blocks,),
        in_specs=[pl.BlockSpec(memory_space=pl.ANY),
                  w_spec((d, f), 0), w_spec((d, f), 0), w_spec((f, d), 0),
                  w_spec((d, f), 1), w_spec((d, f), 1), w_spec((f, d), 1)],
        out_specs=pl.BlockSpec(memory_space=pl.ANY),
        scratch_shapes=[xbuf, xbuf, ybuf, ybuf]
                       + [pltpu.VMEM((d, f), BF16), pltpu.VMEM((d, f), BF16), pltpu.VMEM((f, d), BF16)] * 2
                       + [pltpu.SemaphoreType.DMA((1,)), pltpu.SemaphoreType.DMA((1,))],
    )
    y = pl.pallas_call(
        functools.partial(_expert_kernel, n_tok=t, n_classes=N_CLASSES),
        grid_spec=grid_spec,
        out_shape=jax.ShapeDtypeStruct((n_rows // SUBLANES, SUBLANES, dw), jnp.uint32),
        compiler_params=pltpu.CompilerParams(
            dimension_semantics=("arbitrary",), vmem_limit_bytes=_vmem_limit(est)),
        name="experts",
    )(block_experts, slot_tok, n_active, x_pk.reshape(t // SUBLANES, SUBLANES, dw_in),
      w_gate, w_up, w_down, w_gate, w_up, w_down)
    return y.reshape(n_rows, dw)


def _combine_kernel(x_ref, y_ref, g_ref, b_ref, o_ref):
    z = DEEPNORM_ALPHA * x_ref[...] + _unpack_bf16_pairs(y_ref[...])
    o_ref[...] = _layer_norm_rows(z, g_ref[...], b_ref[...])


def _combine_ln(x, y_tok, ln_g, ln_b):
    t, d = x.shape
    bt = COMBINE_BT
    return pl.pallas_call(
        _combine_kernel,
        grid=(t // bt,),
        in_specs=[pl.BlockSpec((bt, d), lambda i: (i, 0)),
                  pl.BlockSpec((bt, d // 2), lambda i: (i, 0)),
                  pl.BlockSpec((1, d), lambda i: (0, 0)),
                  pl.BlockSpec((1, d), lambda i: (0, 0))],
        out_specs=pl.BlockSpec((bt, d), lambda i: (i, 0)),
        out_shape=jax.ShapeDtypeStruct((t, d), F32),
        compiler_params=pltpu.CompilerParams(
            dimension_semantics=("parallel",), vmem_limit_bytes=_vmem_limit(10 * bt * d * 4)),
        name="combine_ln",
    )(x, y_tok, ln_g.reshape(1, d), ln_b.reshape(1, d))


def _moe_residual_ln(x, x_pk, cls, w_gate, w_up, w_down, layer, ln_g, ln_b):
    slot_tok, block_experts, n_active = _route_slots(cls, EXPERT_BM)
    y_tok = _experts(x_pk, slot_tok, block_experts, n_active, w_gate, w_up, w_down, layer)
    return _combine_ln(x, y_tok, ln_g, ln_b)


def _na_bias_pairs(rpb):
    w = GRID_W
    kc = min(WIN_COLS, w)
    cols = np.arange(w)
    col_start = np.clip(cols - kc // 2, 0, w - kc)
    col_mask = (cols[None, :] >= col_start[:, None]) & (cols[None, :] < col_start[:, None] + kc)
    col_idx = np.clip(cols[None, :] - cols[:, None] + WIN_COLS - 1, 0, 2 * WIN_COLS - 2)
    n_ci = 2 * WIN_COLS - 1
    pick = (col_idx[None] == np.arange(n_ci)[:, None, None]).astype(np.float32)
    tb = jnp.einsum("hrc,cqk->hrqk", rpb.astype(F32), jnp.asarray(pick), precision=lax.Precision.HIGHEST)
    tb = jnp.where(col_mask[None, None], tb, NEG_BIAS)
    neg = jnp.full((N_HEADS, 1, w, w), NEG_BIAS, F32)
    ext = jnp.concatenate([neg, tb, neg], axis=1)
    return jnp.concatenate([ext[:, :-1], ext[:, 1:]], axis=-1)


def _na_tile_plan(rows):
    kr = min(WIN_ROWS, rows)
    n_blocks = rows // NA_RQ
    plan = []
    for blk in (0, 1, n_blocks - 1):
        ks = int(np.clip(blk - 1, 0, n_blocks - 3)) * NA_RQ
        per_rq = []
        for rq in range(NA_RQ):
            r = blk * NA_RQ + rq
            rs = int(np.clip(r - kr // 2, 0, rows - kr))
            ri = [ks + j - r + WIN_ROWS - 1 if rs <= ks + j < rs + kr else None for j in range(NA_KROWS)]
            tiles = []
            for jp in range(NA_KROWS // 2):
                a, b = ri[2 * jp], ri[2 * jp + 1]
                if a is not None and b is not None:
                    tiles.append(("both", b))
                elif b is not None:
                    tiles.append(("hi", b))
                elif a is not None:
                    tiles.append(("lo", a + 1))
                else:
                    tiles.append(("none", 0))
            per_rq.append(tiles)
        plan.append(per_rq)
    return plan


def _na_kernel(q_ref, k0_ref, k1_ref, k2_ref, v0_ref, v1_ref, v2_ref, pair_ref, o_ref, bias_ref, *, plan):
    dh = HEAD_DIM
    w = GRID_W
    nt = (((1,), (1,)), ((), ()))
    scale = HEAD_DIM ** -0.5
    i = pl.program_id(2)
    n_blocks = pl.num_programs(2)

    def rebuild(per_rq):
        upper = lax.broadcasted_iota(I32, (w, 2 * w), 1) >= w
        for rq, tiles in enumerate(per_rq):
            for jp, (kind, k) in enumerate(tiles):
                for h in range(NA_HEADS_PER_STEP):
                    if kind == "none":
                        tile = jnp.full((w, 2 * w), NEG_BIAS, F32)
                    elif kind == "both":
                        tile = pair_ref[h, k]
                    elif kind == "hi":
                        tile = jnp.where(upper, pair_ref[h, k], NEG_BIAS)
                    else:
                        tile = jnp.where(upper, NEG_BIAS, pair_ref[h, k])
                    bias_ref[h, rq * w:(rq + 1) * w, jp * 2 * w:(jp + 1) * 2 * w] = tile

    for ty, first_step in enumerate((0, 1, n_blocks - 1)):
        @pl.when(i == first_step)
        def _():
            rebuild(plan[ty])

    assert 2 * dh == LANES
    tq = q_ref.shape[1]
    low = lax.broadcasted_iota(I32, (tq, LANES), 1) < dh
    low_k = lax.broadcasted_iota(I32, (3 * tq, LANES), 1) < dh
    outs = []
    for hp in range(NA_HEADS_PER_STEP // 2):
        cols = slice(hp * LANES, (hp + 1) * LANES)
        q2 = q_ref[0, :, cols] * scale
        k2 = jnp.concatenate([k0_ref[0, :, cols], k1_ref[0, :, cols], k2_ref[0, :, cols]], axis=0)
        v2 = jnp.concatenate([v0_ref[0, :, cols], v1_ref[0, :, cols], v2_ref[0, :, cols]], axis=0)
        halves = []
        for sub in range(2):
            mine, mine_k = (low, low_k) if sub == 0 else (~low, ~low_k)
            q = jnp.where(mine, q2, jnp.zeros_like(q2))
            v = jnp.where(mine_k, v2, jnp.ones_like(v2))
            s = lax.dot_general(q, k2, nt, preferred_element_type=F32) + bias_ref[2 * hp + sub]
            m = jnp.max(s, axis=-1, keepdims=True)
            p = jnp.exp((s - m).astype(BF16))
            o = jnp.dot(p, v, preferred_element_type=F32)
            denom = o[:, dh:dh + 1] if sub == 0 else o[:, 0:1]
            halves.append(o * (1.0 / denom))
        outs.append(jnp.where(low, halves[0], halves[1]))
    o_ref[0] = jnp.concatenate(outs, axis=-1).astype(o_ref.dtype)


def _neighbourhood_attention(qkv, bias_pairs, rows):
    b, s, _ = qkv.shape
    d = D_MODEL
    w = GRID_W
    tq = NA_RQ * w
    n_blocks = rows // NA_RQ
    hps = NA_HEADS_PER_STEP
    hw = hps * HEAD_DIM
    n_hh = d // hw
    assert NA_KROWS * w == 3 * tq and n_blocks >= 3 and NA_KROWS % 2 == 0

    def kv_map(part, j):
        def index_map(hh, bi, i):
            return (bi, jnp.clip(i - 1, 0, n_blocks - 3) + j, part * n_hh + hh)
        return index_map

    blk = (1, tq, hw)
    pair_blk = (hps,) + bias_pairs.shape[1:]
    est = (2 * (7 * tq * hw * 2 + tq * hw * 2) + 2 * 4 * int(np.prod(pair_blk))
           + hps * tq * 3 * tq * 4 + 8 * tq * 3 * tq * 4)
    return pl.pallas_call(
        functools.partial(_na_kernel, plan=_na_tile_plan(rows)),
        grid=(n_hh, b, n_blocks),
        in_specs=[pl.BlockSpec(blk, lambda hh, bi, i: (bi, i, hh))]
                 + [pl.BlockSpec(blk, kv_map(1, j)) for j in range(3)]
                 + [pl.BlockSpec(blk, kv_map(2, j)) for j in range(3)]
                 + [pl.BlockSpec(pair_blk, lambda hh, bi, i: (hh, 0, 0, 0))],
        out_specs=pl.BlockSpec(blk, lambda hh, bi, i: (bi, i, hh)),
        out_shape=jax.ShapeDtypeStruct((b, s, d), BF16),
        scratch_shapes=[pltpu.VMEM((hps, tq, 3 * tq), F32)],
        compiler_params=pltpu.CompilerParams(
            dimension_semantics=("arbitrary", "arbitrary", "arbitrary"), vmem_limit_bytes=_vmem_limit(est)),
        name="neighbourhood_attention",
    )(qkv, qkv, qkv, qkv, qkv, qkv, qkv, bias_pairs)


def kernel(x, fourier_w_in, fourier_w_out, na_w_qkv, na_rpb, na_w_out, router_w, router_b,
           expert_w_gate, expert_w_up, expert_w_down, ln_g, ln_b):
    b, s, d = x.shape
    t = b * s
    rows = s // GRID_W
    w_router_t = router_w.T
    xt = x.reshape(t, d)

    f = _fourier_mixer_pre_out(x, fourier_w_in[0])
    xt, xt_pk, cls = _proj_residual_ln(f.reshape(t, d), fourier_w_out[0].astype(BF16), xt,
                                       ln_g[0, 0], ln_b[0, 0], w_router_t, router_b)
    xt = _moe_residual_ln(xt, xt_pk, cls, expert_w_gate, expert_w_up, expert_w_down, 0,
                          ln_g[0, 1], ln_b[0, 1])

    qkv = _matmul(xt, na_w_qkv[0].astype(BF16), BF16, MM_BM, 3 * d // 2).reshape(b, s, 3 * d)
    bias = _na_bias_pairs(na_rpb[0])
    o = _neighbourhood_attention(qkv, bias, rows)
    xt, xt_pk, cls = _proj_residual_ln(o.reshape(t, d), na_w_out[0].astype(BF16), xt,
                                       ln_g[1, 0], ln_b[1, 0], w_router_t, router_b)
    xt = _moe_residual_ln(xt, xt_pk, cls, expert_w_gate, expert_w_up, expert_w_down, 1,
                          ln_g[1, 1], ln_b[1, 1])
    return xt.reshape(b, s, d)
```

```python
import functools

import jax
import jax.numpy as jnp
import numpy as np
from jax import lax
from jax.experimental import pallas as pl
from jax.experimental.pallas import tpu as pltpu

F32 = jnp.float32
BF16 = jnp.bfloat16
I32 = jnp.int32

D_MODEL = 1024
GRID_W = 64
N_FOURIER_GROUPS = 4
FOURIER_GROUP_DIM = D_MODEL // N_FOURIER_GROUPS
N_HEADS = 16
HEAD_DIM = D_MODEL // N_HEADS
WIN_ROWS = 8
WIN_COLS = 16
N_EXPERTS = 32
N_GROUPS = 4
EXPERTS_PER_GROUP = N_EXPERTS // N_GROUPS
D_EXPERT = D_MODEL // 2
PAIRS_PER_GROUP = EXPERTS_PER_GROUP * (EXPERTS_PER_GROUP - 1) // 2
N_CLASSES = N_GROUPS * PAIRS_PER_GROUP
DEPTH = 2
DEEPNORM_ALPHA = (2 * DEPTH) ** 0.25
LN_EPS = 1e-5

V7X_VMEM_BYTES = 64 * 1024 * 1024
LANES = 128

FFT_N1 = 64
FFT_N2 = 128
FFT_CHUNK = 128

MM_BM = 1024
LN_BM = 512
EXPERT_BM = 384
COMBINE_BT = 1024
NA_RQ = 4
NA_KROWS = NA_RQ + WIN_ROWS
NA_HEADS_PER_STEP = 16
NEG_BIAS = -1e30


def _vmem_limit(nbytes):
    return int(min(max(nbytes, 32 * 1024 * 1024), V7X_VMEM_BYTES - 8 * 1024 * 1024))


def _mm_kernel(a_ref, b_ref, o_ref, *, precision):
    if precision is None:
        a = a_ref[...].astype(BF16)
        b = b_ref[...].astype(BF16)
        acc = jnp.dot(a, b, preferred_element_type=F32)
    else:
        acc = jnp.dot(a_ref[...], b_ref[...], preferred_element_type=F32, precision=precision)
    o_ref[...] = acc.astype(o_ref.dtype)


def _matmul(a, b, out_dtype, bm, bn, precision=None):
    m, k = a.shape
    _, n = b.shape
    est = 2 * (bm * k * a.dtype.itemsize + k * bn * b.dtype.itemsize + bm * bn * 4) + 3 * bm * bn * 4
    return pl.pallas_call(
        functools.partial(_mm_kernel, precision=precision),
        grid=(m // bm, n // bn),
        in_specs=[pl.BlockSpec((bm, k), lambda i, j: (i, 0)),
                  pl.BlockSpec((k, bn), lambda i, j: (0, j))],
        out_specs=pl.BlockSpec((bm, bn), lambda i, j: (i, j)),
        out_shape=jax.ShapeDtypeStruct((m, n), out_dtype),
        compiler_params=pltpu.CompilerParams(
            dimension_semantics=("parallel", "parallel"), vmem_limit_bytes=_vmem_limit(est)),
        name="matmul",
    )(a, b)


def _fourier_channel_tables():
    n = FOURIER_GROUP_DIM
    c = np.arange(n)
    ang = 2.0 * np.pi * ((c[:, None] * c[None, :]) % n) / n
    cos, sin = np.cos(ang) / np.sqrt(n), np.sin(ang) / np.sqrt(n)
    half = FFT_CHUNK
    tabs = [np.concatenate([cos[:, h * half:(h + 1) * half], sin[:, h * half:(h + 1) * half]], axis=1)
            for h in range(n // half)]
    return np.stack(tabs).astype(np.float32)


def _fourier_seq_tables(seq):
    n1, n2 = FFT_N1, FFT_N2
    assert n1 * n2 == seq
    k2 = np.arange(n2)
    s2 = np.arange(n2)
    m1 = np.empty((n1, 2 * n2, 2 * n2), np.float32)
    for s1 in range(n1):
        ang = 2.0 * np.pi * ((k2[:, None] * (s1 + n1 * s2[None, :])) % seq) / seq
        mr, mi = np.cos(ang) / np.sqrt(n2), np.sin(ang) / np.sqrt(n2)
        m1[s1] = np.block([[mr, -mi], [mi, mr]])
    k1 = np.arange(n1)
    ang = 2.0 * np.pi * ((k1[:, None] * k1[None, :]) % n1) / n1
    w2 = np.concatenate([np.cos(ang), -np.sin(ang)], axis=1) / np.sqrt(n1)
    return m1, w2.astype(np.float32)


def _fft_kernel(a_ref, b_ref, m1_ref, w2_ref, o_ref, zs_ref):
    n1, n2, c = FFT_N1, FFT_N2, FFT_CHUNK

    def stage1(s1, carry):
        rows = pl.ds(s1, n2, stride=n1)
        x = jnp.concatenate([a_ref[0, rows, :], b_ref[0, rows, :]], axis=0).astype(BF16)
        z = jnp.dot(m1_ref[s1], x, preferred_element_type=F32)
        zs_ref[pl.ds(pl.multiple_of(s1 * 2 * n2, 2 * n2), 2 * n2), :] = z
        return carry

    lax.fori_loop(0, n1, stage1, 0, unroll=16)

    def stage2(kk, carry):
        k2 = 2 * kk
        parts = []
        for d in range(2):
            zr = zs_ref[pl.ds(k2 + d, n1, stride=2 * n2), :]
            zi = zs_ref[pl.ds(n2 + k2 + d, n1, stride=2 * n2), :]
            parts.append(jnp.concatenate([zr, zi], axis=0))
        z = jnp.concatenate(parts, axis=1).astype(BF16)
        y = jnp.dot(w2_ref[...], z, preferred_element_type=F32)
        o_ref[0, pl.ds(k2, n1, stride=n2), :] = y[:, :c]
        o_ref[0, pl.ds(k2 + 1, n1, stride=n2), :] = y[:, c:]
        return carry

    lax.fori_loop(0, n2 // 2, stage2, 0, unroll=16)


def _seq_fft(ab, m1, w2):
    b, s, two_d = ab.shape
    d = two_d // 2
    c = FFT_CHUNK
    est = 2 * (s * 2 * c * 4 + m1.size * 2 + s * c * 4) + FFT_N1 * 2 * FFT_N2 * c * 4 + (4 << 20)
    return pl.pallas_call(
        _fft_kernel,
        grid=(b, d // c),
        in_specs=[pl.BlockSpec((1, s, c), lambda i, j: (i, 0, 2 * j)),
                  pl.BlockSpec((1, s, c), lambda i, j: (i, 0, 2 * j + 1)),
                  pl.BlockSpec(m1.shape, lambda i, j: (0, 0, 0)),
                  pl.BlockSpec(w2.shape, lambda i, j: (0, 0))],
        out_specs=pl.BlockSpec((1, s, c), lambda i, j: (i, 0, j)),
        out_shape=jax.ShapeDtypeStruct((b, s, d), F32),
        scratch_shapes=[pltpu.VMEM((FFT_N1 * 2 * FFT_N2, c), F32)],
        compiler_params=pltpu.CompilerParams(
            dimension_semantics=("parallel", "parallel"), vmem_limit_bytes=_vmem_limit(est)),
        name="seq_fft",
    )(ab, ab, m1, w2)


def _fourier_mixer_pre_out(x, w_in):
    b, s, d = x.shape
    g, gd, c = N_FOURIER_GROUPS, FOURIER_GROUP_DIM, FFT_CHUNK
    halves = gd // c
    cs = jnp.asarray(_fourier_channel_tables())
    w_ab = pl.pallas_call(
        functools.partial(_mm_kernel, precision=lax.Precision.HIGHEST),
        grid=(g, halves),
        in_specs=[pl.BlockSpec((d, gd), lambda i, h: (0, i)),
                  pl.BlockSpec((None, gd, 2 * c), lambda i, h: (h, 0, 0))],
        out_specs=pl.BlockSpec((d, 2 * c), lambda i, h: (0, i * halves + h)),
        out_shape=jax.ShapeDtypeStruct((d, 2 * d), BF16),
        name="fourier_weight_fold",
    )(w_in, cs)
    ab = _matmul(x.reshape(b * s, d), w_ab, F32, MM_BM, 2 * d).reshape(b, s, 2 * d)
    m1, w2 = _fourier_seq_tables(s)
    return _seq_fft(ab, jnp.asarray(m1, dtype=BF16), jnp.asarray(w2, dtype=BF16))


def _layer_norm_rows(z, g, b):
    mu = jnp.mean(z, axis=-1, keepdims=True)
    zc = z - mu
    var = jnp.mean(zc * zc, axis=-1, keepdims=True)
    return zc * lax.rsqrt(var + LN_EPS) * g + b


def _pack_bf16_pairs(z):
    m = z.shape[1] // 2
    zb = z.astype(BF16).astype(F32)
    bits = lax.bitcast_convert_type(zb, jnp.uint32)
    return jnp.bitwise_or(bits[:, m:], lax.shift_right_logical(bits[:, :m], jnp.uint32(16)))


def _unpack_bf16_pairs(w):
    lo = lax.bitcast_convert_type(lax.shift_left(w, jnp.uint32(16)), F32)
    hi = lax.bitcast_convert_type(jnp.bitwise_and(w, jnp.uint32(0xFFFF0000)), F32)
    return jnp.concatenate([lo, hi], axis=1)


def _mm_ln_kernel(a_ref, w_ref, x_ref, g_ref, b_ref, wr_ref, br_ref, o_ref, opk_ref, cls_ref):
    y = jnp.dot(a_ref[...].astype(BF16), w_ref[...], preferred_element_type=F32)
    z = DEEPNORM_ALPHA * x_ref[...] + y
    out = _layer_norm_rows(z, g_ref[...], b_ref[...])
    o_ref[...] = out
    cls, gates = _route_rows(out, wr_ref[...], br_ref[...])
    cls_ref[...] = cls
    bm, dw = out.shape[0], out.shape[1] // 2
    on_diag = lax.broadcasted_iota(I32, (LANES, LANES), 0) == lax.broadcasted_iota(I32, (LANES, LANES), 1)
    lane = lax.broadcasted_iota(I32, (LANES, LANES), 1)
    tiles = []
    for c in range(bm // LANES):
        tile = jnp.zeros((LANES, LANES), F32)
        for k in range(2):
            g = gates[k:k + 1, c * LANES:(c + 1) * LANES]
            col = jnp.sum(jnp.where(on_diag, g, 0.0), axis=1, keepdims=True)
            tile = jnp.where(lane == k, col, tile)
        tiles.append(tile)
    opk_ref[:, :dw] = _pack_bf16_pairs(out)
    opk_ref[:, dw:] = lax.bitcast_convert_type(jnp.concatenate(tiles, axis=0), jnp.uint32)


def _proj_residual_ln(a, w_bf16, x, ln_g, ln_b, w_router_t, b_router):
    t, k = a.shape
    d = w_bf16.shape[1]
    e = w_router_t.shape[0]
    bm = LN_BM
    est = 2 * (bm * k * a.dtype.itemsize + k * d * 2 + 3 * bm * d * 4) + 8 * bm * d * 4
    return pl.pallas_call(
        _mm_ln_kernel,
        grid=(t // bm,),
        in_specs=[pl.BlockSpec((bm, k), lambda i: (i, 0)),
                  pl.BlockSpec((k, d), lambda i: (0, 0)),
                  pl.BlockSpec((bm, d), lambda i: (i, 0)),
                  pl.BlockSpec((1, d), lambda i: (0, 0)),
                  pl.BlockSpec((1, d), lambda i: (0, 0)),
                  pl.BlockSpec((e, d), lambda i: (0, 0)),
                  pl.BlockSpec((e, 1), lambda i: (0, 0))],
        out_specs=[pl.BlockSpec((bm, d), lambda i: (i, 0)),
                   pl.BlockSpec((bm, d // 2 + LANES), lambda i: (i, 0)),
                   pl.BlockSpec((1, bm), lambda i: (0, i))],
        out_shape=[jax.ShapeDtypeStruct((t, d), F32), jax.ShapeDtypeStruct((t, d // 2 + LANES), jnp.uint32),
                   jax.ShapeDtypeStruct((1, t), I32)],
        compiler_params=pltpu.CompilerParams(
            dimension_semantics=("parallel",), vmem_limit_bytes=_vmem_limit(est)),
        name="proj_residual_ln",
    )(a, w_bf16, x, ln_g.reshape(1, d), ln_b.reshape(1, d), w_router_t, b_router.reshape(e, 1))


def _top2_rows(v, iota):
    n_rows = v.shape[0]
    m1 = jnp.max(v, axis=0, keepdims=True)
    i1 = jnp.min(jnp.where(v == m1, iota, n_rows), axis=0, keepdims=True)
    v2 = jnp.where(iota == i1, -jnp.inf, v)
    m2 = jnp.max(v2, axis=0, keepdims=True)
    i2 = jnp.min(jnp.where(v2 == m2, iota, n_rows), axis=0, keepdims=True)
    return m1, i1, m2, i2


def _route_rows(x, w, b):
    xh = x.astype(BF16)
    xl = (x - xh.astype(F32)).astype(BF16)
    wh = w.astype(BF16)
    wl = (w - wh.astype(F32)).astype(BF16)
    nt = (((1,), (1,)), ((), ()))
    logits = (lax.dot_general(wh, xh, nt, preferred_element_type=F32)
              + lax.dot_general(wh, xl, nt, preferred_element_type=F32)
              + lax.dot_general(wl, xh, nt, preferred_element_type=F32))
    scores = 1.0 / (1.0 + jnp.exp(-logits))
    sel = scores + b
    epg = EXPERTS_PER_GROUP
    bt = x.shape[0]
    iota = lax.broadcasted_iota(I32, (epg, bt), 0)

    best = None
    for g in range(N_GROUPS):
        m1, _, m2, _ = _top2_rows(sel[g * epg:(g + 1) * epg], iota)
        gs = m1 + m2
        if best is None:
            best, gidx = gs, jnp.zeros((1, bt), I32)
        else:
            better = gs > best
            gidx = jnp.where(better, g, gidx)
            best = jnp.where(better, gs, best)

    sel_in = sel[0:epg]
    sc_in = scores[0:epg]
    for g in range(1, N_GROUPS):
        pick = gidx == g
        sel_in = jnp.where(pick, sel[g * epg:(g + 1) * epg], sel_in)
        sc_in = jnp.where(pick, scores[g * epg:(g + 1) * epg], sc_in)
    _, i1, _, i2 = _top2_rows(sel_in, iota)
    g1 = jnp.sum(jnp.where(iota == i1, sc_in, 0.0), axis=0, keepdims=True)
    g2 = jnp.sum(jnp.where(iota == i2, sc_in, 0.0), axis=0, keepdims=True)
    denom = g1 + g2
    first_lo = i1 < i2
    lo = jnp.where(first_lo, i1, i2)
    hi = jnp.where(first_lo, i2, i1)
    pair = lax.shift_right_logical(lo * (2 * epg - 1 - lo), 1) + (hi - lo - 1)
    cls = gidx * PAIRS_PER_GROUP + pair
    gates = jnp.concatenate([jnp.where(first_lo, g1, g2) / denom, jnp.where(first_lo, g2, g1) / denom], axis=0)
    return cls, gates


RANK_SUB = 256
RANK_NSUB = 8


def _rank_kernel(keys_ref, tri_ref, rank_ref, counts_ref, carry_ref, *, n_classes):
    nsub, sub = RANK_NSUB, RANK_SUB

    @pl.when(pl.program_id(0) == 0)
    def _():
        carry_ref[...] = jnp.zeros_like(carry_ref)

    cls = lax.broadcasted_iota(I32, (nsub, n_classes, sub), 1)
    onehot = cls == keys_ref[...]
    oh = jnp.where(onehot, 1.0, 0.0).reshape(nsub * n_classes, sub).astype(BF16)
    pref = jnp.dot(oh, tri_ref[...], preferred_element_type=F32).reshape(nsub, n_classes, sub)
    carry = carry_ref[...]
    for j in range(nsub):
        before = pref[j] + (carry - 1.0)
        rank_ref[j] = jnp.sum(jnp.where(onehot[j], before, 0.0), axis=0, keepdims=True).astype(I32)
        carry = carry + pref[j][:, sub - 1:sub]
    carry_ref[...] = carry
    counts_ref[...] = carry.astype(I32)


def _rank_within_class(keys, n_classes):
    n = keys.shape[0]
    nsub, sub = RANK_NSUB, RANK_SUB
    tri = jnp.asarray(np.triu(np.ones((sub, sub), np.float32)), dtype=BF16)
    rank, counts = pl.pallas_call(
        functools.partial(_rank_kernel, n_classes=n_classes),
        grid=(n // (nsub * sub),),
        in_specs=[pl.BlockSpec((nsub, 1, sub), lambda i: (i, 0, 0)),
                  pl.BlockSpec((sub, sub), lambda i: (0, 0))],
        out_specs=[pl.BlockSpec((nsub, 1, sub), lambda i: (i, 0, 0)),
                   pl.BlockSpec((n_classes, 1), lambda i: (0, 0))],
        out_shape=[jax.ShapeDtypeStruct((n // sub, 1, sub), I32),
                   jax.ShapeDtypeStruct((n_classes, 1), I32)],
        scratch_shapes=[pltpu.VMEM((n_classes, 1), F32)],
        compiler_params=pltpu.CompilerParams(dimension_semantics=("arbitrary",)),
        name="rank_within_class",
    )(keys.reshape(n // sub, 1, sub), tri)
    return rank.reshape(n), counts.reshape(n_classes)


FILL_CHUNK = 4096


def _fill_slots_kernel(dest_ref, default_hbm, slot_ref, sem):
    step = pl.program_id(0)

    @pl.when(step == 0)
    def _():
        copy = pltpu.make_async_copy(default_hbm, slot_ref, sem.at[0])
        copy.start()
        copy.wait()

    base = step * FILL_CHUNK

    def place(j, carry):
        slot_ref[dest_ref[j]] = base + j
        return carry

    lax.fori_loop(0, FILL_CHUNK, place, 0, unroll=32)


def _fill_slots(dest, default):
    n_items = dest.shape[0]
    assert n_items % FILL_CHUNK == 0
    return pl.pallas_call(
        _fill_slots_kernel,
        grid=(n_items // FILL_CHUNK,),
        in_specs=[pl.BlockSpec((FILL_CHUNK,), lambda i: (i,), memory_space=pltpu.SMEM),
                  pl.BlockSpec(memory_space=pl.ANY)],
        out_specs=pl.BlockSpec(memory_space=pltpu.SMEM),
        out_shape=jax.ShapeDtypeStruct(default.shape, I32),
        scratch_shapes=[pltpu.SemaphoreType.DMA((1,))],
        compiler_params=pltpu.CompilerParams(dimension_semantics=("arbitrary",)),
        name="fill_slots",
    )(dest, default)


def _class_experts():
    epg = EXPERTS_PER_GROUP
    pairs = [(lo, hi) for lo in range(epg) for hi in range(lo + 1, epg)]
    return np.array([[g * epg + lo, g * epg + hi] for g in range(N_GROUPS) for lo, hi in pairs], np.int32)


def _route_slots(cls, bm):
    t = cls.shape[1]
    keys = cls.reshape(t)
    rank, counts = _rank_within_class(keys, N_CLASSES)
    padded = (counts + bm - 1) // bm * bm
    pad_end = jnp.cumsum(padded)
    pad_start = pad_end - padded
    class_ids = jnp.arange(N_CLASSES, dtype=I32)
    dest = jnp.sum(jnp.where(keys[:, None] == class_ids[None, :], pad_start[None, :], 0), axis=1) + rank
    nb = t // bm + N_CLASSES + 1
    block_start = jnp.arange(nb, dtype=I32) * bm
    block_class = jnp.minimum(
        jnp.sum((pad_end[None, :] <= block_start[:, None]).astype(I32), axis=1), N_CLASSES - 1)
    row = jnp.arange(bm, dtype=I32)[None, :]
    in_use = (block_start < pad_end[-1])[:, None]
    default = jnp.where(in_use, t + block_class[:, None] * bm + row, row).reshape(nb * bm)
    slot_tok = _fill_slots(dest, default)
    in_class = block_class[:, None, None] == class_ids[None, :, None]
    block_experts = jnp.sum(jnp.where(in_class, jnp.asarray(_class_experts())[None], 0), axis=1).reshape(2 * nb)
    n_active = (pad_end[-1] // bm).astype(I32).reshape(1)
    return slot_tok, block_experts, n_active


SUBLANES = 8
SUBLANE_SHIFT = SUBLANES.bit_length() - 1


def _expert_kernel(be_ref, tok_ref, nact_ref, x_hbm, wga_ref, wua_ref, wda_ref, wgb_ref, wub_ref, wdb_ref,
                   y_hbm, xnext, xcur, ycur, yout, wga_bf, wua_bf, wda_bf, wgb_bf, wub_bf, wdb_bf, gsem, ssem,
                   *, n_tok, n_classes):
    bm = EXPERT_BM
    tiles = bm // SUBLANES
    dw = ycur.shape[-1]
    i = pl.program_id(0)
    n_blocks = pl.num_programs(0)
    nact = nact_ref[0]
    w_f32 = ((wga_ref, wua_ref, wda_ref), (wgb_ref, wub_ref, wdb_ref))
    wbf = (wga_bf, wua_bf, wda_bf, wgb_bf, wub_bf, wdb_bf)

    def row_copies(blk, gather, unrolled):
        base = blk * bm

        def body(rt, carry):
            for u in range(SUBLANES):
                row = tok_ref[base + rt * SUBLANES + u]
                if gather:
                    row = jnp.bitwise_and(row, n_tok - 1)
                    pltpu.make_async_copy(
                        x_hbm.at[lax.shift_right_logical(row, SUBLANE_SHIFT),
                                 pl.ds(jnp.bitwise_and(row, SUBLANES - 1), 1), :],
                        xnext.at[rt, pl.ds(u, 1), :], gsem.at[0]).start()
                else:
                    pltpu.make_async_copy(
                        yout.at[rt, pl.ds(u, 1), :],
                        y_hbm.at[lax.shift_right_logical(row, SUBLANE_SHIFT),
                                 pl.ds(jnp.bitwise_and(row, SUBLANES - 1), 1), :],
                        ssem.at[0]).start()
            return carry

        if unrolled is None:
            lax.fori_loop(0, tiles, body, 0)
        else:
            for rt in range(*unrolled):
                body(rt, 0)

    def wait_gather():
        pltpu.make_async_copy(x_hbm.at[pl.ds(0, tiles)], xnext, gsem.at[0]).wait()

    def wait_scatter():
        pltpu.make_async_copy(yout, y_hbm.at[pl.ds(0, tiles)], ssem.at[0]).wait()

    @pl.when(i == 0)
    def _():
        row_copies(0, True, unrolled=None)
        yout[...] = jnp.zeros(yout.shape, yout.dtype)
        ycur[...] = jnp.zeros(ycur.shape, ycur.dtype)
        spare_tile0 = n_tok // SUBLANES

        def spare_copy(k):
            return pltpu.make_async_copy(
                yout, y_hbm.at[pl.ds(spare_tile0 + k * tiles, tiles)], ssem.at[0])

        @pl.loop(0, n_classes)
        def _(k):
            spare_copy(k).start()

        @pl.loop(0, n_classes)
        def _(k):
            spare_copy(k).wait()

    for side in range(2):
        idx = 2 * i + side
        changed = jnp.logical_or(i == 0, be_ref[idx] != be_ref[jnp.maximum(idx - 2, 0)])

        @pl.when(jnp.logical_and(changed, i < nact))
        def _():
            for m in range(3):
                wbf[3 * side + m][...] = w_f32[side][m][0].astype(BF16)

    @pl.when(i < nact)
    def _():
        wait_gather()
        xcur[...] = xnext[...]

        @pl.when(i >= 1)
        def _():
            wait_scatter()

        yout[...] = ycur[...]

    @pl.when(nact - i >= 1)
    def _():
        row_copies(jnp.minimum(i + 1, n_blocks - 1), True, unrolled=(0, tiles))
        row_copies(jnp.maximum(i - 1, 0), False, unrolled=(0, tiles))
        rows = xcur[...].reshape(bm, dw + LANES)
        x = _unpack_bf16_pairs(rows[:, :dw]).astype(BF16)
        gates = lax.bitcast_convert_type(rows[:, dw:], F32)
        y = None
        for side in range(2):
            wg_bf, wu_bf, wd_bf = wbf[3 * side], wbf[3 * side + 1], wbf[3 * side + 2]
            gate = jnp.dot(x, wg_bf[...], preferred_element_type=F32)
            up = jnp.dot(x, wu_bf[...], preferred_element_type=F32)
            h = (gate / (1.0 + jnp.exp(-gate))) * up
            ys = jnp.dot(h.astype(BF16), wd_bf[...], preferred_element_type=F32) * gates[:, side:side + 1]
            y = ys if y is None else y + ys
        ycur[...] = _pack_bf16_pairs(y).reshape(tiles, SUBLANES, dw)

    @pl.when(i == nact)
    def _():
        wait_gather()
        wait_scatter()
        yout[...] = ycur[...]
        row_copies(i - 1, False, unrolled=None)
        wait_scatter()


def _experts(x_pk, slot_tok, block_experts, n_active, w_gate, w_up, w_down, layer):
    t, dw_in = x_pk.shape
    _, _, d, f = w_gate.shape
    dw = d // 2
    assert dw_in == dw + LANES
    bm = EXPERT_BM
    n_blocks = slot_tok.shape[0] // bm
    n_rows = t + N_CLASSES * bm
    assert t & (t - 1) == 0 and t % SUBLANES == 0 and bm % SUBLANES == 0
    est = 2 * 6 * d * f * 4 + 6 * d * f * 2 + 4 * bm * d * 2 + 10 * bm * d * 4

    def w_spec(shape, side):
        return pl.BlockSpec((None, 1) + shape, lambda i, be, tk, na: (layer, be[2 * i + side], 0, 0))

    xbuf = pltpu.VMEM((bm // SUBLANES, SUBLANES, dw_in), jnp.uint32)
    ybuf = pltpu.VMEM((bm // SUBLANES, SUBLANES, dw), jnp.uint32)
    grid_spec = pltpu.PrefetchScalarGridSpec(
        num_scalar_prefetch=3,
        grid=(n_blocks,),
        in_specs=[pl.BlockSpec(memory_space=pl.ANY),
                  w_spec((d, f), 0), w_spec((d, f), 0), w_spec((f, d), 0),
                  w_spec((d, f), 1), w_spec((d, f), 1), w_spec((f, d), 1)],
        out_specs=pl.BlockSpec(memory_space=pl.ANY),
        scratch_shapes=[xbuf, xbuf, ybuf, ybuf]
                       + [pltpu.VMEM((d, f), BF16), pltpu.VMEM((d, f), BF16), pltpu.VMEM((f, d), BF16)] * 2
                       + [pltpu.SemaphoreType.DMA((1,)), pltpu.SemaphoreType.DMA((1,))],
    )
    y = pl.pallas_call(
        functools.partial(_expert_kernel, n_tok=t, n_classes=N_CLASSES),
        grid_spec=grid_spec,
        out_shape=jax.ShapeDtypeStruct((n_rows // SUBLANES, SUBLANES, dw), jnp.uint32),
        compiler_params=pltpu.CompilerParams(
            dimension_semantics=("arbitrary",), vmem_limit_bytes=_vmem_limit(est)),
        name="experts",
    )(block_experts, slot_tok, n_active, x_pk.reshape(t // SUBLANES, SUBLANES, dw_in),
      w_gate, w_up, w_down, w_gate, w_up, w_down)
    return y.reshape(n_rows, dw)


def _combine_kernel(x_ref, y_ref, g_ref, b_ref, o_ref):
    z = DEEPNORM_ALPHA * x_ref[...] + _unpack_bf16_pairs(y_ref[...])
    o_ref[...] = _layer_norm_rows(z, g_ref[...], b_ref[...])


def _combine_ln(x, y_tok, ln_g, ln_b):
    t, d = x.shape
    bt = COMBINE_BT
    return pl.pallas_call(
        _combine_kernel,
        grid=(t // bt,),
        in_specs=[pl.BlockSpec((bt, d), lambda i: (i, 0)),
                  pl.BlockSpec((bt, d // 2), lambda i: (i, 0)),
                  pl.BlockSpec((1, d), lambda i: (0, 0)),
                  pl.BlockSpec((1, d), lambda i: (0, 0))],
        out_specs=pl.BlockSpec((bt, d), lambda i: (i, 0)),
        out_shape=jax.ShapeDtypeStruct((t, d), F32),
        compiler_params=pltpu.CompilerParams(
            dimension_semantics=("parallel",), vmem_limit_bytes=_vmem_limit(10 * bt * d * 4)),
        name="combine_ln",
    )(x, y_tok, ln_g.reshape(1, d), ln_b.reshape(1, d))


def _moe_residual_ln(x, x_pk, cls, w_gate, w_up, w_down, layer, ln_g, ln_b):
    slot_tok, block_experts, n_active = _route_slots(cls, EXPERT_BM)
    y_tok = _experts(x_pk, slot_tok, block_experts, n_active, w_gate, w_up, w_down, layer)
    return _combine_ln(x, y_tok, ln_g, ln_b)


def _na_bias_pairs(rpb):
    w = GRID_W
    kc = min(WIN_COLS, w)
    cols = np.arange(w)
    col_start = np.clip(cols - kc // 2, 0, w - kc)
    col_mask = (cols[None, :] >= col_start[:, None]) & (cols[None, :] < col_start[:, None] + kc)
    col_idx = np.clip(cols[None, :] - cols[:, None] + WIN_COLS - 1, 0, 2 * WIN_COLS - 2)
    n_ci = 2 * WIN_COLS - 1
    pick = (col_idx[None] == np.arange(n_ci)[:, None, None]).astype(np.float32)
    tb = jnp.einsum("hrc,cqk->hrqk", rpb.astype(F32), jnp.asarray(pick), precision=lax.Precision.HIGHEST)
    tb = jnp.where(col_mask[None, None], tb, NEG_BIAS)
    neg = jnp.full((N_HEADS, 1, w, w), NEG_BIAS, F32)
    ext = jnp.concatenate([neg, tb, neg], axis=1)
    return jnp.concatenate([ext[:, :-1], ext[:, 1:]], axis=-1)


def _na_tile_plan(rows):
    kr = min(WIN_ROWS, rows)
    n_blocks = rows // NA_RQ
    plan = []
    for blk in (0, 1, n_blocks - 1):
        ks = int(np.clip(blk - 1, 0, n_blocks - 3)) * NA_RQ
        per_rq = []
        for rq in range(NA_RQ):
            r = blk * NA_RQ + rq
            rs = int(np.clip(r - kr // 2, 0, rows - kr))
            ri = [ks + j - r + WIN_ROWS - 1 if rs <= ks + j < rs + kr else None for j in range(NA_KROWS)]
            tiles = []
            for jp in range(NA_KROWS // 2):
                a, b = ri[2 * jp], ri[2 * jp + 1]
                if a is not None and b is not None:
                    tiles.append(("both", b))
                elif b is not None:
                    tiles.append(("hi", b))
                elif a is not None:
                    tiles.append(("lo", a + 1))
                else:
                    tiles.append(("none", 0))
            per_rq.append(tiles)
        plan.append(per_rq)
    return plan


def _na_kernel(q_ref, k0_ref, k1_ref, k2_ref, v0_ref, v1_ref, v2_ref, pair_ref, o_ref, bias_ref, *, plan):
    dh = HEAD_DIM
    w = GRID_W
    nt = (((1,), (1,)), ((), ()))
    scale = HEAD_DIM ** -0.5
    i = pl.program_id(2)
    n_blocks = pl.num_programs(2)

    def rebuild(per_rq):
        upper = lax.broadcasted_iota(I32, (w, 2 * w), 1) >= w
        for rq, tiles in enumerate(per_rq):
            for jp, (kind, k) in enumerate(tiles):
                for h in range(NA_HEADS_PER_STEP):
                    if kind == "none":
                        tile = jnp.full((w, 2 * w), NEG_BIAS, F32)
                    elif kind == "both":
                        tile = pair_ref[h, k]
                    elif kind == "hi":
                        tile = jnp.where(upper, pair_ref[h, k], NEG_BIAS)
                    else:
                        tile = jnp.where(upper, NEG_BIAS, pair_ref[h, k])
                    bias_ref[h, rq * w:(rq + 1) * w, jp * 2 * w:(jp + 1) * 2 * w] = tile

    for ty, first_step in enumerate((0, 1, n_blocks - 1)):
        @pl.when(i == first_step)
        def _():
            rebuild(plan[ty])

    assert 2 * dh == LANES
    tq = q_ref.shape[1]
    low = lax.broadcasted_iota(I32, (tq, LANES), 1) < dh
    low_k = lax.broadcasted_iota(I32, (3 * tq, LANES), 1) < dh
    outs = []
    for hp in range(NA_HEADS_PER_STEP // 2):
        cols = slice(hp * LANES, (hp + 1) * LANES)
        q2 = q_ref[0, :, cols] * scale
        k2 = jnp.concatenate([k0_ref[0, :, cols], k1_ref[0, :, cols], k2_ref[0, :, cols]], axis=0)
        v2 = jnp.concatenate([v0_ref[0, :, cols], v1_ref[0, :, cols], v2_ref[0, :, cols]], axis=0)
        halves = []
        for sub in range(2):
            mine, mine_k = (low, low_k) if sub == 0 else (~low, ~low_k)
            q = jnp.where(mine, q2, jnp.zeros_like(q2))
            v = jnp.where(mine_k, v2, jnp.ones_like(v2))
            s = lax.dot_general(q, k2, nt, preferred_element_type=F32) + bias_ref[2 * hp + sub]
            m = jnp.max(s, axis=-1, keepdims=True)
            p = jnp.exp((s - m).astype(BF16))
            o = jnp.dot(p, v, preferred_element_type=F32)
            denom = o[:, dh:dh + 1] if sub == 0 else o[:, 0:1]
            halves.append(o * (1.0 / denom))
        outs.append(jnp.where(low, halves[0], halves[1]))
    o_ref[0] = jnp.concatenate(outs, axis=-1).astype(o_ref.dtype)


def _neighbourhood_attention(qkv, bias_pairs, rows):
    b, s, _ = qkv.shape
    d = D_MODEL
    w = GRID_W
    tq = NA_RQ * w
    n_blocks = rows // NA_RQ
    hps = NA_HEADS_PER_STEP
    hw = hps * HEAD_DIM
    n_hh = d // hw
    assert NA_KROWS * w == 3 * tq and n_blocks >= 3 and NA_KROWS % 2 == 0

    def kv_map(part, j):
        def index_map(hh, bi, i):
            return (bi, jnp.clip(i - 1, 0, n_blocks - 3) + j, part * n_hh + hh)
        return index_map

    blk = (1, tq, hw)
    pair_blk = (hps,) + bias_pairs.shape[1:]
    est = (2 * (7 * tq * hw * 2 + tq * hw * 2) + 2 * 4 * int(np.prod(pair_blk))
           + hps * tq * 3 * tq * 4 + 8 * tq * 3 * tq * 4)
    return pl.pallas_call(
        functools.partial(_na_kernel, plan=_na_tile_plan(rows)),
        grid=(n_hh, b, n_blocks),
        in_specs=[pl.BlockSpec(blk, lambda hh, bi, i: (bi, i, hh))]
                 + [pl.BlockSpec(blk, kv_map(1, j)) for j in range(3)]
                 + [pl.BlockSpec(blk, kv_map(2, j)) for j in range(3)]
                 + [pl.BlockSpec(pair_blk, lambda hh, bi, i: (hh, 0, 0, 0))],
        out_specs=pl.BlockSpec(blk, lambda hh, bi, i: (bi, i, hh)),
        out_shape=jax.ShapeDtypeStruct((b, s, d), BF16),
        scratch_shapes=[pltpu.VMEM((hps, tq, 3 * tq), F32)],
        compiler_params=pltpu.CompilerParams(
            dimension_semantics=("arbitrary", "arbitrary", "arbitrary"), vmem_limit_bytes=_vmem_limit(est)),
        name="neighbourhood_attention",
    )(qkv, qkv, qkv, qkv, qkv, qkv, qkv, bias_pairs)


def kernel(x, fourier_w_in, fourier_w_out, na_w_qkv, na_rpb, na_w_out, router_w, router_b,
           expert_w_gate, expert_w_up, expert_w_down, ln_g, ln_b):
    b, s, d = x.shape
    t = b * s
    rows = s // GRID_W
    w_router_t = router_w.T
    xt = x.reshape(t, d)

    f = _fourier_mixer_pre_out(x, fourier_w_in[0])
    xt, xt_pk, cls = _proj_residual_ln(f.reshape(t, d), fourier_w_out[0].astype(BF16), xt,
                                       ln_g[0, 0], ln_b[0, 0], w_router_t, router_b)
    xt = _moe_residual_ln(xt, xt_pk, cls, expert_w_gate, expert_w_up, expert_w_down, 0,
                          ln_g[0, 1], ln_b[0, 1])

    qkv = _matmul(xt, na_w_qkv[0].astype(BF16), BF16, MM_BM, 3 * d).reshape(b, s, 3 * d)
    bias = _na_bias_pairs(na_rpb[0])
    o = _neighbourhood_attention(qkv, bias, rows)
    xt, xt_pk, cls = _proj_residual_ln(o.reshape(t, d), na_w_out[0].astype(BF16), xt,
                                       ln_g[1, 0], ln_b[1, 0], w_router_t, router_b)
    xt = _moe_residual_ln(xt, xt_pk, cls, expert_w_gate, expert_w_up, expert_w_down, 1,
                          ln_g[1, 1], ln_b[1, 1])
    return xt.reshape(b, s, d)
```
